```python
import math
import jax
import jax.numpy as jnp
from jax import lax
import numpy as np

D_MODEL = 1024
BATCH = 8
SEQ = 2048
DEPTH = 2
DEC_BATCH = 128
DEC_SEQ = 1
PAST_LEN = 16384
PAGE_SIZE = 128

H_A = 16
P_A = 64
D_A = H_A * P_A
N_A = 64
G_A = 2
CONV_A = 4
D_XBC = D_A + 2 * G_A * N_A
H_R = 8
DK_R = 64
DV_R = 128
D_QK = H_R * DK_R
D_RV = H_R * DV_R
ROPE_BASE = 10000.0
GS_C = 16
G_C = 64
D_C = G_C * GS_C
P_C = 64
D_FF = 2816
CONV_F = 3
CHUNK = 128
N_BRANCH = 3
EPS = 1e-6
D_IN = D_A + D_XBC + H_A + 2 * D_QK + 2 * D_RV + D_C + N_BRANCH * D_MODEL

kernel_name = 'hybrid_ssd_retention_s5_step'


def rms_unit(x):
    xf = x.astype(jnp.float32)
    return xf * lax.rsqrt(jnp.mean(xf * xf, axis=-1, keepdims=True) + EPS)


def rmsnorm(x, w):
    return (rms_unit(x) * w.astype(jnp.float32)).astype(x.dtype)


def causal_dwconv(x, buf, w, b):
    width = w.shape[0]
    L = x.shape[1]
    xp = jnp.concatenate([buf.astype(x.dtype), x], axis=1)
    y = b
    for tap in range(width):
        y = y + xp[:, tap:tap + L] * w[tap]
    return y, xp[:, L:]


def rotary(x, pos):
    half = x.shape[-1] // 2
    freqs = ROPE_BASE ** (-jnp.arange(half, dtype=jnp.float32) / half)
    ang = pos[:, None] * freqs[None, :]
    cos = jnp.cos(ang)[None, :, None, :]
    sin = jnp.sin(ang)[None, :, None, :]
    x1 = x[..., :half].astype(jnp.float32)
    x2 = x[..., half:].astype(jnp.float32)
    return jnp.concatenate([x1 * cos - x2 * sin, x1 * sin + x2 * cos], axis=-1)


def ssd_scan(x, dt, a, bm, cm, h0):
    bsz, L, H, P = x.shape
    N = bm.shape[-1]
    q = math.gcd(L, CHUNK)
    nc = L // q
    xf = x.astype(jnp.float32).reshape(bsz, nc, q, H, P)
    bmf = bm.astype(jnp.float32).reshape(bsz, nc, q, H, N)
    cmf = cm.astype(jnp.float32).reshape(bsz, nc, q, H, N)
    dtc = dt.reshape(bsz, nc, q, H)
    xdt = xf * dtc[..., None]
    cum = jnp.cumsum(dtc * a, axis=2)
    seg = cum[:, :, :, None, :] - cum[:, :, None, :, :]
    causal = jnp.tril(jnp.ones((q, q), dtype=bool))[None, None, :, :, None]
    decay = jnp.exp(jnp.where(causal, seg, -jnp.inf))
    scores = jnp.einsum('bcihn,bcjhn->bcijh', cmf, bmf) * decay
    y_intra = jnp.einsum('bcijh,bcjhp->bcihp', scores, xdt)
    to_end = jnp.exp(cum[:, :, -1:, :] - cum)
    chunk_states = jnp.einsum('bcjh,bcjhn,bcjhp->bchpn', to_end, bmf, xdt)
    chunk_decay = jnp.exp(cum[:, :, -1, :])

    def step(h, inp):
        dec, cs = inp
        return dec[:, :, None, None] * h + cs, h

    h_last, h_in = lax.scan(step, h0.astype(jnp.float32),
                            (jnp.moveaxis(chunk_decay, 1, 0), jnp.moveaxis(chunk_states, 1, 0)))
    h_in = jnp.moveaxis(h_in, 0, 1)
    y_inter = jnp.einsum('bcihn,bchpn->bcihp', cmf, h_in) * jnp.exp(cum)[..., None]
    return (y_intra + y_inter).reshape(bsz, L, H, P), h_last


def retention_scan(q, k, v, s0):
    bsz, L, H, DK = q.shape
    DV = v.shape[-1]
    cq = math.gcd(L, CHUNK)
    nc = L // cq
    lg = jnp.log1p(-jnp.exp2(-5.0 - jnp.arange(H, dtype=jnp.float32)))
    idx = jnp.arange(cq, dtype=jnp.float32)
    rel = idx[:, None] - idx[None, :]
    intra = jnp.where((rel >= 0)[..., None], jnp.exp(jnp.maximum(rel, 0.0)[..., None] * lg), 0.0)
    qf = q.astype(jnp.float32).reshape(bsz, nc, cq, H, DK)
    kf = k.astype(jnp.float32).reshape(bsz, nc, cq, H, DK)
    vf = v.astype(jnp.float32).reshape(bsz, nc, cq, H, DV)
    scores = jnp.einsum('bcihd,bcjhd->bcijh', qf, kf) * intra
    y_intra = jnp.einsum('bcijh,bcjhv->bcihv', scores, vf)
    k_dec = jnp.exp((cq - 1.0 - idx)[:, None] * lg)
    chunk_kv = jnp.einsum('jh,bcjhd,bcjhv->bchdv', k_dec, kf, vf)
    chunk_decay = jnp.exp(cq * lg)

    def step(s, ckv):
        return chunk_decay[None, :, None, None] * s + ckv, s

    s_last, s_in = lax.scan(step, s0.astype(jnp.float32), jnp.moveaxis(chunk_kv, 1, 0))
    s_in = jnp.moveaxis(s_in, 0, 1)
    q_dec = jnp.exp((idx + 1.0)[:, None] * lg)
    y_inter = jnp.einsum('bcihd,bchdv->bcihv', qf, s_in) * q_dec[:, :, None]
    return (y_intra + y_inter).reshape(bsz, L, H, DV), s_last


def s5_scan(u, h0_re, h0_im, a_re, a_im, log_dt, b_re, b_im, c_re, c_im, d):
    bsz, L, _ = u.shape
    f32 = jnp.float32
    uf = u.astype(f32).reshape(bsz, L, G_C, GS_C)
    ar = a_re.astype(f32)
    ai = a_im.astype(f32)
    dt = jnp.exp(log_dt.astype(f32))[:, None]
    mag = jnp.exp(ar * dt)
    ab_re = mag * jnp.cos(ai * dt)
    ab_im = mag * jnp.sin(ai * dt)
    den = ar * ar + ai * ai
    num_re = ab_re - 1.0
    coef_re = (num_re * ar + ab_im * ai) / den
    coef_im = (ab_im * ar - num_re * ai) / den
    br = b_re.astype(f32)
    bi = b_im.astype(f32)
    bb_re = coef_re[..., None] * br - coef_im[..., None] * bi
    bb_im = coef_re[..., None] * bi + coef_im[..., None] * br
    bu_re = jnp.einsum('gpc,blgc->blgp', bb_re, uf)
    bu_im = jnp.einsum('gpc,blgc->blgp', bb_im, uf)
    hr = h0_re.astype(f32)
    hi = h0_im.astype(f32)
    bu_re = bu_re.at[:, 0].add(ab_re * hr - ab_im * hi)
    bu_im = bu_im.at[:, 0].add(ab_re * hi + ab_im * hr)
    a_r = jnp.broadcast_to(ab_re, (1, L, G_C, P_C))
    a_i = jnp.broadcast_to(ab_im, (1, L, G_C, P_C))

    def combine(e1, e2):
        ar1, ai1, br1, bi1 = e1
        ar2, ai2, br2, bi2 = e2
        return (ar2 * ar1 - ai2 * ai1, ar2 * ai1 + ai2 * ar1,
                ar2 * br1 - ai2 * bi1 + br2, ar2 * bi1 + ai2 * br1 + bi2)

    _, _, xs_re, xs_im = lax.associative_scan(combine, (a_r, a_i, bu_re, bu_im), axis=1)
    y = (jnp.einsum('gcp,blgp->blgc', c_re.astype(f32), xs_re)
         - jnp.einsum('gcp,blgp->blgc', c_im.astype(f32), xs_im))
    y = y.reshape(bsz, L, D_C) + d.astype(f32) * uf.reshape(bsz, L, D_C)
    return y, xs_re[:, -1], xs_im[:, -1]


def trunk_layer(x, pos, conv_a_buf, h_ssd, s_ret, h5_re, h5_im, conv_f_buf, w):
    (norm_mix_w, w_in, conv_a_w, conv_a_b, dt_bias, a_log, d_a, norm_a_w,
     s5_a_re, s5_a_im, s5_log_dt, s5_b_re, s5_b_im, s5_c_re, s5_c_im, s5_d,
     w_glu, b_glu, w_br_a, w_br_b, w_br_c, w_out,
     norm_ffn_w, w_up, conv_f_w, conv_f_b, w_down) = w
    bsz, L, _ = x.shape
    h = rmsnorm(x, norm_mix_w)
    proj = h @ w_in
    splits = np.cumsum([D_A, D_XBC, H_A, D_QK, D_QK, D_RV, D_RV, D_C]).tolist()
    z, xbc, dt_raw, q, k, v, g, u, gates = jnp.split(proj, splits, axis=-1)
    xbc_c, conv_a_new = causal_dwconv(xbc, conv_a_buf, conv_a_w, conv_a_b)
    xbc_c = jax.nn.silu(xbc_c)
    xa, ba, ca = jnp.split(xbc_c, [D_A, D_A + G_A * N_A], axis=-1)
    xa = xa.reshape(bsz, L, H_A, P_A)
    ba = jnp.repeat(ba.reshape(bsz, L, G_A, N_A), H_A // G_A, axis=2)
    ca = jnp.repeat(ca.reshape(bsz, L, G_A, N_A), H_A // G_A, axis=2)
    dt = jax.nn.softplus(dt_raw.astype(jnp.float32) + dt_bias.astype(jnp.float32))
    a = -jnp.exp(a_log.astype(jnp.float32))
    ya, h_ssd_new = ssd_scan(xa, dt, a, ba, ca, h_ssd)
    ya = (ya + d_a.astype(jnp.float32)[:, None] * xa).reshape(bsz, L, D_A)
    ya = rmsnorm(ya * jax.nn.silu(z.astype(jnp.float32)), norm_a_w)
    qr = rotary(q.reshape(bsz, L, H_R, DK_R), pos)
    kr = rotary(k.reshape(bsz, L, H_R, DK_R), pos) * (DK_R ** -0.5)
    vr = v.reshape(bsz, L, H_R, DV_R)
    yb, s_ret_new = retention_scan(qr, kr, vr, s_ret)
    yb = jax.nn.silu(g.astype(jnp.float32)) * rms_unit(yb).reshape(bsz, L, D_RV)
    yc, h5_re_new, h5_im_new = s5_scan(u, h5_re, h5_im, s5_a_re, s5_a_im, s5_log_dt,
                                       s5_b_re, s5_b_im, s5_c_re, s5_c_im, s5_d)
    yc = jax.nn.gelu(yc)
    yc = yc * jax.nn.sigmoid(yc @ w_glu + b_glu)
    ga, gb, gc = jnp.split(jax.nn.sigmoid(gates.astype(jnp.float32)), N_BRANCH, axis=-1)
    merged = ga * (ya @ w_br_a) + gb * (yb @ w_br_b) + gc * (yc @ w_br_c)
    x = x + (merged @ w_out).astype(x.dtype)
    hf = rmsnorm(x, norm_ffn_w)
    up, conv_f_new = causal_dwconv(hf @ w_up, conv_f_buf, conv_f_w, conv_f_b)
    gate_f, val_f = jnp.split(up, 2, axis=-1)
    x = x + ((jax.nn.silu(gate_f) * val_f) @ w_down).astype(x.dtype)
    new_state = (conv_a_new.astype(conv_a_buf.dtype), h_ssd_new.astype(h_ssd.dtype),
                 s_ret_new.astype(s_ret.dtype), h5_re_new.astype(h5_re.dtype),
                 h5_im_new.astype(h5_im.dtype), conv_f_new.astype(conv_f_buf.dtype))
    return x, new_state


def run_trunk(x, start, states, weights, norm_f_w):
    pos = start + jnp.arange(x.shape[1], dtype=jnp.float32)
    per_layer = []
    for l in range(DEPTH):
        x, ns = trunk_layer(x, pos, *[s[l] for s in states], [wt[l] for wt in weights])
        per_layer.append(ns)
    y = rmsnorm(x, norm_f_w)
    new_states = [jnp.stack([ns[i] for ns in per_layer]) for i in range(len(states))]
    return y, new_states


def empty_state(bsz, dtype):
    return (jnp.zeros((DEPTH, bsz, CONV_A - 1, D_XBC), dtype),
            jnp.zeros((DEPTH, bsz, H_A, P_A, N_A), dtype),
            jnp.zeros((DEPTH, bsz, H_R, DK_R, DV_R), dtype),
            jnp.zeros((DEPTH, bsz, G_C, P_C), dtype),
            jnp.zeros((DEPTH, bsz, G_C, P_C), dtype),
            jnp.zeros((DEPTH, bsz, CONV_F - 1, 2 * D_FF), dtype))


def setup_inputs(seed: int = 0) -> dict:
    key = jax.random.key(seed)
    ks = jax.random.split(key, 40)
    f32 = jnp.float32

    def nrm(i, shape, s):
        return s * jax.random.normal(ks[i], shape, f32)

    dt_init = jnp.exp(jax.random.uniform(ks[30], (DEPTH, H_A), f32, math.log(1e-3), math.log(1e-1)))
    dt_bias = dt_init + jnp.log(-jnp.expm1(-dt_init))
    a_log = jnp.log(jax.random.uniform(ks[31], (DEPTH, H_A), f32, 1.0, 16.0))
    s5_log_dt = jax.random.uniform(ks[32], (DEPTH, G_C), f32, math.log(1e-3), math.log(1e-1))
    s5_a_re = -0.5 + nrm(33, (DEPTH, G_C, P_C), 0.01)
    s5_a_im = math.pi * jnp.arange(P_C, dtype=f32) + nrm(34, (DEPTH, G_C, P_C), 0.01)
    return {
        'x_prompt': nrm(0, (BATCH, SEQ, D_MODEL), 1.0),
        'x_sample': nrm(1, (DEC_BATCH, DEC_SEQ, D_MODEL), 1.0),
        'state_ssd_conv': nrm(2, (DEPTH, DEC_BATCH, CONV_A - 1, D_XBC), 1.0),
        'state_ssd': nrm(3, (DEPTH, DEC_BATCH, H_A, P_A, N_A), 0.1),
        'state_ret': nrm(4, (DEPTH, DEC_BATCH, H_R, DK_R, DV_R), 0.1),
        'state_s5_re': nrm(5, (DEPTH, DEC_BATCH, G_C, P_C), 0.3),
        'state_s5_im': nrm(6, (DEPTH, DEC_BATCH, G_C, P_C), 0.3),
        'state_ffn_conv': nrm(7, (DEPTH, DEC_BATCH, CONV_F - 1, 2 * D_FF), 1.0),
        'norm_mix_w': 1.0 + nrm(8, (DEPTH, D_MODEL), 0.02),
        'w_in': nrm(9, (DEPTH, D_MODEL, D_IN), D_MODEL ** -0.5),
        'conv_a_w': nrm(10, (DEPTH, CONV_A, D_XBC), CONV_A ** -0.5),
        'conv_a_b': nrm(11, (DEPTH, D_XBC), 0.02),
        'dt_bias': dt_bias,
        'a_log': a_log,
        'd_a': 1.0 + nrm(12, (DEPTH, H_A), 0.02),
        'norm_a_w': 1.0 + nrm(13, (DEPTH, D_A), 0.02),
        's5_a_re': s5_a_re,
        's5_a_im': s5_a_im,
        's5_log_dt': s5_log_dt,
        's5_b_re': nrm(14, (DEPTH, G_C, P_C, GS_C), (2 * GS_C) ** -0.5),
        's5_b_im': nrm(15, (DEPTH, G_C, P_C, GS_C), (2 * GS_C) ** -0.5),
        's5_c_re': nrm(16, (DEPTH, G_C, GS_C, P_C), P_C ** -0.5),
        's5_c_im': nrm(17, (DEPTH, G_C, GS_C, P_C), P_C ** -0.5),
        's5_d': nrm(18, (DEPTH, D_C), 1.0),
        'w_glu': nrm(19, (DEPTH, D_C, D_C), D_C ** -0.5),
        'b_glu': nrm(20, (DEPTH, D_C), 0.02),
        'w_br_a': nrm(21, (DEPTH, D_A, D_MODEL), D_A ** -0.5),
        'w_br_b': nrm(22, (DEPTH, D_RV, D_MODEL), D_RV ** -0.5),
        'w_br_c': nrm(23, (DEPTH, D_C, D_MODEL), D_C ** -0.5),
        'w_out': nrm(24, (DEPTH, D_MODEL, D_MODEL), D_MODEL ** -0.5),
        'norm_ffn_w': 1.0 + nrm(25, (DEPTH, D_MODEL), 0.02),
        'w_up': nrm(26, (DEPTH, D_MODEL, 2 * D_FF), D_MODEL ** -0.5),
        'conv_f_w': nrm(27, (DEPTH, CONV_F, 2 * D_FF), CONV_F ** -0.5),
        'conv_f_b': nrm(28, (DEPTH, 2 * D_FF), 0.02),
        'w_down': nrm(29, (DEPTH, D_FF, D_MODEL), D_FF ** -0.5),
        'norm_f_w': 1.0 + nrm(35, (D_MODEL,), 0.02),
    }


def reference(x_prompt, x_sample, state_ssd_conv, state_ssd, state_ret, state_s5_re, state_s5_im,
              state_ffn_conv, norm_mix_w, w_in, conv_a_w, conv_a_b, dt_bias, a_log, d_a, norm_a_w,
              s5_a_re, s5_a_im, s5_log_dt, s5_b_re, s5_b_im, s5_c_re, s5_c_im, s5_d, w_glu, b_glu,
              w_br_a, w_br_b, w_br_c, w_out, norm_ffn_w, w_up, conv_f_w, conv_f_b, w_down, norm_f_w):
    weights = (norm_mix_w, w_in, conv_a_w, conv_a_b, dt_bias, a_log, d_a, norm_a_w,
               s5_a_re, s5_a_im, s5_log_dt, s5_b_re, s5_b_im, s5_c_re, s5_c_im, s5_d,
               w_glu, b_glu, w_br_a, w_br_b, w_br_c, w_out,
               norm_ffn_w, w_up, conv_f_w, conv_f_b, w_down)
    y_prompt, p_new = run_trunk(x_prompt, 0.0, empty_state(x_prompt.shape[0], x_prompt.dtype),
                                weights, norm_f_w)
    s_states = (state_ssd_conv, state_ssd, state_ret, state_s5_re, state_s5_im, state_ffn_conv)
    y_sample, s_new = run_trunk(x_sample, float(PAST_LEN), s_states, weights, norm_f_w)
    p_ssd_conv, p_ssd, p_ret, p_s5_re, p_s5_im, p_ffn_conv = p_new
    s_ssd_conv, s_ssd, s_ret, s_s5_re, s_s5_im, s_ffn_conv = s_new
    return (y_prompt, y_sample, p_ssd_conv, p_ssd, p_ret, p_s5_re, p_s5_im, p_ffn_conv,
            s_ssd_conv, s_ssd, s_ret, s_s5_re, s_s5_im, s_ffn_conv)
```

```python
import functools
import math

import jax
import jax.numpy as jnp
import numpy as np
from jax import lax
from jax.experimental import pallas as pl
from jax.experimental.pallas import tpu as pltpu

F32 = jnp.float32
BF16 = jnp.bfloat16
HIGHEST = lax.Precision.HIGHEST

D_MODEL = 1024
DEPTH = 2
PAST_LEN = 16384
H_A, P_A, N_A, G_A, CONV_A = 16, 64, 64, 2, 4
D_A = H_A * P_A
D_XBC = D_A + 2 * G_A * N_A
H_R, DK_R, DV_R = 8, 64, 128
D_QK = H_R * DK_R
D_RV = H_R * DV_R
ROPE_BASE = 10000.0
GS_C, G_C, P_C = 16, 64, 64
D_C = G_C * GS_C
D_S5 = G_C * P_C
D_FF = 2816
CONV_F = 3
CHUNK = 128
EPS = 1e-6
LANES = 128
DT_PAD = LANES
HPG = H_A // G_A
VMEM_LIMIT = 56 * 1024 * 1024

_LOG_GAMMA = [math.log1p(-(2.0 ** (-5.0 - h))) for h in range(H_R)]

_SEGS = (("z", D_A), ("xbc", D_XBC), ("dt", DT_PAD), ("q", D_QK), ("k", D_QK),
         ("v", D_RV), ("g", D_RV), ("u", D_C), ("gates", 3 * D_MODEL))


def _rms_unit(x):
    return x * lax.rsqrt(jnp.mean(x * x, axis=-1, keepdims=True) + EPS)


def _silu(x):
    return x * jax.nn.sigmoid(x)


def _softplus(x):
    return jnp.maximum(x, 0.0) + jnp.log1p(jnp.exp(-jnp.abs(x)))


def _bdot(a, b):
    return jnp.dot(a.astype(BF16), b.astype(BF16), preferred_element_type=F32)


def _const_spec(shape):
    nd = len(shape)
    return pl.BlockSpec(shape, lambda *_: (0,) * nd, pipeline_mode=pl.Buffered(1))


def _params(n_grid):
    return pltpu.CompilerParams(dimension_semantics=("arbitrary",) * n_grid,
                                vmem_limit_bytes=VMEM_LIMIT)


def _inproj_body(x_ref, nw_ref, *refs):
    n = len(_SEGS)
    h = (_rms_unit(x_ref[...]) * nw_ref[...]).astype(BF16)
    for w_ref, o_ref in zip(refs[:n], refs[n:]):
        o_ref[...] = jnp.dot(h, w_ref[...], preferred_element_type=F32)


def _inproj(x2d, nw, ws, nb, seq, tm):
    nt = seq // tm
    in_specs = [pl.BlockSpec((tm, D_MODEL), lambda b, i: (b * nt + i, 0)),
                _const_spec((1, D_MODEL))]
    in_specs += [_const_spec(w.shape) for w in ws]
    out_specs, out_shape = [], []
    for name, width in _SEGS:
        if name == "u":
            out_specs.append(pl.BlockSpec((tm, width), lambda b, i: (i, b)))
            out_shape.append(jax.ShapeDtypeStruct((seq, nb * width), F32))
        else:
            out_specs.append(pl.BlockSpec((tm, width), lambda b, i: (b * nt + i, 0)))
            out_shape.append(jax.ShapeDtypeStruct((nb * seq, width), F32))
    outs = pl.pallas_call(
        _inproj_body, grid=(nb, nt), in_specs=in_specs, out_specs=out_specs,
        out_shape=out_shape, compiler_params=_params(2), name="inproj")(x2d, nw, *ws)
    res = {name: o for (name, _), o in zip(_SEGS, outs)}
    res["u"] = res["u"].reshape(seq * nb, D_C)
    return res


def _rope(x, cos_f, sin_a, sin_b):
    half = DK_R // 2
    return (x * cos_f + pltpu.roll(x, D_QK - half, 1) * sin_a
            + pltpu.roll(x, half, 1) * sin_b)


def _head_expand():
    lo = lax.broadcasted_iota(jnp.int32, (LANES, D_A), 0) * P_A
    c = lax.broadcasted_iota(jnp.int32, (LANES, D_A), 1)
    return jnp.where(c >= lo, jnp.where(c < lo + P_A, 1.0, 0.0), 0.0).astype(F32)


def _mix_body(z_ref, xbc_ref, dt_ref, q_ref, k_ref, v_ref, g_ref,
              cos_ref, sina_ref, sinb_ref, lgx_ref,
              cw_ref, cb_ref, dtb_ref, alog_ref, dx_ref, naw_ref,
              ya_ref, yb_ref, cs_ref, hs_ref, ss_ref,
              xp_s, ht_s, s_s, intra_s, qdec_s, kdect_s, *, tsteps):
    i = pl.program_id(1)
    first = jnp.logical_and(pl.program_id(0) == 0, i == 0)
    rows_i = lax.broadcasted_iota(jnp.int32, (CHUNK, CHUNK), 0)
    cols_i = lax.broadcasted_iota(jnp.int32, (CHUNK, CHUNK), 1)
    causal = rows_i >= cols_i

    @pl.when(first)
    def _():
        rel = (rows_i - cols_i).astype(F32)
        for h in range(H_R):
            intra_s[h] = jnp.where(causal, jnp.exp(jnp.maximum(rel, 0.0) * _LOG_GAMMA[h]), 0.0)
        ri = lax.broadcasted_iota(jnp.int32, (CHUNK, D_QK), 0).astype(F32)
        lgx = lgx_ref[...]
        qdec_s[...] = jnp.exp((ri + 1.0) * lgx)
        kdect_s[...] = jnp.exp((CHUNK - 1.0 - ri) * lgx).T

    @pl.when(i == 0)
    def _():
        xp_s[0:8, :] = jnp.zeros((8, D_XBC), F32)
        ht_s[...] = jnp.zeros_like(ht_s)
        s_s[...] = jnp.zeros_like(s_s)

    xp_s[8:8 + tsteps, :] = xbc_ref[...]
    tril = causal.astype(F32)
    expand = _head_expand()
    a_neg = -jnp.exp(alog_ref[...])

    for c in range(tsteps // CHUNK):
        r0 = c * CHUNK
        rows = slice(r0, r0 + CHUNK)
        acc = cb_ref[...]
        for tap in range(CONV_A):
            acc = acc + xp_s[r0 + 5 + tap:r0 + 5 + tap + CHUNK, :] * cw_ref[tap:tap + 1, :]
        xc = _silu(acc)
        xa = xc[:, :D_A]
        bm = xc[:, D_A:D_A + G_A * N_A]
        cm = xc[:, D_A + G_A * N_A:]
        dt = _softplus(dt_ref[rows, :] + dtb_ref[...])
        cum = jnp.dot(tril, dt * a_neg, precision=HIGHEST, preferred_element_type=F32)
        cum_t = cum.T
        dt_t = dt.T
        cum_last = cum[CHUNK - 1:CHUNK, :]
        expcum = jnp.exp(cum)
        w_end = dt * jnp.exp(cum_last - cum)
        scale = jnp.concatenate(
            [w_end, jnp.broadcast_to(jnp.exp(cum_last), (8, LANES))], axis=0)
        scale_x = jnp.dot(scale, expand, precision=HIGHEST, preferred_element_type=F32)
        w_x = scale_x[:CHUNK]
        dec_x = scale_x[CHUNK:CHUNK + 1]
        bm_t = bm.T
        ys = []
        for g in range(G_A):
            gl = slice(g * N_A, (g + 1) * N_A)
            hl = slice(g * HPG * P_A, (g + 1) * HPG * P_A)
            cg = cm[:, gl]
            scores = lax.dot_general(cg.astype(BF16), bm[:, gl].astype(BF16),
                                     (((1,), (1,)), ((), ())), preferred_element_type=F32)
            scores = jnp.where(causal, scores, 0.0)
            ht_g = ht_s[:, hl]
            for hh in range(HPG):
                h = g * HPG + hh
                seg = cum[:, h:h + 1] - cum_t[h:h + 1, :]
                sp = scores * jnp.exp(jnp.minimum(seg, 0.0)) * dt_t[h:h + 1, :]
                cx = cg * expcum[:, h:h + 1]
                lhs = jnp.concatenate([sp, cx], axis=1)
                rhs = jnp.concatenate([xa[:, h * P_A:(h + 1) * P_A],
                                       ht_g[:, hh * P_A:(hh + 1) * P_A]], axis=0)
                ys.append(_bdot(lhs, rhs))
            ht_s[:, hl] = dec_x[:, hl] * ht_g + _bdot(bm_t[gl, :], xa[:, hl] * w_x[:, hl])
        y = jnp.concatenate(ys, axis=1) + dx_ref[...] * xa
        y = y * _silu(z_ref[rows, :])
        ya_ref[rows, :] = _rms_unit(y) * naw_ref[...]

        cos_f, sin_a, sin_b = cos_ref[rows, :], sina_ref[rows, :], sinb_ref[rows, :]
        qr = _rope(q_ref[rows, :], cos_f, sin_a, sin_b)
        kr = _rope(k_ref[rows, :], cos_f, sin_a, sin_b) * (DK_R ** -0.5)
        k_t = kr.T
        kd_t = k_t * kdect_s[...]
        qd = qr * qdec_s[...]
        for h in range(H_R):
            kl = slice(h * DK_R, (h + 1) * DK_R)
            vl = slice(h * DV_R, (h + 1) * DV_R)
            sc = _bdot(qr[:, kl], k_t[kl, :]) * intra_s[h]
            s_h = s_s[kl, :]
            v_h = v_ref[rows, vl]
            y_h = _bdot(jnp.concatenate([sc, qd[:, kl]], axis=1),
                        jnp.concatenate([v_h, s_h], axis=0))
            s_s[kl, :] = math.exp(CHUNK * _LOG_GAMMA[h]) * s_h + _bdot(kd_t[kl, :], v_h)
            yb_ref[rows, vl] = _silu(g_ref[rows, vl]) * _rms_unit(y_h)

    xp_s[5:8, :] = xp_s[tsteps + 5:tsteps + 8, :]
    cs_ref[...] = xp_s[5:8, :]
    hs_ref[...] = ht_s[...]
    ss_ref[...] = s_s[...]


def _mix_prompt(p, rot, lw, nb, seq, tsteps):
    nt = seq // tsteps
    row = lambda w: pl.BlockSpec((tsteps, w), lambda b, i: (b * nt + i, 0))
    tab = pl.BlockSpec((tsteps, D_QK), lambda b, i: (i, 0))
    in_specs = [row(D_A), row(D_XBC), row(DT_PAD), row(D_QK), row(D_QK), row(D_RV), row(D_RV),
                tab, tab, tab, _const_spec((1, D_QK)),
                _const_spec((CONV_A, D_XBC)), _const_spec((1, D_XBC)), _const_spec((1, DT_PAD)),
                _const_spec((1, DT_PAD)), _const_spec((1, D_A)), _const_spec((1, D_A))]
    out_specs = [row(D_A), row(D_RV),
                 pl.BlockSpec((None, CONV_A - 1, D_XBC), lambda b, i: (b, 0, 0)),
                 pl.BlockSpec((None, N_A, D_A), lambda b, i: (b, 0, 0)),
                 pl.BlockSpec((None, D_QK, DV_R), lambda b, i: (b, 0, 0))]
    out_shape = [jax.ShapeDtypeStruct((nb * seq, D_A), F32),
                 jax.ShapeDtypeStruct((nb * seq, D_RV), F32),
                 jax.ShapeDtypeStruct((nb, CONV_A - 1, D_XBC), F32),
                 jax.ShapeDtypeStruct((nb, N_A, D_A), F32),
                 jax.ShapeDtypeStruct((nb, D_QK, DV_R), F32)]
    scratch = [pltpu.VMEM((8 + tsteps, D_XBC), F32), pltpu.VMEM((N_A, D_A), F32),
               pltpu.VMEM((D_QK, DV_R), F32), pltpu.VMEM((H_R, CHUNK, CHUNK), F32),
               pltpu.VMEM((CHUNK, D_QK), F32), pltpu.VMEM((D_QK, CHUNK), F32)]
    ya, yb, cs, hs, ss = pl.pallas_call(
        functools.partial(_mix_body, tsteps=tsteps), grid=(nb, nt), in_specs=in_specs,
        out_specs=out_specs, out_shape=out_shape, scratch_shapes=scratch,
        compiler_params=_params(2), name="mix_prompt")(
            p["z"], p["xbc"], p["dt"], p["q"], p["k"], p["v"], p["g"],
            rot["cos"], rot["sin_a"], rot["sin_b"], rot["lgx"],
            lw["conv_a_w"], lw["conv_a_b"], lw["dt_bias"], lw["a_log"], lw["d_x"], lw["norm_a_w"])
    hs = jnp.transpose(hs.reshape(nb, N_A, H_A, P_A), (0, 2, 3, 1))
    return ya, yb, cs, hs, ss.reshape(nb, H_R, DK_R, DV_R)


def _s5_disc_body(are_ref, aim_ref, ldt_ref, bre_ref, bim_ref,
                  abre_ref, abim_ref, bbre_ref, bbim_ref):
    ar, ai = are_ref[...], aim_ref[...]
    dt = jnp.exp(ldt_ref[...])
    mag = jnp.exp(ar * dt)
    ab_re = mag * jnp.cos(ai * dt)
    ab_im = mag * jnp.sin(ai * dt)
    den = ar * ar + ai * ai
    num_re = ab_re - 1.0
    coef_re = (num_re * ar + ab_im * ai) / den
    coef_im = (ab_im * ar - num_re * ai) / den
    abre_ref[...] = ab_re
    abim_ref[...] = ab_im
    for c in range(GS_C):
        cl = slice(c * P_C, (c + 1) * P_C)
        br, bi = bre_ref[:, cl], bim_ref[:, cl]
        bbre_ref[:, cl] = coef_re * br - coef_im * bi
        bbim_ref[:, cl] = coef_re * bi + coef_im * br


def _s5_discretise(a_re, a_im, log_dt, b_re, b_im):
    gp = jax.ShapeDtypeStruct((G_C, P_C), F32)
    gcp = jax.ShapeDtypeStruct((G_C, GS_C * P_C), F32)
    b_t = lambda b: jnp.transpose(b, (0, 2, 1)).reshape(G_C, GS_C * P_C)
    ab_re, ab_im, bb_re, bb_im = pl.pallas_call(
        _s5_disc_body, out_shape=[gp, gp, gcp, gcp], name="s5_disc")(
            a_re, a_im, log_dt.reshape(G_C, 1), b_t(b_re), b_t(b_im))
    return ab_re, ab_im, bb_re.reshape(G_C, GS_C, P_C), bb_im.reshape(G_C, GS_C, P_C)


_S5_GB = LANES // GS_C
_S5_NBLK = G_C // _S5_GB
_S5_SB = _S5_GB * P_C


def _block_diag(m):
    g, r, c = m.shape
    m = m.reshape(_S5_NBLK, _S5_GB, r, c)
    eye = jnp.eye(_S5_GB, dtype=m.dtype)
    return jnp.einsum("jgrc,gk->jgrkc", m, eye).reshape(_S5_NBLK, _S5_GB * r, _S5_GB * c)


def _s5_body(u_ref, h0re_ref, h0im_ref, are_ref, aim_ref, bblk_ref, cre_ref, cim_ref,
             d_ref, wglu_ref, bglu_ref, yc_ref, hre_ref, him_ref,
             xre_s, xim_s, sre_s, sim_s, *, nb, tt):
    i = pl.program_id(0)

    @pl.when(i == 0)
    def _():
        sre_s[...] = h0re_ref[...]
        sim_s[...] = h0im_ref[...]

    u = u_ref[...]
    ub = u.astype(BF16)
    for j in range(_S5_NBLK):
        bu = jnp.dot(ub[:, j * LANES:(j + 1) * LANES], bblk_ref[j], preferred_element_type=F32)
        xre_s[:, j * _S5_SB:(j + 1) * _S5_SB] = bu[:, :_S5_SB]
        xim_s[:, j * _S5_SB:(j + 1) * _S5_SB] = bu[:, _S5_SB:]

    cw = 512
    for cb in range(D_S5 // cw):
        cl = slice(cb * cw, (cb + 1) * cw)
        ar = jnp.broadcast_to(are_ref[:, cl], (nb, cw))
        ai = jnp.broadcast_to(aim_ref[:, cl], (nb, cw))

        def step(t, carry):
            xr, xi = carry
            r = pl.ds(pl.multiple_of(t * nb, nb), nb)
            nr = ar * xr - ai * xi + xre_s[r, cl]
            ni = ar * xi + ai * xr + xim_s[r, cl]
            xre_s[r, cl] = nr
            xim_s[r, cl] = ni
            return nr, ni

        xr, xi = lax.fori_loop(0, tt, step, (sre_s[:, cl], sim_s[:, cl]), unroll=min(tt, 8))
        sre_s[:, cl] = xr
        sim_s[:, cl] = xi

    ys = []
    for j in range(_S5_NBLK):
        sl = slice(j * _S5_SB, (j + 1) * _S5_SB)
        ys.append(_bdot(xre_s[:, sl], cre_ref[j]) - _bdot(xim_s[:, sl], cim_ref[j]))
    y = jnp.concatenate(ys, axis=1) + d_ref[...] * u
    y = jax.nn.gelu(y)
    yc_ref[...] = y * jax.nn.sigmoid(_bdot(y, wglu_ref[...]) + bglu_ref[...])
    hre_ref[...] = sre_s[...]
    him_ref[...] = sim_s[...]


def _s5(u_tm, h0_re, h0_im, lw, nb, seq, tt):
    rows = tt * nb
    in_specs = [pl.BlockSpec((rows, D_C), lambda i: (i, 0)),
                _const_spec((nb, D_S5)), _const_spec((nb, D_S5)),
                _const_spec((1, D_S5)), _const_spec((1, D_S5)),
                _const_spec((_S5_NBLK, LANES, 2 * _S5_SB)),
                _const_spec((_S5_NBLK, _S5_SB, LANES)), _const_spec((_S5_NBLK, _S5_SB, LANES)),
                _const_spec((1, D_C)), _const_spec((D_C, D_C)), _const_spec((1, D_C))]
    st = pl.BlockSpec((nb, D_S5), lambda i: (0, 0))
    return pl.pallas_call(
        functools.partial(_s5_body, nb=nb, tt=tt), grid=(seq // tt,), in_specs=in_specs,
        out_specs=[pl.BlockSpec((rows, D_C), lambda i: (i, 0)), st, st],
        out_shape=[jax.ShapeDtypeStruct((seq * nb, D_C), F32),
                   jax.ShapeDtypeStruct((nb, D_S5), F32), jax.ShapeDtypeStruct((nb, D_S5), F32)],
        scratch_shapes=[pltpu.VMEM((rows, D_S5), F32), pltpu.VMEM((rows, D_S5), F32),
                        pltpu.VMEM((nb, D_S5), F32), pltpu.VMEM((nb, D_S5), F32)],
        compiler_params=_params(1), name="s5")(
            u_tm, h0_re, h0_im, lw["s5_ab_re"], lw["s5_ab_im"], lw["s5_bblk"],
            lw["s5_cre"], lw["s5_cim"], lw["s5_d"], lw["w_glu"], lw["b_glu"])


def _merge_body(x_ref, ya_ref, yb_ref, yc_ref, gt_ref, wa_ref, wb_ref, wc_ref, wo_ref, o_ref):
    gt = jax.nn.sigmoid(gt_ref[...])
    merged = (gt[:, :D_MODEL] * _bdot(ya_ref[...], wa_ref[...])
              + gt[:, D_MODEL:2 * D_MODEL] * _bdot(yb_ref[...], wb_ref[...])
              + gt[:, 2 * D_MODEL:] * _bdot(yc_ref[...], wc_ref[...]))
    o_ref[...] = x_ref[...] + _bdot(merged, wo_ref[...])


def _merge(x2d, ya, yb, yc_tm, gates, lw, nb, seq, tm):
    nt = seq // tm
    row = lambda w: pl.BlockSpec((tm, w), lambda b, i: (b * nt + i, 0))
    wspec = _const_spec((D_MODEL, D_MODEL))
    return pl.pallas_call(
        _merge_body, grid=(nb, nt),
        in_specs=[row(D_MODEL), row(D_A), row(D_RV),
                  pl.BlockSpec((tm, D_C), lambda b, i: (i, b)),
                  row(3 * D_MODEL), wspec, wspec, wspec, wspec],
        out_specs=row(D_MODEL), out_shape=jax.ShapeDtypeStruct((nb * seq, D_MODEL), F32),
        compiler_params=_params(2), name="merge")(
            x2d, ya, yb, yc_tm.reshape(seq, nb * D_C), gates,
            lw["w_br_a"], lw["w_br_b"], lw["w_br_c"], lw["w_out"])


_FFN_CW = 256


def _ffn_tail(x, up_s, act_s, cw_ref, cb_ref, wdn_ref, tm):
    for c in range(0, D_FF, _FFN_CW):
        def conv(c0):
            cl = slice(c0, c0 + _FFN_CW)
            y = cb_ref[:, cl]
            for tap in range(CONV_F):
                y = y + up_s[6 + tap:6 + tap + tm, cl] * cw_ref[tap:tap + 1, cl]
            return y
        act_s[:, c:c + _FFN_CW] = (_silu(conv(c)) * conv(D_FF + c)).astype(BF16)
    return x + jnp.dot(act_s[...], wdn_ref[...], preferred_element_type=F32)


def _ffn_prompt_body(x_ref, nw_ref, wup_ref, cw_ref, cb_ref, wdn_ref, nf_ref,
                     o_ref, cs_ref, up_s, act_s, *, tm, final):
    @pl.when(pl.program_id(1) == 0)
    def _():
        up_s[0:8, :] = jnp.zeros((8, 2 * D_FF), F32)

    x = x_ref[...]
    hf = (_rms_unit(x) * nw_ref[...]).astype(BF16)
    up_s[8:8 + tm, :] = jnp.dot(hf, wup_ref[...], preferred_element_type=F32)
    out = _ffn_tail(x, up_s, act_s, cw_ref, cb_ref, wdn_ref, tm)
    o_ref[...] = _rms_unit(out) * nf_ref[...] if final else out
    tail = up_s[tm + 6:tm + 8, :]
    cs_ref[...] = tail
    up_s[6:8, :] = tail


def _ffn_sample_body(x_ref, st_ref, nw_ref, wup_ref, cw_ref, cb_ref, wdn_ref, nf_ref,
                     o_ref, cs_ref, act_s, *, final):
    x = x_ref[...]
    hf = (_rms_unit(x) * nw_ref[...]).astype(BF16)
    up = jnp.dot(hf, wup_ref[...], preferred_element_type=F32)
    prev2, prev1 = st_ref[:, :2 * D_FF], st_ref[:, 2 * D_FF:]
    cs_ref[:, :2 * D_FF] = prev1
    cs_ref[:, 2 * D_FF:] = up
    for c in range(0, D_FF, _FFN_CW):
        def conv(c0):
            cl = slice(c0, c0 + _FFN_CW)
            return (cb_ref[:, cl] + prev2[:, cl] * cw_ref[0:1, cl] + prev1[:, cl] * cw_ref[1:2, cl]
                    + up[:, cl] * cw_ref[2:3, cl])
        act_s[:, c:c + _FFN_CW] = (_silu(conv(c)) * conv(D_FF + c)).astype(BF16)
    out = x + jnp.dot(act_s[...], wdn_ref[...], preferred_element_type=F32)
    o_ref[...] = _rms_unit(out) * nf_ref[...] if final else out


def _ffn(x2d, state, lw, nf, nb, seq, tm, final):
    wspecs = [_const_spec((1, D_MODEL)), _const_spec((D_MODEL, 2 * D_FF)),
              _const_spec((CONV_F, 2 * D_FF)), _const_spec((1, 2 * D_FF)),
              _const_spec((D_FF, D_MODEL)), _const_spec((1, D_MODEL))]
    wargs = (lw["norm_ffn_w"], lw["w_up"], lw["conv_f_w"], lw["conv_f_b"], lw["w_down"], nf)
    act = pltpu.VMEM((tm, D_FF), BF16)
    if state is None:
        nt = seq // tm
        row = pl.BlockSpec((tm, D_MODEL), lambda b, i: (b * nt + i, 0))
        return pl.pallas_call(
            functools.partial(_ffn_prompt_body, tm=tm, final=final), grid=(nb, nt),
            in_specs=[row] + wspecs,
            out_specs=[row, pl.BlockSpec((None, CONV_F - 1, 2 * D_FF), lambda b, i: (b, 0, 0))],
            out_shape=[jax.ShapeDtypeStruct((nb * seq, D_MODEL), F32),
                       jax.ShapeDtypeStruct((nb, CONV_F - 1, 2 * D_FF), F32)],
            scratch_shapes=[pltpu.VMEM((8 + tm, 2 * D_FF), F32), act],
            compiler_params=_params(2), name="ffn_prompt")(x2d, *wargs)
    sw = (CONV_F - 1) * 2 * D_FF
    return pl.pallas_call(
        functools.partial(_ffn_sample_body, final=final), grid=(nb // tm,),
        in_specs=[pl.BlockSpec((tm, D_MODEL), lambda i: (i, 0)),
                  pl.BlockSpec((tm, sw), lambda i: (i, 0))] + wspecs,
        out_specs=[pl.BlockSpec((tm, D_MODEL), lambda i: (i, 0)),
                   pl.BlockSpec((tm, sw), lambda i: (i, 0))],
        out_shape=[jax.ShapeDtypeStruct((nb, D_MODEL), F32), jax.ShapeDtypeStruct((nb, sw), F32)],
        scratch_shapes=[act], compiler_params=_params(1), name="ffn_sample")(x2d, state, *wargs)


def _mix_sample_body(z_ref, xbc_ref, dt_ref, q_ref, k_ref, v_ref, g_ref, cst_ref,
                     cos_ref, sina_ref, sinb_ref,
                     cw_ref, cb_ref, dtb_ref, alog_ref, dx_ref, naw_ref,
                     hin_ref, sin_ref,
                     ya_ref, yb_ref, cso_ref, hout_ref, sout_ref,
                     xa_s, xdt_t_s, dec_s, bm_s, cm_t_s, qr_s, k_t_s, yt_s, yr_s, *, nb):
    b = pl.program_id(0)

    @pl.when(b == 0)
    def _():
        acc = cb_ref[...]
        for tap in range(CONV_A - 1):
            acc = acc + cst_ref[:, tap * D_XBC:(tap + 1) * D_XBC] * cw_ref[tap:tap + 1, :]
        acc = acc + xbc_ref[...] * cw_ref[CONV_A - 1:CONV_A, :]
        cso_ref[:, :(CONV_A - 2) * D_XBC] = cst_ref[:, D_XBC:]
        cso_ref[:, (CONV_A - 2) * D_XBC:] = xbc_ref[...]
        xc = _silu(acc)
        xa = xc[:, :D_A]
        xa_s[...] = xa
        bm_s[...] = xc[:, D_A:D_A + G_A * N_A]
        cm_t_s[...] = xc[:, D_A + G_A * N_A:].T
        dt = _softplus(dt_ref[...] + dtb_ref[...])
        dec_s[...] = jnp.exp(dt * -jnp.exp(alog_ref[...]))
        dt_x = jnp.dot(dt, _head_expand(), precision=HIGHEST, preferred_element_type=F32)
        xdt_t_s[...] = (xa * dt_x).T
        cos_f, sin_a, sin_b = cos_ref[...], sina_ref[...], sinb_ref[...]
        qr_s[...] = _rope(q_ref[...], cos_f, sin_a, sin_b)
        k_t_s[...] = (_rope(k_ref[...], cos_f, sin_a, sin_b) * (DK_R ** -0.5)).T
        yt_s[...] = jnp.zeros_like(yt_s)
        yr_s[...] = jnp.zeros_like(yr_s)

    row_is_b = lax.broadcasted_iota(jnp.int32, (nb, 1), 0) == b
    lane_is_b = lax.broadcasted_iota(jnp.int32, (1, nb), 1) == b
    dec_row = dec_s[pl.ds(b, 1), :]

    for g in range(G_A):
        gl = slice(g * N_A, (g + 1) * N_A)
        hl = slice(g * HPG * P_A, (g + 1) * HPG * P_A)
        b_sel = jnp.where(row_is_b, bm_s[:, gl], 0.0)
        outer = _bdot(xdt_t_s[hl, :], b_sel)
        for hh in range(HPG):
            h = g * HPG + hh
            pr = slice(h * P_A, (h + 1) * P_A)
            hout_ref[pr, :] = (hin_ref[pr, :] * dec_row[:, h:h + 1]
                               + outer[hh * P_A:(hh + 1) * P_A, :])
        c_sel = jnp.where(lane_is_b, cm_t_s[gl, :], 0.0)
        yt_s[hl, :] += _bdot(hout_ref[hl, :], c_sel)

    for h in range(H_R):
        kl = slice(h * DK_R, (h + 1) * DK_R)
        vl = slice(h * DV_R, (h + 1) * DV_R)
        v_sel = jnp.where(row_is_b, v_ref[:, vl], 0.0)
        s_new = math.exp(_LOG_GAMMA[h]) * sin_ref[kl, :] + _bdot(k_t_s[kl, :], v_sel)
        sout_ref[kl, :] = s_new
        q_sel = jnp.where(row_is_b, qr_s[:, kl], 0.0)
        yr_s[h] += _bdot(q_sel, s_new)

    @pl.when(b == nb - 1)
    def _():
        xa = xa_s[...]
        y = yt_s[...].T + dx_ref[...] * xa
        y = y * _silu(z_ref[...])
        ya_ref[...] = _rms_unit(y) * naw_ref[...]
        for h in range(H_R):
            vl = slice(h * DV_R, (h + 1) * DV_R)
            yb_ref[:, vl] = _silu(g_ref[:, vl]) * _rms_unit(yr_s[h])


def _mix_sample(p, conv_st, h_ssd, s_ret, rot, lw, nb):
    full = lambda w: _const_spec((nb, w))
    cw3 = (CONV_A - 1) * D_XBC
    in_specs = [full(D_A), full(D_XBC), full(DT_PAD), full(D_QK), full(D_QK), full(D_RV), full(D_RV),
                full(cw3), _const_spec((1, D_QK)), _const_spec((1, D_QK)), _const_spec((1, D_QK)),
                _const_spec((CONV_A, D_XBC)), _const_spec((1, D_XBC)), _const_spec((1, DT_PAD)),
                _const_spec((1, DT_PAD)), _const_spec((1, D_A)), _const_spec((1, D_A)),
                pl.BlockSpec((None, D_A, N_A), lambda b: (b, 0, 0)),
                pl.BlockSpec((None, D_QK, DV_R), lambda b: (b, 0, 0))]
    keep = lambda w: pl.BlockSpec((nb, w), lambda b: (0, 0))
    out_specs = [keep(D_A), keep(D_RV), keep(cw3),
                 pl.BlockSpec((None, D_A, N_A), lambda b: (b, 0, 0)),
                 pl.BlockSpec((None, D_QK, DV_R), lambda b: (b, 0, 0))]
    out_shape = [jax.ShapeDtypeStruct((nb, D_A), F32), jax.ShapeDtypeStruct((nb, D_RV), F32),
                 jax.ShapeDtypeStruct((nb, cw3), F32),
                 jax.ShapeDtypeStruct((nb, D_A, N_A), F32),
                 jax.ShapeDtypeStruct((nb, D_QK, DV_R), F32)]
    scratch = [pltpu.VMEM((nb, D_A), F32), pltpu.VMEM((D_A, nb), F32), pltpu.VMEM((nb, LANES), F32),
               pltpu.VMEM((nb, G_A * N_A), F32), pltpu.VMEM((G_A * N_A, nb), F32),
               pltpu.VMEM((nb, D_QK), F32), pltpu.VMEM((D_QK, nb), F32),
               pltpu.VMEM((D_A, nb), F32), pltpu.VMEM((H_R, nb, DV_R), F32)]
    ya, yb, cs, hs, ss = pl.pallas_call(
        functools.partial(_mix_sample_body, nb=nb), grid=(nb,), in_specs=in_specs,
        out_specs=out_specs, out_shape=out_shape, scratch_shapes=scratch,
        compiler_params=_params(1), name="mix_sample")(
            p["z"], p["xbc"], p["dt"], p["q"], p["k"], p["v"], p["g"], conv_st.reshape(nb, cw3),
            rot["cos"], rot["sin_a"], rot["sin_b"],
            lw["conv_a_w"], lw["conv_a_b"], lw["dt_bias"], lw["a_log"], lw["d_x"], lw["norm_a_w"],
            h_ssd.reshape(nb, D_A, N_A), s_ret.reshape(nb, D_QK, DV_R))
    return (ya, yb, cs.reshape(nb, CONV_A - 1, D_XBC), hs.reshape(nb, H_A, P_A, N_A),
            ss.reshape(nb, H_R, DK_R, DV_R))


def _rotary_tables(start, length):
    half = DK_R // 2
    pos = start + jnp.arange(length, dtype=F32)
    freqs = ROPE_BASE ** (-jnp.arange(half, dtype=F32) / half)
    ang = pos[:, None] * freqs[None, :]
    cos, sin, zero = jnp.cos(ang), jnp.sin(ang), jnp.zeros_like(ang)
    tile = lambda a, b: jnp.tile(jnp.concatenate([a, b], axis=-1), (1, H_R))
    lg = jnp.log1p(-jnp.exp2(-5.0 - jnp.arange(H_R, dtype=F32)))
    return {"cos": tile(cos, cos), "sin_a": tile(-sin, zero), "sin_b": tile(zero, sin),
            "lgx": jnp.repeat(lg, DK_R)[None, :]}


def _layer_weights(l, w):
    w_in = w["w_in"][l]
    offs = np.cumsum([0, D_A, D_XBC, H_A, D_QK, D_QK, D_RV, D_RV, D_C, 3 * D_MODEL])
    cols = [w_in[:, offs[i]:offs[i + 1]] for i in range(9)]
    cols[2] = jnp.pad(cols[2], ((0, 0), (0, DT_PAD - H_A)))
    pad_row = lambda a: jnp.pad(a, (0, DT_PAD - H_A))[None, :]
    ab_re, ab_im, bb_re, bb_im = _s5_discretise(
        w["s5_a_re"][l], w["s5_a_im"][l], w["s5_log_dt"][l], w["s5_b_re"][l], w["s5_b_im"][l])
    bblk = jnp.concatenate([_block_diag(bb_re), _block_diag(bb_im)], axis=-1)
    c_t = lambda c: _block_diag(jnp.transpose(c, (0, 2, 1)))
    return {
        "norm_mix_w": w["norm_mix_w"][l][None, :],
        "w_in": [c.astype(BF16) for c in cols],
        "conv_a_w": w["conv_a_w"][l], "conv_a_b": w["conv_a_b"][l][None, :],
        "dt_bias": pad_row(w["dt_bias"][l]), "a_log": pad_row(w["a_log"][l]),
        "d_x": jnp.repeat(w["d_a"][l], P_A)[None, :], "norm_a_w": w["norm_a_w"][l][None, :],
        "s5_ab_re": ab_re.reshape(1, D_S5), "s5_ab_im": ab_im.reshape(1, D_S5),
        "s5_bblk": bblk.astype(BF16),
        "s5_cre": c_t(w["s5_c_re"][l]).astype(BF16), "s5_cim": c_t(w["s5_c_im"][l]).astype(BF16),
        "s5_d": w["s5_d"][l][None, :], "w_glu": w["w_glu"][l].astype(BF16),
        "b_glu": w["b_glu"][l][None, :],
        "w_br_a": w["w_br_a"][l].astype(BF16), "w_br_b": w["w_br_b"][l].astype(BF16),
        "w_br_c": w["w_br_c"][l].astype(BF16), "w_out": w["w_out"][l].astype(BF16),
        "norm_ffn_w": w["norm_ffn_w"][l][None, :], "w_up": w["w_up"][l].astype(BF16),
        "conv_f_w": w["conv_f_w"][l], "conv_f_b": w["conv_f_b"][l][None, :],
        "w_down": w["w_down"][l].astype(BF16),
    }


def _prompt_trunk(x, lws, nf, tm_proj, tsteps, tt, tm_merge, tm_ffn):
    nb, seq, _ = x.shape
    rot = _rotary_tables(0.0, seq)
    x2d = x.reshape(nb * seq, D_MODEL)
    zeros = jnp.zeros((nb, D_S5), F32)
    states = []
    for l, lw in enumerate(lws):
        p = _inproj(x2d, lw["norm_mix_w"], lw["w_in"], nb, seq, tm_proj)
        ya, yb, cs, hs, ss = _mix_prompt(p, rot, lw, nb, seq, tsteps)
        yc, h_re, h_im = _s5(p["u"], zeros, zeros, lw, nb, seq, tt)
        x2d = _merge(x2d, ya, yb, yc, p["gates"], lw, nb, seq, tm_merge)
        x2d, fs = _ffn(x2d, None, lw, nf, nb, seq, tm_ffn, final=(l == len(lws) - 1))
        states.append((cs, hs, ss, h_re.reshape(nb, G_C, P_C), h_im.reshape(nb, G_C, P_C), fs))
    return x2d.reshape(nb, seq, D_MODEL), [jnp.stack(s) for s in zip(*states)]


def _sample_trunk(x, st, lws, nf):
    nb = x.shape[0]
    rot = _rotary_tables(float(PAST_LEN), 1)
    x2d = x.reshape(nb, D_MODEL)
    st_conv, st_ssd, st_ret, st_re, st_im, st_ffn = st
    states = []
    for l, lw in enumerate(lws):
        p = _inproj(x2d, lw["norm_mix_w"], lw["w_in"], 1, nb, nb)
        ya, yb, cs, hs, ss = _mix_sample(p, st_conv[l], st_ssd[l], st_ret[l], rot, lw, nb)
        yc, h_re, h_im = _s5(p["u"], st_re[l].reshape(nb, D_S5), st_im[l].reshape(nb, D_S5),
                             lw, nb, 1, 1)
        x2d = _merge(x2d, ya, yb, yc, p["gates"], lw, 1, nb, nb)
        x2d, fs = _ffn(x2d, st_ffn[l].reshape(nb, -1), lw, nf, nb, 1, nb,
                       final=(l == len(lws) - 1))
        states.append((cs, hs, ss, h_re.reshape(nb, G_C, P_C), h_im.reshape(nb, G_C, P_C),
                       fs.reshape(nb, CONV_F - 1, 2 * D_FF)))
    return x2d.reshape(nb, 1, D_MODEL), [jnp.stack(s) for s in zip(*states)]


def kernel(x_prompt, x_sample, state_ssd_conv, state_ssd, state_ret, state_s5_re, state_s5_im,
           state_ffn_conv, norm_mix_w, w_in, conv_a_w, conv_a_b, dt_bias, a_log, d_a, norm_a_w,
           s5_a_re, s5_a_im, s5_log_dt, s5_b_re, s5_b_im, s5_c_re, s5_c_im, s5_d, w_glu, b_glu,
           w_br_a, w_br_b, w_br_c, w_out, norm_ffn_w, w_up, conv_f_w, conv_f_b, w_down, norm_f_w):
    w = dict(norm_mix_w=norm_mix_w, w_in=w_in, conv_a_w=conv_a_w, conv_a_b=conv_a_b,
             dt_bias=dt_bias, a_log=a_log, d_a=d_a, norm_a_w=norm_a_w, s5_a_re=s5_a_re,
             s5_a_im=s5_a_im, s5_log_dt=s5_log_dt, s5_b_re=s5_b_re, s5_b_im=s5_b_im,
             s5_c_re=s5_c_re, s5_c_im=s5_c_im, s5_d=s5_d, w_glu=w_glu, b_glu=b_glu,
             w_br_a=w_br_a, w_br_b=w_br_b, w_br_c=w_br_c, w_out=w_out, norm_ffn_w=norm_ffn_w,
             w_up=w_up, conv_f_w=conv_f_w, conv_f_b=conv_f_b, w_down=w_down)
    lws = [_layer_weights(l, w) for l in range(DEPTH)]
    nf = norm_f_w[None, :]
    seq = x_prompt.shape[1]
    y_p, p_st = _prompt_trunk(x_prompt, lws, nf, tm_proj=min(256, seq), tsteps=min(256, seq),
                              tt=min(32, seq), tm_merge=min(512, seq), tm_ffn=min(256, seq))
    y_s, s_st = _sample_trunk(
        x_sample, (state_ssd_conv, state_ssd, state_ret, state_s5_re, state_s5_im, state_ffn_conv),
        lws, nf)
    return (y_p, y_s, *p_st, *s_st)
```

```python
import functools
import math

import jax
import jax.numpy as jnp
import numpy as np
from jax import lax
from jax.experimental import pallas as pl
from jax.experimental.pallas import tpu as pltpu

F32 = jnp.float32
BF16 = jnp.bfloat16

D_MODEL = 1024
DEPTH = 2
PAST_LEN = 16384
H_A, P_A, N_A, G_A, CONV_A = 16, 64, 64, 2, 4
D_A = H_A * P_A
D_XBC = D_A + 2 * G_A * N_A
H_R, DK_R, DV_R = 8, 64, 128
D_QK = H_R * DK_R
D_RV = H_R * DV_R
ROPE_BASE = 10000.0
GS_C, G_C, P_C = 16, 64, 64
D_C = G_C * GS_C
D_S5 = G_C * P_C
D_FF = 2816
CONV_F = 3
CHUNK = 128
EPS = 1e-6
LANES = 128
DT_PAD = LANES
HPG = H_A // G_A
VMEM_LIMIT = 56 * 1024 * 1024

_LOG_GAMMA = [math.log1p(-(2.0 ** (-5.0 - h))) for h in range(H_R)]

_SEGS = (("z", D_A), ("xbc", D_XBC), ("dt", DT_PAD), ("q", D_QK), ("k", D_QK),
         ("v", D_RV), ("g", D_RV), ("u", D_C), ("gates", 3 * D_MODEL))


def _rms_unit(x):
    return x * lax.rsqrt(jnp.mean(x * x, axis=-1, keepdims=True) + EPS)


def _silu(x):
    return x * jax.nn.sigmoid(x)


def _softplus(x):
    return jnp.maximum(x, 0.0) + jnp.log1p(jnp.exp(-jnp.abs(x)))


def _bdot(a, b):
    return jnp.dot(a.astype(BF16), b.astype(BF16), preferred_element_type=F32)


def _split3(x):
    hi = x.astype(BF16)
    r1 = x - hi.astype(F32)
    mid = r1.astype(BF16)
    return hi, mid, (r1 - mid.astype(F32)).astype(BF16)


def _dot01(a, b, f32_side):
    if f32_side == "lhs":
        return sum(jnp.dot(p, b, preferred_element_type=F32) for p in _split3(a))
    return sum(jnp.dot(a, p, preferred_element_type=F32) for p in _split3(b))


def _const_spec(shape):
    nd = len(shape)
    return pl.BlockSpec(shape, lambda *_: (0,) * nd, pipeline_mode=pl.Buffered(1))


def _params(n_grid):
    return pltpu.CompilerParams(dimension_semantics=("arbitrary",) * n_grid,
                                vmem_limit_bytes=VMEM_LIMIT)


def _inproj_body(x_ref, nw_ref, *refs):
    n = len(_SEGS)
    h = (_rms_unit(x_ref[...]) * nw_ref[...]).astype(BF16)
    for w_ref, o_ref in zip(refs[:n], refs[n:]):
        o_ref[...] = jnp.dot(h, w_ref[...], preferred_element_type=F32)


def _inproj(x2d, nw, ws, nb, seq, tm):
    nt = seq // tm
    in_specs = [pl.BlockSpec((tm, D_MODEL), lambda b, i: (b * nt + i, 0)),
                _const_spec((1, D_MODEL))]
    in_specs += [_const_spec(w.shape) for w in ws]
    out_specs, out_shape = [], []
    for _, width in _SEGS:
        out_specs.append(pl.BlockSpec((tm, width), lambda b, i: (b * nt + i, 0)))
        out_shape.append(jax.ShapeDtypeStruct((nb * seq, width), F32))
    outs = pl.pallas_call(
        _inproj_body, grid=(nb, nt), in_specs=in_specs, out_specs=out_specs,
        out_shape=out_shape, compiler_params=_params(2), name="inproj")(x2d, nw, *ws)
    return {name: o for (name, _), o in zip(_SEGS, outs)}


def _rope(x, cos_f, sin_a, sin_b):
    half = DK_R // 2
    return (x * cos_f + pltpu.roll(x, D_QK - half, 1) * sin_a
            + pltpu.roll(x, half, 1) * sin_b)


def _head_expand():
    lo = lax.broadcasted_iota(jnp.int32, (LANES, D_A), 0) * P_A
    c = lax.broadcasted_iota(jnp.int32, (LANES, D_A), 1)
    return jnp.where(c >= lo, jnp.where(c < lo + P_A, 1.0, 0.0), 0.0).astype(BF16)


def _mix_body(*refs, tsteps, nt, n_prev):
    (z_ref, xbc_ref, dt_ref, q_ref, k_ref, v_ref, g_ref, cos_ref, sina_ref, sinb_ref, lgx_ref,
     cw_ref, cb_ref, dtb_ref, alog_ref, dx_ref, naw_ref) = refs[:17]
    (ya_ref, yb_ref, cs_ref, hs_ref, ss_ref,
     xp_s, ht_s, s_s, intra_s, qdec_s, kdect_s) = refs[17 + n_prev:]
    i = pl.program_id(1)
    first = jnp.logical_and(pl.program_id(0) == 0, i == 0)
    rows_i = lax.broadcasted_iota(jnp.int32, (CHUNK, CHUNK), 0)
    cols_i = lax.broadcasted_iota(jnp.int32, (CHUNK, CHUNK), 1)
    causal = rows_i >= cols_i

    @pl.when(first)
    def _():
        rel = (rows_i - cols_i).astype(F32)
        for h in range(H_R):
            intra_s[h] = jnp.where(causal, jnp.exp(jnp.maximum(rel, 0.0) * _LOG_GAMMA[h]), 0.0)
        ri = lax.broadcasted_iota(jnp.int32, (CHUNK, D_QK), 0).astype(F32)
        lgx = lgx_ref[...]
        qdec_s[...] = jnp.exp((ri + 1.0) * lgx)
        kdect_s[...] = jnp.exp((CHUNK - 1.0 - ri) * lgx).T

    @pl.when(i == 0)
    def _():
        xp_s[0:8, :] = jnp.zeros((8, D_XBC), F32)
        ht_s[...] = jnp.zeros_like(ht_s)
        s_s[...] = jnp.zeros_like(s_s)

    xp_s[8:8 + tsteps, :] = xbc_ref[...]
    tril = jnp.where(causal, 1.0, 0.0).astype(BF16)
    expand = _head_expand()
    a_neg = -jnp.exp(alog_ref[...])

    for c in range(tsteps // CHUNK):
        r0 = c * CHUNK
        rows = slice(r0, r0 + CHUNK)
        acc = cb_ref[...]
        for tap in range(CONV_A):
            acc = acc + xp_s[r0 + 5 + tap:r0 + 5 + tap + CHUNK, :] * cw_ref[tap:tap + 1, :]
        xc = _silu(acc)
        xa = xc[:, :D_A]
        bm = xc[:, D_A:D_A + G_A * N_A]
        cm = xc[:, D_A + G_A * N_A:]
        dt = _softplus(dt_ref[rows, :] + dtb_ref[...])
        cum = _dot01(tril, dt * a_neg, "rhs")
        cum_t = cum.T
        dt_t = dt.T
        cum_last = cum[CHUNK - 1:CHUNK, :]
        expcum = jnp.exp(cum)
        w_end = dt * jnp.exp(cum_last - cum)
        scale = jnp.concatenate(
            [w_end, jnp.broadcast_to(jnp.exp(cum_last), (8, LANES))], axis=0)
        scale_x = _dot01(scale, expand, "lhs")
        w_x = scale_x[:CHUNK]
        dec_x = scale_x[CHUNK:CHUNK + 1]
        bm_t = bm.T
        ys = []
        for g in range(G_A):
            gl = slice(g * N_A, (g + 1) * N_A)
            hl = slice(g * HPG * P_A, (g + 1) * HPG * P_A)
            cg = cm[:, gl]
            scores = lax.dot_general(cg.astype(BF16), bm[:, gl].astype(BF16),
                                     (((1,), (1,)), ((), ())), preferred_element_type=F32)
            scores = jnp.where(causal, scores, 0.0)
            ht_g = ht_s[:, hl]
            for hh in range(HPG):
                h = g * HPG + hh
                seg = cum[:, h:h + 1] - cum_t[h:h + 1, :]
                sp = scores * jnp.exp(jnp.minimum(seg, 0.0)) * dt_t[h:h + 1, :]
                cx = cg * expcum[:, h:h + 1]
                lhs = jnp.concatenate([sp, cx], axis=1)
                rhs = jnp.concatenate([xa[:, h * P_A:(h + 1) * P_A],
                                       ht_g[:, hh * P_A:(hh + 1) * P_A]], axis=0)
                ys.append(_bdot(lhs, rhs))
            ht_s[:, hl] = dec_x[:, hl] * ht_g + _bdot(bm_t[gl, :], xa[:, hl] * w_x[:, hl])
        y = jnp.concatenate(ys, axis=1) + dx_ref[...] * xa
        y = y * _silu(z_ref[rows, :])
        ya_ref[rows, :] = _rms_unit(y) * naw_ref[...]

        cos_f, sin_a, sin_b = cos_ref[rows, :], sina_ref[rows, :], sinb_ref[rows, :]
        qr = _rope(q_ref[rows, :], cos_f, sin_a, sin_b)
        kr = _rope(k_ref[rows, :], cos_f, sin_a, sin_b) * (DK_R ** -0.5)
        k_t = kr.T
        kd_t = k_t * kdect_s[...]
        qd = qr * qdec_s[...]
        for h in range(H_R):
            kl = slice(h * DK_R, (h + 1) * DK_R)
            vl = slice(h * DV_R, (h + 1) * DV_R)
            sc = _bdot(qr[:, kl], k_t[kl, :]) * intra_s[h]
            s_h = s_s[kl, :]
            v_h = v_ref[rows, vl]
            y_h = _bdot(jnp.concatenate([sc, qd[:, kl]], axis=1),
                        jnp.concatenate([v_h, s_h], axis=0))
            s_s[kl, :] = math.exp(CHUNK * _LOG_GAMMA[h]) * s_h + _bdot(kd_t[kl, :], v_h)
            yb_ref[rows, vl] = _silu(g_ref[rows, vl]) * _rms_unit(y_h)

    xp_s[5:8, :] = xp_s[tsteps + 5:tsteps + 8, :]
    cs_ref[...] = xp_s[5:8, :]

    @pl.when(i == nt - 1)
    def _():
        for h in range(H_A):
            hs_ref[h] = ht_s[:, h * P_A:(h + 1) * P_A].T
        for h in range(H_R):
            ss_ref[h] = s_s[h * DK_R:(h + 1) * DK_R, :]


def _layer_block(l, tail):
    zeros = (0,) * len(tail)
    return pl.BlockSpec((None, None) + tail, lambda b, *_: (l, b) + zeros)


def _mix_prompt(p, rot, lw, nb, seq, tsteps, l, prev):
    nt = seq // tsteps
    row = lambda w: pl.BlockSpec((tsteps, w), lambda b, i: (b * nt + i, 0))
    tab = pl.BlockSpec((tsteps, D_QK), lambda b, i: (i, 0))
    in_specs = [row(D_A), row(D_XBC), row(DT_PAD), row(D_QK), row(D_QK), row(D_RV), row(D_RV),
                tab, tab, tab, _const_spec((1, D_QK)),
                _const_spec((CONV_A, D_XBC)), _const_spec((1, D_XBC)), _const_spec((1, DT_PAD)),
                _const_spec((1, DT_PAD)), _const_spec((1, D_A)), _const_spec((1, D_A))]
    prev = () if prev is None else tuple(prev)
    in_specs += [pl.BlockSpec(memory_space=pl.ANY)] * len(prev)
    aliases = {len(in_specs) - len(prev) + k: 3 + k for k in range(len(prev))}
    out_specs = [row(D_A), row(D_RV),
                 pl.BlockSpec((None, CONV_A - 1, D_XBC), lambda b, i: (b, 0, 0)),
                 _layer_block(l, (H_A, P_A, N_A)), _layer_block(l, (H_R, DK_R, DV_R))]
    out_shape = [jax.ShapeDtypeStruct((nb * seq, D_A), F32),
                 jax.ShapeDtypeStruct((nb * seq, D_RV), F32),
                 jax.ShapeDtypeStruct((nb, CONV_A - 1, D_XBC), F32),
                 jax.ShapeDtypeStruct((DEPTH, nb, H_A, P_A, N_A), F32),
                 jax.ShapeDtypeStruct((DEPTH, nb, H_R, DK_R, DV_R), F32)]
    scratch = [pltpu.VMEM((8 + tsteps, D_XBC), F32), pltpu.VMEM((N_A, D_A), F32),
               pltpu.VMEM((D_QK, DV_R), F32), pltpu.VMEM((H_R, CHUNK, CHUNK), F32),
               pltpu.VMEM((CHUNK, D_QK), F32), pltpu.VMEM((D_QK, CHUNK), F32)]
    return pl.pallas_call(
        functools.partial(_mix_body, tsteps=tsteps, nt=nt, n_prev=len(prev)), grid=(nb, nt),
        in_specs=in_specs, out_specs=out_specs, out_shape=out_shape, scratch_shapes=scratch,
        input_output_aliases=aliases, compiler_params=_params(2), name="mix_prompt")(
            p["z"], p["xbc"], p["dt"], p["q"], p["k"], p["v"], p["g"],
            rot["cos"], rot["sin_a"], rot["sin_b"], rot["lgx"],
            lw["conv_a_w"], lw["conv_a_b"], lw["dt_bias"], lw["a_log"], lw["d_x"], lw["norm_a_w"],
            *prev)


def _s5_disc_body(are_ref, aim_ref, ldt_ref, bre_ref, bim_ref,
                  abre_ref, abim_ref, bbre_ref, bbim_ref):
    ar, ai = are_ref[...], aim_ref[...]
    dt = jnp.exp(ldt_ref[...])
    mag = jnp.exp(ar * dt)
    ab_re = mag * jnp.cos(ai * dt)
    ab_im = mag * jnp.sin(ai * dt)
    den = ar * ar + ai * ai
    num_re = ab_re - 1.0
    coef_re = (num_re * ar + ab_im * ai) / den
    coef_im = (ab_im * ar - num_re * ai) / den
    abre_ref[...] = ab_re
    abim_ref[...] = ab_im
    for c in range(GS_C):
        cl = slice(c * P_C, (c + 1) * P_C)
        br, bi = bre_ref[:, cl], bim_ref[:, cl]
        bbre_ref[:, cl] = coef_re * br - coef_im * bi
        bbim_ref[:, cl] = coef_re * bi + coef_im * br


def _s5_discretise(a_re, a_im, log_dt, b_re, b_im):
    gp = jax.ShapeDtypeStruct((G_C, P_C), F32)
    gcp = jax.ShapeDtypeStruct((G_C, GS_C * P_C), F32)
    b_t = lambda b: jnp.transpose(b, (0, 2, 1)).reshape(G_C, GS_C * P_C)
    ab_re, ab_im, bb_re, bb_im = pl.pallas_call(
        _s5_disc_body, out_shape=[gp, gp, gcp, gcp], name="s5_disc")(
            a_re, a_im, log_dt.reshape(G_C, 1), b_t(b_re), b_t(b_im))
    return ab_re, ab_im, bb_re.reshape(G_C, GS_C, P_C), bb_im.reshape(G_C, GS_C, P_C)


_S5_GB = LANES // GS_C
_S5_NBLK = G_C // _S5_GB
_S5_SB = _S5_GB * P_C


def _block_diag(m):
    g, r, c = m.shape
    m = m.reshape(_S5_NBLK, _S5_GB, r, c)
    eye = jnp.eye(_S5_GB, dtype=m.dtype)
    return jnp.einsum("jgrc,gk->jgrkc", m, eye).reshape(_S5_NBLK, _S5_GB * r, _S5_GB * c)


def _row_perm(rows, log_a, log_b):
    r = lax.broadcasted_iota(jnp.int32, (rows, rows), 0)
    c = lax.broadcasted_iota(jnp.int32, (rows, rows), 1)
    same_i = (r >> log_a) == (c & ((1 << log_b) - 1))
    same_j = (r & ((1 << log_a) - 1)) == (c >> log_b)
    return jnp.where(same_i, jnp.where(same_j, 1.0, 0.0), 0.0).astype(BF16)


def _s5_body(u_ref, h0re_ref, h0im_ref, are_ref, aim_ref, bblk_ref, cre_ref, cim_ref,
             d_ref, wglu_ref, bglu_ref, yc_ref, hre_ref, him_ref,
             xre_s, xim_s, sre_s, sim_s, *, nb, tt):
    i = pl.program_id(0)
    rows = nb * tt

    @pl.when(i == 0)
    def _():
        sre_s[...] = h0re_ref[...]
        sim_s[...] = h0im_ref[...]

    u = u_ref[...].reshape(rows, D_C)
    ub = u.astype(BF16)
    if tt > 1:
        to_tm = _row_perm(rows, int(math.log2(nb)), int(math.log2(tt)))
        ub = jnp.dot(to_tm, ub, preferred_element_type=F32).astype(BF16)
    for j in range(_S5_NBLK):
        bu = jnp.dot(ub[:, j * LANES:(j + 1) * LANES], bblk_ref[j], preferred_element_type=F32)
        xre_s[:, j * _S5_SB:(j + 1) * _S5_SB] = bu[:, :_S5_SB]
        xim_s[:, j * _S5_SB:(j + 1) * _S5_SB] = bu[:, _S5_SB:]

    cw = 512
    for cb in range(D_S5 // cw):
        cl = slice(cb * cw, (cb + 1) * cw)
        ar = jnp.broadcast_to(are_ref[:, cl], (nb, cw))
        ai = jnp.broadcast_to(aim_ref[:, cl], (nb, cw))

        def step(t, carry):
            xr, xi = carry
            r = pl.ds(pl.multiple_of(t * nb, nb), nb)
            nr = ar * xr - ai * xi + xre_s[r, cl]
            ni = ar * xi + ai * xr + xim_s[r, cl]
            xre_s[r, cl] = nr
            xim_s[r, cl] = ni
            return nr, ni

        xr, xi = lax.fori_loop(0, tt, step, (sre_s[:, cl], sim_s[:, cl]), unroll=min(tt, 8))
        sre_s[:, cl] = xr
        sim_s[:, cl] = xi

    ys = []
    for j in range(_S5_NBLK):
        sl = slice(j * _S5_SB, (j + 1) * _S5_SB)
        ys.append(_bdot(xre_s[:, sl], cre_ref[j]) - _bdot(xim_s[:, sl], cim_ref[j]))
    y = jnp.concatenate(ys, axis=1)
    if tt > 1:
        to_bm = _row_perm(rows, int(math.log2(tt)), int(math.log2(nb)))
        y = _dot01(to_bm, y, "rhs")
    y = jax.nn.gelu(y + d_ref[...] * u)
    y = y * jax.nn.sigmoid(_bdot(y, wglu_ref[...]) + bglu_ref[...])
    yc_ref[...] = y.reshape(yc_ref.shape)
    hre_ref[...] = sre_s[...]
    him_ref[...] = sim_s[...]


def _s5(u, h0_re, h0_im, lw, nb, seq, tt):
    rows = tt * nb
    if tt > 1:
        assert nb & (nb - 1) == 0 and tt & (tt - 1) == 0 and tt % 8 == 0
        u = u.reshape(nb, seq, D_C)
        io_spec = pl.BlockSpec((nb, tt, D_C), lambda i: (0, i, 0))
    else:
        io_spec = pl.BlockSpec((rows, D_C), lambda i: (i, 0))
    in_specs = [io_spec,
                _const_spec((nb, D_S5)), _const_spec((nb, D_S5)),
                _const_spec((1, D_S5)), _const_spec((1, D_S5)),
                _const_spec((_S5_NBLK, LANES, 2 * _S5_SB)),
                _const_spec((_S5_NBLK, _S5_SB, LANES)), _const_spec((_S5_NBLK, _S5_SB, LANES)),
                _const_spec((1, D_C)), _const_spec((D_C, D_C)), _const_spec((1, D_C))]
    st = pl.BlockSpec((nb, D_S5), lambda i: (0, 0))
    yc, h_re, h_im = pl.pallas_call(
        functools.partial(_s5_body, nb=nb, tt=tt), grid=(seq // tt,), in_specs=in_specs,
        out_specs=[io_spec, st, st],
        out_shape=[jax.ShapeDtypeStruct(u.shape, F32),
                   jax.ShapeDtypeStruct((nb, D_S5), F32), jax.ShapeDtypeStruct((nb, D_S5), F32)],
        scratch_shapes=[pltpu.VMEM((rows, D_S5), F32), pltpu.VMEM((rows, D_S5), F32),
                        pltpu.VMEM((nb, D_S5), F32), pltpu.VMEM((nb, D_S5), F32)],
        compiler_params=_params(1), name="s5")(
            u, h0_re, h0_im, lw["s5_ab_re"], lw["s5_ab_im"], lw["s5_bblk"],
            lw["s5_cre"], lw["s5_cim"], lw["s5_d"], lw["w_glu"], lw["b_glu"])
    return yc.reshape(nb * seq, D_C), h_re, h_im


def _merge_body(x_ref, ya_ref, yb_ref, yc_ref, gt_ref, wa_ref, wb_ref, wc_ref, wo_ref, o_ref):
    gt = jax.nn.sigmoid(gt_ref[...])
    merged = (gt[:, :D_MODEL] * _bdot(ya_ref[...], wa_ref[...])
              + gt[:, D_MODEL:2 * D_MODEL] * _bdot(yb_ref[...], wb_ref[...])
              + gt[:, 2 * D_MODEL:] * _bdot(yc_ref[...], wc_ref[...]))
    o_ref[...] = x_ref[...] + _bdot(merged, wo_ref[...])


def _merge(x2d, ya, yb, yc, gates, lw, nb, seq, tm):
    nt = seq // tm
    row = lambda w: pl.BlockSpec((tm, w), lambda b, i: (b * nt + i, 0))
    wspec = _const_spec((D_MODEL, D_MODEL))
    return pl.pallas_call(
        _merge_body, grid=(nb, nt),
        in_specs=[row(D_MODEL), row(D_A), row(D_RV), row(D_C),
                  row(3 * D_MODEL), wspec, wspec, wspec, wspec],
        out_specs=row(D_MODEL), out_shape=jax.ShapeDtypeStruct((nb * seq, D_MODEL), F32),
        compiler_params=_params(2), name="merge")(
            x2d, ya, yb, yc, gates,
            lw["w_br_a"], lw["w_br_b"], lw["w_br_c"], lw["w_out"])


_FFN_CW = 256


def _ffn_tail(x, up_s, act_s, cw_ref, cb_ref, wdn_ref, tm):
    for c in range(0, D_FF, _FFN_CW):
        def conv(c0):
            cl = slice(c0, c0 + _FFN_CW)
            y = cb_ref[:, cl]
            for tap in range(CONV_F):
                y = y + up_s[6 + tap:6 + tap + tm, cl] * cw_ref[tap:tap + 1, cl]
            return y
        act_s[:, c:c + _FFN_CW] = (_silu(conv(c)) * conv(D_FF + c)).astype(BF16)
    return x + jnp.dot(act_s[...], wdn_ref[...], preferred_element_type=F32)


def _ffn_prompt_body(x_ref, nw_ref, wup_ref, cw_ref, cb_ref, wdn_ref, nf_ref,
                     o_ref, cs_ref, up_s, act_s, *, tm, final):
    @pl.when(pl.program_id(1) == 0)
    def _():
        up_s[0:8, :] = jnp.zeros((8, 2 * D_FF), F32)

    x = x_ref[...]
    hf = (_rms_unit(x) * nw_ref[...]).astype(BF16)
    up_s[8:8 + tm, :] = jnp.dot(hf, wup_ref[...], preferred_element_type=F32)
    out = _ffn_tail(x, up_s, act_s, cw_ref, cb_ref, wdn_ref, tm)
    o_ref[...] = _rms_unit(out) * nf_ref[...] if final else out
    tail = up_s[tm + 6:tm + 8, :]
    cs_ref[...] = tail
    up_s[6:8, :] = tail


def _ffn_sample_body(x_ref, st_ref, nw_ref, wup_ref, cw_ref, cb_ref, wdn_ref, nf_ref,
                     o_ref, cs_ref, act_s, *, final):
    x = x_ref[...]
    hf = (_rms_unit(x) * nw_ref[...]).astype(BF16)
    up = jnp.dot(hf, wup_ref[...], preferred_element_type=F32)
    prev2, prev1 = st_ref[:, :2 * D_FF], st_ref[:, 2 * D_FF:]
    cs_ref[:, :2 * D_FF] = prev1
    cs_ref[:, 2 * D_FF:] = up
    for c in range(0, D_FF, _FFN_CW):
        def conv(c0):
            cl = slice(c0, c0 + _FFN_CW)
            return (cb_ref[:, cl] + prev2[:, cl] * cw_ref[0:1, cl] + prev1[:, cl] * cw_ref[1:2, cl]
                    + up[:, cl] * cw_ref[2:3, cl])
        act_s[:, c:c + _FFN_CW] = (_silu(conv(c)) * conv(D_FF + c)).astype(BF16)
    out = x + jnp.dot(act_s[...], wdn_ref[...], preferred_element_type=F32)
    o_ref[...] = _rms_unit(out) * nf_ref[...] if final else out


def _ffn(x2d, state, lw, nf, nb, seq, tm, final):
    wspecs = [_const_spec((1, D_MODEL)), _const_spec((D_MODEL, 2 * D_FF)),
              _const_spec((CONV_F, 2 * D_FF)), _const_spec((1, 2 * D_FF)),
              _const_spec((D_FF, D_MODEL)), _const_spec((1, D_MODEL))]
    wargs = (lw["norm_ffn_w"], lw["w_up"], lw["conv_f_w"], lw["conv_f_b"], lw["w_down"], nf)
    act = pltpu.VMEM((tm, D_FF), BF16)
    if state is None:
        nt = seq // tm
        row = pl.BlockSpec((tm, D_MODEL), lambda b, i: (b * nt + i, 0))
        return pl.pallas_call(
            functools.partial(_ffn_prompt_body, tm=tm, final=final), grid=(nb, nt),
            in_specs=[row] + wspecs,
            out_specs=[row, pl.BlockSpec((None, CONV_F - 1, 2 * D_FF), lambda b, i: (b, 0, 0))],
            out_shape=[jax.ShapeDtypeStruct((nb * seq, D_MODEL), F32),
                       jax.ShapeDtypeStruct((nb, CONV_F - 1, 2 * D_FF), F32)],
            scratch_shapes=[pltpu.VMEM((8 + tm, 2 * D_FF), F32), act],
            compiler_params=_params(2), name="ffn_prompt")(x2d, *wargs)
    sw = (CONV_F - 1) * 2 * D_FF
    return pl.pallas_call(
        functools.partial(_ffn_sample_body, final=final), grid=(nb // tm,),
        in_specs=[pl.BlockSpec((tm, D_MODEL), lambda i: (i, 0)),
                  pl.BlockSpec((tm, sw), lambda i: (i, 0))] + wspecs,
        out_specs=[pl.BlockSpec((tm, D_MODEL), lambda i: (i, 0)),
                   pl.BlockSpec((tm, sw), lambda i: (i, 0))],
        out_shape=[jax.ShapeDtypeStruct((nb, D_MODEL), F32), jax.ShapeDtypeStruct((nb, sw), F32)],
        scratch_shapes=[act], compiler_params=_params(1), name="ffn_sample")(x2d, state, *wargs)


def _mix_sample_body(*refs, nb, n_prev):
    (z_ref, xbc_ref, dt_ref, q_ref, k_ref, v_ref, g_ref, cst_ref, cos_ref, sina_ref, sinb_ref,
     cw_ref, cb_ref, dtb_ref, alog_ref, dx_ref, naw_ref, hin_ref, sin_ref) = refs[:19]
    (ya_ref, yb_ref, cso_ref, hout_ref, sout_ref,
     xa_s, xdt_t_s, dec_s, bm_s, cm_t_s, qr_s, k_t_s, yt_s, yr_s) = refs[19 + n_prev:]
    b = pl.program_id(0)

    @pl.when(b == 0)
    def _():
        acc = cb_ref[...]
        for tap in range(CONV_A - 1):
            acc = acc + cst_ref[:, tap * D_XBC:(tap + 1) * D_XBC] * cw_ref[tap:tap + 1, :]
        acc = acc + xbc_ref[...] * cw_ref[CONV_A - 1:CONV_A, :]
        cso_ref[:, :(CONV_A - 2) * D_XBC] = cst_ref[:, D_XBC:]
        cso_ref[:, (CONV_A - 2) * D_XBC:] = xbc_ref[...]
        xc = _silu(acc)
        xa = xc[:, :D_A]
        xa_s[...] = xa
        bm_s[...] = xc[:, D_A:D_A + G_A * N_A]
        cm_t_s[...] = xc[:, D_A + G_A * N_A:].T
        dt = _softplus(dt_ref[...] + dtb_ref[...])
        dec_s[...] = jnp.exp(dt * -jnp.exp(alog_ref[...]))
        dt_x = _dot01(dt, _head_expand(), "lhs")
        xdt_t_s[...] = (xa * dt_x).T
        cos_f, sin_a, sin_b = cos_ref[...], sina_ref[...], sinb_ref[...]
        qr_s[...] = _rope(q_ref[...], cos_f, sin_a, sin_b)
        k_t_s[...] = (_rope(k_ref[...], cos_f, sin_a, sin_b) * (DK_R ** -0.5)).T
        yt_s[...] = jnp.zeros_like(yt_s)
        yr_s[...] = jnp.zeros_like(yr_s)

    row_is_b = lax.broadcasted_iota(jnp.int32, (nb, 1), 0) == b
    lane_is_b = lax.broadcasted_iota(jnp.int32, (1, nb), 1) == b
    dec_row = dec_s[pl.ds(b, 1), :]

    for g in range(G_A):
        gl = slice(g * N_A, (g + 1) * N_A)
        hl = slice(g * HPG * P_A, (g + 1) * HPG * P_A)
        b_sel = jnp.where(row_is_b, bm_s[:, gl], 0.0)
        outer = _bdot(xdt_t_s[hl, :], b_sel)
        for hh in range(HPG):
            h = g * HPG + hh
            hout_ref[h] = hin_ref[h] * dec_row[:, h:h + 1] + outer[hh * P_A:(hh + 1) * P_A, :]
        c_sel = jnp.where(lane_is_b, cm_t_s[gl, :], 0.0)
        h_new = hout_ref[g * HPG:(g + 1) * HPG].reshape(HPG * P_A, N_A)
        yt_s[hl, :] += _bdot(h_new, c_sel)

    for h in range(H_R):
        kl = slice(h * DK_R, (h + 1) * DK_R)
        vl = slice(h * DV_R, (h + 1) * DV_R)
        v_sel = jnp.where(row_is_b, v_ref[:, vl], 0.0)
        s_new = math.exp(_LOG_GAMMA[h]) * sin_ref[h] + _bdot(k_t_s[kl, :], v_sel)
        sout_ref[h] = s_new
        q_sel = jnp.where(row_is_b, qr_s[:, kl], 0.0)
        yr_s[h] += _bdot(q_sel, s_new)

    @pl.when(b == nb - 1)
    def _():
        xa = xa_s[...]
        y = yt_s[...].T + dx_ref[...] * xa
        y = y * _silu(z_ref[...])
        ya_ref[...] = _rms_unit(y) * naw_ref[...]
        for h in range(H_R):
            vl = slice(h * DV_R, (h + 1) * DV_R)
            yb_ref[:, vl] = _silu(g_ref[:, vl]) * _rms_unit(yr_s[h])


def _mix_sample(p, conv_st, h_ssd, s_ret, rot, lw, nb, l, prev):
    full = lambda w: _const_spec((nb, w))
    cw3 = (CONV_A - 1) * D_XBC
    in_specs = [full(D_A), full(D_XBC), full(DT_PAD), full(D_QK), full(D_QK), full(D_RV), full(D_RV),
                full(cw3), _const_spec((1, D_QK)), _const_spec((1, D_QK)), _const_spec((1, D_QK)),
                _const_spec((CONV_A, D_XBC)), _const_spec((1, D_XBC)), _const_spec((1, DT_PAD)),
                _const_spec((1, DT_PAD)), _const_spec((1, D_A)), _const_spec((1, D_A)),
                _layer_block(l, (H_A, P_A, N_A)), _layer_block(l, (H_R, DK_R, DV_R))]
    prev = () if prev is None else tuple(prev)
    in_specs += [pl.BlockSpec(memory_space=pl.ANY)] * len(prev)
    aliases = {len(in_specs) - len(prev) + k: 3 + k for k in range(len(prev))}
    keep = lambda w: pl.BlockSpec((nb, w), lambda b: (0, 0))
    out_specs = [keep(D_A), keep(D_RV), keep(cw3),
                 _layer_block(l, (H_A, P_A, N_A)), _layer_block(l, (H_R, DK_R, DV_R))]
    out_shape = [jax.ShapeDtypeStruct((nb, D_A), F32), jax.ShapeDtypeStruct((nb, D_RV), F32),
                 jax.ShapeDtypeStruct((nb, cw3), F32),
                 jax.ShapeDtypeStruct((DEPTH, nb, H_A, P_A, N_A), F32),
                 jax.ShapeDtypeStruct((DEPTH, nb, H_R, DK_R, DV_R), F32)]
    scratch = [pltpu.VMEM((nb, D_A), F32), pltpu.VMEM((D_A, nb), F32), pltpu.VMEM((nb, LANES), F32),
               pltpu.VMEM((nb, G_A * N_A), F32), pltpu.VMEM((G_A * N_A, nb), F32),
               pltpu.VMEM((nb, D_QK), F32), pltpu.VMEM((D_QK, nb), F32),
               pltpu.VMEM((D_A, nb), F32), pltpu.VMEM((H_R, nb, DV_R), F32)]
    ya, yb, cs, hs, ss = pl.pallas_call(
        functools.partial(_mix_sample_body, nb=nb, n_prev=len(prev)), grid=(nb,),
        in_specs=in_specs, out_specs=out_specs, out_shape=out_shape, scratch_shapes=scratch,
        input_output_aliases=aliases, compiler_params=_params(1), name="mix_sample")(
            p["z"], p["xbc"], p["dt"], p["q"], p["k"], p["v"], p["g"], conv_st.reshape(nb, cw3),
            rot["cos"], rot["sin_a"], rot["sin_b"],
            lw["conv_a_w"], lw["conv_a_b"], lw["dt_bias"], lw["a_log"], lw["d_x"], lw["norm_a_w"],
            h_ssd, s_ret, *prev)
    return ya, yb, cs.reshape(nb, CONV_A - 1, D_XBC), hs, ss


def _rotary_tables(start, length):
    half = DK_R // 2
    pos = start + jnp.arange(length, dtype=F32)
    freqs = ROPE_BASE ** (-jnp.arange(half, dtype=F32) / half)
    ang = pos[:, None] * freqs[None, :]
    cos, sin, zero = jnp.cos(ang), jnp.sin(ang), jnp.zeros_like(ang)
    tile = lambda a, b: jnp.tile(jnp.concatenate([a, b], axis=-1), (1, H_R))
    lg = jnp.log1p(-jnp.exp2(-5.0 - jnp.arange(H_R, dtype=F32)))
    return {"cos": tile(cos, cos), "sin_a": tile(-sin, zero), "sin_b": tile(zero, sin),
            "lgx": jnp.repeat(lg, DK_R)[None, :]}


def _layer_weights(l, w):
    w_in = w["w_in"][l]
    offs = np.cumsum([0, D_A, D_XBC, H_A, D_QK, D_QK, D_RV, D_RV, D_C, 3 * D_MODEL])
    cols = [w_in[:, offs[i]:offs[i + 1]] for i in range(9)]
    cols[2] = jnp.pad(cols[2], ((0, 0), (0, DT_PAD - H_A)))
    pad_row = lambda a: jnp.pad(a, (0, DT_PAD - H_A))[None, :]
    ab_re, ab_im, bb_re, bb_im = _s5_discretise(
        w["s5_a_re"][l], w["s5_a_im"][l], w["s5_log_dt"][l], w["s5_b_re"][l], w["s5_b_im"][l])
    bblk = jnp.concatenate([_block_diag(bb_re), _block_diag(bb_im)], axis=-1)
    c_t = lambda c: _block_diag(jnp.transpose(c, (0, 2, 1)))
    return {
        "norm_mix_w": w["norm_mix_w"][l][None, :],
        "w_in": [c.astype(BF16) for c in cols],
        "conv_a_w": w["conv_a_w"][l], "conv_a_b": w["conv_a_b"][l][None, :],
        "dt_bias": pad_row(w["dt_bias"][l]), "a_log": pad_row(w["a_log"][l]),
        "d_x": jnp.repeat(w["d_a"][l], P_A)[None, :], "norm_a_w": w["norm_a_w"][l][None, :],
        "s5_ab_re": ab_re.reshape(1, D_S5), "s5_ab_im": ab_im.reshape(1, D_S5),
        "s5_bblk": bblk.astype(BF16),
        "s5_cre": c_t(w["s5_c_re"][l]).astype(BF16), "s5_cim": c_t(w["s5_c_im"][l]).astype(BF16),
        "s5_d": w["s5_d"][l][None, :], "w_glu": w["w_glu"][l].astype(BF16),
        "b_glu": w["b_glu"][l][None, :],
        "w_br_a": w["w_br_a"][l].astype(BF16), "w_br_b": w["w_br_b"][l].astype(BF16),
        "w_br_c": w["w_br_c"][l].astype(BF16), "w_out": w["w_out"][l].astype(BF16),
        "norm_ffn_w": w["norm_ffn_w"][l][None, :], "w_up": w["w_up"][l].astype(BF16),
        "conv_f_w": w["conv_f_w"][l], "conv_f_b": w["conv_f_b"][l][None, :],
        "w_down": w["w_down"][l].astype(BF16),
    }


def _prompt_trunk(x, lws, nf, tm_proj, tsteps, tt, tm_merge, tm_ffn):
    nb, seq, _ = x.shape
    rot = _rotary_tables(0.0, seq)
    x2d = x.reshape(nb * seq, D_MODEL)
    zeros = jnp.zeros((nb, D_S5), F32)
    states, big = [], None
    for l, lw in enumerate(lws):
        p = _inproj(x2d, lw["norm_mix_w"], lw["w_in"], nb, seq, tm_proj)
        ya, yb, cs, *big = _mix_prompt(p, rot, lw, nb, seq, tsteps, l, big)
        yc, h_re, h_im = _s5(p["u"], zeros, zeros, lw, nb, seq, tt)
        x2d = _merge(x2d, ya, yb, yc, p["gates"], lw, nb, seq, tm_merge)
        x2d, fs = _ffn(x2d, None, lw, nf, nb, seq, tm_ffn, final=(l == len(lws) - 1))
        states.append((cs, h_re.reshape(nb, G_C, P_C), h_im.reshape(nb, G_C, P_C), fs))
    cs, h_re, h_im, fs = [jnp.stack(s) for s in zip(*states)]
    return x2d.reshape(nb, seq, D_MODEL), [cs, big[0], big[1], h_re, h_im, fs]


def _sample_trunk(x, st, lws, nf):
    nb = x.shape[0]
    rot = _rotary_tables(float(PAST_LEN), 1)
    x2d = x.reshape(nb, D_MODEL)
    st_conv, st_ssd, st_ret, st_re, st_im, st_ffn = st
    states, big = [], None
    for l, lw in enumerate(lws):
        p = _inproj(x2d, lw["norm_mix_w"], lw["w_in"], 1, nb, nb)
        ya, yb, cs, *big = _mix_sample(p, st_conv[l], st_ssd, st_ret, rot, lw, nb, l, big)
        yc, h_re, h_im = _s5(p["u"], st_re[l].reshape(nb, D_S5), st_im[l].reshape(nb, D_S5),
                             lw, nb, 1, 1)
        x2d = _merge(x2d, ya, yb, yc, p["gates"], lw, 1, nb, nb)
        x2d, fs = _ffn(x2d, st_ffn[l].reshape(nb, -1), lw, nf, nb, 1, nb,
                       final=(l == len(lws) - 1))
        states.append((cs, h_re.reshape(nb, G_C, P_C), h_im.reshape(nb, G_C, P_C),
                       fs.reshape(nb, CONV_F - 1, 2 * D_FF)))
    cs, h_re, h_im, fs = [jnp.stack(s) for s in zip(*states)]
    return x2d.reshape(nb, 1, D_MODEL), [cs, big[0], big[1], h_re, h_im, fs]


def kernel(x_prompt, x_sample, state_ssd_conv, state_ssd, state_ret, state_s5_re, state_s5_im,
           state_ffn_conv, norm_mix_w, w_in, conv_a_w, conv_a_b, dt_bias, a_log, d_a, norm_a_w,
           s5_a_re, s5_a_im, s5_log_dt, s5_b_re, s5_b_im, s5_c_re, s5_c_im, s5_d, w_glu, b_glu,
           w_br_a, w_br_b, w_br_c, w_out, norm_ffn_w, w_up, conv_f_w, conv_f_b, w_down, norm_f_w):
    w = dict(norm_mix_w=norm_mix_w, w_in=w_in, conv_a_w=conv_a_w, conv_a_b=conv_a_b,
             dt_bias=dt_bias, a_log=a_log, d_a=d_a, norm_a_w=norm_a_w, s5_a_re=s5_a_re,
             s5_a_im=s5_a_im, s5_log_dt=s5_log_dt, s5_b_re=s5_b_re, s5_b_im=s5_b_im,
             s5_c_re=s5_c_re, s5_c_im=s5_c_im, s5_d=s5_d, w_glu=w_glu, b_glu=b_glu,
             w_br_a=w_br_a, w_br_b=w_br_b, w_br_c=w_br_c, w_out=w_out, norm_ffn_w=norm_ffn_w,
             w_up=w_up, conv_f_w=conv_f_w, conv_f_b=conv_f_b, w_down=w_down)
    lws = [_layer_weights(l, w) for l in range(DEPTH)]
    nf = norm_f_w[None, :]
    seq = x_prompt.shape[1]
    y_p, p_st = _prompt_trunk(x_prompt, lws, nf, tm_proj=min(256, seq), tsteps=min(256, seq),
                              tt=min(32, seq), tm_merge=min(512, seq), tm_ffn=min(256, seq))
    y_s, s_st = _sample_trunk(
        x_sample, (state_ssd_conv, state_ssd, state_ret, state_s5_re, state_s5_im, state_ffn_conv),
        lws, nf)
    return (y_p, y_s, *p_st, *s_st)
```

```python
import functools
import math

import jax
import jax.numpy as jnp
import numpy as np
from jax import lax
from jax.experimental import pallas as pl
from jax.experimental.pallas import tpu as pltpu

F32 = jnp.float32
BF16 = jnp.bfloat16

D_MODEL = 1024
DEPTH = 2
PAST_LEN = 16384
H_A, P_A, N_A, G_A, CONV_A = 16, 64, 64, 2, 4
D_A = H_A * P_A
D_XBC = D_A + 2 * G_A * N_A
H_R, DK_R, DV_R = 8, 64, 128
D_QK = H_R * DK_R
D_RV = H_R * DV_R
ROPE_BASE = 10000.0
GS_C, G_C, P_C = 16, 64, 64
D_C = G_C * GS_C
D_S5 = G_C * P_C
D_FF = 2816
CONV_F = 3
CHUNK = 128
EPS = 1e-6
LANES = 128
DT_PAD = LANES
HPG = H_A // G_A
VMEM_LIMIT = 56 * 1024 * 1024

_LOG_GAMMA = [math.log1p(-(2.0 ** (-5.0 - h))) for h in range(H_R)]

_SEGS = (("z", D_A), ("xbc", D_XBC), ("dt", DT_PAD), ("q", D_QK), ("k", D_QK),
         ("v", D_RV), ("g", D_RV), ("u", D_C), ("gates", 3 * D_MODEL))


def _rms_unit(x):
    return x * lax.rsqrt(jnp.mean(x * x, axis=-1, keepdims=True) + EPS)


def _silu(x):
    return x * jax.nn.sigmoid(x)


def _softplus(x):
    return jnp.maximum(x, 0.0) + jnp.log1p(jnp.exp(-jnp.abs(x)))


def _bdot(a, b):
    return jnp.dot(a.astype(BF16), b.astype(BF16), preferred_element_type=F32)


def _split3(x):
    hi = x.astype(BF16)
    r1 = x - hi.astype(F32)
    mid = r1.astype(BF16)
    return hi, mid, (r1 - mid.astype(F32)).astype(BF16)


def _dot01(a, b, f32_side):
    if f32_side == "lhs":
        return sum(jnp.dot(p, b, preferred_element_type=F32) for p in _split3(a))
    return sum(jnp.dot(a, p, preferred_element_type=F32) for p in _split3(b))


def _const_spec(shape):
    nd = len(shape)
    return pl.BlockSpec(shape, lambda *_: (0,) * nd, pipeline_mode=pl.Buffered(1))


def _params(n_grid):
    return pltpu.CompilerParams(dimension_semantics=("arbitrary",) * n_grid,
                                vmem_limit_bytes=VMEM_LIMIT)


def _inproj_body(x_ref, nw_ref, *refs):
    n = len(_SEGS)
    h = (_rms_unit(x_ref[...]) * nw_ref[...]).astype(BF16)
    for w_ref, o_ref in zip(refs[:n], refs[n:]):
        o_ref[...] = jnp.dot(h, w_ref[...], preferred_element_type=F32)


def _inproj(x2d, nw, ws, nb, seq, tm):
    nt = seq // tm
    in_specs = [pl.BlockSpec((tm, D_MODEL), lambda b, i: (b * nt + i, 0)),
                _const_spec((1, D_MODEL))]
    in_specs += [_const_spec(w.shape) for w in ws]
    out_specs, out_shape = [], []
    for _, width in _SEGS:
        out_specs.append(pl.BlockSpec((tm, width), lambda b, i: (b * nt + i, 0)))
        out_shape.append(jax.ShapeDtypeStruct((nb * seq, width), F32))
    outs = pl.pallas_call(
        _inproj_body, grid=(nb, nt), in_specs=in_specs, out_specs=out_specs,
        out_shape=out_shape, compiler_params=_params(2), name="inproj")(x2d, nw, *ws)
    return {name: o for (name, _), o in zip(_SEGS, outs)}


def _rope(x, cos_f, sin_a, sin_b):
    half = DK_R // 2
    return (x * cos_f + pltpu.roll(x, D_QK - half, 1) * sin_a
            + pltpu.roll(x, half, 1) * sin_b)


def _head_expand():
    lo = lax.broadcasted_iota(jnp.int32, (LANES, D_A), 0) * P_A
    c = lax.broadcasted_iota(jnp.int32, (LANES, D_A), 1)
    return jnp.where(c >= lo, jnp.where(c < lo + P_A, 1.0, 0.0), 0.0).astype(BF16)


def _mix_body(*refs, tsteps, nt, n_prev):
    (z_ref, xbc_ref, dt_ref, q_ref, k_ref, v_ref, g_ref, cos_ref, sina_ref, sinb_ref, lgx_ref,
     cw_ref, cb_ref, dtb_ref, alog_ref, dx_ref, naw_ref) = refs[:17]
    (ya_ref, yb_ref, cs_ref, hs_ref, ss_ref,
     xp_s, ht_s, s_s, intra_s, qdec_s, kdect_s) = refs[17 + n_prev:]
    i = pl.program_id(1)
    first = jnp.logical_and(pl.program_id(0) == 0, i == 0)
    rows_i = lax.broadcasted_iota(jnp.int32, (CHUNK, CHUNK), 0)
    cols_i = lax.broadcasted_iota(jnp.int32, (CHUNK, CHUNK), 1)
    causal = rows_i >= cols_i

    @pl.when(first)
    def _():
        rel = (rows_i - cols_i).astype(F32)
        for h in range(H_R):
            intra_s[h] = jnp.where(causal, jnp.exp(jnp.maximum(rel, 0.0) * _LOG_GAMMA[h]), 0.0)
        ri = lax.broadcasted_iota(jnp.int32, (CHUNK, D_QK), 0).astype(F32)
        lgx = lgx_ref[...]
        qdec_s[...] = jnp.exp((ri + 1.0) * lgx)
        kdect_s[...] = jnp.exp((CHUNK - 1.0 - ri) * lgx).T

    @pl.when(i == 0)
    def _():
        xp_s[0:8, :] = jnp.zeros((8, D_XBC), F32)
        ht_s[...] = jnp.zeros_like(ht_s)
        s_s[...] = jnp.zeros_like(s_s)

    xp_s[8:8 + tsteps, :] = xbc_ref[...]
    tril = jnp.where(causal, 1.0, 0.0).astype(BF16)
    expand = _head_expand()
    a_neg = -jnp.exp(alog_ref[...])

    for c in range(tsteps // CHUNK):
        r0 = c * CHUNK
        rows = slice(r0, r0 + CHUNK)
        acc = cb_ref[...]
        for tap in range(CONV_A):
            acc = acc + xp_s[r0 + 5 + tap:r0 + 5 + tap + CHUNK, :] * cw_ref[tap:tap + 1, :]
        xc = _silu(acc)
        xa = xc[:, :D_A]
        bm = xc[:, D_A:D_A + G_A * N_A]
        cm = xc[:, D_A + G_A * N_A:]
        dt = _softplus(dt_ref[rows, :] + dtb_ref[...])
        cum = _dot01(tril, dt * a_neg, "rhs")
        cum_t = cum.T
        dt_t = dt.T
        cum_last = cum[CHUNK - 1:CHUNK, :]
        expcum = jnp.exp(cum)
        w_end = dt * jnp.exp(cum_last - cum)
        scale = jnp.concatenate(
            [w_end, jnp.broadcast_to(jnp.exp(cum_last), (8, LANES))], axis=0)
        scale_x = _dot01(scale, expand, "lhs")
        w_x = scale_x[:CHUNK]
        dec_x = scale_x[CHUNK:CHUNK + 1]
        bm_t = bm.T
        ys = []
        for g in range(G_A):
            gl = slice(g * N_A, (g + 1) * N_A)
            hl = slice(g * HPG * P_A, (g + 1) * HPG * P_A)
            cg = cm[:, gl]
            scores = lax.dot_general(cg.astype(BF16), bm[:, gl].astype(BF16),
                                     (((1,), (1,)), ((), ())), preferred_element_type=F32)
            scores = jnp.where(causal, scores, 0.0)
            ht_g = ht_s[:, hl]
            for hh in range(HPG):
                h = g * HPG + hh
                seg = cum[:, h:h + 1] - cum_t[h:h + 1, :]
                sp = scores * jnp.exp(jnp.minimum(seg, 0.0)) * dt_t[h:h + 1, :]
                cx = cg * expcum[:, h:h + 1]
                lhs = jnp.concatenate([sp, cx], axis=1)
                rhs = jnp.concatenate([xa[:, h * P_A:(h + 1) * P_A],
                                       ht_g[:, hh * P_A:(hh + 1) * P_A]], axis=0)
                ys.append(_bdot(lhs, rhs))
            ht_s[:, hl] = dec_x[:, hl] * ht_g + _bdot(bm_t[gl, :], xa[:, hl] * w_x[:, hl])
        y = jnp.concatenate(ys, axis=1) + dx_ref[...] * xa
        y = y * _silu(z_ref[rows, :])
        ya_ref[rows, :] = _rms_unit(y) * naw_ref[...]

        cos_f, sin_a, sin_b = cos_ref[rows, :], sina_ref[rows, :], sinb_ref[rows, :]
        qr = _rope(q_ref[rows, :], cos_f, sin_a, sin_b)
        kr = _rope(k_ref[rows, :], cos_f, sin_a, sin_b) * (DK_R ** -0.5)
        k_t = kr.T
        kd_t = k_t * kdect_s[...]
        qd = qr * qdec_s[...]
        for h in range(H_R):
            kl = slice(h * DK_R, (h + 1) * DK_R)
            vl = slice(h * DV_R, (h + 1) * DV_R)
            sc = _bdot(qr[:, kl], k_t[kl, :]) * intra_s[h]
            s_h = s_s[kl, :]
            v_h = v_ref[rows, vl]
            y_h = _bdot(jnp.concatenate([sc, qd[:, kl]], axis=1),
                        jnp.concatenate([v_h, s_h], axis=0))
            s_s[kl, :] = math.exp(CHUNK * _LOG_GAMMA[h]) * s_h + _bdot(kd_t[kl, :], v_h)
            yb_ref[rows, vl] = _silu(g_ref[rows, vl]) * _rms_unit(y_h)

    xp_s[5:8, :] = xp_s[tsteps + 5:tsteps + 8, :]
    cs_ref[...] = xp_s[5:8, :]

    @pl.when(i == nt - 1)
    def _():
        for h in range(H_A):
            hs_ref[h] = ht_s[:, h * P_A:(h + 1) * P_A].T
        for h in range(H_R):
            ss_ref[h] = s_s[h * DK_R:(h + 1) * DK_R, :]


def _layer_block(l, tail):
    zeros = (0,) * len(tail)
    return pl.BlockSpec((None, None) + tail, lambda b, *_: (l, b) + zeros)


def _mix_prompt(p, rot, lw, nb, seq, tsteps, l, prev):
    nt = seq // tsteps
    row = lambda w: pl.BlockSpec((tsteps, w), lambda b, i: (b * nt + i, 0))
    tab = pl.BlockSpec((tsteps, D_QK), lambda b, i: (i, 0))
    in_specs = [row(D_A), row(D_XBC), row(DT_PAD), row(D_QK), row(D_QK), row(D_RV), row(D_RV),
                tab, tab, tab, _const_spec((1, D_QK)),
                _const_spec((CONV_A, D_XBC)), _const_spec((1, D_XBC)), _const_spec((1, DT_PAD)),
                _const_spec((1, DT_PAD)), _const_spec((1, D_A)), _const_spec((1, D_A))]
    prev = () if prev is None else tuple(prev)
    in_specs += [pl.BlockSpec(memory_space=pl.ANY)] * len(prev)
    aliases = {len(in_specs) - len(prev) + k: 3 + k for k in range(len(prev))}
    out_specs = [row(D_A), row(D_RV),
                 pl.BlockSpec((None, CONV_A - 1, D_XBC), lambda b, i: (b, 0, 0)),
                 _layer_block(l, (H_A, P_A, N_A)), _layer_block(l, (H_R, DK_R, DV_R))]
    out_shape = [jax.ShapeDtypeStruct((nb * seq, D_A), F32),
                 jax.ShapeDtypeStruct((nb * seq, D_RV), F32),
                 jax.ShapeDtypeStruct((nb, CONV_A - 1, D_XBC), F32),
                 jax.ShapeDtypeStruct((DEPTH, nb, H_A, P_A, N_A), F32),
                 jax.ShapeDtypeStruct((DEPTH, nb, H_R, DK_R, DV_R), F32)]
    scratch = [pltpu.VMEM((8 + tsteps, D_XBC), F32), pltpu.VMEM((N_A, D_A), F32),
               pltpu.VMEM((D_QK, DV_R), F32), pltpu.VMEM((H_R, CHUNK, CHUNK), F32),
               pltpu.VMEM((CHUNK, D_QK), F32), pltpu.VMEM((D_QK, CHUNK), F32)]
    return pl.pallas_call(
        functools.partial(_mix_body, tsteps=tsteps, nt=nt, n_prev=len(prev)), grid=(nb, nt),
        in_specs=in_specs, out_specs=out_specs, out_shape=out_shape, scratch_shapes=scratch,
        input_output_aliases=aliases, compiler_params=_params(2), name="mix_prompt")(
            p["z"], p["xbc"], p["dt"], p["q"], p["k"], p["v"], p["g"],
            rot["cos"], rot["sin_a"], rot["sin_b"], rot["lgx"],
            lw["conv_a_w"], lw["conv_a_b"], lw["dt_bias"], lw["a_log"], lw["d_x"], lw["norm_a_w"],
            *prev)


def _s5_disc_body(are_ref, aim_ref, ldt_ref, bre_ref, bim_ref,
                  abre_ref, abim_ref, bbre_ref, bbim_ref):
    ar, ai = are_ref[...], aim_ref[...]
    dt = jnp.exp(ldt_ref[...])
    mag = jnp.exp(ar * dt)
    ab_re = mag * jnp.cos(ai * dt)
    ab_im = mag * jnp.sin(ai * dt)
    den = ar * ar + ai * ai
    num_re = ab_re - 1.0
    coef_re = (num_re * ar + ab_im * ai) / den
    coef_im = (ab_im * ar - num_re * ai) / den
    abre_ref[...] = ab_re
    abim_ref[...] = ab_im
    for c in range(GS_C):
        cl = slice(c * P_C, (c + 1) * P_C)
        br, bi = bre_ref[:, cl], bim_ref[:, cl]
        bbre_ref[:, cl] = coef_re * br - coef_im * bi
        bbim_ref[:, cl] = coef_re * bi + coef_im * br


def _s5_discretise(a_re, a_im, log_dt, b_re, b_im):
    gp = jax.ShapeDtypeStruct((G_C, P_C), F32)
    gcp = jax.ShapeDtypeStruct((G_C, GS_C * P_C), F32)
    b_t = lambda b: jnp.transpose(b, (0, 2, 1)).reshape(G_C, GS_C * P_C)
    ab_re, ab_im, bb_re, bb_im = pl.pallas_call(
        _s5_disc_body, out_shape=[gp, gp, gcp, gcp], name="s5_disc")(
            a_re, a_im, log_dt.reshape(G_C, 1), b_t(b_re), b_t(b_im))
    return ab_re, ab_im, bb_re.reshape(G_C, GS_C, P_C), bb_im.reshape(G_C, GS_C, P_C)


_S5_GB = LANES // GS_C
_S5_NBLK = G_C // _S5_GB
_S5_SB = _S5_GB * P_C


def _block_diag(m):
    g, r, c = m.shape
    m = m.reshape(_S5_NBLK, _S5_GB, r, c)
    eye = jnp.eye(_S5_GB, dtype=m.dtype)
    return jnp.einsum("jgrc,gk->jgrkc", m, eye).reshape(_S5_NBLK, _S5_GB * r, _S5_GB * c)


def _s5_body(u_ref, h0re_ref, h0im_ref, are_ref, aim_ref, bblk_ref, cre_ref, cim_ref,
             d_ref, wglu_ref, bglu_ref, yc_ref, hre_ref, him_ref,
             xre_s, xim_s, sre_s, sim_s, *, nb, tt):
    i = pl.program_id(0)
    rows = nb * tt

    @pl.when(i == 0)
    def _():
        sre_s[...] = h0re_ref[...]
        sim_s[...] = h0im_ref[...]

    u = u_ref[...]
    if tt > 1:
        u = pltpu.einshape("btc->tbc", u).reshape(rows, D_C)
    ub = u.astype(BF16)

    ys = []
    for j in range(_S5_NBLK):
        sl = slice(j * _S5_SB, (j + 1) * _S5_SB)
        bu = jnp.dot(ub[:, j * LANES:(j + 1) * LANES], bblk_ref[j], preferred_element_type=F32)
        xre_s[:, sl] = bu[:, :_S5_SB]
        xim_s[:, sl] = bu[:, _S5_SB:]
        ar = jnp.broadcast_to(are_ref[:, sl], (nb, _S5_SB))
        ai = jnp.broadcast_to(aim_ref[:, sl], (nb, _S5_SB))
        xr, xi = sre_s[:, sl], sim_s[:, sl]
        for t in range(tt):
            r = slice(t * nb, (t + 1) * nb)
            xr, xi = (ar * xr - ai * xi + xre_s[r, sl], ar * xi + ai * xr + xim_s[r, sl])
            xre_s[r, sl] = xr
            xim_s[r, sl] = xi
        sre_s[:, sl] = xr
        sim_s[:, sl] = xi
        ys.append(_bdot(xre_s[:, sl], cre_ref[j]) - _bdot(xim_s[:, sl], cim_ref[j]))
    y = jax.nn.gelu(jnp.concatenate(ys, axis=1) + d_ref[...] * u)
    y = y * jax.nn.sigmoid(_bdot(y, wglu_ref[...]) + bglu_ref[...])
    if tt > 1:
        y = pltpu.einshape("tbc->btc", y.reshape(tt, nb, D_C))
    yc_ref[...] = y
    hre_ref[...] = sre_s[...]
    him_ref[...] = sim_s[...]


def _s5(u, h0_re, h0_im, lw, nb, seq, tt):
    rows = tt * nb
    if tt > 1:
        assert nb % 8 == 0 and tt % 8 == 0
        u = u.reshape(nb, seq, D_C)
        io_spec = pl.BlockSpec((nb, tt, D_C), lambda i: (0, i, 0))
    else:
        io_spec = pl.BlockSpec((rows, D_C), lambda i: (i, 0))
    in_specs = [io_spec,
                _const_spec((nb, D_S5)), _const_spec((nb, D_S5)),
                _const_spec((1, D_S5)), _const_spec((1, D_S5)),
                _const_spec((_S5_NBLK, LANES, 2 * _S5_SB)),
                _const_spec((_S5_NBLK, _S5_SB, LANES)), _const_spec((_S5_NBLK, _S5_SB, LANES)),
                _const_spec((1, D_C)), _const_spec((D_C, D_C)), _const_spec((1, D_C))]
    st = pl.BlockSpec((nb, D_S5), lambda i: (0, 0))
    yc, h_re, h_im = pl.pallas_call(
        functools.partial(_s5_body, nb=nb, tt=tt), grid=(seq // tt,), in_specs=in_specs,
        out_specs=[io_spec, st, st],
        out_shape=[jax.ShapeDtypeStruct(u.shape, F32),
                   jax.ShapeDtypeStruct((nb, D_S5), F32), jax.ShapeDtypeStruct((nb, D_S5), F32)],
        scratch_shapes=[pltpu.VMEM((rows, D_S5), F32), pltpu.VMEM((rows, D_S5), F32),
                        pltpu.VMEM((nb, D_S5), F32), pltpu.VMEM((nb, D_S5), F32)],
        compiler_params=_params(1), name="s5")(
            u, h0_re, h0_im, lw["s5_ab_re"], lw["s5_ab_im"], lw["s5_bblk"],
            lw["s5_cre"], lw["s5_cim"], lw["s5_d"], lw["w_glu"], lw["b_glu"])
    return yc.reshape(nb * seq, D_C), h_re, h_im


def _merge_body(x_ref, ya_ref, yb_ref, yc_ref, gt_ref, wa_ref, wb_ref, wc_ref, wo_ref, o_ref):
    gt = jax.nn.sigmoid(gt_ref[...])
    merged = (gt[:, :D_MODEL] * _bdot(ya_ref[...], wa_ref[...])
              + gt[:, D_MODEL:2 * D_MODEL] * _bdot(yb_ref[...], wb_ref[...])
              + gt[:, 2 * D_MODEL:] * _bdot(yc_ref[...], wc_ref[...]))
    o_ref[...] = x_ref[...] + _bdot(merged, wo_ref[...])


def _merge(x2d, ya, yb, yc, gates, lw, nb, seq, tm):
    nt = seq // tm
    row = lambda w: pl.BlockSpec((tm, w), lambda b, i: (b * nt + i, 0))
    wspec = _const_spec((D_MODEL, D_MODEL))
    return pl.pallas_call(
        _merge_body, grid=(nb, nt),
        in_specs=[row(D_MODEL), row(D_A), row(D_RV), row(D_C),
                  row(3 * D_MODEL), wspec, wspec, wspec, wspec],
        out_specs=row(D_MODEL), out_shape=jax.ShapeDtypeStruct((nb * seq, D_MODEL), F32),
        compiler_params=_params(2), name="merge")(
            x2d, ya, yb, yc, gates,
            lw["w_br_a"], lw["w_br_b"], lw["w_br_c"], lw["w_out"])


_FFN_CW = 256


def _ffn_tail(x, up_s, act_s, cw_ref, cb_ref, wdn_ref, tm):
    for c in range(0, D_FF, _FFN_CW):
        def conv(c0):
            cl = slice(c0, c0 + _FFN_CW)
            y = cb_ref[:, cl]
            for tap in range(CONV_F):
                y = y + up_s[6 + tap:6 + tap + tm, cl] * cw_ref[tap:tap + 1, cl]
            return y
        act_s[:, c:c + _FFN_CW] = (_silu(conv(c)) * conv(D_FF + c)).astype(BF16)
    return x + jnp.dot(act_s[...], wdn_ref[...], preferred_element_type=F32)


def _ffn_prompt_body(x_ref, nw_ref, wup_ref, cw_ref, cb_ref, wdn_ref, nf_ref,
                     o_ref, cs_ref, up_s, act_s, *, tm, final):
    @pl.when(pl.program_id(1) == 0)
    def _():
        up_s[0:8, :] = jnp.zeros((8, 2 * D_FF), F32)

    x = x_ref[...]
    hf = (_rms_unit(x) * nw_ref[...]).astype(BF16)
    up_s[8:8 + tm, :] = jnp.dot(hf, wup_ref[...], preferred_element_type=F32)
    out = _ffn_tail(x, up_s, act_s, cw_ref, cb_ref, wdn_ref, tm)
    o_ref[...] = _rms_unit(out) * nf_ref[...] if final else out
    tail = up_s[tm + 6:tm + 8, :]
    cs_ref[...] = tail
    up_s[6:8, :] = tail


def _ffn_sample_body(x_ref, st_ref, nw_ref, wup_ref, cw_ref, cb_ref, wdn_ref, nf_ref,
                     o_ref, cs_ref, act_s, *, final):
    x = x_ref[...]
    hf = (_rms_unit(x) * nw_ref[...]).astype(BF16)
    up = jnp.dot(hf, wup_ref[...], preferred_element_type=F32)
    prev2, prev1 = st_ref[:, :2 * D_FF], st_ref[:, 2 * D_FF:]
    cs_ref[:, :2 * D_FF] = prev1
    cs_ref[:, 2 * D_FF:] = up
    for c in range(0, D_FF, _FFN_CW):
        def conv(c0):
            cl = slice(c0, c0 + _FFN_CW)
            return (cb_ref[:, cl] + prev2[:, cl] * cw_ref[0:1, cl] + prev1[:, cl] * cw_ref[1:2, cl]
                    + up[:, cl] * cw_ref[2:3, cl])
        act_s[:, c:c + _FFN_CW] = (_silu(conv(c)) * conv(D_FF + c)).astype(BF16)
    out = x + jnp.dot(act_s[...], wdn_ref[...], preferred_element_type=F32)
    o_ref[...] = _rms_unit(out) * nf_ref[...] if final else out


def _ffn(x2d, state, lw, nf, nb, seq, tm, final):
    wspecs = [_const_spec((1, D_MODEL)), _const_spec((D_MODEL, 2 * D_FF)),
              _const_spec((CONV_F, 2 * D_FF)), _const_spec((1, 2 * D_FF)),
              _const_spec((D_FF, D_MODEL)), _const_spec((1, D_MODEL))]
    wargs = (lw["norm_ffn_w"], lw["w_up"], lw["conv_f_w"], lw["conv_f_b"], lw["w_down"], nf)
    act = pltpu.VMEM((tm, D_FF), BF16)
    if state is None:
        nt = seq // tm
        row = pl.BlockSpec((tm, D_MODEL), lambda b, i: (b * nt + i, 0))
        return pl.pallas_call(
            functools.partial(_ffn_prompt_body, tm=tm, final=final), grid=(nb, nt),
            in_specs=[row] + wspecs,
            out_specs=[row, pl.BlockSpec((None, CONV_F - 1, 2 * D_FF), lambda b, i: (b, 0, 0))],
            out_shape=[jax.ShapeDtypeStruct((nb * seq, D_MODEL), F32),
                       jax.ShapeDtypeStruct((nb, CONV_F - 1, 2 * D_FF), F32)],
            scratch_shapes=[pltpu.VMEM((8 + tm, 2 * D_FF), F32), act],
            compiler_params=_params(2), name="ffn_prompt")(x2d, *wargs)
    sw = (CONV_F - 1) * 2 * D_FF
    return pl.pallas_call(
        functools.partial(_ffn_sample_body, final=final), grid=(nb // tm,),
        in_specs=[pl.BlockSpec((tm, D_MODEL), lambda i: (i, 0)),
                  pl.BlockSpec((tm, sw), lambda i: (i, 0))] + wspecs,
        out_specs=[pl.BlockSpec((tm, D_MODEL), lambda i: (i, 0)),
                   pl.BlockSpec((tm, sw), lambda i: (i, 0))],
        out_shape=[jax.ShapeDtypeStruct((nb, D_MODEL), F32), jax.ShapeDtypeStruct((nb, sw), F32)],
        scratch_shapes=[act], compiler_params=_params(1), name="ffn_sample")(x2d, state, *wargs)


def _ssd_sample_body(*refs, n_prev):
    (z_ref, xbc_ref, dt_ref, cst_ref, cw_ref, cb_ref, dtb_ref, alog_ref, dx_ref, naw_ref,
     hin_ref) = refs[:11]
    (ya_ref, cso_ref, hout_ref, xa_s, xdt_t_s, dec_t_s, bm_t_s, cm_t_s, yt_s) = refs[11 + n_prev:]
    h = pl.program_id(0)

    @pl.when(h == 0)
    def _():
        acc = cb_ref[...]
        for tap in range(CONV_A - 1):
            acc = acc + cst_ref[:, tap * D_XBC:(tap + 1) * D_XBC] * cw_ref[tap:tap + 1, :]
        acc = acc + xbc_ref[...] * cw_ref[CONV_A - 1:CONV_A, :]
        cso_ref[:, :(CONV_A - 2) * D_XBC] = cst_ref[:, D_XBC:]
        cso_ref[:, (CONV_A - 2) * D_XBC:] = xbc_ref[...]
        xc = _silu(acc)
        xa = xc[:, :D_A]
        xa_s[...] = xa
        bm_t_s[...] = xc[:, D_A:D_A + G_A * N_A].T
        cm_t_s[...] = xc[:, D_A + G_A * N_A:].T
        dt = _softplus(dt_ref[...] + dtb_ref[...])
        dec_t_s[...] = jnp.exp(dt * -jnp.exp(alog_ref[...])).T
        dt_x = _dot01(dt, _head_expand(), "lhs")
        xdt_t_s[...] = (xa * dt_x).T

    g0 = pl.multiple_of((h // HPG) * N_A, N_A)
    p0 = pl.multiple_of(h * P_A, P_A)
    b_t = bm_t_s[pl.ds(g0, N_A), :]
    c_t = cm_t_s[pl.ds(g0, N_A), :]
    dec = dec_t_s[pl.ds(h, 1), :]

    def per_8p(k, carry):
        r = pl.ds(pl.multiple_of(p0 + k * 8, 8), 8)
        x_rows = xdt_t_s[r, :]
        ys = []
        for u in range(8):
            h_new = hin_ref[k * 8 + u] * dec + x_rows[u:u + 1, :] * b_t
            hout_ref[k * 8 + u] = h_new
            ys.append(jnp.sum(h_new * c_t, axis=0, keepdims=True))
        yt_s[r, :] = jnp.concatenate(ys, axis=0)
        return carry

    lax.fori_loop(0, P_A // 8, per_8p, 0)

    @pl.when(h == H_A - 1)
    def _():
        xa = xa_s[...]
        y = yt_s[...].T + dx_ref[...] * xa
        y = y * _silu(z_ref[...])
        ya_ref[...] = _rms_unit(y) * naw_ref[...]


def _ssd_sample(p, conv_st, h_ssd_t, lw, nb, l, prev):
    full = lambda w: _const_spec((nb, w))
    cw3 = (CONV_A - 1) * D_XBC
    st = pl.BlockSpec((None, None, P_A, N_A, nb), lambda h: (l, h, 0, 0, 0))
    in_specs = [full(D_A), full(D_XBC), full(DT_PAD), full(cw3),
                _const_spec((CONV_A, D_XBC)), _const_spec((1, D_XBC)), _const_spec((1, DT_PAD)),
                _const_spec((1, DT_PAD)), _const_spec((1, D_A)), _const_spec((1, D_A)), st]
    prev = () if prev is None else (prev,)
    in_specs += [pl.BlockSpec(memory_space=pl.ANY)] * len(prev)
    aliases = {len(in_specs) - 1: 2} if prev else {}
    keep = lambda w: pl.BlockSpec((nb, w), lambda h: (0, 0))
    scratch = [pltpu.VMEM((nb, D_A), F32), pltpu.VMEM((D_A, nb), F32), pltpu.VMEM((LANES, nb), F32),
               pltpu.VMEM((G_A * N_A, nb), F32), pltpu.VMEM((G_A * N_A, nb), F32),
               pltpu.VMEM((D_A, nb), F32)]
    ya, cs, hs = pl.pallas_call(
        functools.partial(_ssd_sample_body, n_prev=len(prev)), grid=(H_A,),
        in_specs=in_specs, out_specs=[keep(D_A), keep(cw3), st],
        out_shape=[jax.ShapeDtypeStruct((nb, D_A), F32), jax.ShapeDtypeStruct((nb, cw3), F32),
                   jax.ShapeDtypeStruct((DEPTH, H_A, P_A, N_A, nb), F32)],
        scratch_shapes=scratch, input_output_aliases=aliases,
        compiler_params=_params(1), name="ssd_sample")(
            p["z"], p["xbc"], p["dt"], conv_st.reshape(nb, cw3),
            lw["conv_a_w"], lw["conv_a_b"], lw["dt_bias"], lw["a_log"], lw["d_x"], lw["norm_a_w"],
            h_ssd_t, *prev)
    return ya, cs.reshape(nb, CONV_A - 1, D_XBC), hs


_RET_BB = 8


def _ret_sample_body(*refs, nb, n_prev):
    q_ref, k_ref, v_ref, g_ref, cos_ref, sina_ref, sinb_ref, sin_ref = refs[:8]
    yb_ref, sout_ref, q_t_s, k_t_s, yr_s = refs[8 + n_prev:]
    i = pl.program_id(0)

    @pl.when(i == 0)
    def _():
        cos_f, sin_a, sin_b = cos_ref[...], sina_ref[...], sinb_ref[...]
        q_t_s[...] = _rope(q_ref[...], cos_f, sin_a, sin_b).T.astype(BF16)
        k_t_s[...] = (_rope(k_ref[...], cos_f, sin_a, sin_b) * (DK_R ** -0.5)).T.astype(BF16)

    rows_i = lax.broadcasted_iota(jnp.int32, (nb, DV_R), 0)
    blk = pl.ds(pl.multiple_of(i * _RET_BB, _RET_BB), _RET_BB)
    v_blk = v_ref[blk, :]
    ys = [[] for _ in range(H_R)]
    for j in range(_RET_BB):
        onehot = jnp.where(rows_i == i * _RET_BB + j, 1.0, 0.0).astype(BF16)
        k_bc = jnp.dot(k_t_s[...], onehot, preferred_element_type=F32)
        q_bc = jnp.dot(q_t_s[...], onehot, preferred_element_type=F32)
        for h in range(H_R):
            kl = slice(h * DK_R, (h + 1) * DK_R)
            vl = slice(h * DV_R, (h + 1) * DV_R)
            s_new = math.exp(_LOG_GAMMA[h]) * sin_ref[j, h] + k_bc[kl, :] * v_blk[j:j + 1, vl]
            sout_ref[j, h] = s_new
            ys[h].append(jnp.sum(q_bc[kl, :] * s_new, axis=0, keepdims=True))
    for h in range(H_R):
        yr_s[blk, h * DV_R:(h + 1) * DV_R] = jnp.concatenate(ys[h], axis=0)

    @pl.when(i == nb // _RET_BB - 1)
    def _():
        for h in range(H_R):
            vl = slice(h * DV_R, (h + 1) * DV_R)
            yb_ref[:, vl] = _silu(g_ref[:, vl]) * _rms_unit(yr_s[:, vl])


def _ret_sample(p, s_ret, rot, nb, l, prev):
    full = lambda w: _const_spec((nb, w))
    st = pl.BlockSpec((None, _RET_BB, H_R, DK_R, DV_R), lambda i: (l, i, 0, 0, 0))
    in_specs = [full(D_QK), full(D_QK), full(D_RV), full(D_RV),
                _const_spec((1, D_QK)), _const_spec((1, D_QK)), _const_spec((1, D_QK)), st]
    prev = () if prev is None else (prev,)
    in_specs += [pl.BlockSpec(memory_space=pl.ANY)] * len(prev)
    aliases = {len(in_specs) - 1: 1} if prev else {}
    return pl.pallas_call(
        functools.partial(_ret_sample_body, nb=nb, n_prev=len(prev)), grid=(nb // _RET_BB,),
        in_specs=in_specs, out_specs=[pl.BlockSpec((nb, D_RV), lambda i: (0, 0)), st],
        out_shape=[jax.ShapeDtypeStruct((nb, D_RV), F32),
                   jax.ShapeDtypeStruct((DEPTH, nb, H_R, DK_R, DV_R), F32)],
        scratch_shapes=[pltpu.VMEM((D_QK, nb), BF16), pltpu.VMEM((D_QK, nb), BF16),
                        pltpu.VMEM((nb, D_RV), F32)],
        input_output_aliases=aliases, compiler_params=_params(1), name="ret_sample")(
            p["q"], p["k"], p["v"], p["g"], rot["cos"], rot["sin_a"], rot["sin_b"], s_ret, *prev)


def _rotary_tables(start, length):
    half = DK_R // 2
    pos = start + jnp.arange(length, dtype=F32)
    freqs = ROPE_BASE ** (-jnp.arange(half, dtype=F32) / half)
    ang = pos[:, None] * freqs[None, :]
    cos, sin, zero = jnp.cos(ang), jnp.sin(ang), jnp.zeros_like(ang)
    tile = lambda a, b: jnp.tile(jnp.concatenate([a, b], axis=-1), (1, H_R))
    lg = jnp.log1p(-jnp.exp2(-5.0 - jnp.arange(H_R, dtype=F32)))
    return {"cos": tile(cos, cos), "sin_a": tile(-sin, zero), "sin_b": tile(zero, sin),
            "lgx": jnp.repeat(lg, DK_R)[None, :]}


def _layer_weights(l, w):
    w_in = w["w_in"][l]
    offs = np.cumsum([0, D_A, D_XBC, H_A, D_QK, D_QK, D_RV, D_RV, D_C, 3 * D_MODEL])
    cols = [w_in[:, offs[i]:offs[i + 1]] for i in range(9)]
    cols[2] = jnp.pad(cols[2], ((0, 0), (0, DT_PAD - H_A)))
    pad_row = lambda a: jnp.pad(a, (0, DT_PAD - H_A))[None, :]
    ab_re, ab_im, bb_re, bb_im = _s5_discretise(
        w["s5_a_re"][l], w["s5_a_im"][l], w["s5_log_dt"][l], w["s5_b_re"][l], w["s5_b_im"][l])
    bblk = jnp.concatenate([_block_diag(bb_re), _block_diag(bb_im)], axis=-1)
    c_t = lambda c: _block_diag(jnp.transpose(c, (0, 2, 1)))
    return {
        "norm_mix_w": w["norm_mix_w"][l][None, :],
        "w_in": [c.astype(BF16) for c in cols],
        "conv_a_w": w["conv_a_w"][l], "conv_a_b": w["conv_a_b"][l][None, :],
        "dt_bias": pad_row(w["dt_bias"][l]), "a_log": pad_row(w["a_log"][l]),
        "d_x": jnp.repeat(w["d_a"][l], P_A)[None, :], "norm_a_w": w["norm_a_w"][l][None, :],
        "s5_ab_re": ab_re.reshape(1, D_S5), "s5_ab_im": ab_im.reshape(1, D_S5),
        "s5_bblk": bblk.astype(BF16),
        "s5_cre": c_t(w["s5_c_re"][l]).astype(BF16), "s5_cim": c_t(w["s5_c_im"][l]).astype(BF16),
        "s5_d": w["s5_d"][l][None, :], "w_glu": w["w_glu"][l].astype(BF16),
        "b_glu": w["b_glu"][l][None, :],
        "w_br_a": w["w_br_a"][l].astype(BF16), "w_br_b": w["w_br_b"][l].astype(BF16),
        "w_br_c": w["w_br_c"][l].astype(BF16), "w_out": w["w_out"][l].astype(BF16),
        "norm_ffn_w": w["norm_ffn_w"][l][None, :], "w_up": w["w_up"][l].astype(BF16),
        "conv_f_w": w["conv_f_w"][l], "conv_f_b": w["conv_f_b"][l][None, :],
        "w_down": w["w_down"][l].astype(BF16),
    }


def _prompt_trunk(x, lws, nf, tm_proj, tsteps, tt, tm_merge, tm_ffn):
    nb, seq, _ = x.shape
    rot = _rotary_tables(0.0, seq)
    x2d = x.reshape(nb * seq, D_MODEL)
    zeros = jnp.zeros((nb, D_S5), F32)
    states, big = [], None
    for l, lw in enumerate(lws):
        p = _inproj(x2d, lw["norm_mix_w"], lw["w_in"], nb, seq, tm_proj)
        ya, yb, cs, *big = _mix_prompt(p, rot, lw, nb, seq, tsteps, l, big)
        yc, h_re, h_im = _s5(p["u"], zeros, zeros, lw, nb, seq, tt)
        x2d = _merge(x2d, ya, yb, yc, p["gates"], lw, nb, seq, tm_merge)
        x2d, fs = _ffn(x2d, None, lw, nf, nb, seq, tm_ffn, final=(l == len(lws) - 1))
        states.append((cs, h_re.reshape(nb, G_C, P_C), h_im.reshape(nb, G_C, P_C), fs))
    cs, h_re, h_im, fs = [jnp.stack(s) for s in zip(*states)]
    return x2d.reshape(nb, seq, D_MODEL), [cs, big[0], big[1], h_re, h_im, fs]


def _sample_trunk(x, st, lws, nf):
    nb = x.shape[0]
    rot = _rotary_tables(float(PAST_LEN), 1)
    x2d = x.reshape(nb, D_MODEL)
    st_conv, st_ssd, st_ret, st_re, st_im, st_ffn = st
    st_ssd_t = jnp.transpose(st_ssd, (0, 2, 3, 4, 1))
    states, hs, ss = [], None, None
    for l, lw in enumerate(lws):
        p = _inproj(x2d, lw["norm_mix_w"], lw["w_in"], 1, nb, nb)
        ya, cs, hs = _ssd_sample(p, st_conv[l], st_ssd_t, lw, nb, l, hs)
        yb, ss = _ret_sample(p, st_ret, rot, nb, l, ss)
        yc, h_re, h_im = _s5(p["u"], st_re[l].reshape(nb, D_S5), st_im[l].reshape(nb, D_S5),
                             lw, nb, 1, 1)
        x2d = _merge(x2d, ya, yb, yc, p["gates"], lw, 1, nb, nb)
        x2d, fs = _ffn(x2d, st_ffn[l].reshape(nb, -1), lw, nf, nb, 1, nb,
                       final=(l == len(lws) - 1))
        states.append((cs, h_re.reshape(nb, G_C, P_C), h_im.reshape(nb, G_C, P_C),
                       fs.reshape(nb, CONV_F - 1, 2 * D_FF)))
    cs, h_re, h_im, fs = [jnp.stack(s) for s in zip(*states)]
    hs = jnp.transpose(hs, (0, 4, 1, 2, 3))
    return x2d.reshape(nb, 1, D_MODEL), [cs, hs, ss, h_re, h_im, fs]


def kernel(x_prompt, x_sample, state_ssd_conv, state_ssd, state_ret, state_s5_re, state_s5_im,
           state_ffn_conv, norm_mix_w, w_in, conv_a_w, conv_a_b, dt_bias, a_log, d_a, norm_a_w,
           s5_a_re, s5_a_im, s5_log_dt, s5_b_re, s5_b_im, s5_c_re, s5_c_im, s5_d, w_glu, b_glu,
           w_br_a, w_br_b, w_br_c, w_out, norm_ffn_w, w_up, conv_f_w, conv_f_b, w_down, norm_f_w):
    w = dict(norm_mix_w=norm_mix_w, w_in=w_in, conv_a_w=conv_a_w, conv_a_b=conv_a_b,
             dt_bias=dt_bias, a_log=a_log, d_a=d_a, norm_a_w=norm_a_w, s5_a_re=s5_a_re,
             s5_a_im=s5_a_im, s5_log_dt=s5_log_dt, s5_b_re=s5_b_re, s5_b_im=s5_b_im,
             s5_c_re=s5_c_re, s5_c_im=s5_c_im, s5_d=s5_d, w_glu=w_glu, b_glu=b_glu,
             w_br_a=w_br_a, w_br_b=w_br_b, w_br_c=w_br_c, w_out=w_out, norm_ffn_w=norm_ffn_w,
             w_up=w_up, conv_f_w=conv_f_w, conv_f_b=conv_f_b, w_down=w_down)
    lws = [_layer_weights(l, w) for l in range(DEPTH)]
    nf = norm_f_w[None, :]
    seq = x_prompt.shape[1]
    y_p, p_st = _prompt_trunk(x_prompt, lws, nf, tm_proj=min(256, seq), tsteps=min(256, seq),
                              tt=min(32, seq), tm_merge=min(512, seq), tm_ffn=min(256, seq))
    y_s, s_st = _sample_trunk(
        x_sample, (state_ssd_conv, state_ssd, state_ret, state_s5_re, state_s5_im, state_ffn_conv),
        lws, nf)
    return (y_p, y_s, *p_st, *s_st)
```

```python
import functools
import math

import jax
import jax.numpy as jnp
import numpy as np
from jax import lax
from jax.experimental import pallas as pl
from jax.experimental.pallas import tpu as pltpu

F32 = jnp.float32
BF16 = jnp.bfloat16

D_MODEL = 1024
DEPTH = 2
PAST_LEN = 16384
H_A, P_A, N_A, G_A, CONV_A = 16, 64, 64, 2, 4
D_A = H_A * P_A
D_XBC = D_A + 2 * G_A * N_A
H_R, DK_R, DV_R = 8, 64, 128
D_QK = H_R * DK_R
D_RV = H_R * DV_R
ROPE_BASE = 10000.0
GS_C, G_C, P_C = 16, 64, 64
D_C = G_C * GS_C
D_S5 = G_C * P_C
D_FF = 2816
CONV_F = 3
CHUNK = 128
EPS = 1e-6
LANES = 128
DT_PAD = LANES
HPG = H_A // G_A
VMEM_LIMIT = 56 * 1024 * 1024

_LOG_GAMMA = [math.log1p(-(2.0 ** (-5.0 - h))) for h in range(H_R)]

_SEGS = (("z", D_A), ("xbc", D_XBC), ("dt", DT_PAD), ("q", D_QK), ("k", D_QK),
         ("v", D_RV), ("g", D_RV), ("u", D_C), ("gates", 3 * D_MODEL))


def _rms_unit(x):
    return x * lax.rsqrt(jnp.mean(x * x, axis=-1, keepdims=True) + EPS)


def _silu(x):
    return x * jax.nn.sigmoid(x)


def _softplus(x):
    return jnp.maximum(x, 0.0) + jnp.log1p(jnp.exp(-jnp.abs(x)))


def _bdot(a, b):
    return jnp.dot(a.astype(BF16), b.astype(BF16), preferred_element_type=F32)


def _split3(x):
    hi = x.astype(BF16)
    r1 = x - hi.astype(F32)
    mid = r1.astype(BF16)
    return hi, mid, (r1 - mid.astype(F32)).astype(BF16)


def _tile3(m01, axis):
    return jnp.concatenate([m01] * 3, axis=axis)


def _dot01(a, b, f32_side):
    if f32_side == "lhs":
        return jnp.dot(jnp.concatenate(_split3(a), axis=1), b, preferred_element_type=F32)
    return jnp.dot(a, jnp.concatenate(_split3(b), axis=0), preferred_element_type=F32)


def _const_spec(shape):
    nd = len(shape)
    return pl.BlockSpec(shape, lambda *_: (0,) * nd, pipeline_mode=pl.Buffered(1))


def _params(n_grid):
    return pltpu.CompilerParams(dimension_semantics=("arbitrary",) * n_grid,
                                vmem_limit_bytes=VMEM_LIMIT)


def _inproj_body(x_ref, nw_ref, cw_ref, cb_ref, *refs, tm, conv):
    n = len(_SEGS)
    h = (_rms_unit(x_ref[...]) * nw_ref[...]).astype(BF16)
    for (name, _), w_ref, o_ref in zip(_SEGS, refs[:n], refs[n:2 * n]):
        y = jnp.dot(h, w_ref[...], preferred_element_type=F32)
        if name in ("z", "g"):
            y = _silu(y)
        elif name == "gates":
            y = jax.nn.sigmoid(y)
        elif name == "xbc" and conv:
            cs_ref, xp_s = refs[2 * n:]

            @pl.when(pl.program_id(1) == 0)
            def _():
                xp_s[0:8, :] = jnp.zeros((8, D_XBC), F32)

            xp_s[8:8 + tm, :] = y
            y = cb_ref[...]
            for tap in range(CONV_A):
                y = y + xp_s[5 + tap:5 + tap + tm, :] * cw_ref[tap:tap + 1, :]
            y = _silu(y)
            tail = xp_s[tm + 5:tm + 8, :]
            cs_ref[...] = tail
            xp_s[5:8, :] = tail
        o_ref[...] = y


def _inproj(x2d, nw, lw, nb, seq, tm, conv):
    nt = seq // tm
    ws = lw["w_in"]
    in_specs = [pl.BlockSpec((tm, D_MODEL), lambda b, i: (b * nt + i, 0)),
                _const_spec((1, D_MODEL)), _const_spec((CONV_A, D_XBC)), _const_spec((1, D_XBC))]
    in_specs += [_const_spec(w.shape) for w in ws]
    out_specs, out_shape, scratch = [], [], []
    for _, width in _SEGS:
        out_specs.append(pl.BlockSpec((tm, width), lambda b, i: (b * nt + i, 0)))
        out_shape.append(jax.ShapeDtypeStruct((nb * seq, width), F32))
    if conv:
        out_specs.append(pl.BlockSpec((None, CONV_A - 1, D_XBC), lambda b, i: (b, 0, 0)))
        out_shape.append(jax.ShapeDtypeStruct((nb, CONV_A - 1, D_XBC), F32))
        scratch.append(pltpu.VMEM((8 + tm, D_XBC), F32))
    outs = pl.pallas_call(
        functools.partial(_inproj_body, tm=tm, conv=conv), grid=(nb, nt), in_specs=in_specs,
        out_specs=out_specs, out_shape=out_shape, scratch_shapes=scratch,
        compiler_params=_params(2), name="inproj")(
            x2d, nw, lw["conv_a_w"], lw["conv_a_b"], *ws)
    res = {name: o for (name, _), o in zip(_SEGS, outs)}
    if conv:
        res["conv_state"] = outs[-1]
    return res


def _rope(x, cos_f, sin_a, sin_b):
    half = DK_R // 2
    return (x * cos_f + pltpu.roll(x, D_QK - half, 1) * sin_a
            + pltpu.roll(x, half, 1) * sin_b)


def _head_expand():
    lo = lax.broadcasted_iota(jnp.int32, (LANES, D_A), 0) * P_A
    c = lax.broadcasted_iota(jnp.int32, (LANES, D_A), 1)
    return jnp.where(c >= lo, jnp.where(c < lo + P_A, 1.0, 0.0), 0.0).astype(BF16)


def _mix_body(*refs, tsteps, nt, n_prev):
    (zs_ref, xc_ref, dt_ref, q_ref, k_ref, v_ref, gs_ref, cos_ref, sina_ref, sinb_ref, lgx_ref,
     dtb_ref, alog_ref, dx_ref, naw_ref) = refs[:15]
    (ya_ref, yb_ref, hs_ref, ss_ref,
     ht_s, s_s, intra_s, qdec_s, kdect_s) = refs[15 + n_prev:]
    i = pl.program_id(1)
    first = jnp.logical_and(pl.program_id(0) == 0, i == 0)
    rows_i = lax.broadcasted_iota(jnp.int32, (CHUNK, CHUNK), 0)
    cols_i = lax.broadcasted_iota(jnp.int32, (CHUNK, CHUNK), 1)
    causal = rows_i >= cols_i
    lane_lo = cols_i < LANES // 2

    @pl.when(first)
    def _():
        rel = (rows_i - cols_i).astype(F32)
        for h in range(H_R):
            intra_s[h] = jnp.where(causal, jnp.exp(jnp.maximum(rel, 0.0) * _LOG_GAMMA[h]), 0.0)
        ri = lax.broadcasted_iota(jnp.int32, (CHUNK, D_QK), 0).astype(F32)
        lgx = lgx_ref[...]
        qdec_s[...] = jnp.exp((ri + 1.0) * lgx)
        kdect_s[...] = jnp.exp((CHUNK - 1.0 - ri) * lgx).T

    @pl.when(i == 0)
    def _():
        ht_s[...] = jnp.zeros_like(ht_s)
        s_s[...] = jnp.zeros_like(s_s)

    tril = _tile3(jnp.where(causal, 1.0, 0.0).astype(BF16), 1)
    expand = _tile3(_head_expand(), 0)
    a_neg = -jnp.exp(alog_ref[...])

    for c in range(tsteps // CHUNK):
        r0 = c * CHUNK
        rows = slice(r0, r0 + CHUNK)
        xa = xc_ref[rows, :D_A]
        bm = xc_ref[rows, D_A:D_A + G_A * N_A]
        cm = xc_ref[rows, D_A + G_A * N_A:]
        dt = _softplus(dt_ref[rows, :] + dtb_ref[...])
        cum = _dot01(tril, dt * a_neg, "rhs")
        cum_t = cum.T
        dt_t = dt.T
        cum_last = cum[CHUNK - 1:CHUNK, :]
        scale = jnp.concatenate(
            [dt * jnp.exp(cum_last - cum), jnp.exp(cum),
             jnp.broadcast_to(jnp.exp(cum_last), (8, LANES))], axis=0)
        scale_x = _dot01(scale, expand, "lhs")
        w_x = scale_x[:CHUNK]
        ecum_x = scale_x[CHUNK:2 * CHUNK]
        dec_x = scale_x[2 * CHUNK:2 * CHUNK + 1]
        bm_t = bm.T
        ys = []
        for g in range(G_A):
            gl = slice(g * N_A, (g + 1) * N_A)
            hl = slice(g * HPG * P_A, (g + 1) * HPG * P_A)
            cg = cm[:, gl]
            scores = lax.dot_general(cg.astype(BF16), bm[:, gl].astype(BF16),
                                     (((1,), (1,)), ((), ())), preferred_element_type=F32)
            scores = jnp.where(causal, scores, 0.0)
            ht_g = ht_s[:, hl]
            y_inter = _bdot(cg, ht_g) * ecum_x[:, hl]
            for pair in range(HPG // 2):
                pl_ = slice(g * HPG * P_A + pair * LANES, g * HPG * P_A + (pair + 1) * LANES)
                x_pair = xa[:, pl_]
                sps = []
                for h in (g * HPG + 2 * pair, g * HPG + 2 * pair + 1):
                    seg = cum[:, h:h + 1] - cum_t[h:h + 1, :]
                    sps.append(scores * jnp.exp(jnp.minimum(seg, 0.0)) * dt_t[h:h + 1, :])
                rhs = jnp.concatenate([jnp.where(lane_lo, x_pair, 0.0),
                                       jnp.where(lane_lo, 0.0, x_pair)], axis=0)
                ys.append(_bdot(jnp.concatenate(sps, axis=1), rhs)
                          + y_inter[:, pair * LANES:(pair + 1) * LANES])
            ht_s[:, hl] = dec_x[:, hl] * ht_g + _bdot(bm_t[gl, :], xa[:, hl] * w_x[:, hl])
        y = jnp.concatenate(ys, axis=1) + dx_ref[...] * xa
        y = y * zs_ref[rows, :]
        ya_ref[rows, :] = _rms_unit(y) * naw_ref[...]

        cos_f, sin_a, sin_b = cos_ref[rows, :], sina_ref[rows, :], sinb_ref[rows, :]
        qr = _rope(q_ref[rows, :], cos_f, sin_a, sin_b)
        kr = _rope(k_ref[rows, :], cos_f, sin_a, sin_b) * (DK_R ** -0.5)
        k_t = kr.T
        kd_t = k_t * kdect_s[...]
        qd = qr * qdec_s[...]
        for pair in range(H_R // 2):
            pl_ = slice(pair * LANES, (pair + 1) * LANES)
            s_pair = s_s[pl_, :]
            for h, keep in ((2 * pair, lane_lo), (2 * pair + 1, jnp.logical_not(lane_lo))):
                kl = slice(h * DK_R, (h + 1) * DK_R)
                vl = slice(h * DV_R, (h + 1) * DV_R)
                v_h = v_ref[rows, vl]
                sc = _bdot(jnp.where(keep, qr[:, pl_], 0.0), k_t[pl_, :]) * intra_s[h]
                y_h = _bdot(jnp.concatenate([sc, jnp.where(keep, qd[:, pl_], 0.0)], axis=1),
                            jnp.concatenate([v_h, s_pair], axis=0))
                s_s[kl, :] = (math.exp(CHUNK * _LOG_GAMMA[h]) * s_s[kl, :]
                              + _bdot(kd_t[kl, :], v_h))
                yb_ref[rows, vl] = gs_ref[rows, vl] * _rms_unit(y_h)

    @pl.when(i == nt - 1)
    def _():
        for h in range(H_A):
            hs_ref[h] = ht_s[:, h * P_A:(h + 1) * P_A].T
        for h in range(H_R):
            ss_ref[h] = s_s[h * DK_R:(h + 1) * DK_R, :]


def _layer_block(l, tail):
    zeros = (0,) * len(tail)
    return pl.BlockSpec((None, None) + tail, lambda b, *_: (l, b) + zeros)


def _mix_prompt(p, rot, lw, nb, seq, tsteps, l, prev):
    nt = seq // tsteps
    row = lambda w: pl.BlockSpec((tsteps, w), lambda b, i: (b * nt + i, 0))
    tab = pl.BlockSpec((tsteps, D_QK), lambda b, i: (i, 0))
    in_specs = [row(D_A), row(D_XBC), row(DT_PAD), row(D_QK), row(D_QK), row(D_RV), row(D_RV),
                tab, tab, tab, _const_spec((1, D_QK)), _const_spec((1, DT_PAD)),
                _const_spec((1, DT_PAD)), _const_spec((1, D_A)), _const_spec((1, D_A))]
    prev = () if prev is None else tuple(prev)
    in_specs += [pl.BlockSpec(memory_space=pl.ANY)] * len(prev)
    aliases = {len(in_specs) - len(prev) + k: 2 + k for k in range(len(prev))}
    out_specs = [row(D_A), row(D_RV),
                 _layer_block(l, (H_A, P_A, N_A)), _layer_block(l, (H_R, DK_R, DV_R))]
    out_shape = [jax.ShapeDtypeStruct((nb * seq, D_A), F32),
                 jax.ShapeDtypeStruct((nb * seq, D_RV), F32),
                 jax.ShapeDtypeStruct((DEPTH, nb, H_A, P_A, N_A), F32),
                 jax.ShapeDtypeStruct((DEPTH, nb, H_R, DK_R, DV_R), F32)]
    scratch = [pltpu.VMEM((N_A, D_A), F32),
               pltpu.VMEM((D_QK, DV_R), F32), pltpu.VMEM((H_R, CHUNK, CHUNK), F32),
               pltpu.VMEM((CHUNK, D_QK), F32), pltpu.VMEM((D_QK, CHUNK), F32)]
    return pl.pallas_call(
        functools.partial(_mix_body, tsteps=tsteps, nt=nt, n_prev=len(prev)), grid=(nb, nt),
        in_specs=in_specs, out_specs=out_specs, out_shape=out_shape, scratch_shapes=scratch,
        input_output_aliases=aliases, compiler_params=_params(2), name="mix_prompt")(
            p["z"], p["xbc"], p["dt"], p["q"], p["k"], p["v"], p["g"],
            rot["cos"], rot["sin_a"], rot["sin_b"], rot["lgx"],
            lw["dt_bias"], lw["a_log"], lw["d_x"], lw["norm_a_w"], *prev)


def _s5_disc_body(are_ref, aim_ref, ldt_ref, bre_ref, bim_ref,
                  abre_ref, abim_ref, bbre_ref, bbim_ref):
    ar, ai = are_ref[...], aim_ref[...]
    dt = jnp.exp(ldt_ref[...])
    mag = jnp.exp(ar * dt)
    ab_re = mag * jnp.cos(ai * dt)
    ab_im = mag * jnp.sin(ai * dt)
    den = ar * ar + ai * ai
    num_re = ab_re - 1.0
    coef_re = (num_re * ar + ab_im * ai) / den
    coef_im = (ab_im * ar - num_re * ai) / den
    abre_ref[...] = ab_re
    abim_ref[...] = ab_im
    for c in range(GS_C):
        cl = slice(c * P_C, (c + 1) * P_C)
        br, bi = bre_ref[:, cl], bim_ref[:, cl]
        bbre_ref[:, cl] = coef_re * br - coef_im * bi
        bbim_ref[:, cl] = coef_re * bi + coef_im * br


def _s5_discretise(a_re, a_im, log_dt, b_re, b_im):
    gp = jax.ShapeDtypeStruct((G_C, P_C), F32)
    gcp = jax.ShapeDtypeStruct((G_C, GS_C * P_C), F32)
    b_t = lambda b: jnp.transpose(b, (0, 2, 1)).reshape(G_C, GS_C * P_C)
    ab_re, ab_im, bb_re, bb_im = pl.pallas_call(
        _s5_disc_body, out_shape=[gp, gp, gcp, gcp], name="s5_disc")(
            a_re, a_im, log_dt.reshape(G_C, 1), b_t(b_re), b_t(b_im))
    return ab_re, ab_im, bb_re.reshape(G_C, GS_C, P_C), bb_im.reshape(G_C, GS_C, P_C)


_S5_GB = LANES // GS_C
_S5_NBLK = G_C // _S5_GB
_S5_SB = _S5_GB * P_C


def _block_diag(m):
    g, r, c = m.shape
    m = m.reshape(_S5_NBLK, _S5_GB, r, c)
    eye = jnp.eye(_S5_GB, dtype=m.dtype)
    return jnp.einsum("jgrc,gk->jgrkc", m, eye).reshape(_S5_NBLK, _S5_GB * r, _S5_GB * c)


def _s5_body(u_ref, h0re_ref, h0im_ref, are_ref, aim_ref, bblk_ref, cre_ref, cim_ref,
             d_ref, wglu_ref, bglu_ref, yc_ref, hre_ref, him_ref,
             xre_s, xim_s, sre_s, sim_s, *, nb, tt):
    i = pl.program_id(0)
    rows = nb * tt

    @pl.when(i == 0)
    def _():
        sre_s[...] = h0re_ref[...]
        sim_s[...] = h0im_ref[...]

    u = u_ref[...]
    if tt > 1:
        u = jnp.swapaxes(u, 0, 1).reshape(rows, D_C)
    ub = u.astype(BF16)

    ys = []
    for j in range(_S5_NBLK):
        sl = slice(j * _S5_SB, (j + 1) * _S5_SB)
        bu = jnp.dot(ub[:, j * LANES:(j + 1) * LANES], bblk_ref[j], preferred_element_type=F32)
        xre_s[:, sl] = bu[:, :_S5_SB]
        xim_s[:, sl] = bu[:, _S5_SB:]
        ar = jnp.broadcast_to(are_ref[:, sl], (nb, _S5_SB))
        ai = jnp.broadcast_to(aim_ref[:, sl], (nb, _S5_SB))
        xr, xi = sre_s[:, sl], sim_s[:, sl]
        for t in range(tt):
            r = slice(t * nb, (t + 1) * nb)
            xr, xi = (ar * xr - ai * xi + xre_s[r, sl], ar * xi + ai * xr + xim_s[r, sl])
            xre_s[r, sl] = xr
            xim_s[r, sl] = xi
        sre_s[:, sl] = xr
        sim_s[:, sl] = xi
        ys.append(_bdot(xre_s[:, sl], cre_ref[j]) - _bdot(xim_s[:, sl], cim_ref[j]))
    y = jax.nn.gelu(jnp.concatenate(ys, axis=1) + d_ref[...] * u)
    y = y * jax.nn.sigmoid(_bdot(y, wglu_ref[...]) + bglu_ref[...])
    if tt > 1:
        y = jnp.swapaxes(y.reshape(tt, nb, D_C), 0, 1)
    yc_ref[...] = y
    hre_ref[...] = sre_s[...]
    him_ref[...] = sim_s[...]


def _s5(u, h0_re, h0_im, lw, nb, seq, tt):
    rows = tt * nb
    if tt > 1:
        assert nb % 8 == 0 and tt % 8 == 0
        u = u.reshape(nb, seq, D_C)
        io_spec = pl.BlockSpec((nb, tt, D_C), lambda i: (0, i, 0))
    else:
        io_spec = pl.BlockSpec((rows, D_C), lambda i: (i, 0))
    in_specs = [io_spec,
                _const_spec((nb, D_S5)), _const_spec((nb, D_S5)),
                _const_spec((1, D_S5)), _const_spec((1, D_S5)),
                _const_spec((_S5_NBLK, LANES, 2 * _S5_SB)),
                _const_spec((_S5_NBLK, _S5_SB, LANES)), _const_spec((_S5_NBLK, _S5_SB, LANES)),
                _const_spec((1, D_C)), _const_spec((D_C, D_C)), _const_spec((1, D_C))]
    st = pl.BlockSpec((nb, D_S5), lambda i: (0, 0))
    yc, h_re, h_im = pl.pallas_call(
        functools.partial(_s5_body, nb=nb, tt=tt), grid=(seq // tt,), in_specs=in_specs,
        out_specs=[io_spec, st, st],
        out_shape=[jax.ShapeDtypeStruct(u.shape, F32),
                   jax.ShapeDtypeStruct((nb, D_S5), F32), jax.ShapeDtypeStruct((nb, D_S5), F32)],
        scratch_shapes=[pltpu.VMEM((rows, D_S5), F32), pltpu.VMEM((rows, D_S5), F32),
                        pltpu.VMEM((nb, D_S5), F32), pltpu.VMEM((nb, D_S5), F32)],
        compiler_params=_params(1), name="s5")(
            u, h0_re, h0_im, lw["s5_ab_re"], lw["s5_ab_im"], lw["s5_bblk"],
            lw["s5_cre"], lw["s5_cim"], lw["s5_d"], lw["w_glu"], lw["b_glu"])
    return yc.reshape(nb * seq, D_C), h_re, h_im


def _merge_body(x_ref, ya_ref, yb_ref, yc_ref, gt_ref, wa_ref, wb_ref, wc_ref, wo_ref, o_ref):
    merged = (gt_ref[:, :D_MODEL] * _bdot(ya_ref[...], wa_ref[...])
              + gt_ref[:, D_MODEL:2 * D_MODEL] * _bdot(yb_ref[...], wb_ref[...])
              + gt_ref[:, 2 * D_MODEL:] * _bdot(yc_ref[...], wc_ref[...]))
    o_ref[...] = x_ref[...] + _bdot(merged, wo_ref[...])


def _merge(x2d, ya, yb, yc, gates, lw, nb, seq, tm):
    nt = seq // tm
    row = lambda w: pl.BlockSpec((tm, w), lambda b, i: (b * nt + i, 0))
    wspec = _const_spec((D_MODEL, D_MODEL))
    return pl.pallas_call(
        _merge_body, grid=(nb, nt),
        in_specs=[row(D_MODEL), row(D_A), row(D_RV), row(D_C),
                  row(3 * D_MODEL), wspec, wspec, wspec, wspec],
        out_specs=row(D_MODEL), out_shape=jax.ShapeDtypeStruct((nb * seq, D_MODEL), F32),
        compiler_params=_params(2), name="merge")(
            x2d, ya, yb, yc, gates,
            lw["w_br_a"], lw["w_br_b"], lw["w_br_c"], lw["w_out"])


_FFN_CW = 256


def _ffn_tail(x, up_s, act_s, cw_ref, cb_ref, wdn_ref, tm):
    for c in range(0, D_FF, _FFN_CW):
        def conv(c0):
            cl = slice(c0, c0 + _FFN_CW)
            y = cb_ref[:, cl]
            for tap in range(CONV_F):
                y = y + up_s[6 + tap:6 + tap + tm, cl] * cw_ref[tap:tap + 1, cl]
            return y
        act_s[:, c:c + _FFN_CW] = (_silu(conv(c)) * conv(D_FF + c)).astype(BF16)
    return x + jnp.dot(act_s[...], wdn_ref[...], preferred_element_type=F32)


def _ffn_prompt_body(x_ref, nw_ref, wup_ref, cw_ref, cb_ref, wdn_ref, nf_ref,
                     o_ref, cs_ref, up_s, act_s, *, tm, final):
    @pl.when(pl.program_id(1) == 0)
    def _():
        up_s[0:8, :] = jnp.zeros((8, 2 * D_FF), F32)

    x = x_ref[...]
    hf = (_rms_unit(x) * nw_ref[...]).astype(BF16)
    up_s[8:8 + tm, :] = jnp.dot(hf, wup_ref[...], preferred_element_type=F32)
    out = _ffn_tail(x, up_s, act_s, cw_ref, cb_ref, wdn_ref, tm)
    o_ref[...] = _rms_unit(out) * nf_ref[...] if final else out
    tail = up_s[tm + 6:tm + 8, :]
    cs_ref[...] = tail
    up_s[6:8, :] = tail


def _ffn_sample_body(x_ref, st_ref, nw_ref, wup_ref, cw_ref, cb_ref, wdn_ref, nf_ref,
                     o_ref, cs_ref, act_s, *, final):
    x = x_ref[...]
    hf = (_rms_unit(x) * nw_ref[...]).astype(BF16)
    up = jnp.dot(hf, wup_ref[...], preferred_element_type=F32)
    prev2, prev1 = st_ref[:, :2 * D_FF], st_ref[:, 2 * D_FF:]
    cs_ref[:, :2 * D_FF] = prev1
    cs_ref[:, 2 * D_FF:] = up
    for c in range(0, D_FF, _FFN_CW):
        def conv(c0):
            cl = slice(c0, c0 + _FFN_CW)
            return (cb_ref[:, cl] + prev2[:, cl] * cw_ref[0:1, cl] + prev1[:, cl] * cw_ref[1:2, cl]
                    + up[:, cl] * cw_ref[2:3, cl])
        act_s[:, c:c + _FFN_CW] = (_silu(conv(c)) * conv(D_FF + c)).astype(BF16)
    out = x + jnp.dot(act_s[...], wdn_ref[...], preferred_element_type=F32)
    o_ref[...] = _rms_unit(out) * nf_ref[...] if final else out


def _ffn(x2d, state, lw, nf, nb, seq, tm, final):
    wspecs = [_const_spec((1, D_MODEL)), _const_spec((D_MODEL, 2 * D_FF)),
              _const_spec((CONV_F, 2 * D_FF)), _const_spec((1, 2 * D_FF)),
              _const_spec((D_FF, D_MODEL)), _const_spec((1, D_MODEL))]
    wargs = (lw["norm_ffn_w"], lw["w_up"], lw["conv_f_w"], lw["conv_f_b"], lw["w_down"], nf)
    act = pltpu.VMEM((tm, D_FF), BF16)
    if state is None:
        nt = seq // tm
        row = pl.BlockSpec((tm, D_MODEL), lambda b, i: (b * nt + i, 0))
        return pl.pallas_call(
            functools.partial(_ffn_prompt_body, tm=tm, final=final), grid=(nb, nt),
            in_specs=[row] + wspecs,
            out_specs=[row, pl.BlockSpec((None, CONV_F - 1, 2 * D_FF), lambda b, i: (b, 0, 0))],
            out_shape=[jax.ShapeDtypeStruct((nb * seq, D_MODEL), F32),
                       jax.ShapeDtypeStruct((nb, CONV_F - 1, 2 * D_FF), F32)],
            scratch_shapes=[pltpu.VMEM((8 + tm, 2 * D_FF), F32), act],
            compiler_params=_params(2), name="ffn_prompt")(x2d, *wargs)
    sw = (CONV_F - 1) * 2 * D_FF
    return pl.pallas_call(
        functools.partial(_ffn_sample_body, final=final), grid=(nb // tm,),
        in_specs=[pl.BlockSpec((tm, D_MODEL), lambda i: (i, 0)),
                  pl.BlockSpec((tm, sw), lambda i: (i, 0))] + wspecs,
        out_specs=[pl.BlockSpec((tm, D_MODEL), lambda i: (i, 0)),
                   pl.BlockSpec((tm, sw), lambda i: (i, 0))],
        out_shape=[jax.ShapeDtypeStruct((nb, D_MODEL), F32), jax.ShapeDtypeStruct((nb, sw), F32)],
        scratch_shapes=[act], compiler_params=_params(1), name="ffn_sample")(x2d, state, *wargs)


def _ssd_sample_body(*refs, n_prev):
    (zs_ref, xbc_ref, dt_ref, cst_ref, cw_ref, cb_ref, dtb_ref, alog_ref, dx_ref, naw_ref,
     hin_ref) = refs[:11]
    (ya_ref, cso_ref, hout_ref, xa_s, xdt_t_s, dec_t_s, bm_t_s, cm_t_s, yt_s) = refs[11 + n_prev:]
    h = pl.program_id(0)

    @pl.when(h == 0)
    def _():
        acc = cb_ref[...]
        for tap in range(CONV_A - 1):
            acc = acc + cst_ref[:, tap * D_XBC:(tap + 1) * D_XBC] * cw_ref[tap:tap + 1, :]
        acc = acc + xbc_ref[...] * cw_ref[CONV_A - 1:CONV_A, :]
        cso_ref[:, :(CONV_A - 2) * D_XBC] = cst_ref[:, D_XBC:]
        cso_ref[:, (CONV_A - 2) * D_XBC:] = xbc_ref[...]
        xc = _silu(acc)
        xa = xc[:, :D_A]
        xa_s[...] = xa
        bm_t_s[...] = xc[:, D_A:D_A + G_A * N_A].T
        cm_t_s[...] = xc[:, D_A + G_A * N_A:].T
        dt = _softplus(dt_ref[...] + dtb_ref[...])
        dec_t_s[...] = jnp.exp(dt * -jnp.exp(alog_ref[...])).T
        dt_x = _dot01(dt, _tile3(_head_expand(), 0), "lhs")
        xdt_t_s[...] = (xa * dt_x).T

    g0 = pl.multiple_of((h // HPG) * N_A, N_A)
    p0 = pl.multiple_of(h * P_A, P_A)
    b_t = bm_t_s[pl.ds(g0, N_A), :]
    c_t = cm_t_s[pl.ds(g0, N_A), :]
    dec = dec_t_s[pl.ds(h, 1), :]

    def per_8p(k, carry):
        r = pl.ds(pl.multiple_of(p0 + k * 8, 8), 8)
        x_rows = xdt_t_s[r, :]
        ys = []
        for u in range(8):
            h_new = hin_ref[k * 8 + u] * dec + x_rows[u:u + 1, :] * b_t
            hout_ref[k * 8 + u] = h_new
            ys.append(jnp.sum(h_new * c_t, axis=0, keepdims=True))
        yt_s[r, :] = jnp.concatenate(ys, axis=0)
        return carry

    lax.fori_loop(0, P_A // 8, per_8p, 0)

    @pl.when(h == H_A - 1)
    def _():
        xa = xa_s[...]
        y = yt_s[...].T + dx_ref[...] * xa
        y = y * zs_ref[...]
        ya_ref[...] = _rms_unit(y) * naw_ref[...]


def _ssd_sample(p, conv_st, h_ssd_t, lw, nb, l, prev):
    full = lambda w: _const_spec((nb, w))
    cw3 = (CONV_A - 1) * D_XBC
    st = pl.BlockSpec((None, None, P_A, N_A, nb), lambda h: (l, h, 0, 0, 0))
    in_specs = [full(D_A), full(D_XBC), full(DT_PAD), full(cw3),
                _const_spec((CONV_A, D_XBC)), _const_spec((1, D_XBC)), _const_spec((1, DT_PAD)),
                _const_spec((1, DT_PAD)), _const_spec((1, D_A)), _const_spec((1, D_A)), st]
    prev = () if prev is None else (prev,)
    in_specs += [pl.BlockSpec(memory_space=pl.ANY)] * len(prev)
    aliases = {len(in_specs) - 1: 2} if prev else {}
    keep = lambda w: pl.BlockSpec((nb, w), lambda h: (0, 0))
    scratch = [pltpu.VMEM((nb, D_A), F32), pltpu.VMEM((D_A, nb), F32), pltpu.VMEM((LANES, nb), F32),
               pltpu.VMEM((G_A * N_A, nb), F32), pltpu.VMEM((G_A * N_A, nb), F32),
               pltpu.VMEM((D_A, nb), F32)]
    ya, cs, hs = pl.pallas_call(
        functools.partial(_ssd_sample_body, n_prev=len(prev)), grid=(H_A,),
        in_specs=in_specs, out_specs=[keep(D_A), keep(cw3), st],
        out_shape=[jax.ShapeDtypeStruct((nb, D_A), F32), jax.ShapeDtypeStruct((nb, cw3), F32),
                   jax.ShapeDtypeStruct((DEPTH, H_A, P_A, N_A, nb), F32)],
        scratch_shapes=scratch, input_output_aliases=aliases,
        compiler_params=_params(1), name="ssd_sample")(
            p["z"], p["xbc"], p["dt"], conv_st.reshape(nb, cw3),
            lw["conv_a_w"], lw["conv_a_b"], lw["dt_bias"], lw["a_log"], lw["d_x"], lw["norm_a_w"],
            h_ssd_t, *prev)
    return ya, cs.reshape(nb, CONV_A - 1, D_XBC), hs


_RET_BB = 8


def _ret_sample_body(*refs, nb, n_prev):
    q_ref, k_ref, v_ref, gs_ref, cos_ref, sina_ref, sinb_ref, sin_ref = refs[:8]
    yb_ref, sout_ref, q_t_s, k_t_s, yr_s = refs[8 + n_prev:]
    i = pl.program_id(0)

    @pl.when(i == 0)
    def _():
        cos_f, sin_a, sin_b = cos_ref[...], sina_ref[...], sinb_ref[...]
        q_t_s[...] = _rope(q_ref[...], cos_f, sin_a, sin_b).T.astype(BF16)
        k_t_s[...] = (_rope(k_ref[...], cos_f, sin_a, sin_b) * (DK_R ** -0.5)).T.astype(BF16)

    rows_i = lax.broadcasted_iota(jnp.int32, (nb, DV_R), 0)
    blk = pl.ds(pl.multiple_of(i * _RET_BB, _RET_BB), _RET_BB)
    v_blk = v_ref[blk, :]
    ys = [[] for _ in range(H_R)]
    for j in range(_RET_BB):
        onehot = jnp.where(rows_i == i * _RET_BB + j, 1.0, 0.0).astype(BF16)
        k_bc = jnp.dot(k_t_s[...], onehot, preferred_element_type=F32)
        q_bc = jnp.dot(q_t_s[...], onehot, preferred_element_type=F32)
        for h in range(H_R):
            kl = slice(h * DK_R, (h + 1) * DK_R)
            vl = slice(h * DV_R, (h + 1) * DV_R)
            s_new = math.exp(_LOG_GAMMA[h]) * sin_ref[j, h] + k_bc[kl, :] * v_blk[j:j + 1, vl]
            sout_ref[j, h] = s_new
            ys[h].append(jnp.sum(q_bc[kl, :] * s_new, axis=0, keepdims=True))
    for h in range(H_R):
        yr_s[blk, h * DV_R:(h + 1) * DV_R] = jnp.concatenate(ys[h], axis=0)

    @pl.when(i == nb // _RET_BB - 1)
    def _():
        for h in range(H_R):
            vl = slice(h * DV_R, (h + 1) * DV_R)
            yb_ref[:, vl] = gs_ref[:, vl] * _rms_unit(yr_s[:, vl])


def _ret_sample(p, s_ret, rot, nb, l, prev):
    full = lambda w: _const_spec((nb, w))
    st = pl.BlockSpec((None, _RET_BB, H_R, DK_R, DV_R), lambda i: (l, i, 0, 0, 0))
    in_specs = [full(D_QK), full(D_QK), full(D_RV), full(D_RV),
                _const_spec((1, D_QK)), _const_spec((1, D_QK)), _const_spec((1, D_QK)), st]
    prev = () if prev is None else (prev,)
    in_specs += [pl.BlockSpec(memory_space=pl.ANY)] * len(prev)
    aliases = {len(in_specs) - 1: 1} if prev else {}
    return pl.pallas_call(
        functools.partial(_ret_sample_body, nb=nb, n_prev=len(prev)), grid=(nb // _RET_BB,),
        in_specs=in_specs, out_specs=[pl.BlockSpec((nb, D_RV), lambda i: (0, 0)), st],
        out_shape=[jax.ShapeDtypeStruct((nb, D_RV), F32),
                   jax.ShapeDtypeStruct((DEPTH, nb, H_R, DK_R, DV_R), F32)],
        scratch_shapes=[pltpu.VMEM((D_QK, nb), BF16), pltpu.VMEM((D_QK, nb), BF16),
                        pltpu.VMEM((nb, D_RV), F32)],
        input_output_aliases=aliases, compiler_params=_params(1), name="ret_sample")(
            p["q"], p["k"], p["v"], p["g"], rot["cos"], rot["sin_a"], rot["sin_b"], s_ret, *prev)


def _rotary_tables(start, length):
    half = DK_R // 2
    pos = start + jnp.arange(length, dtype=F32)
    freqs = ROPE_BASE ** (-jnp.arange(half, dtype=F32) / half)
    ang = pos[:, None] * freqs[None, :]
    cos, sin, zero = jnp.cos(ang), jnp.sin(ang), jnp.zeros_like(ang)
    tile = lambda a, b: jnp.tile(jnp.concatenate([a, b], axis=-1), (1, H_R))
    lg = jnp.log1p(-jnp.exp2(-5.0 - jnp.arange(H_R, dtype=F32)))
    return {"cos": tile(cos, cos), "sin_a": tile(-sin, zero), "sin_b": tile(zero, sin),
            "lgx": jnp.repeat(lg, DK_R)[None, :]}


def _layer_weights(l, w):
    w_in = w["w_in"][l]
    offs = np.cumsum([0, D_A, D_XBC, H_A, D_QK, D_QK, D_RV, D_RV, D_C, 3 * D_MODEL])
    cols = [w_in[:, offs[i]:offs[i + 1]] for i in range(9)]
    cols[2] = jnp.pad(cols[2], ((0, 0), (0, DT_PAD - H_A)))
    pad_row = lambda a: jnp.pad(a, (0, DT_PAD - H_A))[None, :]
    ab_re, ab_im, bb_re, bb_im = _s5_discretise(
        w["s5_a_re"][l], w["s5_a_im"][l], w["s5_log_dt"][l], w["s5_b_re"][l], w["s5_b_im"][l])
    bblk = jnp.concatenate([_block_diag(bb_re), _block_diag(bb_im)], axis=-1)
    c_t = lambda c: _block_diag(jnp.transpose(c, (0, 2, 1)))
    return {
        "norm_mix_w": w["norm_mix_w"][l][None, :],
        "w_in": [c.astype(BF16) for c in cols],
        "conv_a_w": w["conv_a_w"][l], "conv_a_b": w["conv_a_b"][l][None, :],
        "dt_bias": pad_row(w["dt_bias"][l]), "a_log": pad_row(w["a_log"][l]),
        "d_x": jnp.repeat(w["d_a"][l], P_A)[None, :], "norm_a_w": w["norm_a_w"][l][None, :],
        "s5_ab_re": ab_re.reshape(1, D_S5), "s5_ab_im": ab_im.reshape(1, D_S5),
        "s5_bblk": bblk.astype(BF16),
        "s5_cre": c_t(w["s5_c_re"][l]).astype(BF16), "s5_cim": c_t(w["s5_c_im"][l]).astype(BF16),
        "s5_d": w["s5_d"][l][None, :], "w_glu": w["w_glu"][l].astype(BF16),
        "b_glu": w["b_glu"][l][None, :],
        "w_br_a": w["w_br_a"][l].astype(BF16), "w_br_b": w["w_br_b"][l].astype(BF16),
        "w_br_c": w["w_br_c"][l].astype(BF16), "w_out": w["w_out"][l].astype(BF16),
        "norm_ffn_w": w["norm_ffn_w"][l][None, :], "w_up": w["w_up"][l].astype(BF16),
        "conv_f_w": w["conv_f_w"][l], "conv_f_b": w["conv_f_b"][l][None, :],
        "w_down": w["w_down"][l].astype(BF16),
    }


def _prompt_trunk(x, lws, nf, tm_proj, tsteps, tt, tm_merge, tm_ffn):
    nb, seq, _ = x.shape
    rot = _rotary_tables(0.0, seq)
    x2d = x.reshape(nb * seq, D_MODEL)
    zeros = jnp.zeros((nb, D_S5), F32)
    states, big = [], None
    for l, lw in enumerate(lws):
        p = _inproj(x2d, lw["norm_mix_w"], lw, nb, seq, tm_proj, conv=True)
        ya, yb, *big = _mix_prompt(p, rot, lw, nb, seq, tsteps, l, big)
        yc, h_re, h_im = _s5(p["u"], zeros, zeros, lw, nb, seq, tt)
        x2d = _merge(x2d, ya, yb, yc, p["gates"], lw, nb, seq, tm_merge)
        x2d, fs = _ffn(x2d, None, lw, nf, nb, seq, tm_ffn, final=(l == len(lws) - 1))
        states.append((p["conv_state"], h_re.reshape(nb, G_C, P_C), h_im.reshape(nb, G_C, P_C), fs))
    cs, h_re, h_im, fs = [jnp.stack(s) for s in zip(*states)]
    return x2d.reshape(nb, seq, D_MODEL), [cs, big[0], big[1], h_re, h_im, fs]


def _sample_trunk(x, st, lws, nf):
    nb = x.shape[0]
    rot = _rotary_tables(float(PAST_LEN), 1)
    x2d = x.reshape(nb, D_MODEL)
    st_conv, st_ssd, st_ret, st_re, st_im, st_ffn = st
    st_ssd_t = jnp.transpose(st_ssd, (0, 2, 3, 4, 1))
    states, hs, ss = [], None, None
    for l, lw in enumerate(lws):
        p = _inproj(x2d, lw["norm_mix_w"], lw, 1, nb, nb, conv=False)
        ya, cs, hs = _ssd_sample(p, st_conv[l], st_ssd_t, lw, nb, l, hs)
        yb, ss = _ret_sample(p, st_ret, rot, nb, l, ss)
        yc, h_re, h_im = _s5(p["u"], st_re[l].reshape(nb, D_S5), st_im[l].reshape(nb, D_S5),
                             lw, nb, 1, 1)
        x2d = _merge(x2d, ya, yb, yc, p["gates"], lw, 1, nb, nb)
        x2d, fs = _ffn(x2d, st_ffn[l].reshape(nb, -1), lw, nf, nb, 1, nb,
                       final=(l == len(lws) - 1))
        states.append((cs, h_re.reshape(nb, G_C, P_C), h_im.reshape(nb, G_C, P_C),
                       fs.reshape(nb, CONV_F - 1, 2 * D_FF)))
    cs, h_re, h_im, fs = [jnp.stack(s) for s in zip(*states)]
    hs = jnp.transpose(hs, (0, 4, 1, 2, 3))
    return x2d.reshape(nb, 1, D_MODEL), [cs, hs, ss, h_re, h_im, fs]


def kernel(x_prompt, x_sample, state_ssd_conv, state_ssd, state_ret, state_s5_re, state_s5_im,
           state_ffn_conv, norm_mix_w, w_in, conv_a_w, conv_a_b, dt_bias, a_log, d_a, norm_a_w,
           s5_a_re, s5_a_im, s5_log_dt, s5_b_re, s5_b_im, s5_c_re, s5_c_im, s5_d, w_glu, b_glu,
           w_br_a, w_br_b, w_br_c, w_out, norm_ffn_w, w_up, conv_f_w, conv_f_b, w_down, norm_f_w):
    w = dict(norm_mix_w=norm_mix_w, w_in=w_in, conv_a_w=conv_a_w, conv_a_b=conv_a_b,
             dt_bias=dt_bias, a_log=a_log, d_a=d_a, norm_a_w=norm_a_w, s5_a_re=s5_a_re,
             s5_a_im=s5_a_im, s5_log_dt=s5_log_dt, s5_b_re=s5_b_re, s5_b_im=s5_b_im,
             s5_c_re=s5_c_re, s5_c_im=s5_c_im, s5_d=s5_d, w_glu=w_glu, b_glu=b_glu,
             w_br_a=w_br_a, w_br_b=w_br_b, w_br_c=w_br_c, w_out=w_out, norm_ffn_w=norm_ffn_w,
             w_up=w_up, conv_f_w=conv_f_w, conv_f_b=conv_f_b, w_down=w_down)
    lws = [_layer_weights(l, w) for l in range(DEPTH)]
    nf = norm_f_w[None, :]
    seq = x_prompt.shape[1]
    y_p, p_st = _prompt_trunk(x_prompt, lws, nf, tm_proj=min(256, seq), tsteps=min(256, seq),
                              tt=min(32, seq), tm_merge=min(512, seq), tm_ffn=min(256, seq))
    y_s, s_st = _sample_trunk(
        x_sample, (state_ssd_conv, state_ssd, state_ret, state_s5_re, state_s5_im, state_ffn_conv),
        lws, nf)
    return (y_p, y_s, *p_st, *s_st)
```

```python
import functools
import math

import jax
import jax.numpy as jnp
import numpy as np
from jax import lax
from jax.experimental import pallas as pl
from jax.experimental.pallas import tpu as pltpu

F32 = jnp.float32
BF16 = jnp.bfloat16

D_MODEL = 1024
DEPTH = 2
PAST_LEN = 16384
H_A, P_A, N_A, G_A, CONV_A = 16, 64, 64, 2, 4
D_A = H_A * P_A
D_XBC = D_A + 2 * G_A * N_A
H_R, DK_R, DV_R = 8, 64, 128
D_QK = H_R * DK_R
D_RV = H_R * DV_R
ROPE_BASE = 10000.0
GS_C, G_C, P_C = 16, 64, 64
D_C = G_C * GS_C
D_S5 = G_C * P_C
D_FF = 2816
CONV_F = 3
CHUNK = 128
EPS = 1e-6
LANES = 128
DT_PAD = LANES
HPG = H_A // G_A
VMEM_LIMIT = 56 * 1024 * 1024

_LOG_GAMMA = [math.log1p(-(2.0 ** (-5.0 - h))) for h in range(H_R)]

_SEGS = (("z", D_A), ("xbc", D_XBC), ("dt", DT_PAD), ("q", D_QK), ("k", D_QK),
         ("v", D_RV), ("g", D_RV), ("u", D_C))


def _rms_unit(x):
    return x * lax.rsqrt(jnp.mean(x * x, axis=-1, keepdims=True) + EPS)


def _silu(x):
    return x * jax.nn.sigmoid(x)


def _softplus(x):
    return jnp.maximum(x, 0.0) + jnp.log1p(jnp.exp(-jnp.abs(x)))


def _bdot(a, b):
    return jnp.dot(a.astype(BF16), b.astype(BF16), preferred_element_type=F32)


def _split3(x):
    hi = x.astype(BF16)
    r1 = x - hi.astype(F32)
    mid = r1.astype(BF16)
    return hi, mid, (r1 - mid.astype(F32)).astype(BF16)


def _tile3(m01, axis):
    return jnp.concatenate([m01] * 3, axis=axis)


def _dot01(a, b, f32_side):
    if f32_side == "lhs":
        return jnp.dot(jnp.concatenate(_split3(a), axis=1), b, preferred_element_type=F32)
    return jnp.dot(a, jnp.concatenate(_split3(b), axis=0), preferred_element_type=F32)


def _const_spec(shape):
    nd = len(shape)
    return pl.BlockSpec(shape, lambda *_: (0,) * nd, pipeline_mode=pl.Buffered(1))


def _params(n_grid):
    return pltpu.CompilerParams(dimension_semantics=("arbitrary",) * n_grid,
                                vmem_limit_bytes=VMEM_LIMIT)


def _inproj_body(x_ref, nw_ref, cw_ref, cb_ref, *refs, tm, conv):
    n = len(_SEGS)
    h = (_rms_unit(x_ref[...]) * nw_ref[...]).astype(BF16)
    for (name, _), w_ref, o_ref in zip(_SEGS, refs[:n], refs[n:2 * n]):
        y = jnp.dot(h, w_ref[...], preferred_element_type=F32)
        if name in ("z", "g"):
            y = _silu(y)
        elif name == "xbc" and conv:
            cs_ref, xp_s = refs[2 * n:]

            @pl.when(pl.program_id(1) == 0)
            def _():
                xp_s[0:8, :] = jnp.zeros((8, D_XBC), F32)

            xp_s[8:8 + tm, :] = y
            y = cb_ref[...]
            for tap in range(CONV_A):
                y = y + xp_s[5 + tap:5 + tap + tm, :] * cw_ref[tap:tap + 1, :]
            y = _silu(y)
            tail = xp_s[tm + 5:tm + 8, :]
            cs_ref[...] = tail
            xp_s[5:8, :] = tail
        o_ref[...] = y


def _inproj(x2d, nw, lw, nb, seq, tm, conv):
    nt = seq // tm
    ws = lw["w_in"]
    in_specs = [pl.BlockSpec((tm, D_MODEL), lambda b, i: (b * nt + i, 0)),
                _const_spec((1, D_MODEL)), _const_spec((CONV_A, D_XBC)), _const_spec((1, D_XBC))]
    in_specs += [_const_spec(w.shape) for w in ws]
    out_specs, out_shape, scratch = [], [], []
    for _, width in _SEGS:
        out_specs.append(pl.BlockSpec((tm, width), lambda b, i: (b * nt + i, 0)))
        out_shape.append(jax.ShapeDtypeStruct((nb * seq, width), F32))
    if conv:
        out_specs.append(pl.BlockSpec((None, CONV_A - 1, D_XBC), lambda b, i: (b, 0, 0)))
        out_shape.append(jax.ShapeDtypeStruct((nb, CONV_A - 1, D_XBC), F32))
        scratch.append(pltpu.VMEM((8 + tm, D_XBC), F32))
    outs = pl.pallas_call(
        functools.partial(_inproj_body, tm=tm, conv=conv), grid=(nb, nt), in_specs=in_specs,
        out_specs=out_specs, out_shape=out_shape, scratch_shapes=scratch,
        compiler_params=_params(2), name="inproj")(
            x2d, nw, lw["conv_a_w"], lw["conv_a_b"], *ws)
    res = {name: o for (name, _), o in zip(_SEGS, outs)}
    if conv:
        res["conv_state"] = outs[-1]
    return res


def _rope(x, cos_f, sin_a, sin_b):
    half = DK_R // 2
    return (x * cos_f + pltpu.roll(x, D_QK - half, 1) * sin_a
            + pltpu.roll(x, half, 1) * sin_b)


def _head_expand():
    lo = lax.broadcasted_iota(jnp.int32, (LANES, D_A), 0) * P_A
    c = lax.broadcasted_iota(jnp.int32, (LANES, D_A), 1)
    return jnp.where(c >= lo, jnp.where(c < lo + P_A, 1.0, 0.0), 0.0).astype(BF16)


def _mix_body(*refs, tsteps, nt, n_prev):
    (zs_ref, xc_ref, dt_ref, q_ref, k_ref, v_ref, gs_ref, cos_ref, sina_ref, sinb_ref, lgx_ref,
     dtb_ref, alog_ref, dx_ref, naw_ref) = refs[:15]
    (ya_ref, yb_ref, hs_ref, ss_ref,
     ht_s, s_s, intra_s, qdec_s, kdect_s) = refs[15 + n_prev:]
    i = pl.program_id(1)
    first = jnp.logical_and(pl.program_id(0) == 0, i == 0)
    rows_i = lax.broadcasted_iota(jnp.int32, (CHUNK, CHUNK), 0)
    cols_i = lax.broadcasted_iota(jnp.int32, (CHUNK, CHUNK), 1)
    causal = rows_i >= cols_i
    lane_lo = cols_i < LANES // 2

    @pl.when(first)
    def _():
        rel = (rows_i - cols_i).astype(F32)
        for h in range(H_R):
            intra_s[h] = jnp.where(causal, jnp.exp(jnp.maximum(rel, 0.0) * _LOG_GAMMA[h]), 0.0)
        ri = lax.broadcasted_iota(jnp.int32, (CHUNK, D_QK), 0).astype(F32)
        lgx = lgx_ref[...]
        qdec_s[...] = jnp.exp((ri + 1.0) * lgx)
        kdect_s[...] = jnp.exp((CHUNK - 1.0 - ri) * lgx).T

    @pl.when(i == 0)
    def _():
        ht_s[...] = jnp.zeros_like(ht_s)
        s_s[...] = jnp.zeros_like(s_s)

    tril = _tile3(jnp.where(causal, 1.0, 0.0).astype(BF16), 1)
    expand = _tile3(_head_expand(), 0)
    a_neg = -jnp.exp(alog_ref[...])

    for c in range(tsteps // CHUNK):
        r0 = c * CHUNK
        rows = slice(r0, r0 + CHUNK)
        xa = xc_ref[rows, :D_A]
        bm = xc_ref[rows, D_A:D_A + G_A * N_A]
        cm = xc_ref[rows, D_A + G_A * N_A:]
        dt = _softplus(dt_ref[rows, :] + dtb_ref[...])
        cum = _dot01(tril, dt * a_neg, "rhs")
        cum_t = cum.T
        dt_t = dt.T
        cum_last = cum[CHUNK - 1:CHUNK, :]
        scale = jnp.concatenate(
            [dt * jnp.exp(cum_last - cum), jnp.exp(cum),
             jnp.broadcast_to(jnp.exp(cum_last), (8, LANES))], axis=0)
        scale_x = _dot01(scale, expand, "lhs")
        w_x = scale_x[:CHUNK]
        ecum_x = scale_x[CHUNK:2 * CHUNK]
        dec_x = scale_x[2 * CHUNK:2 * CHUNK + 1]
        bm_t = bm.T
        ys = []
        for g in range(G_A):
            gl = slice(g * N_A, (g + 1) * N_A)
            hl = slice(g * HPG * P_A, (g + 1) * HPG * P_A)
            cg = cm[:, gl]
            scores = lax.dot_general(cg.astype(BF16), bm[:, gl].astype(BF16),
                                     (((1,), (1,)), ((), ())), preferred_element_type=F32)
            scores = jnp.where(causal, scores, 0.0)
            ht_g = ht_s[:, hl]
            y_inter = _bdot(cg, ht_g) * ecum_x[:, hl]
            for pair in range(HPG // 2):
                pl_ = slice(g * HPG * P_A + pair * LANES, g * HPG * P_A + (pair + 1) * LANES)
                x_pair = xa[:, pl_]
                sps = []
                for h in (g * HPG + 2 * pair, g * HPG + 2 * pair + 1):
                    seg = cum[:, h:h + 1] - cum_t[h:h + 1, :]
                    sps.append(scores * jnp.exp(jnp.minimum(seg, 0.0)) * dt_t[h:h + 1, :])
                rhs = jnp.concatenate([jnp.where(lane_lo, x_pair, 0.0),
                                       jnp.where(lane_lo, 0.0, x_pair)], axis=0)
                ys.append(_bdot(jnp.concatenate(sps, axis=1), rhs)
                          + y_inter[:, pair * LANES:(pair + 1) * LANES])
            ht_s[:, hl] = dec_x[:, hl] * ht_g + _bdot(bm_t[gl, :], xa[:, hl] * w_x[:, hl])
        y = jnp.concatenate(ys, axis=1) + dx_ref[...] * xa
        y = y * zs_ref[rows, :]
        ya_ref[rows, :] = _rms_unit(y) * naw_ref[...]

        cos_f, sin_a, sin_b = cos_ref[rows, :], sina_ref[rows, :], sinb_ref[rows, :]
        qr = _rope(q_ref[rows, :], cos_f, sin_a, sin_b)
        kr = _rope(k_ref[rows, :], cos_f, sin_a, sin_b) * (DK_R ** -0.5)
        k_t = kr.T
        kd_t = k_t * kdect_s[...]
        qd = qr * qdec_s[...]
        for pair in range(H_R // 2):
            pl_ = slice(pair * LANES, (pair + 1) * LANES)
            s_pair = s_s[pl_, :]
            for h, keep in ((2 * pair, lane_lo), (2 * pair + 1, jnp.logical_not(lane_lo))):
                kl = slice(h * DK_R, (h + 1) * DK_R)
                vl = slice(h * DV_R, (h + 1) * DV_R)
                v_h = v_ref[rows, vl]
                sc = _bdot(jnp.where(keep, qr[:, pl_], 0.0), k_t[pl_, :]) * intra_s[h]
                y_h = _bdot(jnp.concatenate([sc, jnp.where(keep, qd[:, pl_], 0.0)], axis=1),
                            jnp.concatenate([v_h, s_pair], axis=0))
                s_s[kl, :] = (math.exp(CHUNK * _LOG_GAMMA[h]) * s_s[kl, :]
                              + _bdot(kd_t[kl, :], v_h))
                yb_ref[rows, vl] = gs_ref[rows, vl] * _rms_unit(y_h)

    @pl.when(i == nt - 1)
    def _():
        for h in range(H_A):
            hs_ref[h] = ht_s[:, h * P_A:(h + 1) * P_A].T
        for h in range(H_R):
            ss_ref[h] = s_s[h * DK_R:(h + 1) * DK_R, :]


def _layer_block(l, tail):
    zeros = (0,) * len(tail)
    return pl.BlockSpec((None, None) + tail, lambda b, *_: (l, b) + zeros)


def _mix_prompt(p, rot, lw, nb, seq, tsteps, l, prev):
    nt = seq // tsteps
    row = lambda w: pl.BlockSpec((tsteps, w), lambda b, i: (b * nt + i, 0))
    tab = pl.BlockSpec((tsteps, D_QK), lambda b, i: (i, 0))
    in_specs = [row(D_A), row(D_XBC), row(DT_PAD), row(D_QK), row(D_QK), row(D_RV), row(D_RV),
                tab, tab, tab, _const_spec((1, D_QK)), _const_spec((1, DT_PAD)),
                _const_spec((1, DT_PAD)), _const_spec((1, D_A)), _const_spec((1, D_A))]
    prev = () if prev is None else tuple(prev)
    in_specs += [pl.BlockSpec(memory_space=pl.ANY)] * len(prev)
    aliases = {len(in_specs) - len(prev) + k: 2 + k for k in range(len(prev))}
    out_specs = [row(D_A), row(D_RV),
                 _layer_block(l, (H_A, P_A, N_A)), _layer_block(l, (H_R, DK_R, DV_R))]
    out_shape = [jax.ShapeDtypeStruct((nb * seq, D_A), F32),
                 jax.ShapeDtypeStruct((nb * seq, D_RV), F32),
                 jax.ShapeDtypeStruct((DEPTH, nb, H_A, P_A, N_A), F32),
                 jax.ShapeDtypeStruct((DEPTH, nb, H_R, DK_R, DV_R), F32)]
    scratch = [pltpu.VMEM((N_A, D_A), F32),
               pltpu.VMEM((D_QK, DV_R), F32), pltpu.VMEM((H_R, CHUNK, CHUNK), F32),
               pltpu.VMEM((CHUNK, D_QK), F32), pltpu.VMEM((D_QK, CHUNK), F32)]
    return pl.pallas_call(
        functools.partial(_mix_body, tsteps=tsteps, nt=nt, n_prev=len(prev)), grid=(nb, nt),
        in_specs=in_specs, out_specs=out_specs, out_shape=out_shape, scratch_shapes=scratch,
        input_output_aliases=aliases, compiler_params=_params(2), name="mix_prompt")(
            p["z"], p["xbc"], p["dt"], p["q"], p["k"], p["v"], p["g"],
            rot["cos"], rot["sin_a"], rot["sin_b"], rot["lgx"],
            lw["dt_bias"], lw["a_log"], lw["d_x"], lw["norm_a_w"], *prev)


def _s5_disc_body(are_ref, aim_ref, ldt_ref, bre_ref, bim_ref,
                  abre_ref, abim_ref, bbre_ref, bbim_ref):
    ar, ai = are_ref[...], aim_ref[...]
    dt = jnp.exp(ldt_ref[...])
    mag = jnp.exp(ar * dt)
    ab_re = mag * jnp.cos(ai * dt)
    ab_im = mag * jnp.sin(ai * dt)
    den = ar * ar + ai * ai
    num_re = ab_re - 1.0
    coef_re = (num_re * ar + ab_im * ai) / den
    coef_im = (ab_im * ar - num_re * ai) / den
    abre_ref[...] = ab_re
    abim_ref[...] = ab_im
    for c in range(GS_C):
        cl = slice(c * P_C, (c + 1) * P_C)
        br, bi = bre_ref[:, cl], bim_ref[:, cl]
        bbre_ref[:, cl] = coef_re * br - coef_im * bi
        bbim_ref[:, cl] = coef_re * bi + coef_im * br


def _s5_discretise(a_re, a_im, log_dt, b_re, b_im):
    gp = jax.ShapeDtypeStruct((G_C, P_C), F32)
    gcp = jax.ShapeDtypeStruct((G_C, GS_C * P_C), F32)
    b_t = lambda b: jnp.transpose(b, (0, 2, 1)).reshape(G_C, GS_C * P_C)
    ab_re, ab_im, bb_re, bb_im = pl.pallas_call(
        _s5_disc_body, out_shape=[gp, gp, gcp, gcp], name="s5_disc")(
            a_re, a_im, log_dt.reshape(G_C, 1), b_t(b_re), b_t(b_im))
    return ab_re, ab_im, bb_re.reshape(G_C, GS_C, P_C), bb_im.reshape(G_C, GS_C, P_C)


_S5_GB = LANES // GS_C
_S5_NBLK = G_C // _S5_GB
_S5_SB = _S5_GB * P_C


def _block_diag(m):
    g, r, c = m.shape
    m = m.reshape(_S5_NBLK, _S5_GB, r, c)
    eye = jnp.eye(_S5_GB, dtype=m.dtype)
    return jnp.einsum("jgrc,gk->jgrkc", m, eye).reshape(_S5_NBLK, _S5_GB * r, _S5_GB * c)


def _s5_body(u_ref, h0re_ref, h0im_ref, are_ref, aim_ref, bblk_ref, cre_ref, cim_ref,
             d_ref, wglu_ref, bglu_ref, yc_ref, hre_ref, him_ref,
             xre_s, xim_s, sre_s, sim_s, *, nb, tt):
    i = pl.program_id(0)
    rows = nb * tt

    @pl.when(i == 0)
    def _():
        sre_s[...] = h0re_ref[...]
        sim_s[...] = h0im_ref[...]

    u = u_ref[...]
    if tt > 1:
        u = jnp.swapaxes(u, 0, 1).reshape(rows, D_C)
    ub = u.astype(BF16)

    ys = []
    for j in range(_S5_NBLK):
        sl = slice(j * _S5_SB, (j + 1) * _S5_SB)
        bu = jnp.dot(ub[:, j * LANES:(j + 1) * LANES], bblk_ref[j], preferred_element_type=F32)
        xre_s[:, sl] = bu[:, :_S5_SB]
        xim_s[:, sl] = bu[:, _S5_SB:]
        ar = jnp.broadcast_to(are_ref[:, sl], (nb, _S5_SB))
        ai = jnp.broadcast_to(aim_ref[:, sl], (nb, _S5_SB))
        xr, xi = sre_s[:, sl], sim_s[:, sl]
        for t in range(tt):
            r = slice(t * nb, (t + 1) * nb)
            xr, xi = (ar * xr - ai * xi + xre_s[r, sl], ar * xi + ai * xr + xim_s[r, sl])
            xre_s[r, sl] = xr
            xim_s[r, sl] = xi
        sre_s[:, sl] = xr
        sim_s[:, sl] = xi
        ys.append(_bdot(xre_s[:, sl], cre_ref[j]) - _bdot(xim_s[:, sl], cim_ref[j]))
    y = jax.nn.gelu(jnp.concatenate(ys, axis=1) + d_ref[...] * u)
    y = y * jax.nn.sigmoid(_bdot(y, wglu_ref[...]) + bglu_ref[...])
    if tt > 1:
        y = jnp.swapaxes(y.reshape(tt, nb, D_C), 0, 1)
    yc_ref[...] = y
    hre_ref[...] = sre_s[...]
    him_ref[...] = sim_s[...]


def _s5(u, h0_re, h0_im, lw, nb, seq, tt):
    rows = tt * nb
    if tt > 1:
        assert nb % 8 == 0 and tt % 8 == 0
        u = u.reshape(nb, seq, D_C)
        io_spec = pl.BlockSpec((nb, tt, D_C), lambda i: (0, i, 0))
    else:
        io_spec = pl.BlockSpec((rows, D_C), lambda i: (i, 0))
    in_specs = [io_spec,
                _const_spec((nb, D_S5)), _const_spec((nb, D_S5)),
                _const_spec((1, D_S5)), _const_spec((1, D_S5)),
                _const_spec((_S5_NBLK, LANES, 2 * _S5_SB)),
                _const_spec((_S5_NBLK, _S5_SB, LANES)), _const_spec((_S5_NBLK, _S5_SB, LANES)),
                _const_spec((1, D_C)), _const_spec((D_C, D_C)), _const_spec((1, D_C))]
    st = pl.BlockSpec((nb, D_S5), lambda i: (0, 0))
    yc, h_re, h_im = pl.pallas_call(
        functools.partial(_s5_body, nb=nb, tt=tt), grid=(seq // tt,), in_specs=in_specs,
        out_specs=[io_spec, st, st],
        out_shape=[jax.ShapeDtypeStruct(u.shape, F32),
                   jax.ShapeDtypeStruct((nb, D_S5), F32), jax.ShapeDtypeStruct((nb, D_S5), F32)],
        scratch_shapes=[pltpu.VMEM((rows, D_S5), F32), pltpu.VMEM((rows, D_S5), F32),
                        pltpu.VMEM((nb, D_S5), F32), pltpu.VMEM((nb, D_S5), F32)],
        compiler_params=_params(1), name="s5")(
            u, h0_re, h0_im, lw["s5_ab_re"], lw["s5_ab_im"], lw["s5_bblk"],
            lw["s5_cre"], lw["s5_cim"], lw["s5_d"], lw["w_glu"], lw["b_glu"])
    return yc.reshape(nb * seq, D_C), h_re, h_im


def _merge_body(x_ref, nw_ref, ya_ref, yb_ref, yc_ref, wg_ref, wa_ref, wb_ref, wc_ref, wo_ref,
                o_ref):
    x = x_ref[...]
    h = (_rms_unit(x) * nw_ref[...]).astype(BF16)
    merged = None
    for k, (y_ref, w_ref) in enumerate(((ya_ref, wa_ref), (yb_ref, wb_ref), (yc_ref, wc_ref))):
        gate = jax.nn.sigmoid(jnp.dot(h, wg_ref[:, k * D_MODEL:(k + 1) * D_MODEL],
                                      preferred_element_type=F32))
        term = gate * _bdot(y_ref[...], w_ref[...])
        merged = term if merged is None else merged + term
    o_ref[...] = x + _bdot(merged, wo_ref[...])


def _merge(x2d, ya, yb, yc, lw, nb, seq, tm):
    nt = seq // tm
    row = lambda w: pl.BlockSpec((tm, w), lambda b, i: (b * nt + i, 0))
    wspec = _const_spec((D_MODEL, D_MODEL))
    return pl.pallas_call(
        _merge_body, grid=(nb, nt),
        in_specs=[row(D_MODEL), _const_spec((1, D_MODEL)), row(D_A), row(D_RV), row(D_C),
                  _const_spec((D_MODEL, 3 * D_MODEL)), wspec, wspec, wspec, wspec],
        out_specs=row(D_MODEL), out_shape=jax.ShapeDtypeStruct((nb * seq, D_MODEL), F32),
        compiler_params=_params(2), name="merge")(
            x2d, lw["norm_mix_w"], ya, yb, yc, lw["w_gates"],
            lw["w_br_a"], lw["w_br_b"], lw["w_br_c"], lw["w_out"])


_FFN_CW = 256


def _ffn_tail(x, up_s, act_s, cw_ref, cb_ref, wdn_ref, tm):
    for c in range(0, D_FF, _FFN_CW):
        def conv(c0):
            cl = slice(c0, c0 + _FFN_CW)
            y = cb_ref[:, cl]
            for tap in range(CONV_F):
                y = y + up_s[6 + tap:6 + tap + tm, cl] * cw_ref[tap:tap + 1, cl]
            return y
        act_s[:, c:c + _FFN_CW] = (_silu(conv(c)) * conv(D_FF + c)).astype(BF16)
    return x + jnp.dot(act_s[...], wdn_ref[...], preferred_element_type=F32)


def _ffn_prompt_body(x_ref, nw_ref, wup_ref, cw_ref, cb_ref, wdn_ref, nf_ref,
                     o_ref, cs_ref, up_s, act_s, *, tm, final):
    @pl.when(pl.program_id(1) == 0)
    def _():
        up_s[0:8, :] = jnp.zeros((8, 2 * D_FF), F32)

    x = x_ref[...]
    hf = (_rms_unit(x) * nw_ref[...]).astype(BF16)
    up_s[8:8 + tm, :] = jnp.dot(hf, wup_ref[...], preferred_element_type=F32)
    out = _ffn_tail(x, up_s, act_s, cw_ref, cb_ref, wdn_ref, tm)
    o_ref[...] = _rms_unit(out) * nf_ref[...] if final else out
    tail = up_s[tm + 6:tm + 8, :]
    cs_ref[...] = tail
    up_s[6:8, :] = tail


def _ffn_sample_body(x_ref, st_ref, nw_ref, wup_ref, cw_ref, cb_ref, wdn_ref, nf_ref,
                     o_ref, cs_ref, act_s, *, final):
    x = x_ref[...]
    hf = (_rms_unit(x) * nw_ref[...]).astype(BF16)
    up = jnp.dot(hf, wup_ref[...], preferred_element_type=F32)
    prev2, prev1 = st_ref[:, :2 * D_FF], st_ref[:, 2 * D_FF:]
    cs_ref[:, :2 * D_FF] = prev1
    cs_ref[:, 2 * D_FF:] = up
    for c in range(0, D_FF, _FFN_CW):
        def conv(c0):
            cl = slice(c0, c0 + _FFN_CW)
            return (cb_ref[:, cl] + prev2[:, cl] * cw_ref[0:1, cl] + prev1[:, cl] * cw_ref[1:2, cl]
                    + up[:, cl] * cw_ref[2:3, cl])
        act_s[:, c:c + _FFN_CW] = (_silu(conv(c)) * conv(D_FF + c)).astype(BF16)
    out = x + jnp.dot(act_s[...], wdn_ref[...], preferred_element_type=F32)
    o_ref[...] = _rms_unit(out) * nf_ref[...] if final else out


def _ffn(x2d, state, lw, nf, nb, seq, tm, final):
    wspecs = [_const_spec((1, D_MODEL)), _const_spec((D_MODEL, 2 * D_FF)),
              _const_spec((CONV_F, 2 * D_FF)), _const_spec((1, 2 * D_FF)),
              _const_spec((D_FF, D_MODEL)), _const_spec((1, D_MODEL))]
    wargs = (lw["norm_ffn_w"], lw["w_up"], lw["conv_f_w"], lw["conv_f_b"], lw["w_down"], nf)
    act = pltpu.VMEM((tm, D_FF), BF16)
    if state is None:
        nt = seq // tm
        row = pl.BlockSpec((tm, D_MODEL), lambda b, i: (b * nt + i, 0))
        return pl.pallas_call(
            functools.partial(_ffn_prompt_body, tm=tm, final=final), grid=(nb, nt),
            in_specs=[row] + wspecs,
            out_specs=[row, pl.BlockSpec((None, CONV_F - 1, 2 * D_FF), lambda b, i: (b, 0, 0))],
            out_shape=[jax.ShapeDtypeStruct((nb * seq, D_MODEL), F32),
                       jax.ShapeDtypeStruct((nb, CONV_F - 1, 2 * D_FF), F32)],
            scratch_shapes=[pltpu.VMEM((8 + tm, 2 * D_FF), F32), act],
            compiler_params=_params(2), name="ffn_prompt")(x2d, *wargs)
    sw = (CONV_F - 1) * 2 * D_FF
    return pl.pallas_call(
        functools.partial(_ffn_sample_body, final=final), grid=(nb // tm,),
        in_specs=[pl.BlockSpec((tm, D_MODEL), lambda i: (i, 0)),
                  pl.BlockSpec((tm, sw), lambda i: (i, 0))] + wspecs,
        out_specs=[pl.BlockSpec((tm, D_MODEL), lambda i: (i, 0)),
                   pl.BlockSpec((tm, sw), lambda i: (i, 0))],
        out_shape=[jax.ShapeDtypeStruct((nb, D_MODEL), F32), jax.ShapeDtypeStruct((nb, sw), F32)],
        scratch_shapes=[act], compiler_params=_params(1), name="ffn_sample")(x2d, state, *wargs)


def _ssd_sample_body(*refs, n_prev):
    (zs_ref, xbc_ref, dt_ref, cst_ref, cw_ref, cb_ref, dtb_ref, alog_ref, dx_ref, naw_ref,
     hin_ref) = refs[:11]
    (ya_ref, cso_ref, hout_ref, xa_s, xdt_t_s, dec_t_s, bm_t_s, cm_t_s, yt_s) = refs[11 + n_prev:]
    h = pl.program_id(0)

    @pl.when(h == 0)
    def _():
        acc = cb_ref[...]
        for tap in range(CONV_A - 1):
            acc = acc + cst_ref[:, tap * D_XBC:(tap + 1) * D_XBC] * cw_ref[tap:tap + 1, :]
        acc = acc + xbc_ref[...] * cw_ref[CONV_A - 1:CONV_A, :]
        cso_ref[:, :(CONV_A - 2) * D_XBC] = cst_ref[:, D_XBC:]
        cso_ref[:, (CONV_A - 2) * D_XBC:] = xbc_ref[...]
        xc = _silu(acc)
        xa = xc[:, :D_A]
        xa_s[...] = xa
        bm_t_s[...] = xc[:, D_A:D_A + G_A * N_A].T
        cm_t_s[...] = xc[:, D_A + G_A * N_A:].T
        dt = _softplus(dt_ref[...] + dtb_ref[...])
        dec_t_s[...] = jnp.exp(dt * -jnp.exp(alog_ref[...])).T
        dt_x = _dot01(dt, _tile3(_head_expand(), 0), "lhs")
        xdt_t_s[...] = (xa * dt_x).T

    g0 = pl.multiple_of((h // HPG) * N_A, N_A)
    p0 = pl.multiple_of(h * P_A, P_A)
    b_t = bm_t_s[pl.ds(g0, N_A), :]
    c_t = cm_t_s[pl.ds(g0, N_A), :]
    dec = dec_t_s[pl.ds(h, 1), :]

    def per_8p(k, carry):
        r = pl.ds(pl.multiple_of(p0 + k * 8, 8), 8)
        x_rows = xdt_t_s[r, :]
        ys = []
        for u in range(8):
            h_new = hin_ref[k * 8 + u] * dec + x_rows[u:u + 1, :] * b_t
            hout_ref[k * 8 + u] = h_new
            ys.append(jnp.sum(h_new * c_t, axis=0, keepdims=True))
        yt_s[r, :] = jnp.concatenate(ys, axis=0)
        return carry

    lax.fori_loop(0, P_A // 8, per_8p, 0)

    @pl.when(h == H_A - 1)
    def _():
        xa = xa_s[...]
        y = yt_s[...].T + dx_ref[...] * xa
        y = y * zs_ref[...]
        ya_ref[...] = _rms_unit(y) * naw_ref[...]


def _ssd_sample(p, conv_st, h_ssd_t, lw, nb, l, prev):
    full = lambda w: _const_spec((nb, w))
    cw3 = (CONV_A - 1) * D_XBC
    st = pl.BlockSpec((None, None, P_A, N_A, nb), lambda h: (l, h, 0, 0, 0))
    in_specs = [full(D_A), full(D_XBC), full(DT_PAD), full(cw3),
                _const_spec((CONV_A, D_XBC)), _const_spec((1, D_XBC)), _const_spec((1, DT_PAD)),
                _const_spec((1, DT_PAD)), _const_spec((1, D_A)), _const_spec((1, D_A)), st]
    prev = () if prev is None else (prev,)
    in_specs += [pl.BlockSpec(memory_space=pl.ANY)] * len(prev)
    aliases = {len(in_specs) - 1: 2} if prev else {}
    keep = lambda w: pl.BlockSpec((nb, w), lambda h: (0, 0))
    scratch = [pltpu.VMEM((nb, D_A), F32), pltpu.VMEM((D_A, nb), F32), pltpu.VMEM((LANES, nb), F32),
               pltpu.VMEM((G_A * N_A, nb), F32), pltpu.VMEM((G_A * N_A, nb), F32),
               pltpu.VMEM((D_A, nb), F32)]
    ya, cs, hs = pl.pallas_call(
        functools.partial(_ssd_sample_body, n_prev=len(prev)), grid=(H_A,),
        in_specs=in_specs, out_specs=[keep(D_A), keep(cw3), st],
        out_shape=[jax.ShapeDtypeStruct((nb, D_A), F32), jax.ShapeDtypeStruct((nb, cw3), F32),
                   jax.ShapeDtypeStruct((DEPTH, H_A, P_A, N_A, nb), F32)],
        scratch_shapes=scratch, input_output_aliases=aliases,
        compiler_params=_params(1), name="ssd_sample")(
            p["z"], p["xbc"], p["dt"], conv_st.reshape(nb, cw3),
            lw["conv_a_w"], lw["conv_a_b"], lw["dt_bias"], lw["a_log"], lw["d_x"], lw["norm_a_w"],
            h_ssd_t, *prev)
    return ya, cs.reshape(nb, CONV_A - 1, D_XBC), hs


_RET_BB = 8


def _ret_sample_body(*refs, nb, n_prev):
    q_ref, k_ref, v_ref, gs_ref, cos_ref, sina_ref, sinb_ref, sin_ref = refs[:8]
    yb_ref, sout_ref, q_t_s, k_t_s, yr_s = refs[8 + n_prev:]
    i = pl.program_id(0)

    @pl.when(i == 0)
    def _():
        cos_f, sin_a, sin_b = cos_ref[...], sina_ref[...], sinb_ref[...]
        q_t_s[...] = _rope(q_ref[...], cos_f, sin_a, sin_b).T.astype(BF16)
        k_t_s[...] = (_rope(k_ref[...], cos_f, sin_a, sin_b) * (DK_R ** -0.5)).T.astype(BF16)

    rows_i = lax.broadcasted_iota(jnp.int32, (nb, DV_R), 0)
    blk = pl.ds(pl.multiple_of(i * _RET_BB, _RET_BB), _RET_BB)
    v_blk = v_ref[blk, :]
    ys = [[] for _ in range(H_R)]
    for j in range(_RET_BB):
        onehot = jnp.where(rows_i == i * _RET_BB + j, 1.0, 0.0).astype(BF16)
        k_bc = jnp.dot(k_t_s[...], onehot, preferred_element_type=F32)
        q_bc = jnp.dot(q_t_s[...], onehot, preferred_element_type=F32)
        for h in range(H_R):
            kl = slice(h * DK_R, (h + 1) * DK_R)
            vl = slice(h * DV_R, (h + 1) * DV_R)
            s_new = math.exp(_LOG_GAMMA[h]) * sin_ref[j, h] + k_bc[kl, :] * v_blk[j:j + 1, vl]
            sout_ref[j, h] = s_new
            ys[h].append(jnp.sum(q_bc[kl, :] * s_new, axis=0, keepdims=True))
    for h in range(H_R):
        yr_s[blk, h * DV_R:(h + 1) * DV_R] = jnp.concatenate(ys[h], axis=0)

    @pl.when(i == nb // _RET_BB - 1)
    def _():
        for h in range(H_R):
            vl = slice(h * DV_R, (h + 1) * DV_R)
            yb_ref[:, vl] = gs_ref[:, vl] * _rms_unit(yr_s[:, vl])


def _ret_sample(p, s_ret, rot, nb, l, prev):
    full = lambda w: _const_spec((nb, w))
    st = pl.BlockSpec((None, _RET_BB, H_R, DK_R, DV_R), lambda i: (l, i, 0, 0, 0))
    in_specs = [full(D_QK), full(D_QK), full(D_RV), full(D_RV),
                _const_spec((1, D_QK)), _const_spec((1, D_QK)), _const_spec((1, D_QK)), st]
    prev = () if prev is None else (prev,)
    in_specs += [pl.BlockSpec(memory_space=pl.ANY)] * len(prev)
    aliases = {len(in_specs) - 1: 1} if prev else {}
    return pl.pallas_call(
        functools.partial(_ret_sample_body, nb=nb, n_prev=len(prev)), grid=(nb // _RET_BB,),
        in_specs=in_specs, out_specs=[pl.BlockSpec((nb, D_RV), lambda i: (0, 0)), st],
        out_shape=[jax.ShapeDtypeStruct((nb, D_RV), F32),
                   jax.ShapeDtypeStruct((DEPTH, nb, H_R, DK_R, DV_R), F32)],
        scratch_shapes=[pltpu.VMEM((D_QK, nb), BF16), pltpu.VMEM((D_QK, nb), BF16),
                        pltpu.VMEM((nb, D_RV), F32)],
        input_output_aliases=aliases, compiler_params=_params(1), name="ret_sample")(
            p["q"], p["k"], p["v"], p["g"], rot["cos"], rot["sin_a"], rot["sin_b"], s_ret, *prev)


def _rotary_tables(start, length):
    half = DK_R // 2
    pos = start + jnp.arange(length, dtype=F32)
    freqs = ROPE_BASE ** (-jnp.arange(half, dtype=F32) / half)
    ang = pos[:, None] * freqs[None, :]
    cos, sin, zero = jnp.cos(ang), jnp.sin(ang), jnp.zeros_like(ang)
    tile = lambda a, b: jnp.tile(jnp.concatenate([a, b], axis=-1), (1, H_R))
    lg = jnp.log1p(-jnp.exp2(-5.0 - jnp.arange(H_R, dtype=F32)))
    return {"cos": tile(cos, cos), "sin_a": tile(-sin, zero), "sin_b": tile(zero, sin),
            "lgx": jnp.repeat(lg, DK_R)[None, :]}


def _layer_weights(l, w):
    w_in = w["w_in"][l]
    offs = np.cumsum([0, D_A, D_XBC, H_A, D_QK, D_QK, D_RV, D_RV, D_C, 3 * D_MODEL])
    cols = [w_in[:, offs[i]:offs[i + 1]] for i in range(9)]
    cols[2] = jnp.pad(cols[2], ((0, 0), (0, DT_PAD - H_A)))
    pad_row = lambda a: jnp.pad(a, (0, DT_PAD - H_A))[None, :]
    ab_re, ab_im, bb_re, bb_im = _s5_discretise(
        w["s5_a_re"][l], w["s5_a_im"][l], w["s5_log_dt"][l], w["s5_b_re"][l], w["s5_b_im"][l])
    bblk = jnp.concatenate([_block_diag(bb_re), _block_diag(bb_im)], axis=-1)
    c_t = lambda c: _block_diag(jnp.transpose(c, (0, 2, 1)))
    return {
        "norm_mix_w": w["norm_mix_w"][l][None, :],
        "w_in": [c.astype(BF16) for c in cols[:len(_SEGS)]], "w_gates": cols[-1].astype(BF16),
        "conv_a_w": w["conv_a_w"][l], "conv_a_b": w["conv_a_b"][l][None, :],
        "dt_bias": pad_row(w["dt_bias"][l]), "a_log": pad_row(w["a_log"][l]),
        "d_x": jnp.repeat(w["d_a"][l], P_A)[None, :], "norm_a_w": w["norm_a_w"][l][None, :],
        "s5_ab_re": ab_re.reshape(1, D_S5), "s5_ab_im": ab_im.reshape(1, D_S5),
        "s5_bblk": bblk.astype(BF16),
        "s5_cre": c_t(w["s5_c_re"][l]).astype(BF16), "s5_cim": c_t(w["s5_c_im"][l]).astype(BF16),
        "s5_d": w["s5_d"][l][None, :], "w_glu": w["w_glu"][l].astype(BF16),
        "b_glu": w["b_glu"][l][None, :],
        "w_br_a": w["w_br_a"][l].astype(BF16), "w_br_b": w["w_br_b"][l].astype(BF16),
        "w_br_c": w["w_br_c"][l].astype(BF16), "w_out": w["w_out"][l].astype(BF16),
        "norm_ffn_w": w["norm_ffn_w"][l][None, :], "w_up": w["w_up"][l].astype(BF16),
        "conv_f_w": w["conv_f_w"][l], "conv_f_b": w["conv_f_b"][l][None, :],
        "w_down": w["w_down"][l].astype(BF16),
    }


def _prompt_trunk(x, lws, nf, tm_proj, tsteps, tt, tm_merge, tm_ffn):
    nb, seq, _ = x.shape
    rot = _rotary_tables(0.0, seq)
    x2d = x.reshape(nb * seq, D_MODEL)
    zeros = jnp.zeros((nb, D_S5), F32)
    states, big = [], None
    for l, lw in enumerate(lws):
        p = _inproj(x2d, lw["norm_mix_w"], lw, nb, seq, tm_proj, conv=True)
        ya, yb, *big = _mix_prompt(p, rot, lw, nb, seq, tsteps, l, big)
        yc, h_re, h_im = _s5(p["u"], zeros, zeros, lw, nb, seq, tt)
        x2d = _merge(x2d, ya, yb, yc, lw, nb, seq, tm_merge)
        x2d, fs = _ffn(x2d, None, lw, nf, nb, seq, tm_ffn, final=(l == len(lws) - 1))
        states.append((p["conv_state"], h_re.reshape(nb, G_C, P_C), h_im.reshape(nb, G_C, P_C), fs))
    cs, h_re, h_im, fs = [jnp.stack(s) for s in zip(*states)]
    return x2d.reshape(nb, seq, D_MODEL), [cs, big[0], big[1], h_re, h_im, fs]


def _sample_trunk(x, st, lws, nf):
    nb = x.shape[0]
    rot = _rotary_tables(float(PAST_LEN), 1)
    x2d = x.reshape(nb, D_MODEL)
    st_conv, st_ssd, st_ret, st_re, st_im, st_ffn = st
    st_ssd_t = jnp.transpose(st_ssd, (0, 2, 3, 4, 1))
    states, hs, ss = [], None, None
    for l, lw in enumerate(lws):
        p = _inproj(x2d, lw["norm_mix_w"], lw, 1, nb, nb, conv=False)
        ya, cs, hs = _ssd_sample(p, st_conv[l], st_ssd_t, lw, nb, l, hs)
        yb, ss = _ret_sample(p, st_ret, rot, nb, l, ss)
        yc, h_re, h_im = _s5(p["u"], st_re[l].reshape(nb, D_S5), st_im[l].reshape(nb, D_S5),
                             lw, nb, 1, 1)
        x2d = _merge(x2d, ya, yb, yc, lw, 1, nb, nb)
        x2d, fs = _ffn(x2d, st_ffn[l].reshape(nb, -1), lw, nf, nb, 1, nb,
                       final=(l == len(lws) - 1))
        states.append((cs, h_re.reshape(nb, G_C, P_C), h_im.reshape(nb, G_C, P_C),
                       fs.reshape(nb, CONV_F - 1, 2 * D_FF)))
    cs, h_re, h_im, fs = [jnp.stack(s) for s in zip(*states)]
    hs = jnp.transpose(hs, (0, 4, 1, 2, 3))
    return x2d.reshape(nb, 1, D_MODEL), [cs, hs, ss, h_re, h_im, fs]


def kernel(x_prompt, x_sample, state_ssd_conv, state_ssd, state_ret, state_s5_re, state_s5_im,
           state_ffn_conv, norm_mix_w, w_in, conv_a_w, conv_a_b, dt_bias, a_log, d_a, norm_a_w,
           s5_a_re, s5_a_im, s5_log_dt, s5_b_re, s5_b_im, s5_c_re, s5_c_im, s5_d, w_glu, b_glu,
           w_br_a, w_br_b, w_br_c, w_out, norm_ffn_w, w_up, conv_f_w, conv_f_b, w_down, norm_f_w):
    w = dict(norm_mix_w=norm_mix_w, w_in=w_in, conv_a_w=conv_a_w, conv_a_b=conv_a_b,
             dt_bias=dt_bias, a_log=a_log, d_a=d_a, norm_a_w=norm_a_w, s5_a_re=s5_a_re,
             s5_a_im=s5_a_im, s5_log_dt=s5_log_dt, s5_b_re=s5_b_re, s5_b_im=s5_b_im,
             s5_c_re=s5_c_re, s5_c_im=s5_c_im, s5_d=s5_d, w_glu=w_glu, b_glu=b_glu,
             w_br_a=w_br_a, w_br_b=w_br_b, w_br_c=w_br_c, w_out=w_out, norm_ffn_w=norm_ffn_w,
             w_up=w_up, conv_f_w=conv_f_w, conv_f_b=conv_f_b, w_down=w_down)
    lws = [_layer_weights(l, w) for l in range(DEPTH)]
    nf = norm_f_w[None, :]
    seq = x_prompt.shape[1]
    y_p, p_st = _prompt_trunk(x_prompt, lws, nf, tm_proj=min(512, seq), tsteps=min(256, seq),
                              tt=min(32, seq), tm_merge=min(512, seq), tm_ffn=min(512, seq))
    y_s, s_st = _sample_trunk(
        x_sample, (state_ssd_conv, state_ssd, state_ret, state_s5_re, state_s5_im, state_ffn_conv),
        lws, nf)
    return (y_p, y_s, *p_st, *s_st)
```

```python
import functools
import math

import jax
import jax.numpy as jnp
import numpy as np
from jax import lax
from jax.experimental import pallas as pl
from jax.experimental.pallas import tpu as pltpu

F32 = jnp.float32
BF16 = jnp.bfloat16

D_MODEL = 1024
DEPTH = 2
PAST_LEN = 16384
H_A, P_A, N_A, G_A, CONV_A = 16, 64, 64, 2, 4
D_A = H_A * P_A
D_XBC = D_A + 2 * G_A * N_A
H_R, DK_R, DV_R = 8, 64, 128
D_QK = H_R * DK_R
D_RV = H_R * DV_R
ROPE_BASE = 10000.0
GS_C, G_C, P_C = 16, 64, 64
D_C = G_C * GS_C
D_S5 = G_C * P_C
D_FF = 2816
CONV_F = 3
CHUNK = 128
EPS = 1e-6
LANES = 128
DT_PAD = LANES
HPG = H_A // G_A
VMEM_LIMIT = 56 * 1024 * 1024

_LOG_GAMMA = [math.log1p(-(2.0 ** (-5.0 - h))) for h in range(H_R)]

_SEGS = (("z", D_A), ("xbc", D_XBC), ("q", D_QK), ("k", D_QK),
         ("v", D_RV), ("g", D_RV), ("u", D_C), ("dt", DT_PAD))
_SEG_OFF = np.cumsum([0] + [w for _, w in _SEGS]).tolist()
D_PROJ = _SEG_OFF[-1]


def _rms_unit(x):
    return x * lax.rsqrt(jnp.mean(x * x, axis=-1, keepdims=True) + EPS)


def _silu(x):
    return x * jax.nn.sigmoid(x)


def _softplus(x):
    return jnp.maximum(x, 0.0) + jnp.log1p(jnp.exp(-jnp.abs(x)))


def _bdot(a, b):
    return jnp.dot(a.astype(BF16), b.astype(BF16), preferred_element_type=F32)


def _split3(x):
    hi = x.astype(BF16)
    r1 = x - hi.astype(F32)
    mid = r1.astype(BF16)
    return hi, mid, (r1 - mid.astype(F32)).astype(BF16)


def _tile3(m01, axis):
    return jnp.concatenate([m01] * 3, axis=axis)


def _dot01(a, b, f32_side):
    if f32_side == "lhs":
        return jnp.dot(jnp.concatenate(_split3(a), axis=1), b, preferred_element_type=F32)
    return jnp.dot(a, jnp.concatenate(_split3(b), axis=0), preferred_element_type=F32)


def _const_spec(shape):
    nd = len(shape)
    return pl.BlockSpec(shape, lambda *_: (0,) * nd, pipeline_mode=pl.Buffered(1))


def _layer_spec(l, shape):
    nd = len(shape)
    return pl.BlockSpec((None,) + tuple(shape), lambda *_: (l,) + (0,) * nd,
                        pipeline_mode=pl.Buffered(1))


def _params(n_grid):
    return pltpu.CompilerParams(dimension_semantics=("arbitrary",) * n_grid,
                                vmem_limit_bytes=VMEM_LIMIT)


_PROJ_SUB = 128


def _inproj_body(x_ref, nw_ref, cw_ref, cb_ref, w_ref, *refs, tm, conv):
    n = len(_SEGS)
    if conv:
        cs_ref, xp_s = refs[n:]

        @pl.when(pl.program_id(1) == 0)
        def _():
            xp_s[0:8, :] = jnp.zeros((8, D_XBC), F32)

    for r in range(0, tm, _PROJ_SUB):
        rs = slice(r, r + _PROJ_SUB)
        h = (_rms_unit(x_ref[rs, :]) * nw_ref[...]).astype(BF16)
        for (name, width), off, o_ref in zip(_SEGS, _SEG_OFF, refs[:n]):
            y = jnp.dot(h, w_ref[:, off:off + width], preferred_element_type=F32)
            if name in ("z", "g"):
                y = _silu(y)
            elif name == "xbc" and conv:
                xp_s[8 + r:8 + r + _PROJ_SUB, :] = y
                y = cb_ref[...]
                for tap in range(CONV_A):
                    y = y + xp_s[5 + tap + r:5 + tap + r + _PROJ_SUB, :] * cw_ref[tap:tap + 1, :]
                y = _silu(y)
            o_ref[rs, :] = y
    if conv:
        tail = xp_s[tm + 5:tm + 8, :]
        cs_ref[...] = tail
        xp_s[5:8, :] = tail


def _inproj(x2d, nw, lw, nb, seq, tm, conv):
    nt = seq // tm
    in_specs = [pl.BlockSpec((tm, D_MODEL), lambda b, i: (b * nt + i, 0)),
                _const_spec((1, D_MODEL)), _const_spec((CONV_A, D_XBC)), _const_spec((1, D_XBC)),
                _layer_spec(lw["l"], (D_MODEL, D_PROJ))]
    out_specs, out_shape, scratch = [], [], []
    for _, width in _SEGS:
        out_specs.append(pl.BlockSpec((tm, width), lambda b, i: (b * nt + i, 0)))
        out_shape.append(jax.ShapeDtypeStruct((nb * seq, width), F32))
    if conv:
        out_specs.append(pl.BlockSpec((None, CONV_A - 1, D_XBC), lambda b, i: (b, 0, 0)))
        out_shape.append(jax.ShapeDtypeStruct((nb, CONV_A - 1, D_XBC), F32))
        scratch.append(pltpu.VMEM((8 + tm, D_XBC), F32))
    outs = pl.pallas_call(
        functools.partial(_inproj_body, tm=tm, conv=conv), grid=(nb, nt), in_specs=in_specs,
        out_specs=out_specs, out_shape=out_shape, scratch_shapes=scratch,
        compiler_params=_params(2), name="inproj")(
            x2d, nw, lw["conv_a_w"], lw["conv_a_b"], lw["w_proj"])
    res = {name: o for (name, _), o in zip(_SEGS, outs)}
    if conv:
        res["conv_state"] = outs[-1]
    return res


def _rope(x, cos_f, sin_a, sin_b):
    half = DK_R // 2
    return (x * cos_f + pltpu.roll(x, D_QK - half, 1) * sin_a
            + pltpu.roll(x, half, 1) * sin_b)


def _head_expand():
    lo = lax.broadcasted_iota(jnp.int32, (LANES, D_A), 0) * P_A
    c = lax.broadcasted_iota(jnp.int32, (LANES, D_A), 1)
    return jnp.where(c >= lo, jnp.where(c < lo + P_A, 1.0, 0.0), 0.0).astype(BF16)


def _mix_body(*refs, tsteps, nt, n_prev):
    (zs_ref, xc_ref, dt_ref, q_ref, k_ref, v_ref, gs_ref, cos_ref, sina_ref, sinb_ref, lgx_ref,
     dtb_ref, alog_ref, dx_ref, naw_ref) = refs[:15]
    (ya_ref, yb_ref, hs_ref, ss_ref,
     ht_s, s_s, intra_s, qdec_s, kdect_s) = refs[15 + n_prev:]
    i = pl.program_id(1)
    first = jnp.logical_and(pl.program_id(0) == 0, i == 0)
    rows_i = lax.broadcasted_iota(jnp.int32, (CHUNK, CHUNK), 0)
    cols_i = lax.broadcasted_iota(jnp.int32, (CHUNK, CHUNK), 1)
    causal = rows_i >= cols_i
    lane_lo = cols_i < LANES // 2

    @pl.when(first)
    def _():
        rel = (rows_i - cols_i).astype(F32)
        for h in range(H_R):
            intra_s[h] = jnp.where(causal, jnp.exp(jnp.maximum(rel, 0.0) * _LOG_GAMMA[h]), 0.0)
        ri = lax.broadcasted_iota(jnp.int32, (CHUNK, D_QK), 0).astype(F32)
        lgx = lgx_ref[...]
        qdec_s[...] = jnp.exp((ri + 1.0) * lgx)
        kdect_s[...] = jnp.exp((CHUNK - 1.0 - ri) * lgx).T

    @pl.when(i == 0)
    def _():
        ht_s[...] = jnp.zeros_like(ht_s)
        s_s[...] = jnp.zeros_like(s_s)

    tril = _tile3(jnp.where(causal, 1.0, 0.0).astype(BF16), 1)
    expand = _tile3(_head_expand(), 0)
    a_neg = -jnp.exp(alog_ref[...])

    for c in range(tsteps // CHUNK):
        r0 = c * CHUNK
        rows = slice(r0, r0 + CHUNK)
        xa = xc_ref[rows, :D_A]
        bm = xc_ref[rows, D_A:D_A + G_A * N_A]
        cm = xc_ref[rows, D_A + G_A * N_A:]
        dt = _softplus(dt_ref[rows, :] + dtb_ref[...])
        cum = _dot01(tril, dt * a_neg, "rhs")
        cum_t = cum.T
        dt_t = dt.T
        cum_last = cum[CHUNK - 1:CHUNK, :]
        scale = jnp.concatenate(
            [dt * jnp.exp(cum_last - cum), jnp.exp(cum),
             jnp.broadcast_to(jnp.exp(cum_last), (8, LANES))], axis=0)
        scale_x = _dot01(scale, expand, "lhs")
        w_x = scale_x[:CHUNK]
        ecum_x = scale_x[CHUNK:2 * CHUNK]
        dec_x = scale_x[2 * CHUNK:2 * CHUNK + 1]
        bm_t = bm.T
        ys = []
        for g in range(G_A):
            gl = slice(g * N_A, (g + 1) * N_A)
            hl = slice(g * HPG * P_A, (g + 1) * HPG * P_A)
            cg = cm[:, gl]
            scores = lax.dot_general(cg.astype(BF16), bm[:, gl].astype(BF16),
                                     (((1,), (1,)), ((), ())), preferred_element_type=F32)
            scores = jnp.where(causal, scores, 0.0)
            ht_g = ht_s[:, hl]
            y_inter = _bdot(cg, ht_g) * ecum_x[:, hl]
            for pair in range(HPG // 2):
                pl_ = slice(g * HPG * P_A + pair * LANES, g * HPG * P_A + (pair + 1) * LANES)
                x_pair = xa[:, pl_]
                sps = []
                for h in (g * HPG + 2 * pair, g * HPG + 2 * pair + 1):
                    seg = cum[:, h:h + 1] - cum_t[h:h + 1, :]
                    sps.append(scores * jnp.exp(jnp.minimum(seg, 0.0)) * dt_t[h:h + 1, :])
                rhs = jnp.concatenate([jnp.where(lane_lo, x_pair, 0.0),
                                       jnp.where(lane_lo, 0.0, x_pair)], axis=0)
                ys.append(_bdot(jnp.concatenate(sps, axis=1), rhs)
                          + y_inter[:, pair * LANES:(pair + 1) * LANES])
            ht_s[:, hl] = dec_x[:, hl] * ht_g + _bdot(bm_t[gl, :], xa[:, hl] * w_x[:, hl])
        y = jnp.concatenate(ys, axis=1) + dx_ref[...] * xa
        y = y * zs_ref[rows, :]
        ya_ref[rows, :] = _rms_unit(y) * naw_ref[...]

        cos_f, sin_a, sin_b = cos_ref[rows, :], sina_ref[rows, :], sinb_ref[rows, :]
        qr = _rope(q_ref[rows, :], cos_f, sin_a, sin_b)
        kr = _rope(k_ref[rows, :], cos_f, sin_a, sin_b) * (DK_R ** -0.5)
        k_t = kr.T
        kd_t = k_t * kdect_s[...]
        qd = qr * qdec_s[...]
        for pair in range(H_R // 2):
            pl_ = slice(pair * LANES, (pair + 1) * LANES)
            s_pair = s_s[pl_, :]
            for h, keep in ((2 * pair, lane_lo), (2 * pair + 1, jnp.logical_not(lane_lo))):
                kl = slice(h * DK_R, (h + 1) * DK_R)
                vl = slice(h * DV_R, (h + 1) * DV_R)
                v_h = v_ref[rows, vl]
                sc = _bdot(jnp.where(keep, qr[:, pl_], 0.0), k_t[pl_, :]) * intra_s[h]
                y_h = _bdot(jnp.concatenate([sc, jnp.where(keep, qd[:, pl_], 0.0)], axis=1),
                            jnp.concatenate([v_h, s_pair], axis=0))
                s_s[kl, :] = (math.exp(CHUNK * _LOG_GAMMA[h]) * s_s[kl, :]
                              + _bdot(kd_t[kl, :], v_h))
                yb_ref[rows, vl] = gs_ref[rows, vl] * _rms_unit(y_h)

    @pl.when(i == nt - 1)
    def _():
        for h in range(H_A):
            hs_ref[h] = ht_s[:, h * P_A:(h + 1) * P_A].T
        for h in range(H_R):
            ss_ref[h] = s_s[h * DK_R:(h + 1) * DK_R, :]


def _layer_block(l, tail):
    zeros = (0,) * len(tail)
    return pl.BlockSpec((None, None) + tail, lambda b, *_: (l, b) + zeros)


def _mix_prompt(p, rot, lw, nb, seq, tsteps, l, prev):
    nt = seq // tsteps
    row = lambda w: pl.BlockSpec((tsteps, w), lambda b, i: (b * nt + i, 0))
    tab = pl.BlockSpec((tsteps, D_QK), lambda b, i: (i, 0))
    in_specs = [row(D_A), row(D_XBC), row(DT_PAD), row(D_QK), row(D_QK), row(D_RV), row(D_RV),
                tab, tab, tab, _const_spec((1, D_QK)), _const_spec((1, DT_PAD)),
                _const_spec((1, DT_PAD)), _const_spec((1, D_A)), _const_spec((1, D_A))]
    prev = () if prev is None else tuple(prev)
    in_specs += [pl.BlockSpec(memory_space=pl.ANY)] * len(prev)
    aliases = {len(in_specs) - len(prev) + k: 2 + k for k in range(len(prev))}
    out_specs = [row(D_A), row(D_RV),
                 _layer_block(l, (H_A, P_A, N_A)), _layer_block(l, (H_R, DK_R, DV_R))]
    out_shape = [jax.ShapeDtypeStruct((nb * seq, D_A), F32),
                 jax.ShapeDtypeStruct((nb * seq, D_RV), F32),
                 jax.ShapeDtypeStruct((DEPTH, nb, H_A, P_A, N_A), F32),
                 jax.ShapeDtypeStruct((DEPTH, nb, H_R, DK_R, DV_R), F32)]
    scratch = [pltpu.VMEM((N_A, D_A), F32),
               pltpu.VMEM((D_QK, DV_R), F32), pltpu.VMEM((H_R, CHUNK, CHUNK), F32),
               pltpu.VMEM((CHUNK, D_QK), F32), pltpu.VMEM((D_QK, CHUNK), F32)]
    return pl.pallas_call(
        functools.partial(_mix_body, tsteps=tsteps, nt=nt, n_prev=len(prev)), grid=(nb, nt),
        in_specs=in_specs, out_specs=out_specs, out_shape=out_shape, scratch_shapes=scratch,
        input_output_aliases=aliases, compiler_params=_params(2), name="mix_prompt")(
            p["z"], p["xbc"], p["dt"], p["q"], p["k"], p["v"], p["g"],
            rot["cos"], rot["sin_a"], rot["sin_b"], rot["lgx"],
            lw["dt_bias"], lw["a_log"], lw["d_x"], lw["norm_a_w"], *prev)


def _s5_disc_body(are_ref, aim_ref, ldt_ref, bre_ref, bim_ref,
                  abre_ref, abim_ref, bbre_ref, bbim_ref):
    ar, ai = are_ref[...], aim_ref[...]
    dt = jnp.exp(ldt_ref[...])
    mag = jnp.exp(ar * dt)
    ab_re = mag * jnp.cos(ai * dt)
    ab_im = mag * jnp.sin(ai * dt)
    den = ar * ar + ai * ai
    num_re = ab_re - 1.0
    coef_re = (num_re * ar + ab_im * ai) / den
    coef_im = (ab_im * ar - num_re * ai) / den
    abre_ref[...] = ab_re
    abim_ref[...] = ab_im
    for c in range(GS_C):
        cl = slice(c * P_C, (c + 1) * P_C)
        br, bi = bre_ref[:, cl], bim_ref[:, cl]
        bbre_ref[:, cl] = coef_re * br - coef_im * bi
        bbim_ref[:, cl] = coef_re * bi + coef_im * br


def _s5_discretise(a_re, a_im, log_dt, b_re, b_im):
    gp = jax.ShapeDtypeStruct((G_C, P_C), F32)
    gcp = jax.ShapeDtypeStruct((G_C, GS_C * P_C), F32)
    b_t = lambda b: jnp.transpose(b, (0, 2, 1)).reshape(G_C, GS_C * P_C)
    ab_re, ab_im, bb_re, bb_im = pl.pallas_call(
        _s5_disc_body, out_shape=[gp, gp, gcp, gcp], name="s5_disc")(
            a_re, a_im, log_dt.reshape(G_C, 1), b_t(b_re), b_t(b_im))
    return ab_re, ab_im, bb_re.reshape(G_C, GS_C, P_C), bb_im.reshape(G_C, GS_C, P_C)


_S5_GB = LANES // GS_C
_S5_NBLK = G_C // _S5_GB
_S5_SB = _S5_GB * P_C


def _block_diag(m):
    g, r, c = m.shape
    m = m.reshape(_S5_NBLK, _S5_GB, r, c)
    eye = jnp.eye(_S5_GB, dtype=m.dtype)
    return jnp.einsum("jgrc,gk->jgrkc", m, eye).reshape(_S5_NBLK, _S5_GB * r, _S5_GB * c)


def _s5_body(u_ref, h0re_ref, h0im_ref, are_ref, aim_ref, bblk_ref, cre_ref, cim_ref,
             d_ref, wglu_ref, bglu_ref, yc_ref, hre_ref, him_ref,
             xre_s, xim_s, sre_s, sim_s, *, nb, tt):
    i = pl.program_id(0)
    rows = nb * tt

    @pl.when(i == 0)
    def _():
        sre_s[...] = h0re_ref[...]
        sim_s[...] = h0im_ref[...]

    u = u_ref[...]
    if tt > 1:
        u = jnp.swapaxes(u, 0, 1).reshape(rows, D_C)
    ub = u.astype(BF16)

    ys = []
    for j in range(_S5_NBLK):
        sl = slice(j * _S5_SB, (j + 1) * _S5_SB)
        bu = jnp.dot(ub[:, j * LANES:(j + 1) * LANES], bblk_ref[j], preferred_element_type=F32)
        xre_s[:, sl] = bu[:, :_S5_SB]
        xim_s[:, sl] = bu[:, _S5_SB:]
        ar = jnp.broadcast_to(are_ref[:, sl], (nb, _S5_SB))
        ai = jnp.broadcast_to(aim_ref[:, sl], (nb, _S5_SB))
        xr, xi = sre_s[:, sl], sim_s[:, sl]
        for t in range(tt):
            r = slice(t * nb, (t + 1) * nb)
            xr, xi = (ar * xr - ai * xi + xre_s[r, sl], ar * xi + ai * xr + xim_s[r, sl])
            xre_s[r, sl] = xr
            xim_s[r, sl] = xi
        sre_s[:, sl] = xr
        sim_s[:, sl] = xi
        ys.append(_bdot(xre_s[:, sl], cre_ref[j]) - _bdot(xim_s[:, sl], cim_ref[j]))
    y = jax.nn.gelu(jnp.concatenate(ys, axis=1) + d_ref[...] * u)
    y = y * jax.nn.sigmoid(_bdot(y, wglu_ref[...]) + bglu_ref[...])
    if tt > 1:
        y = jnp.swapaxes(y.reshape(tt, nb, D_C), 0, 1)
    yc_ref[...] = y
    hre_ref[...] = sre_s[...]
    him_ref[...] = sim_s[...]


def _s5(u, h0_re, h0_im, lw, nb, seq, tt):
    rows = tt * nb
    if tt > 1:
        assert nb % 8 == 0 and tt % 8 == 0
        u = u.reshape(nb, seq, D_C)
        io_spec = pl.BlockSpec((nb, tt, D_C), lambda i: (0, i, 0))
    else:
        io_spec = pl.BlockSpec((rows, D_C), lambda i: (i, 0))
    in_specs = [io_spec,
                _const_spec((nb, D_S5)), _const_spec((nb, D_S5)),
                _const_spec((1, D_S5)), _const_spec((1, D_S5)),
                _const_spec((_S5_NBLK, LANES, 2 * _S5_SB)),
                _const_spec((_S5_NBLK, _S5_SB, LANES)), _const_spec((_S5_NBLK, _S5_SB, LANES)),
                _const_spec((1, D_C)), _layer_spec(lw["l"], (D_C, D_C)), _const_spec((1, D_C))]
    st = pl.BlockSpec((nb, D_S5), lambda i: (0, 0))
    yc, h_re, h_im = pl.pallas_call(
        functools.partial(_s5_body, nb=nb, tt=tt), grid=(seq // tt,), in_specs=in_specs,
        out_specs=[io_spec, st, st],
        out_shape=[jax.ShapeDtypeStruct(u.shape, F32),
                   jax.ShapeDtypeStruct((nb, D_S5), F32), jax.ShapeDtypeStruct((nb, D_S5), F32)],
        scratch_shapes=[pltpu.VMEM((rows, D_S5), F32), pltpu.VMEM((rows, D_S5), F32),
                        pltpu.VMEM((nb, D_S5), F32), pltpu.VMEM((nb, D_S5), F32)],
        compiler_params=_params(1), name="s5")(
            u, h0_re, h0_im, lw["s5_ab_re"], lw["s5_ab_im"], lw["s5_bblk"],
            lw["s5_cre"], lw["s5_cim"], lw["s5_d"], lw["w_glu"], lw["b_glu"])
    return yc.reshape(nb * seq, D_C), h_re, h_im


def _merge_body(x_ref, nw_ref, ya_ref, yb_ref, yc_ref, wg_ref, wa_ref, wb_ref, wc_ref, wo_ref,
                o_ref):
    x = x_ref[...]
    h = (_rms_unit(x) * nw_ref[...]).astype(BF16)
    merged = None
    for k, (y_ref, w_ref) in enumerate(((ya_ref, wa_ref), (yb_ref, wb_ref), (yc_ref, wc_ref))):
        gate = jax.nn.sigmoid(jnp.dot(h, wg_ref[:, k * D_MODEL:(k + 1) * D_MODEL],
                                      preferred_element_type=F32))
        term = gate * _bdot(y_ref[...], w_ref[...])
        merged = term if merged is None else merged + term
    o_ref[...] = x + _bdot(merged, wo_ref[...])


def _merge(x2d, ya, yb, yc, lw, nb, seq, tm):
    nt = seq // tm
    row = lambda w: pl.BlockSpec((tm, w), lambda b, i: (b * nt + i, 0))
    wspec = _layer_spec(lw["l"], (D_MODEL, D_MODEL))
    return pl.pallas_call(
        _merge_body, grid=(nb, nt),
        in_specs=[row(D_MODEL), _const_spec((1, D_MODEL)), row(D_A), row(D_RV), row(D_C),
                  _layer_spec(lw["l"], (D_MODEL, 3 * D_MODEL)), wspec, wspec, wspec, wspec],
        out_specs=row(D_MODEL), out_shape=jax.ShapeDtypeStruct((nb * seq, D_MODEL), F32),
        compiler_params=_params(2), name="merge")(
            x2d, lw["norm_mix_w"], ya, yb, yc, lw["w_gates"],
            lw["w_br_a"], lw["w_br_b"], lw["w_br_c"], lw["w_out"])


_FFN_CW = 256


def _ffn_tail(x, up_s, act_s, cw_ref, cb_ref, wdn_ref, tm):
    for c in range(0, D_FF, _FFN_CW):
        def conv(c0):
            cl = slice(c0, c0 + _FFN_CW)
            y = cb_ref[:, cl]
            for tap in range(CONV_F):
                y = y + up_s[6 + tap:6 + tap + tm, cl] * cw_ref[tap:tap + 1, cl]
            return y
        act_s[:, c:c + _FFN_CW] = (_silu(conv(c)) * conv(D_FF + c)).astype(BF16)
    return x + jnp.dot(act_s[...], wdn_ref[...], preferred_element_type=F32)


def _ffn_prompt_body(x_ref, nw_ref, wup_ref, cw_ref, cb_ref, wdn_ref, nf_ref,
                     o_ref, cs_ref, up_s, act_s, *, tm, final):
    @pl.when(pl.program_id(1) == 0)
    def _():
        up_s[0:8, :] = jnp.zeros((8, 2 * D_FF), F32)

    x = x_ref[...]
    hf = (_rms_unit(x) * nw_ref[...]).astype(BF16)
    up_s[8:8 + tm, :] = jnp.dot(hf, wup_ref[...], preferred_element_type=F32)
    out = _ffn_tail(x, up_s, act_s, cw_ref, cb_ref, wdn_ref, tm)
    o_ref[...] = _rms_unit(out) * nf_ref[...] if final else out
    tail = up_s[tm + 6:tm + 8, :]
    cs_ref[...] = tail
    up_s[6:8, :] = tail


def _ffn_sample_body(x_ref, st_ref, nw_ref, wup_ref, cw_ref, cb_ref, wdn_ref, nf_ref,
                     o_ref, cs_ref, act_s, *, final):
    x = x_ref[...]
    hf = (_rms_unit(x) * nw_ref[...]).astype(BF16)
    up = jnp.dot(hf, wup_ref[...], preferred_element_type=F32)
    prev2, prev1 = st_ref[:, :2 * D_FF], st_ref[:, 2 * D_FF:]
    cs_ref[:, :2 * D_FF] = prev1
    cs_ref[:, 2 * D_FF:] = up
    for c in range(0, D_FF, _FFN_CW):
        def conv(c0):
            cl = slice(c0, c0 + _FFN_CW)
            return (cb_ref[:, cl] + prev2[:, cl] * cw_ref[0:1, cl] + prev1[:, cl] * cw_ref[1:2, cl]
                    + up[:, cl] * cw_ref[2:3, cl])
        act_s[:, c:c + _FFN_CW] = (_silu(conv(c)) * conv(D_FF + c)).astype(BF16)
    out = x + jnp.dot(act_s[...], wdn_ref[...], preferred_element_type=F32)
    o_ref[...] = _rms_unit(out) * nf_ref[...] if final else out


def _ffn(x2d, state, lw, nf, nb, seq, tm, final):
    wspecs = [_const_spec((1, D_MODEL)), _layer_spec(lw["l"], (D_MODEL, 2 * D_FF)),
              _const_spec((CONV_F, 2 * D_FF)), _const_spec((1, 2 * D_FF)),
              _layer_spec(lw["l"], (D_FF, D_MODEL)), _const_spec((1, D_MODEL))]
    wargs = (lw["norm_ffn_w"], lw["w_up"], lw["conv_f_w"], lw["conv_f_b"], lw["w_down"], nf)
    act = pltpu.VMEM((tm, D_FF), BF16)
    if state is None:
        nt = seq // tm
        row = pl.BlockSpec((tm, D_MODEL), lambda b, i: (b * nt + i, 0))
        return pl.pallas_call(
            functools.partial(_ffn_prompt_body, tm=tm, final=final), grid=(nb, nt),
            in_specs=[row] + wspecs,
            out_specs=[row, pl.BlockSpec((None, CONV_F - 1, 2 * D_FF), lambda b, i: (b, 0, 0))],
            out_shape=[jax.ShapeDtypeStruct((nb * seq, D_MODEL), F32),
                       jax.ShapeDtypeStruct((nb, CONV_F - 1, 2 * D_FF), F32)],
            scratch_shapes=[pltpu.VMEM((8 + tm, 2 * D_FF), F32), act],
            compiler_params=_params(2), name="ffn_prompt")(x2d, *wargs)
    sw = (CONV_F - 1) * 2 * D_FF
    return pl.pallas_call(
        functools.partial(_ffn_sample_body, final=final), grid=(nb // tm,),
        in_specs=[pl.BlockSpec((tm, D_MODEL), lambda i: (i, 0)),
                  pl.BlockSpec((tm, sw), lambda i: (i, 0))] + wspecs,
        out_specs=[pl.BlockSpec((tm, D_MODEL), lambda i: (i, 0)),
                   pl.BlockSpec((tm, sw), lambda i: (i, 0))],
        out_shape=[jax.ShapeDtypeStruct((nb, D_MODEL), F32), jax.ShapeDtypeStruct((nb, sw), F32)],
        scratch_shapes=[act], compiler_params=_params(1), name="ffn_sample")(x2d, state, *wargs)


def _ssd_sample_body(*refs, n_prev):
    (zs_ref, xbc_ref, dt_ref, cst_ref, cw_ref, cb_ref, dtb_ref, alog_ref, dx_ref, naw_ref,
     hin_ref) = refs[:11]
    (ya_ref, cso_ref, hout_ref, xa_s, xdt_t_s, dec_t_s, bm_t_s, cm_t_s, yt_s) = refs[11 + n_prev:]
    h = pl.program_id(0)

    @pl.when(h == 0)
    def _():
        acc = cb_ref[...]
        for tap in range(CONV_A - 1):
            acc = acc + cst_ref[:, tap * D_XBC:(tap + 1) * D_XBC] * cw_ref[tap:tap + 1, :]
        acc = acc + xbc_ref[...] * cw_ref[CONV_A - 1:CONV_A, :]
        cso_ref[:, :(CONV_A - 2) * D_XBC] = cst_ref[:, D_XBC:]
        cso_ref[:, (CONV_A - 2) * D_XBC:] = xbc_ref[...]
        xc = _silu(acc)
        xa = xc[:, :D_A]
        xa_s[...] = xa
        bm_t_s[...] = xc[:, D_A:D_A + G_A * N_A].T
        cm_t_s[...] = xc[:, D_A + G_A * N_A:].T
        dt = _softplus(dt_ref[...] + dtb_ref[...])
        dec_t_s[...] = jnp.exp(dt * -jnp.exp(alog_ref[...])).T
        dt_x = _dot01(dt, _tile3(_head_expand(), 0), "lhs")
        xdt_t_s[...] = (xa * dt_x).T

    g0 = pl.multiple_of((h // HPG) * N_A, N_A)
    p0 = pl.multiple_of(h * P_A, P_A)
    b_t = bm_t_s[pl.ds(g0, N_A), :]
    c_t = cm_t_s[pl.ds(g0, N_A), :]
    dec = dec_t_s[pl.ds(h, 1), :]

    def per_8p(k, carry):
        r = pl.ds(pl.multiple_of(p0 + k * 8, 8), 8)
        x_rows = xdt_t_s[r, :]
        ys = []
        for u in range(8):
            h_new = hin_ref[k * 8 + u] * dec + x_rows[u:u + 1, :] * b_t
            hout_ref[k * 8 + u] = h_new
            ys.append(jnp.sum(h_new * c_t, axis=0, keepdims=True))
        yt_s[r, :] = jnp.concatenate(ys, axis=0)
        return carry

    lax.fori_loop(0, P_A // 8, per_8p, 0)

    @pl.when(h == H_A - 1)
    def _():
        xa = xa_s[...]
        y = yt_s[...].T + dx_ref[...] * xa
        y = y * zs_ref[...]
        ya_ref[...] = _rms_unit(y) * naw_ref[...]


def _ssd_sample(p, conv_st, h_ssd_t, lw, nb, l, prev):
    full = lambda w: _const_spec((nb, w))
    cw3 = (CONV_A - 1) * D_XBC
    st = pl.BlockSpec((None, None, P_A, N_A, nb), lambda h: (l, h, 0, 0, 0))
    in_specs = [full(D_A), full(D_XBC), full(DT_PAD), full(cw3),
                _const_spec((CONV_A, D_XBC)), _const_spec((1, D_XBC)), _const_spec((1, DT_PAD)),
                _const_spec((1, DT_PAD)), _const_spec((1, D_A)), _const_spec((1, D_A)), st]
    prev = () if prev is None else (prev,)
    in_specs += [pl.BlockSpec(memory_space=pl.ANY)] * len(prev)
    aliases = {len(in_specs) - 1: 2} if prev else {}
    keep = lambda w: pl.BlockSpec((nb, w), lambda h: (0, 0))
    scratch = [pltpu.VMEM((nb, D_A), F32), pltpu.VMEM((D_A, nb), F32), pltpu.VMEM((LANES, nb), F32),
               pltpu.VMEM((G_A * N_A, nb), F32), pltpu.VMEM((G_A * N_A, nb), F32),
               pltpu.VMEM((D_A, nb), F32)]
    ya, cs, hs = pl.pallas_call(
        functools.partial(_ssd_sample_body, n_prev=len(prev)), grid=(H_A,),
        in_specs=in_specs, out_specs=[keep(D_A), keep(cw3), st],
        out_shape=[jax.ShapeDtypeStruct((nb, D_A), F32), jax.ShapeDtypeStruct((nb, cw3), F32),
                   jax.ShapeDtypeStruct((DEPTH, H_A, P_A, N_A, nb), F32)],
        scratch_shapes=scratch, input_output_aliases=aliases,
        compiler_params=_params(1), name="ssd_sample")(
            p["z"], p["xbc"], p["dt"], conv_st.reshape(nb, cw3),
            lw["conv_a_w"], lw["conv_a_b"], lw["dt_bias"], lw["a_log"], lw["d_x"], lw["norm_a_w"],
            h_ssd_t, *prev)
    return ya, cs.reshape(nb, CONV_A - 1, D_XBC), hs


_RET_BB = 8


def _ret_sample_body(*refs, nb, n_prev):
    q_ref, k_ref, v_ref, gs_ref, cos_ref, sina_ref, sinb_ref, sin_ref = refs[:8]
    yb_ref, sout_ref, q_t_s, k_t_s, yr_s = refs[8 + n_prev:]
    i = pl.program_id(0)

    @pl.when(i == 0)
    def _():
        cos_f, sin_a, sin_b = cos_ref[...], sina_ref[...], sinb_ref[...]
        q_t_s[...] = _rope(q_ref[...], cos_f, sin_a, sin_b).T.astype(BF16)
        k_t_s[...] = (_rope(k_ref[...], cos_f, sin_a, sin_b) * (DK_R ** -0.5)).T.astype(BF16)

    rows_i = lax.broadcasted_iota(jnp.int32, (nb, DV_R), 0)
    blk = pl.ds(pl.multiple_of(i * _RET_BB, _RET_BB), _RET_BB)
    v_blk = v_ref[blk, :]
    ys = [[] for _ in range(H_R)]
    for j in range(_RET_BB):
        onehot = jnp.where(rows_i == i * _RET_BB + j, 1.0, 0.0).astype(BF16)
        k_bc = jnp.dot(k_t_s[...], onehot, preferred_element_type=F32)
        q_bc = jnp.dot(q_t_s[...], onehot, preferred_element_type=F32)
        for h in range(H_R):
            kl = slice(h * DK_R, (h + 1) * DK_R)
            vl = slice(h * DV_R, (h + 1) * DV_R)
            s_new = math.exp(_LOG_GAMMA[h]) * sin_ref[j, h] + k_bc[kl, :] * v_blk[j:j + 1, vl]
            sout_ref[j, h] = s_new
            ys[h].append(jnp.sum(q_bc[kl, :] * s_new, axis=0, keepdims=True))
    for h in range(H_R):
        yr_s[blk, h * DV_R:(h + 1) * DV_R] = jnp.concatenate(ys[h], axis=0)

    @pl.when(i == nb // _RET_BB - 1)
    def _():
        for h in range(H_R):
            vl = slice(h * DV_R, (h + 1) * DV_R)
            yb_ref[:, vl] = gs_ref[:, vl] * _rms_unit(yr_s[:, vl])


def _ret_sample(p, s_ret, rot, nb, l, prev):
    full = lambda w: _const_spec((nb, w))
    st = pl.BlockSpec((None, _RET_BB, H_R, DK_R, DV_R), lambda i: (l, i, 0, 0, 0))
    in_specs = [full(D_QK), full(D_QK), full(D_RV), full(D_RV),
                _const_spec((1, D_QK)), _const_spec((1, D_QK)), _const_spec((1, D_QK)), st]
    prev = () if prev is None else (prev,)
    in_specs += [pl.BlockSpec(memory_space=pl.ANY)] * len(prev)
    aliases = {len(in_specs) - 1: 1} if prev else {}
    return pl.pallas_call(
        functools.partial(_ret_sample_body, nb=nb, n_prev=len(prev)), grid=(nb // _RET_BB,),
        in_specs=in_specs, out_specs=[pl.BlockSpec((nb, D_RV), lambda i: (0, 0)), st],
        out_shape=[jax.ShapeDtypeStruct((nb, D_RV), F32),
                   jax.ShapeDtypeStruct((DEPTH, nb, H_R, DK_R, DV_R), F32)],
        scratch_shapes=[pltpu.VMEM((D_QK, nb), BF16), pltpu.VMEM((D_QK, nb), BF16),
                        pltpu.VMEM((nb, D_RV), F32)],
        input_output_aliases=aliases, compiler_params=_params(1), name="ret_sample")(
            p["q"], p["k"], p["v"], p["g"], rot["cos"], rot["sin_a"], rot["sin_b"], s_ret, *prev)


def _rotary_tables(start, length):
    half = DK_R // 2
    pos = start + jnp.arange(length, dtype=F32)
    freqs = ROPE_BASE ** (-jnp.arange(half, dtype=F32) / half)
    ang = pos[:, None] * freqs[None, :]
    cos, sin, zero = jnp.cos(ang), jnp.sin(ang), jnp.zeros_like(ang)
    tile = lambda a, b: jnp.tile(jnp.concatenate([a, b], axis=-1), (1, H_R))
    lg = jnp.log1p(-jnp.exp2(-5.0 - jnp.arange(H_R, dtype=F32)))
    return {"cos": tile(cos, cos), "sin_a": tile(-sin, zero), "sin_b": tile(zero, sin),
            "lgx": jnp.repeat(lg, DK_R)[None, :]}


def _matmul_weights(w):
    w_in = w["w_in"]
    o_dt = D_A + D_XBC
    o_gates = w_in.shape[-1] - 3 * D_MODEL
    dt_cols = jnp.pad(w_in[..., o_dt:o_dt + H_A], ((0, 0), (0, 0), (0, DT_PAD - H_A)))
    w_proj = jnp.concatenate([w_in[..., :o_dt], w_in[..., o_dt + H_A:o_gates], dt_cols], axis=-1)
    out = {"w_proj": w_proj.astype(BF16), "w_gates": w_in[..., o_gates:].astype(BF16)}
    for name in ("w_glu", "w_br_a", "w_br_b", "w_br_c", "w_out", "w_up", "w_down"):
        out[name] = w[name].astype(BF16)
    return out


def _layer_weights(l, w, mats):
    pad_row = lambda a: jnp.pad(a, (0, DT_PAD - H_A))[None, :]
    ab_re, ab_im, bb_re, bb_im = _s5_discretise(
        w["s5_a_re"][l], w["s5_a_im"][l], w["s5_log_dt"][l], w["s5_b_re"][l], w["s5_b_im"][l])
    bblk = jnp.concatenate([_block_diag(bb_re), _block_diag(bb_im)], axis=-1)
    c_t = lambda c: _block_diag(jnp.transpose(c, (0, 2, 1)))
    return {
        "l": l, **mats,
        "norm_mix_w": w["norm_mix_w"][l][None, :],
        "conv_a_w": w["conv_a_w"][l], "conv_a_b": w["conv_a_b"][l][None, :],
        "dt_bias": pad_row(w["dt_bias"][l]), "a_log": pad_row(w["a_log"][l]),
        "d_x": jnp.repeat(w["d_a"][l], P_A)[None, :], "norm_a_w": w["norm_a_w"][l][None, :],
        "s5_ab_re": ab_re.reshape(1, D_S5), "s5_ab_im": ab_im.reshape(1, D_S5),
        "s5_bblk": bblk.astype(BF16),
        "s5_cre": c_t(w["s5_c_re"][l]).astype(BF16), "s5_cim": c_t(w["s5_c_im"][l]).astype(BF16),
        "s5_d": w["s5_d"][l][None, :], "b_glu": w["b_glu"][l][None, :],
        "norm_ffn_w": w["norm_ffn_w"][l][None, :],
        "conv_f_w": w["conv_f_w"][l], "conv_f_b": w["conv_f_b"][l][None, :],
    }


def _prompt_trunk(x, lws, nf, tm_proj, tsteps, tt, tm_merge, tm_ffn):
    nb, seq, _ = x.shape
    rot = _rotary_tables(0.0, seq)
    x2d = x.reshape(nb * seq, D_MODEL)
    zeros = jnp.zeros((nb, D_S5), F32)
    states, big = [], None
    for l, lw in enumerate(lws):
        p = _inproj(x2d, lw["norm_mix_w"], lw, nb, seq, tm_proj, conv=True)
        ya, yb, *big = _mix_prompt(p, rot, lw, nb, seq, tsteps, l, big)
        yc, h_re, h_im = _s5(p["u"], zeros, zeros, lw, nb, seq, tt)
        x2d = _merge(x2d, ya, yb, yc, lw, nb, seq, tm_merge)
        x2d, fs = _ffn(x2d, None, lw, nf, nb, seq, tm_ffn, final=(l == len(lws) - 1))
        states.append((p["conv_state"], h_re.reshape(nb, G_C, P_C), h_im.reshape(nb, G_C, P_C), fs))
    cs, h_re, h_im, fs = [jnp.stack(s) for s in zip(*states)]
    return x2d.reshape(nb, seq, D_MODEL), [cs, big[0], big[1], h_re, h_im, fs]


def _sample_trunk(x, st, lws, nf):
    nb = x.shape[0]
    rot = _rotary_tables(float(PAST_LEN), 1)
    x2d = x.reshape(nb, D_MODEL)
    st_conv, st_ssd, st_ret, st_re, st_im, st_ffn = st
    st_ssd_t = jnp.transpose(st_ssd, (0, 2, 3, 4, 1))
    states, hs, ss = [], None, None
    for l, lw in enumerate(lws):
        p = _inproj(x2d, lw["norm_mix_w"], lw, 1, nb, nb, conv=False)
        ya, cs, hs = _ssd_sample(p, st_conv[l], st_ssd_t, lw, nb, l, hs)
        yb, ss = _ret_sample(p, st_ret, rot, nb, l, ss)
        yc, h_re, h_im = _s5(p["u"], st_re[l].reshape(nb, D_S5), st_im[l].reshape(nb, D_S5),
                             lw, nb, 1, 1)
        x2d = _merge(x2d, ya, yb, yc, lw, 1, nb, nb)
        x2d, fs = _ffn(x2d, st_ffn[l].reshape(nb, -1), lw, nf, nb, 1, nb,
                       final=(l == len(lws) - 1))
        states.append((cs, h_re.reshape(nb, G_C, P_C), h_im.reshape(nb, G_C, P_C),
                       fs.reshape(nb, CONV_F - 1, 2 * D_FF)))
    cs, h_re, h_im, fs = [jnp.stack(s) for s in zip(*states)]
    hs = jnp.transpose(hs, (0, 4, 1, 2, 3))
    return x2d.reshape(nb, 1, D_MODEL), [cs, hs, ss, h_re, h_im, fs]


def kernel(x_prompt, x_sample, state_ssd_conv, state_ssd, state_ret, state_s5_re, state_s5_im,
           state_ffn_conv, norm_mix_w, w_in, conv_a_w, conv_a_b, dt_bias, a_log, d_a, norm_a_w,
           s5_a_re, s5_a_im, s5_log_dt, s5_b_re, s5_b_im, s5_c_re, s5_c_im, s5_d, w_glu, b_glu,
           w_br_a, w_br_b, w_br_c, w_out, norm_ffn_w, w_up, conv_f_w, conv_f_b, w_down, norm_f_w):
    w = dict(norm_mix_w=norm_mix_w, w_in=w_in, conv_a_w=conv_a_w, conv_a_b=conv_a_b,
             dt_bias=dt_bias, a_log=a_log, d_a=d_a, norm_a_w=norm_a_w, s5_a_re=s5_a_re,
             s5_a_im=s5_a_im, s5_log_dt=s5_log_dt, s5_b_re=s5_b_re, s5_b_im=s5_b_im,
             s5_c_re=s5_c_re, s5_c_im=s5_c_im, s5_d=s5_d, w_glu=w_glu, b_glu=b_glu,
             w_br_a=w_br_a, w_br_b=w_br_b, w_br_c=w_br_c, w_out=w_out, norm_ffn_w=norm_ffn_w,
             w_up=w_up, conv_f_w=conv_f_w, conv_f_b=conv_f_b, w_down=w_down)
    mats = _matmul_weights(w)
    lws = [_layer_weights(l, w, mats) for l in range(DEPTH)]
    nf = norm_f_w[None, :]
    seq = x_prompt.shape[1]
    y_p, p_st = _prompt_trunk(x_prompt, lws, nf, tm_proj=min(512, seq), tsteps=min(256, seq),
                              tt=min(32, seq), tm_merge=min(512, seq), tm_ffn=min(512, seq))
    y_s, s_st = _sample_trunk(
        x_sample, (state_ssd_conv, state_ssd, state_ret, state_s5_re, state_s5_im, state_ffn_conv),
        lws, nf)
    return (y_p, y_s, *p_st, *s_st)
```

```python
import functools
import math

import jax
import jax.numpy as jnp
import numpy as np
from jax import lax
from jax.experimental import pallas as pl
from jax.experimental.pallas import tpu as pltpu

F32 = jnp.float32
BF16 = jnp.bfloat16

D_MODEL = 1024
DEPTH = 2
PAST_LEN = 16384
H_A, P_A, N_A, G_A, CONV_A = 16, 64, 64, 2, 4
D_A = H_A * P_A
D_XBC = D_A + 2 * G_A * N_A
H_R, DK_R, DV_R = 8, 64, 128
D_QK = H_R * DK_R
D_RV = H_R * DV_R
ROPE_BASE = 10000.0
GS_C, G_C, P_C = 16, 64, 64
D_C = G_C * GS_C
D_S5 = G_C * P_C
D_FF = 2816
CONV_F = 3
CHUNK = 128
EPS = 1e-6
LANES = 128
DT_PAD = LANES
HPG = H_A // G_A
VMEM_LIMIT = 56 * 1024 * 1024

_LOG_GAMMA = [math.log1p(-(2.0 ** (-5.0 - h))) for h in range(H_R)]

_SEGS = (("z", D_A), ("xbc", D_XBC), ("q", D_QK), ("k", D_QK),
         ("v", D_RV), ("g", D_RV), ("u", D_C), ("dt", DT_PAD))
_SEG_OFF = np.cumsum([0] + [w for _, w in _SEGS]).tolist()
D_PROJ = _SEG_OFF[-1]


def _rms_unit(x):
    return x * lax.rsqrt(jnp.mean(x * x, axis=-1, keepdims=True) + EPS)


def _silu(x):
    return x * jax.nn.sigmoid(x)


def _softplus(x):
    return jnp.maximum(x, 0.0) + jnp.log1p(jnp.exp(-jnp.abs(x)))


def _bdot(a, b):
    return jnp.dot(a.astype(BF16), b.astype(BF16), preferred_element_type=F32)


def _split3(x):
    hi = x.astype(BF16)
    r1 = x - hi.astype(F32)
    mid = r1.astype(BF16)
    return hi, mid, (r1 - mid.astype(F32)).astype(BF16)


def _tile3(m01, axis):
    return jnp.concatenate([m01] * 3, axis=axis)


def _dot01(a, b, f32_side):
    if f32_side == "lhs":
        return jnp.dot(jnp.concatenate(_split3(a), axis=1), b, preferred_element_type=F32)
    return jnp.dot(a, jnp.concatenate(_split3(b), axis=0), preferred_element_type=F32)


def _const_spec(shape):
    nd = len(shape)
    return pl.BlockSpec(shape, lambda *_: (0,) * nd, pipeline_mode=pl.Buffered(1))


def _layer_spec(l, shape):
    nd = len(shape)
    return pl.BlockSpec((None,) + tuple(shape), lambda *_: (l,) + (0,) * nd,
                        pipeline_mode=pl.Buffered(1))


def _params(n_grid):
    return pltpu.CompilerParams(dimension_semantics=("arbitrary",) * n_grid,
                                vmem_limit_bytes=VMEM_LIMIT)


_PROJ_SUB = 128


def _inproj_body(x_ref, nw_ref, cw_ref, cb_ref, w_ref, *refs, tm, conv):
    n = len(_SEGS)
    if conv:
        cs_ref, xp_s = refs[n:]

        @pl.when(pl.program_id(1) == 0)
        def _():
            xp_s[0:8, :] = jnp.zeros((8, D_XBC), F32)

    for r in range(0, tm, _PROJ_SUB):
        rs = slice(r, r + _PROJ_SUB)
        h = (_rms_unit(x_ref[rs, :]) * nw_ref[...]).astype(BF16)
        for (name, width), off, o_ref in zip(_SEGS, _SEG_OFF, refs[:n]):
            y = jnp.dot(h, w_ref[:, off:off + width], preferred_element_type=F32)
            if name in ("z", "g"):
                y = _silu(y)
            elif name == "xbc" and conv:
                xp_s[8 + r:8 + r + _PROJ_SUB, :] = y
                y = cb_ref[...]
                for tap in range(CONV_A):
                    y = y + xp_s[5 + tap + r:5 + tap + r + _PROJ_SUB, :] * cw_ref[tap:tap + 1, :]
                y = _silu(y)
            o_ref[rs, :] = y
    if conv:
        tail = xp_s[tm + 5:tm + 8, :]
        cs_ref[...] = tail
        xp_s[5:8, :] = tail


def _inproj(x2d, nw, lw, nb, seq, tm, conv):
    nt = seq // tm
    in_specs = [pl.BlockSpec((tm, D_MODEL), lambda b, i: (b * nt + i, 0)),
                _const_spec((1, D_MODEL)), _const_spec((CONV_A, D_XBC)), _const_spec((1, D_XBC)),
                _layer_spec(lw["l"], (D_MODEL, D_PROJ))]
    out_specs, out_shape, scratch = [], [], []
    for _, width in _SEGS:
        out_specs.append(pl.BlockSpec((tm, width), lambda b, i: (b * nt + i, 0)))
        out_shape.append(jax.ShapeDtypeStruct((nb * seq, width), F32))
    if conv:
        out_specs.append(pl.BlockSpec((None, CONV_A - 1, D_XBC), lambda b, i: (b, 0, 0)))
        out_shape.append(jax.ShapeDtypeStruct((nb, CONV_A - 1, D_XBC), F32))
        scratch.append(pltpu.VMEM((8 + tm, D_XBC), F32))
    outs = pl.pallas_call(
        functools.partial(_inproj_body, tm=tm, conv=conv), grid=(nb, nt), in_specs=in_specs,
        out_specs=out_specs, out_shape=out_shape, scratch_shapes=scratch,
        compiler_params=_params(2), name="inproj")(
            x2d, nw, lw["conv_a_w"], lw["conv_a_b"], lw["w_proj"])
    res = {name: o for (name, _), o in zip(_SEGS, outs)}
    if conv:
        res["conv_state"] = outs[-1]
    return res


def _rope(x, cos_f, sin_a, sin_b):
    half = DK_R // 2
    return (x * cos_f + pltpu.roll(x, D_QK - half, 1) * sin_a
            + pltpu.roll(x, half, 1) * sin_b)


def _head_expand():
    lo = lax.broadcasted_iota(jnp.int32, (LANES, D_A), 0) * P_A
    c = lax.broadcasted_iota(jnp.int32, (LANES, D_A), 1)
    return jnp.where(c >= lo, jnp.where(c < lo + P_A, 1.0, 0.0), 0.0).astype(BF16)


def _mix_body(*refs, tsteps, nt, n_prev):
    (zs_ref, xc_ref, dt_ref, q_ref, k_ref, v_ref, gs_ref, cos_ref, sina_ref, sinb_ref, lgx_ref,
     dtb_ref, alog_ref, dx_ref, naw_ref) = refs[:15]
    (ya_ref, yb_ref, hs_ref, ss_ref,
     ht_s, s_s, intra_s, qdec_s, kdect_s) = refs[15 + n_prev:]
    i = pl.program_id(1)
    first = jnp.logical_and(pl.program_id(0) == 0, i == 0)
    rows_i = lax.broadcasted_iota(jnp.int32, (CHUNK, CHUNK), 0)
    cols_i = lax.broadcasted_iota(jnp.int32, (CHUNK, CHUNK), 1)
    causal = rows_i >= cols_i
    lane_lo = cols_i < LANES // 2

    @pl.when(first)
    def _():
        rel = (rows_i - cols_i).astype(F32)
        for h in range(H_R):
            intra_s[h] = jnp.where(causal, jnp.exp(jnp.maximum(rel, 0.0) * _LOG_GAMMA[h]), 0.0)
        ri = lax.broadcasted_iota(jnp.int32, (CHUNK, D_QK), 0).astype(F32)
        lgx = lgx_ref[...]
        qdec_s[...] = jnp.exp((ri + 1.0) * lgx)
        kdect_s[...] = jnp.exp((CHUNK - 1.0 - ri) * lgx).T

    @pl.when(i == 0)
    def _():
        ht_s[...] = jnp.zeros_like(ht_s)
        s_s[...] = jnp.zeros_like(s_s)

    tril = _tile3(jnp.where(causal, 1.0, 0.0).astype(BF16), 1)
    expand = _tile3(_head_expand(), 0)
    a_neg = -jnp.exp(alog_ref[...])

    for c in range(tsteps // CHUNK):
        r0 = c * CHUNK
        rows = slice(r0, r0 + CHUNK)
        xa = xc_ref[rows, :D_A]
        bm = xc_ref[rows, D_A:D_A + G_A * N_A]
        cm = xc_ref[rows, D_A + G_A * N_A:]
        dt = _softplus(dt_ref[rows, :] + dtb_ref[...])
        cum = _dot01(tril, dt * a_neg, "rhs")
        cum_t = cum.T
        dt_t = dt.T
        cum_last = cum[CHUNK - 1:CHUNK, :]
        scale = jnp.concatenate(
            [dt * jnp.exp(cum_last - cum), jnp.exp(cum),
             jnp.broadcast_to(jnp.exp(cum_last), (8, LANES))], axis=0)
        scale_x = _dot01(scale, expand, "lhs")
        w_x = scale_x[:CHUNK]
        ecum_x = scale_x[CHUNK:2 * CHUNK]
        dec_x = scale_x[2 * CHUNK:2 * CHUNK + 1]
        bm_t = bm.T
        ys = []
        for g in range(G_A):
            gl = slice(g * N_A, (g + 1) * N_A)
            hl = slice(g * HPG * P_A, (g + 1) * HPG * P_A)
            cg = cm[:, gl]
            scores = lax.dot_general(cg.astype(BF16), bm[:, gl].astype(BF16),
                                     (((1,), (1,)), ((), ())), preferred_element_type=F32)
            scores = jnp.where(causal, scores, 0.0)
            ht_g = ht_s[:, hl]
            y_inter = _bdot(cg, ht_g) * ecum_x[:, hl]
            for pair in range(HPG // 2):
                pl_ = slice(g * HPG * P_A + pair * LANES, g * HPG * P_A + (pair + 1) * LANES)
                x_pair = xa[:, pl_]
                sps = []
                for h in (g * HPG + 2 * pair, g * HPG + 2 * pair + 1):
                    seg = cum[:, h:h + 1] - cum_t[h:h + 1, :]
                    sps.append(scores * jnp.exp(jnp.minimum(seg, 0.0)) * dt_t[h:h + 1, :])
                rhs = jnp.concatenate([jnp.where(lane_lo, x_pair, 0.0),
                                       jnp.where(lane_lo, 0.0, x_pair)], axis=0)
                ys.append(_bdot(jnp.concatenate(sps, axis=1), rhs)
                          + y_inter[:, pair * LANES:(pair + 1) * LANES])
            ht_s[:, hl] = dec_x[:, hl] * ht_g + _bdot(bm_t[gl, :], xa[:, hl] * w_x[:, hl])
        y = jnp.concatenate(ys, axis=1) + dx_ref[...] * xa
        y = y * zs_ref[rows, :]
        ya_ref[rows, :] = (_rms_unit(y) * naw_ref[...]).astype(BF16)

        cos_f, sin_a, sin_b = cos_ref[rows, :], sina_ref[rows, :], sinb_ref[rows, :]
        qr = _rope(q_ref[rows, :], cos_f, sin_a, sin_b)
        kr = _rope(k_ref[rows, :], cos_f, sin_a, sin_b) * (DK_R ** -0.5)
        k_t = kr.T
        kd_t = k_t * kdect_s[...]
        qd = qr * qdec_s[...]
        for pair in range(H_R // 2):
            pl_ = slice(pair * LANES, (pair + 1) * LANES)
            s_pair = s_s[pl_, :]
            for h, keep in ((2 * pair, lane_lo), (2 * pair + 1, jnp.logical_not(lane_lo))):
                kl = slice(h * DK_R, (h + 1) * DK_R)
                vl = slice(h * DV_R, (h + 1) * DV_R)
                v_h = v_ref[rows, vl]
                sc = _bdot(jnp.where(keep, qr[:, pl_], 0.0), k_t[pl_, :]) * intra_s[h]
                y_h = _bdot(jnp.concatenate([sc, jnp.where(keep, qd[:, pl_], 0.0)], axis=1),
                            jnp.concatenate([v_h, s_pair], axis=0))
                s_s[kl, :] = (math.exp(CHUNK * _LOG_GAMMA[h]) * s_s[kl, :]
                              + _bdot(kd_t[kl, :], v_h))
                yb_ref[rows, vl] = (gs_ref[rows, vl] * _rms_unit(y_h)).astype(BF16)

    @pl.when(i == nt - 1)
    def _():
        for h in range(H_A):
            hs_ref[h] = ht_s[:, h * P_A:(h + 1) * P_A].T
        for h in range(H_R):
            ss_ref[h] = s_s[h * DK_R:(h + 1) * DK_R, :]


def _layer_block(l, tail):
    zeros = (0,) * len(tail)
    return pl.BlockSpec((None, None) + tail, lambda b, *_: (l, b) + zeros)


def _mix_prompt(p, rot, lw, nb, seq, tsteps, l, prev):
    nt = seq // tsteps
    row = lambda w: pl.BlockSpec((tsteps, w), lambda b, i: (b * nt + i, 0))
    tab = pl.BlockSpec((tsteps, D_QK), lambda b, i: (i, 0))
    in_specs = [row(D_A), row(D_XBC), row(DT_PAD), row(D_QK), row(D_QK), row(D_RV), row(D_RV),
                tab, tab, tab, _const_spec((1, D_QK)), _const_spec((1, DT_PAD)),
                _const_spec((1, DT_PAD)), _const_spec((1, D_A)), _const_spec((1, D_A))]
    prev = () if prev is None else tuple(prev)
    in_specs += [pl.BlockSpec(memory_space=pl.ANY)] * len(prev)
    aliases = {len(in_specs) - len(prev) + k: 2 + k for k in range(len(prev))}
    out_specs = [row(D_A), row(D_RV),
                 _layer_block(l, (H_A, P_A, N_A)), _layer_block(l, (H_R, DK_R, DV_R))]
    out_shape = [jax.ShapeDtypeStruct((nb * seq, D_A), BF16),
                 jax.ShapeDtypeStruct((nb * seq, D_RV), BF16),
                 jax.ShapeDtypeStruct((DEPTH, nb, H_A, P_A, N_A), F32),
                 jax.ShapeDtypeStruct((DEPTH, nb, H_R, DK_R, DV_R), F32)]
    scratch = [pltpu.VMEM((N_A, D_A), F32),
               pltpu.VMEM((D_QK, DV_R), F32), pltpu.VMEM((H_R, CHUNK, CHUNK), F32),
               pltpu.VMEM((CHUNK, D_QK), F32), pltpu.VMEM((D_QK, CHUNK), F32)]
    return pl.pallas_call(
        functools.partial(_mix_body, tsteps=tsteps, nt=nt, n_prev=len(prev)), grid=(nb, nt),
        in_specs=in_specs, out_specs=out_specs, out_shape=out_shape, scratch_shapes=scratch,
        input_output_aliases=aliases, compiler_params=_params(2), name="mix_prompt")(
            p["z"], p["xbc"], p["dt"], p["q"], p["k"], p["v"], p["g"],
            rot["cos"], rot["sin_a"], rot["sin_b"], rot["lgx"],
            lw["dt_bias"], lw["a_log"], lw["d_x"], lw["norm_a_w"], *prev)


def _s5_disc_body(are_ref, aim_ref, ldt_ref, bre_ref, bim_ref,
                  abre_ref, abim_ref, bbre_ref, bbim_ref):
    ar, ai = are_ref[...], aim_ref[...]
    dt = jnp.exp(ldt_ref[...])
    mag = jnp.exp(ar * dt)
    ab_re = mag * jnp.cos(ai * dt)
    ab_im = mag * jnp.sin(ai * dt)
    den = ar * ar + ai * ai
    num_re = ab_re - 1.0
    coef_re = (num_re * ar + ab_im * ai) / den
    coef_im = (ab_im * ar - num_re * ai) / den
    abre_ref[...] = ab_re
    abim_ref[...] = ab_im
    for c in range(GS_C):
        cl = slice(c * P_C, (c + 1) * P_C)
        br, bi = bre_ref[:, cl], bim_ref[:, cl]
        bbre_ref[:, cl] = coef_re * br - coef_im * bi
        bbim_ref[:, cl] = coef_re * bi + coef_im * br


def _s5_discretise(a_re, a_im, log_dt, b_re, b_im):
    gp = jax.ShapeDtypeStruct((G_C, P_C), F32)
    gcp = jax.ShapeDtypeStruct((G_C, GS_C * P_C), F32)
    b_t = lambda b: jnp.transpose(b, (0, 2, 1)).reshape(G_C, GS_C * P_C)
    ab_re, ab_im, bb_re, bb_im = pl.pallas_call(
        _s5_disc_body, out_shape=[gp, gp, gcp, gcp], name="s5_disc")(
            a_re, a_im, log_dt.reshape(G_C, 1), b_t(b_re), b_t(b_im))
    return ab_re, ab_im, bb_re.reshape(G_C, GS_C, P_C), bb_im.reshape(G_C, GS_C, P_C)


_S5_GB = LANES // GS_C
_S5_NBLK = G_C // _S5_GB
_S5_SB = _S5_GB * P_C


def _block_diag(m):
    g, r, c = m.shape
    m = m.reshape(_S5_NBLK, _S5_GB, r, c)
    eye = jnp.eye(_S5_GB, dtype=m.dtype)
    return jnp.einsum("jgrc,gk->jgrkc", m, eye).reshape(_S5_NBLK, _S5_GB * r, _S5_GB * c)


def _s5_body(u_ref, h0re_ref, h0im_ref, are_ref, aim_ref, bblk_ref, cre_ref, cim_ref,
             d_ref, wglu_ref, bglu_ref, yc_ref, hre_ref, him_ref,
             xre_s, xim_s, sre_s, sim_s, *, nb, tt):
    i = pl.program_id(0)
    rows = nb * tt

    @pl.when(i == 0)
    def _():
        sre_s[...] = h0re_ref[...]
        sim_s[...] = h0im_ref[...]

    u = u_ref[...]
    if tt > 1:
        u = jnp.swapaxes(u, 0, 1).reshape(rows, D_C)
    ub = u.astype(BF16)

    ys = []
    for j in range(_S5_NBLK):
        sl = slice(j * _S5_SB, (j + 1) * _S5_SB)
        bu = jnp.dot(ub[:, j * LANES:(j + 1) * LANES], bblk_ref[j], preferred_element_type=F32)
        xre_s[:, sl] = bu[:, :_S5_SB]
        xim_s[:, sl] = bu[:, _S5_SB:]
        ar = jnp.broadcast_to(are_ref[:, sl], (nb, _S5_SB))
        ai = jnp.broadcast_to(aim_ref[:, sl], (nb, _S5_SB))
        xr, xi = sre_s[:, sl], sim_s[:, sl]
        for t in range(tt):
            r = slice(t * nb, (t + 1) * nb)
            xr, xi = (ar * xr - ai * xi + xre_s[r, sl], ar * xi + ai * xr + xim_s[r, sl])
            xre_s[r, sl] = xr
            xim_s[r, sl] = xi
        sre_s[:, sl] = xr
        sim_s[:, sl] = xi
        ys.append(_bdot(xre_s[:, sl], cre_ref[j]) - _bdot(xim_s[:, sl], cim_ref[j]))
    y = jax.nn.gelu(jnp.concatenate(ys, axis=1) + d_ref[...] * u)
    y = y * jax.nn.sigmoid(_bdot(y, wglu_ref[...]) + bglu_ref[...])
    if tt > 1:
        y = jnp.swapaxes(y.reshape(tt, nb, D_C), 0, 1)
    yc_ref[...] = y.astype(BF16)
    hre_ref[...] = sre_s[...]
    him_ref[...] = sim_s[...]


def _s5(u, h0_re, h0_im, lw, nb, seq, tt):
    rows = tt * nb
    if tt > 1:
        assert nb % 8 == 0 and tt % 8 == 0
        u = u.reshape(nb, seq, D_C)
        io_spec = pl.BlockSpec((nb, tt, D_C), lambda i: (0, i, 0))
    else:
        io_spec = pl.BlockSpec((rows, D_C), lambda i: (i, 0))
    in_specs = [io_spec,
                _const_spec((nb, D_S5)), _const_spec((nb, D_S5)),
                _const_spec((1, D_S5)), _const_spec((1, D_S5)),
                _const_spec((_S5_NBLK, LANES, 2 * _S5_SB)),
                _const_spec((_S5_NBLK, _S5_SB, LANES)), _const_spec((_S5_NBLK, _S5_SB, LANES)),
                _const_spec((1, D_C)), _layer_spec(lw["l"], (D_C, D_C)), _const_spec((1, D_C))]
    st = pl.BlockSpec((nb, D_S5), lambda i: (0, 0))
    yc, h_re, h_im = pl.pallas_call(
        functools.partial(_s5_body, nb=nb, tt=tt), grid=(seq // tt,), in_specs=in_specs,
        out_specs=[io_spec, st, st],
        out_shape=[jax.ShapeDtypeStruct(u.shape, BF16),
                   jax.ShapeDtypeStruct((nb, D_S5), F32), jax.ShapeDtypeStruct((nb, D_S5), F32)],
        scratch_shapes=[pltpu.VMEM((rows, D_S5), F32), pltpu.VMEM((rows, D_S5), F32),
                        pltpu.VMEM((nb, D_S5), F32), pltpu.VMEM((nb, D_S5), F32)],
        compiler_params=_params(1), name="s5")(
            u, h0_re, h0_im, lw["s5_ab_re"], lw["s5_ab_im"], lw["s5_bblk"],
            lw["s5_cre"], lw["s5_cim"], lw["s5_d"], lw["w_glu"], lw["b_glu"])
    return yc.reshape(nb * seq, D_C), h_re, h_im


def _merge_body(x_ref, nw_ref, ya_ref, yb_ref, yc_ref, wg_ref, wa_ref, wb_ref, wc_ref, wo_ref,
                o_ref):
    x = x_ref[...]
    h = (_rms_unit(x) * nw_ref[...]).astype(BF16)
    merged = None
    for k, (y_ref, w_ref) in enumerate(((ya_ref, wa_ref), (yb_ref, wb_ref), (yc_ref, wc_ref))):
        gate = jax.nn.sigmoid(jnp.dot(h, wg_ref[:, k * D_MODEL:(k + 1) * D_MODEL],
                                      preferred_element_type=F32))
        term = gate * _bdot(y_ref[...], w_ref[...])
        merged = term if merged is None else merged + term
    o_ref[...] = x + _bdot(merged, wo_ref[...])


def _merge(x2d, ya, yb, yc, lw, nb, seq, tm):
    nt = seq // tm
    row = lambda w: pl.BlockSpec((tm, w), lambda b, i: (b * nt + i, 0))
    wspec = _layer_spec(lw["l"], (D_MODEL, D_MODEL))
    return pl.pallas_call(
        _merge_body, grid=(nb, nt),
        in_specs=[row(D_MODEL), _const_spec((1, D_MODEL)), row(D_A), row(D_RV), row(D_C),
                  _layer_spec(lw["l"], (D_MODEL, 3 * D_MODEL)), wspec, wspec, wspec, wspec],
        out_specs=row(D_MODEL), out_shape=jax.ShapeDtypeStruct((nb * seq, D_MODEL), F32),
        compiler_params=_params(2), name="merge")(
            x2d, lw["norm_mix_w"], ya, yb, yc, lw["w_gates"],
            lw["w_br_a"], lw["w_br_b"], lw["w_br_c"], lw["w_out"])


_FFN_CW = 256


_FFN_SUB = 512


def _ffn_prompt_body(x_ref, nw_ref, wup_ref, cw_ref, cb_ref, wdn_ref, nf_ref,
                     o_ref, cs_ref, up_s, act_s, *, tm, final):
    @pl.when(pl.program_id(1) == 0)
    def _():
        up_s[0:8, :] = jnp.zeros((8, 2 * D_FF), F32)

    sub = min(tm, _FFN_SUB)
    for r in range(0, tm, sub):
        rs = slice(r, r + sub)
        x = x_ref[rs, :]
        hf = (_rms_unit(x) * nw_ref[...]).astype(BF16)
        up_s[8 + r:8 + r + sub, :] = jnp.dot(hf, wup_ref[...], preferred_element_type=F32)
        for c in range(0, D_FF, _FFN_CW):
            def conv(c0):
                cl = slice(c0, c0 + _FFN_CW)
                y = cb_ref[:, cl]
                for tap in range(CONV_F):
                    y = y + up_s[6 + tap + r:6 + tap + r + sub, cl] * cw_ref[tap:tap + 1, cl]
                return y
            act_s[rs, c:c + _FFN_CW] = (_silu(conv(c)) * conv(D_FF + c)).astype(BF16)
        out = x + jnp.dot(act_s[rs, :], wdn_ref[...], preferred_element_type=F32)
        o_ref[rs, :] = _rms_unit(out) * nf_ref[...] if final else out
    tail = up_s[tm + 6:tm + 8, :]
    cs_ref[...] = tail
    up_s[6:8, :] = tail


def _ffn_sample_body(x_ref, st_ref, nw_ref, wup_ref, cw_ref, cb_ref, wdn_ref, nf_ref,
                     o_ref, cs_ref, act_s, *, final):
    x = x_ref[...]
    hf = (_rms_unit(x) * nw_ref[...]).astype(BF16)
    up = jnp.dot(hf, wup_ref[...], preferred_element_type=F32)
    prev2, prev1 = st_ref[:, :2 * D_FF], st_ref[:, 2 * D_FF:]
    cs_ref[:, :2 * D_FF] = prev1
    cs_ref[:, 2 * D_FF:] = up
    for c in range(0, D_FF, _FFN_CW):
        def conv(c0):
            cl = slice(c0, c0 + _FFN_CW)
            return (cb_ref[:, cl] + prev2[:, cl] * cw_ref[0:1, cl] + prev1[:, cl] * cw_ref[1:2, cl]
                    + up[:, cl] * cw_ref[2:3, cl])
        act_s[:, c:c + _FFN_CW] = (_silu(conv(c)) * conv(D_FF + c)).astype(BF16)
    out = x + jnp.dot(act_s[...], wdn_ref[...], preferred_element_type=F32)
    o_ref[...] = _rms_unit(out) * nf_ref[...] if final else out


def _ffn(x2d, state, lw, nf, nb, seq, tm, final):
    wspecs = [_const_spec((1, D_MODEL)), _layer_spec(lw["l"], (D_MODEL, 2 * D_FF)),
              _const_spec((CONV_F, 2 * D_FF)), _const_spec((1, 2 * D_FF)),
              _layer_spec(lw["l"], (D_FF, D_MODEL)), _const_spec((1, D_MODEL))]
    wargs = (lw["norm_ffn_w"], lw["w_up"], lw["conv_f_w"], lw["conv_f_b"], lw["w_down"], nf)
    act = pltpu.VMEM((tm, D_FF), BF16)
    if state is None:
        nt = seq // tm
        row = pl.BlockSpec((tm, D_MODEL), lambda b, i: (b * nt + i, 0))
        return pl.pallas_call(
            functools.partial(_ffn_prompt_body, tm=tm, final=final), grid=(nb, nt),
            in_specs=[row] + wspecs,
            out_specs=[row, pl.BlockSpec((None, CONV_F - 1, 2 * D_FF), lambda b, i: (b, 0, 0))],
            out_shape=[jax.ShapeDtypeStruct((nb * seq, D_MODEL), F32),
                       jax.ShapeDtypeStruct((nb, CONV_F - 1, 2 * D_FF), F32)],
            scratch_shapes=[pltpu.VMEM((8 + tm, 2 * D_FF), F32), act],
            compiler_params=_params(2), name="ffn_prompt")(x2d, *wargs)
    sw = (CONV_F - 1) * 2 * D_FF
    return pl.pallas_call(
        functools.partial(_ffn_sample_body, final=final), grid=(nb // tm,),
        in_specs=[pl.BlockSpec((tm, D_MODEL), lambda i: (i, 0)),
                  pl.BlockSpec((tm, sw), lambda i: (i, 0))] + wspecs,
        out_specs=[pl.BlockSpec((tm, D_MODEL), lambda i: (i, 0)),
                   pl.BlockSpec((tm, sw), lambda i: (i, 0))],
        out_shape=[jax.ShapeDtypeStruct((nb, D_MODEL), F32), jax.ShapeDtypeStruct((nb, sw), F32)],
        scratch_shapes=[act], compiler_params=_params(1), name="ffn_sample")(x2d, state, *wargs)


def _ssd_sample_body(*refs, n_prev):
    (zs_ref, xbc_ref, dt_ref, cst_ref, cw_ref, cb_ref, dtb_ref, alog_ref, dx_ref, naw_ref,
     hin_ref) = refs[:11]
    (ya_ref, cso_ref, hout_ref, xa_s, xdt_t_s, dec_t_s, bm_t_s, cm_t_s, yt_s) = refs[11 + n_prev:]
    h = pl.program_id(0)

    @pl.when(h == 0)
    def _():
        acc = cb_ref[...]
        for tap in range(CONV_A - 1):
            acc = acc + cst_ref[:, tap * D_XBC:(tap + 1) * D_XBC] * cw_ref[tap:tap + 1, :]
        acc = acc + xbc_ref[...] * cw_ref[CONV_A - 1:CONV_A, :]
        cso_ref[:, :(CONV_A - 2) * D_XBC] = cst_ref[:, D_XBC:]
        cso_ref[:, (CONV_A - 2) * D_XBC:] = xbc_ref[...]
        xc = _silu(acc)
        xa = xc[:, :D_A]
        xa_s[...] = xa
        bm_t_s[...] = xc[:, D_A:D_A + G_A * N_A].T
        cm_t_s[...] = xc[:, D_A + G_A * N_A:].T
        dt = _softplus(dt_ref[...] + dtb_ref[...])
        dec_t_s[...] = jnp.exp(dt * -jnp.exp(alog_ref[...])).T
        dt_x = _dot01(dt, _tile3(_head_expand(), 0), "lhs")
        xdt_t_s[...] = (xa * dt_x).T

    g0 = pl.multiple_of((h // HPG) * N_A, N_A)
    p0 = pl.multiple_of(h * P_A, P_A)
    b_t = bm_t_s[pl.ds(g0, N_A), :]
    c_t = cm_t_s[pl.ds(g0, N_A), :]
    dec = dec_t_s[pl.ds(h, 1), :]

    def per_8p(k, carry):
        r = pl.ds(pl.multiple_of(p0 + k * 8, 8), 8)
        x_rows = xdt_t_s[r, :]
        ys = []
        for u in range(8):
            h_new = hin_ref[k * 8 + u] * dec + x_rows[u:u + 1, :] * b_t
            hout_ref[k * 8 + u] = h_new
            ys.append(jnp.sum(h_new * c_t, axis=0, keepdims=True))
        yt_s[r, :] = jnp.concatenate(ys, axis=0)
        return carry

    lax.fori_loop(0, P_A // 8, per_8p, 0)

    @pl.when(h == H_A - 1)
    def _():
        xa = xa_s[...]
        y = yt_s[...].T + dx_ref[...] * xa
        y = y * zs_ref[...]
        ya_ref[...] = (_rms_unit(y) * naw_ref[...]).astype(BF16)


def _ssd_sample(p, conv_st, h_ssd_t, lw, nb, l, prev):
    full = lambda w: _const_spec((nb, w))
    cw3 = (CONV_A - 1) * D_XBC
    st = pl.BlockSpec((None, None, P_A, N_A, nb), lambda h: (l, h, 0, 0, 0))
    in_specs = [full(D_A), full(D_XBC), full(DT_PAD), full(cw3),
                _const_spec((CONV_A, D_XBC)), _const_spec((1, D_XBC)), _const_spec((1, DT_PAD)),
                _const_spec((1, DT_PAD)), _const_spec((1, D_A)), _const_spec((1, D_A)), st]
    prev = () if prev is None else (prev,)
    in_specs += [pl.BlockSpec(memory_space=pl.ANY)] * len(prev)
    aliases = {len(in_specs) - 1: 2} if prev else {}
    keep = lambda w: pl.BlockSpec((nb, w), lambda h: (0, 0))
    scratch = [pltpu.VMEM((nb, D_A), F32), pltpu.VMEM((D_A, nb), F32), pltpu.VMEM((LANES, nb), F32),
               pltpu.VMEM((G_A * N_A, nb), F32), pltpu.VMEM((G_A * N_A, nb), F32),
               pltpu.VMEM((D_A, nb), F32)]
    ya, cs, hs = pl.pallas_call(
        functools.partial(_ssd_sample_body, n_prev=len(prev)), grid=(H_A,),
        in_specs=in_specs, out_specs=[keep(D_A), keep(cw3), st],
        out_shape=[jax.ShapeDtypeStruct((nb, D_A), BF16), jax.ShapeDtypeStruct((nb, cw3), F32),
                   jax.ShapeDtypeStruct((DEPTH, H_A, P_A, N_A, nb), F32)],
        scratch_shapes=scratch, input_output_aliases=aliases,
        compiler_params=_params(1), name="ssd_sample")(
            p["z"], p["xbc"], p["dt"], conv_st.reshape(nb, cw3),
            lw["conv_a_w"], lw["conv_a_b"], lw["dt_bias"], lw["a_log"], lw["d_x"], lw["norm_a_w"],
            h_ssd_t, *prev)
    return ya, cs.reshape(nb, CONV_A - 1, D_XBC), hs


_RET_BB = 8


def _ret_sample_body(*refs, nb, n_prev):
    q_ref, k_ref, v_ref, gs_ref, cos_ref, sina_ref, sinb_ref, sin_ref = refs[:8]
    yb_ref, sout_ref, q_t_s, k_t_s, yr_s = refs[8 + n_prev:]
    i = pl.program_id(0)

    @pl.when(i == 0)
    def _():
        cos_f, sin_a, sin_b = cos_ref[...], sina_ref[...], sinb_ref[...]
        q_t_s[...] = _rope(q_ref[...], cos_f, sin_a, sin_b).T.astype(BF16)
        k_t_s[...] = (_rope(k_ref[...], cos_f, sin_a, sin_b) * (DK_R ** -0.5)).T.astype(BF16)

    rows_i = lax.broadcasted_iota(jnp.int32, (nb, DV_R), 0)
    blk = pl.ds(pl.multiple_of(i * _RET_BB, _RET_BB), _RET_BB)
    v_blk = v_ref[blk, :]
    ys = [[] for _ in range(H_R)]
    for j in range(_RET_BB):
        onehot = jnp.where(rows_i == i * _RET_BB + j, 1.0, 0.0).astype(BF16)
        k_bc = jnp.dot(k_t_s[...], onehot, preferred_element_type=F32)
        q_bc = jnp.dot(q_t_s[...], onehot, preferred_element_type=F32)
        for h in range(H_R):
            kl = slice(h * DK_R, (h + 1) * DK_R)
            vl = slice(h * DV_R, (h + 1) * DV_R)
            s_new = math.exp(_LOG_GAMMA[h]) * sin_ref[j, h] + k_bc[kl, :] * v_blk[j:j + 1, vl]
            sout_ref[j, h] = s_new
            ys[h].append(jnp.sum(q_bc[kl, :] * s_new, axis=0, keepdims=True))
    for h in range(H_R):
        yr_s[blk, h * DV_R:(h + 1) * DV_R] = jnp.concatenate(ys[h], axis=0)

    @pl.when(i == nb // _RET_BB - 1)
    def _():
        for h in range(H_R):
            vl = slice(h * DV_R, (h + 1) * DV_R)
            yb_ref[:, vl] = (gs_ref[:, vl] * _rms_unit(yr_s[:, vl])).astype(BF16)


def _ret_sample(p, s_ret, rot, nb, l, prev):
    full = lambda w: _const_spec((nb, w))
    st = pl.BlockSpec((None, _RET_BB, H_R, DK_R, DV_R), lambda i: (l, i, 0, 0, 0))
    in_specs = [full(D_QK), full(D_QK), full(D_RV), full(D_RV),
                _const_spec((1, D_QK)), _const_spec((1, D_QK)), _const_spec((1, D_QK)), st]
    prev = () if prev is None else (prev,)
    in_specs += [pl.BlockSpec(memory_space=pl.ANY)] * len(prev)
    aliases = {len(in_specs) - 1: 1} if prev else {}
    return pl.pallas_call(
        functools.partial(_ret_sample_body, nb=nb, n_prev=len(prev)), grid=(nb // _RET_BB,),
        in_specs=in_specs, out_specs=[pl.BlockSpec((nb, D_RV), lambda i: (0, 0)), st],
        out_shape=[jax.ShapeDtypeStruct((nb, D_RV), BF16),
                   jax.ShapeDtypeStruct((DEPTH, nb, H_R, DK_R, DV_R), F32)],
        scratch_shapes=[pltpu.VMEM((D_QK, nb), BF16), pltpu.VMEM((D_QK, nb), BF16),
                        pltpu.VMEM((nb, D_RV), F32)],
        input_output_aliases=aliases, compiler_params=_params(1), name="ret_sample")(
            p["q"], p["k"], p["v"], p["g"], rot["cos"], rot["sin_a"], rot["sin_b"], s_ret, *prev)


def _rotary_tables(start, length):
    half = DK_R // 2
    pos = start + jnp.arange(length, dtype=F32)
    freqs = ROPE_BASE ** (-jnp.arange(half, dtype=F32) / half)
    ang = pos[:, None] * freqs[None, :]
    cos, sin, zero = jnp.cos(ang), jnp.sin(ang), jnp.zeros_like(ang)
    tile = lambda a, b: jnp.tile(jnp.concatenate([a, b], axis=-1), (1, H_R))
    lg = jnp.log1p(-jnp.exp2(-5.0 - jnp.arange(H_R, dtype=F32)))
    return {"cos": tile(cos, cos), "sin_a": tile(-sin, zero), "sin_b": tile(zero, sin),
            "lgx": jnp.repeat(lg, DK_R)[None, :]}


def _matmul_weights(w):
    w_in = w["w_in"]
    o_dt = D_A + D_XBC
    o_gates = w_in.shape[-1] - 3 * D_MODEL
    dt_cols = jnp.pad(w_in[..., o_dt:o_dt + H_A], ((0, 0), (0, 0), (0, DT_PAD - H_A)))
    w_proj = jnp.concatenate([w_in[..., :o_dt], w_in[..., o_dt + H_A:o_gates], dt_cols], axis=-1)
    out = {"w_proj": w_proj.astype(BF16), "w_gates": w_in[..., o_gates:].astype(BF16)}
    for name in ("w_glu", "w_br_a", "w_br_b", "w_br_c", "w_out", "w_up", "w_down"):
        out[name] = w[name].astype(BF16)
    return out


def _layer_weights(l, w, mats):
    pad_row = lambda a: jnp.pad(a, (0, DT_PAD - H_A))[None, :]
    ab_re, ab_im, bb_re, bb_im = _s5_discretise(
        w["s5_a_re"][l], w["s5_a_im"][l], w["s5_log_dt"][l], w["s5_b_re"][l], w["s5_b_im"][l])
    bblk = jnp.concatenate([_block_diag(bb_re), _block_diag(bb_im)], axis=-1)
    c_t = lambda c: _block_diag(jnp.transpose(c, (0, 2, 1)))
    return {
        "l": l, **mats,
        "norm_mix_w": w["norm_mix_w"][l][None, :],
        "conv_a_w": w["conv_a_w"][l], "conv_a_b": w["conv_a_b"][l][None, :],
        "dt_bias": pad_row(w["dt_bias"][l]), "a_log": pad_row(w["a_log"][l]),
        "d_x": jnp.repeat(w["d_a"][l], P_A)[None, :], "norm_a_w": w["norm_a_w"][l][None, :],
        "s5_ab_re": ab_re.reshape(1, D_S5), "s5_ab_im": ab_im.reshape(1, D_S5),
        "s5_bblk": bblk.astype(BF16),
        "s5_cre": c_t(w["s5_c_re"][l]).astype(BF16), "s5_cim": c_t(w["s5_c_im"][l]).astype(BF16),
        "s5_d": w["s5_d"][l][None, :], "b_glu": w["b_glu"][l][None, :],
        "norm_ffn_w": w["norm_ffn_w"][l][None, :],
        "conv_f_w": w["conv_f_w"][l], "conv_f_b": w["conv_f_b"][l][None, :],
    }


def _prompt_trunk(x, lws, nf, tm_proj, tsteps, tt, tm_merge, tm_ffn):
    nb, seq, _ = x.shape
    rot = _rotary_tables(0.0, seq)
    x2d = x.reshape(nb * seq, D_MODEL)
    zeros = jnp.zeros((nb, D_S5), F32)
    states, big = [], None
    for l, lw in enumerate(lws):
        p = _inproj(x2d, lw["norm_mix_w"], lw, nb, seq, tm_proj, conv=True)
        ya, yb, *big = _mix_prompt(p, rot, lw, nb, seq, tsteps, l, big)
        yc, h_re, h_im = _s5(p["u"], zeros, zeros, lw, nb, seq, tt)
        x2d = _merge(x2d, ya, yb, yc, lw, nb, seq, tm_merge)
        x2d, fs = _ffn(x2d, None, lw, nf, nb, seq, tm_ffn, final=(l == len(lws) - 1))
        states.append((p["conv_state"], h_re.reshape(nb, G_C, P_C), h_im.reshape(nb, G_C, P_C), fs))
    cs, h_re, h_im, fs = [jnp.stack(s) for s in zip(*states)]
    return x2d.reshape(nb, seq, D_MODEL), [cs, big[0], big[1], h_re, h_im, fs]


def _sample_trunk(x, st, lws, nf):
    nb = x.shape[0]
    rot = _rotary_tables(float(PAST_LEN), 1)
    x2d = x.reshape(nb, D_MODEL)
    st_conv, st_ssd, st_ret, st_re, st_im, st_ffn = st
    st_ssd_t = jnp.transpose(st_ssd, (0, 2, 3, 4, 1))
    states, hs, ss = [], None, None
    for l, lw in enumerate(lws):
        p = _inproj(x2d, lw["norm_mix_w"], lw, 1, nb, nb, conv=False)
        ya, cs, hs = _ssd_sample(p, st_conv[l], st_ssd_t, lw, nb, l, hs)
        yb, ss = _ret_sample(p, st_ret, rot, nb, l, ss)
        yc, h_re, h_im = _s5(p["u"], st_re[l].reshape(nb, D_S5), st_im[l].reshape(nb, D_S5),
                             lw, nb, 1, 1)
        x2d = _merge(x2d, ya, yb, yc, lw, 1, nb, nb)
        x2d, fs = _ffn(x2d, st_ffn[l].reshape(nb, -1), lw, nf, nb, 1, nb,
                       final=(l == len(lws) - 1))
        states.append((cs, h_re.reshape(nb, G_C, P_C), h_im.reshape(nb, G_C, P_C),
                       fs.reshape(nb, CONV_F - 1, 2 * D_FF)))
    cs, h_re, h_im, fs = [jnp.stack(s) for s in zip(*states)]
    hs = jnp.transpose(hs, (0, 4, 1, 2, 3))
    return x2d.reshape(nb, 1, D_MODEL), [cs, hs, ss, h_re, h_im, fs]


def kernel(x_prompt, x_sample, state_ssd_conv, state_ssd, state_ret, state_s5_re, state_s5_im,
           state_ffn_conv, norm_mix_w, w_in, conv_a_w, conv_a_b, dt_bias, a_log, d_a, norm_a_w,
           s5_a_re, s5_a_im, s5_log_dt, s5_b_re, s5_b_im, s5_c_re, s5_c_im, s5_d, w_glu, b_glu,
           w_br_a, w_br_b, w_br_c, w_out, norm_ffn_w, w_up, conv_f_w, conv_f_b, w_down, norm_f_w):
    w = dict(norm_mix_w=norm_mix_w, w_in=w_in, conv_a_w=conv_a_w, conv_a_b=conv_a_b,
             dt_bias=dt_bias, a_log=a_log, d_a=d_a, norm_a_w=norm_a_w, s5_a_re=s5_a_re,
             s5_a_im=s5_a_im, s5_log_dt=s5_log_dt, s5_b_re=s5_b_re, s5_b_im=s5_b_im,
             s5_c_re=s5_c_re, s5_c_im=s5_c_im, s5_d=s5_d, w_glu=w_glu, b_glu=b_glu,
             w_br_a=w_br_a, w_br_b=w_br_b, w_br_c=w_br_c, w_out=w_out, norm_ffn_w=norm_ffn_w,
             w_up=w_up, conv_f_w=conv_f_w, conv_f_b=conv_f_b, w_down=w_down)
    mats = _matmul_weights(w)
    lws = [_layer_weights(l, w, mats) for l in range(DEPTH)]
    nf = norm_f_w[None, :]
    seq = x_prompt.shape[1]
    y_p, p_st = _prompt_trunk(x_prompt, lws, nf, tm_proj=min(512, seq), tsteps=min(512, seq),
                              tt=min(64, seq), tm_merge=min(512, seq), tm_ffn=min(512, seq))
    y_s, s_st = _sample_trunk(
        x_sample, (state_ssd_conv, state_ssd, state_ret, state_s5_re, state_s5_im, state_ffn_conv),
        lws, nf)
    return (y_p, y_s, *p_st, *s_st)
```

```python
import functools
import math

import jax
import jax.numpy as jnp
import numpy as np
from jax import lax
from jax.experimental import pallas as pl
from jax.experimental.pallas import tpu as pltpu

F32 = jnp.float32
BF16 = jnp.bfloat16

D_MODEL = 1024
DEPTH = 2
PAST_LEN = 16384
H_A, P_A, N_A, G_A, CONV_A = 16, 64, 64, 2, 4
D_A = H_A * P_A
D_XBC = D_A + 2 * G_A * N_A
H_R, DK_R, DV_R = 8, 64, 128
D_QK = H_R * DK_R
D_RV = H_R * DV_R
ROPE_BASE = 10000.0
GS_C, G_C, P_C = 16, 64, 64
D_C = G_C * GS_C
D_S5 = G_C * P_C
D_FF = 2816
CONV_F = 3
CHUNK = 128
EPS = 1e-6
LANES = 128
DT_PAD = LANES
HPG = H_A // G_A
VMEM_LIMIT = 56 * 1024 * 1024

_LOG_GAMMA = [math.log1p(-(2.0 ** (-5.0 - h))) for h in range(H_R)]

_SEGS = (("z", D_A), ("xbc", D_XBC), ("q", D_QK), ("k", D_QK),
         ("v", D_RV), ("g", D_RV), ("u", D_C), ("dt", DT_PAD))
_SEG_OFF = np.cumsum([0] + [w for _, w in _SEGS]).tolist()
D_PROJ = _SEG_OFF[-1]


def _rms_unit(x):
    return x * lax.rsqrt(jnp.mean(x * x, axis=-1, keepdims=True) + EPS)


def _silu(x):
    return x * jax.nn.sigmoid(x)


def _softplus(x):
    return jnp.maximum(x, 0.0) + jnp.log1p(jnp.exp(-jnp.abs(x)))


def _bdot(a, b):
    return jnp.dot(a.astype(BF16), b.astype(BF16), preferred_element_type=F32)


def _split3(x):
    hi = x.astype(BF16)
    r1 = x - hi.astype(F32)
    mid = r1.astype(BF16)
    return hi, mid, (r1 - mid.astype(F32)).astype(BF16)


def _tile3(m01, axis):
    return jnp.concatenate([m01] * 3, axis=axis)


def _dot01(a, b, f32_side):
    if f32_side == "lhs":
        return jnp.dot(jnp.concatenate(_split3(a), axis=1), b, preferred_element_type=F32)
    return jnp.dot(a, jnp.concatenate(_split3(b), axis=0), preferred_element_type=F32)


def _const_spec(shape):
    nd = len(shape)
    return pl.BlockSpec(shape, lambda *_: (0,) * nd, pipeline_mode=pl.Buffered(1))


def _layer_spec(l, shape):
    nd = len(shape)
    return pl.BlockSpec((None,) + tuple(shape), lambda *_: (l,) + (0,) * nd,
                        pipeline_mode=pl.Buffered(1))


def _params(n_grid):
    return pltpu.CompilerParams(dimension_semantics=("arbitrary",) * n_grid,
                                vmem_limit_bytes=VMEM_LIMIT)


_PROJ_SUB = 128


def _inproj_body(x_ref, nw_ref, cw_ref, cb_ref, cos_ref, sina_ref, sinb_ref, w_ref, *refs,
                 tm, conv):
    n = len(_SEGS)
    if conv:
        cs_ref, xp_s = refs[n:]

        @pl.when(pl.program_id(1) == 0)
        def _():
            xp_s[0:8, :] = jnp.zeros((8, D_XBC), F32)

    for r in range(0, tm, _PROJ_SUB):
        rs = slice(r, r + _PROJ_SUB)
        h = (_rms_unit(x_ref[rs, :]) * nw_ref[...]).astype(BF16)
        for (name, width), off, o_ref in zip(_SEGS, _SEG_OFF, refs[:n]):
            y = jnp.dot(h, w_ref[:, off:off + width], preferred_element_type=F32)
            if name in ("z", "g"):
                y = _silu(y)
            elif name in ("q", "k") and conv:
                y = _rope(y, cos_ref[rs, :], sina_ref[rs, :], sinb_ref[rs, :])
                if name == "k":
                    y = y * (DK_R ** -0.5)
            elif name == "xbc" and conv:
                xp_s[8 + r:8 + r + _PROJ_SUB, :] = y
                y = cb_ref[...]
                for tap in range(CONV_A):
                    y = y + xp_s[5 + tap + r:5 + tap + r + _PROJ_SUB, :] * cw_ref[tap:tap + 1, :]
                y = _silu(y)
            o_ref[rs, :] = y
    if conv:
        tail = xp_s[tm + 5:tm + 8, :]
        cs_ref[...] = tail
        xp_s[5:8, :] = tail


def _inproj(x2d, nw, lw, rot, nb, seq, tm, conv):
    nt = seq // tm
    tab = (pl.BlockSpec((tm, D_QK), lambda b, i: (i, 0)) if conv
           else _const_spec(rot["cos"].shape))
    in_specs = [pl.BlockSpec((tm, D_MODEL), lambda b, i: (b * nt + i, 0)),
                _const_spec((1, D_MODEL)), _const_spec((CONV_A, D_XBC)), _const_spec((1, D_XBC)),
                tab, tab, tab, _layer_spec(lw["l"], (D_MODEL, D_PROJ))]
    out_specs, out_shape, scratch = [], [], []
    for _, width in _SEGS:
        out_specs.append(pl.BlockSpec((tm, width), lambda b, i: (b * nt + i, 0)))
        out_shape.append(jax.ShapeDtypeStruct((nb * seq, width), F32))
    if conv:
        out_specs.append(pl.BlockSpec((None, CONV_A - 1, D_XBC), lambda b, i: (b, 0, 0)))
        out_shape.append(jax.ShapeDtypeStruct((nb, CONV_A - 1, D_XBC), F32))
        scratch.append(pltpu.VMEM((8 + tm, D_XBC), F32))
    outs = pl.pallas_call(
        functools.partial(_inproj_body, tm=tm, conv=conv), grid=(nb, nt), in_specs=in_specs,
        out_specs=out_specs, out_shape=out_shape, scratch_shapes=scratch,
        compiler_params=_params(2), name="inproj")(
            x2d, nw, lw["conv_a_w"], lw["conv_a_b"], rot["cos"], rot["sin_a"], rot["sin_b"],
            lw["w_proj"])
    res = {name: o for (name, _), o in zip(_SEGS, outs)}
    if conv:
        res["conv_state"] = outs[-1]
    return res


def _rope(x, cos_f, sin_a, sin_b):
    half = DK_R // 2
    return (x * cos_f + pltpu.roll(x, D_QK - half, 1) * sin_a
            + pltpu.roll(x, half, 1) * sin_b)


def _head_expand():
    lo = lax.broadcasted_iota(jnp.int32, (LANES, D_A), 0) * P_A
    c = lax.broadcasted_iota(jnp.int32, (LANES, D_A), 1)
    return jnp.where(c >= lo, jnp.where(c < lo + P_A, 1.0, 0.0), 0.0).astype(BF16)


def _mix_body(*refs, tsteps, nt, n_prev):
    (zs_ref, xc_ref, dt_ref, q_ref, k_ref, v_ref, gs_ref, lgx_ref,
     dtb_ref, alog_ref, dx_ref, naw_ref) = refs[:12]
    (ya_ref, yb_ref, hs_ref, ss_ref,
     ht_s, s_s, intra_s, qdec_s, kdect_s) = refs[12 + n_prev:]
    i = pl.program_id(1)
    first = jnp.logical_and(pl.program_id(0) == 0, i == 0)
    rows_i = lax.broadcasted_iota(jnp.int32, (CHUNK, CHUNK), 0)
    cols_i = lax.broadcasted_iota(jnp.int32, (CHUNK, CHUNK), 1)
    causal = rows_i >= cols_i
    lane_lo = cols_i < LANES // 2

    @pl.when(first)
    def _():
        rel = (rows_i - cols_i).astype(F32)
        for h in range(H_R):
            intra_s[h] = jnp.where(causal, jnp.exp(jnp.maximum(rel, 0.0) * _LOG_GAMMA[h]), 0.0)
        ri = lax.broadcasted_iota(jnp.int32, (CHUNK, D_QK), 0).astype(F32)
        lgx = lgx_ref[...]
        qdec_s[...] = jnp.exp((ri + 1.0) * lgx)
        kdect_s[...] = jnp.exp((CHUNK - 1.0 - ri) * lgx).T

    @pl.when(i == 0)
    def _():
        ht_s[...] = jnp.zeros_like(ht_s)
        s_s[...] = jnp.zeros_like(s_s)

    tril = _tile3(jnp.where(causal, 1.0, 0.0).astype(BF16), 1)
    expand = _tile3(_head_expand(), 0)
    a_neg = -jnp.exp(alog_ref[...])

    for c in range(tsteps // CHUNK):
        r0 = c * CHUNK
        rows = slice(r0, r0 + CHUNK)
        xa = xc_ref[rows, :D_A]
        bm = xc_ref[rows, D_A:D_A + G_A * N_A]
        cm = xc_ref[rows, D_A + G_A * N_A:]
        dt = _softplus(dt_ref[rows, :] + dtb_ref[...])
        cum = _dot01(tril, dt * a_neg, "rhs")
        cum_t = cum.T
        dt_t = dt.T
        cum_last = cum[CHUNK - 1:CHUNK, :]
        scale = jnp.concatenate(
            [dt * jnp.exp(cum_last - cum), jnp.exp(cum),
             jnp.broadcast_to(jnp.exp(cum_last), (8, LANES))], axis=0)
        scale_x = _dot01(scale, expand, "lhs")
        w_x = scale_x[:CHUNK]
        ecum_x = scale_x[CHUNK:2 * CHUNK]
        dec_x = scale_x[2 * CHUNK:2 * CHUNK + 1]
        bm_t = bm.T
        ys = []
        for g in range(G_A):
            gl = slice(g * N_A, (g + 1) * N_A)
            hl = slice(g * HPG * P_A, (g + 1) * HPG * P_A)
            cg = cm[:, gl]
            scores = lax.dot_general(cg.astype(BF16), bm[:, gl].astype(BF16),
                                     (((1,), (1,)), ((), ())), preferred_element_type=F32)
            scores = jnp.where(causal, scores, 0.0)
            ht_g = ht_s[:, hl]
            y_inter = _bdot(cg, ht_g) * ecum_x[:, hl]
            for pair in range(HPG // 2):
                pl_ = slice(g * HPG * P_A + pair * LANES, g * HPG * P_A + (pair + 1) * LANES)
                x_pair = xa[:, pl_]
                sps = []
                for h in (g * HPG + 2 * pair, g * HPG + 2 * pair + 1):
                    seg = cum[:, h:h + 1] - cum_t[h:h + 1, :]
                    sps.append(scores * jnp.exp(jnp.minimum(seg, 0.0)) * dt_t[h:h + 1, :])
                rhs = jnp.concatenate([jnp.where(lane_lo, x_pair, 0.0),
                                       jnp.where(lane_lo, 0.0, x_pair)], axis=0)
                ys.append(_bdot(jnp.concatenate(sps, axis=1), rhs)
                          + y_inter[:, pair * LANES:(pair + 1) * LANES])
            ht_s[:, hl] = dec_x[:, hl] * ht_g + _bdot(bm_t[gl, :], xa[:, hl] * w_x[:, hl])
        y = jnp.concatenate(ys, axis=1) + dx_ref[...] * xa
        y = y * zs_ref[rows, :]
        ya_ref[rows, :] = (_rms_unit(y) * naw_ref[...]).astype(BF16)

        qr = q_ref[rows, :]
        k_t = k_ref[rows, :].T
        kd_t = k_t * kdect_s[...]
        qd = qr * qdec_s[...]
        for pair in range(H_R // 2):
            pl_ = slice(pair * LANES, (pair + 1) * LANES)
            s_pair = s_s[pl_, :]
            for h, keep in ((2 * pair, lane_lo), (2 * pair + 1, jnp.logical_not(lane_lo))):
                kl = slice(h * DK_R, (h + 1) * DK_R)
                vl = slice(h * DV_R, (h + 1) * DV_R)
                v_h = v_ref[rows, vl]
                sc = _bdot(jnp.where(keep, qr[:, pl_], 0.0), k_t[pl_, :]) * intra_s[h]
                y_h = _bdot(jnp.concatenate([sc, jnp.where(keep, qd[:, pl_], 0.0)], axis=1),
                            jnp.concatenate([v_h, s_pair], axis=0))
                s_s[kl, :] = (math.exp(CHUNK * _LOG_GAMMA[h]) * s_s[kl, :]
                              + _bdot(kd_t[kl, :], v_h))
                yb_ref[rows, vl] = (gs_ref[rows, vl] * _rms_unit(y_h)).astype(BF16)

    @pl.when(i == nt - 1)
    def _():
        for h in range(H_A):
            hs_ref[h] = ht_s[:, h * P_A:(h + 1) * P_A].T
        for h in range(H_R):
            ss_ref[h] = s_s[h * DK_R:(h + 1) * DK_R, :]


def _layer_block(l, tail):
    zeros = (0,) * len(tail)
    return pl.BlockSpec((None, None) + tail, lambda b, *_: (l, b) + zeros)


def _mix_prompt(p, rot, lw, nb, seq, tsteps, l, prev):
    nt = seq // tsteps
    row = lambda w: pl.BlockSpec((tsteps, w), lambda b, i: (b * nt + i, 0))
    in_specs = [row(D_A), row(D_XBC), row(DT_PAD), row(D_QK), row(D_QK), row(D_RV), row(D_RV),
                _const_spec((1, D_QK)), _const_spec((1, DT_PAD)),
                _const_spec((1, DT_PAD)), _const_spec((1, D_A)), _const_spec((1, D_A))]
    prev = () if prev is None else tuple(prev)
    in_specs += [pl.BlockSpec(memory_space=pl.ANY)] * len(prev)
    aliases = {len(in_specs) - len(prev) + k: 2 + k for k in range(len(prev))}
    out_specs = [row(D_A), row(D_RV),
                 _layer_block(l, (H_A, P_A, N_A)), _layer_block(l, (H_R, DK_R, DV_R))]
    out_shape = [jax.ShapeDtypeStruct((nb * seq, D_A), BF16),
                 jax.ShapeDtypeStruct((nb * seq, D_RV), BF16),
                 jax.ShapeDtypeStruct((DEPTH, nb, H_A, P_A, N_A), F32),
                 jax.ShapeDtypeStruct((DEPTH, nb, H_R, DK_R, DV_R), F32)]
    scratch = [pltpu.VMEM((N_A, D_A), F32),
               pltpu.VMEM((D_QK, DV_R), F32), pltpu.VMEM((H_R, CHUNK, CHUNK), F32),
               pltpu.VMEM((CHUNK, D_QK), F32), pltpu.VMEM((D_QK, CHUNK), F32)]
    return pl.pallas_call(
        functools.partial(_mix_body, tsteps=tsteps, nt=nt, n_prev=len(prev)), grid=(nb, nt),
        in_specs=in_specs, out_specs=out_specs, out_shape=out_shape, scratch_shapes=scratch,
        input_output_aliases=aliases, compiler_params=_params(2), name="mix_prompt")(
            p["z"], p["xbc"], p["dt"], p["q"], p["k"], p["v"], p["g"],
            rot["lgx"], lw["dt_bias"], lw["a_log"], lw["d_x"], lw["norm_a_w"], *prev)


def _s5_disc_body(are_ref, aim_ref, ldt_ref, bre_ref, bim_ref,
                  abre_ref, abim_ref, bbre_ref, bbim_ref):
    ar, ai = are_ref[...], aim_ref[...]
    dt = jnp.exp(ldt_ref[...])
    mag = jnp.exp(ar * dt)
    ab_re = mag * jnp.cos(ai * dt)
    ab_im = mag * jnp.sin(ai * dt)
    den = ar * ar + ai * ai
    num_re = ab_re - 1.0
    coef_re = (num_re * ar + ab_im * ai) / den
    coef_im = (ab_im * ar - num_re * ai) / den
    abre_ref[...] = ab_re
    abim_ref[...] = ab_im
    for c in range(GS_C):
        cl = slice(c * P_C, (c + 1) * P_C)
        br, bi = bre_ref[:, cl], bim_ref[:, cl]
        bbre_ref[:, cl] = coef_re * br - coef_im * bi
        bbim_ref[:, cl] = coef_re * bi + coef_im * br


def _s5_discretise(a_re, a_im, log_dt, b_re, b_im):
    gp = jax.ShapeDtypeStruct((G_C, P_C), F32)
    gcp = jax.ShapeDtypeStruct((G_C, GS_C * P_C), F32)
    b_t = lambda b: jnp.transpose(b, (0, 2, 1)).reshape(G_C, GS_C * P_C)
    ab_re, ab_im, bb_re, bb_im = pl.pallas_call(
        _s5_disc_body, out_shape=[gp, gp, gcp, gcp], name="s5_disc")(
            a_re, a_im, log_dt.reshape(G_C, 1), b_t(b_re), b_t(b_im))
    return ab_re, ab_im, bb_re.reshape(G_C, GS_C, P_C), bb_im.reshape(G_C, GS_C, P_C)


_S5_GB = LANES // GS_C
_S5_NBLK = G_C // _S5_GB
_S5_SB = _S5_GB * P_C


def _block_diag(m):
    g, r, c = m.shape
    m = m.reshape(_S5_NBLK, _S5_GB, r, c)
    eye = jnp.eye(_S5_GB, dtype=m.dtype)
    return jnp.einsum("jgrc,gk->jgrkc", m, eye).reshape(_S5_NBLK, _S5_GB * r, _S5_GB * c)


def _s5_body(u_ref, h0re_ref, h0im_ref, are_ref, aim_ref, bblk_ref, cre_ref, cim_ref,
             d_ref, wglu_ref, bglu_ref, yc_ref, hre_ref, him_ref,
             xre_s, xim_s, sre_s, sim_s, *, nb, tt, state_t):
    i = pl.program_id(0)
    rows = nb * tt

    @pl.when(i == 0)
    def _():
        sre_s[...] = h0re_ref[...].T if state_t else h0re_ref[...]
        sim_s[...] = h0im_ref[...].T if state_t else h0im_ref[...]

    u = u_ref[...]
    if tt > 1:
        u = jnp.swapaxes(u, 0, 1).reshape(rows, D_C)
    ub = u.astype(BF16)

    ys = []
    for j in range(_S5_NBLK):
        sl = slice(j * _S5_SB, (j + 1) * _S5_SB)
        bu = jnp.dot(ub[:, j * LANES:(j + 1) * LANES], bblk_ref[j], preferred_element_type=F32)
        xre_s[:, sl] = bu[:, :_S5_SB]
        xim_s[:, sl] = bu[:, _S5_SB:]
        ar = jnp.broadcast_to(are_ref[:, sl], (nb, _S5_SB))
        ai = jnp.broadcast_to(aim_ref[:, sl], (nb, _S5_SB))
        xr, xi = sre_s[:, sl], sim_s[:, sl]
        for t in range(tt):
            r = slice(t * nb, (t + 1) * nb)
            xr, xi = (ar * xr - ai * xi + xre_s[r, sl], ar * xi + ai * xr + xim_s[r, sl])
            xre_s[r, sl] = xr
            xim_s[r, sl] = xi
        sre_s[:, sl] = xr
        sim_s[:, sl] = xi
        ys.append(_bdot(xre_s[:, sl], cre_ref[j]) - _bdot(xim_s[:, sl], cim_ref[j]))
    yc = jnp.concatenate(ys, axis=1)
    parts = []
    half = rows // 2 if rows >= 256 else rows
    for r in range(0, rows, half):
        yh = jax.nn.gelu(yc[r:r + half] + d_ref[...] * u[r:r + half])
        parts.append(yh * jax.nn.sigmoid(_bdot(yh, wglu_ref[...]) + bglu_ref[...]))
    y = jnp.concatenate(parts, axis=0)
    if tt > 1:
        y = jnp.swapaxes(y.reshape(tt, nb, D_C), 0, 1)
    yc_ref[...] = y.astype(BF16)
    hre_ref[...] = sre_s[...].T if state_t else sre_s[...]
    him_ref[...] = sim_s[...].T if state_t else sim_s[...]


def _s5(u, h0_re, h0_im, lw, nb, seq, tt, state_t=False):
    rows = tt * nb
    sshape = (D_S5, nb) if state_t else (nb, D_S5)
    if tt > 1:
        assert nb % 8 == 0 and tt % 8 == 0
        u = u.reshape(nb, seq, D_C)
        io_spec = pl.BlockSpec((nb, tt, D_C), lambda i: (0, i, 0))
    else:
        io_spec = pl.BlockSpec((rows, D_C), lambda i: (i, 0))
    in_specs = [io_spec,
                _const_spec(sshape), _const_spec(sshape),
                _const_spec((1, D_S5)), _const_spec((1, D_S5)),
                _const_spec((_S5_NBLK, LANES, 2 * _S5_SB)),
                _const_spec((_S5_NBLK, _S5_SB, LANES)), _const_spec((_S5_NBLK, _S5_SB, LANES)),
                _const_spec((1, D_C)), _layer_spec(lw["l"], (D_C, D_C)), _const_spec((1, D_C))]
    st = pl.BlockSpec(sshape, lambda i: (0, 0))
    yc, h_re, h_im = pl.pallas_call(
        functools.partial(_s5_body, nb=nb, tt=tt, state_t=state_t), grid=(seq // tt,),
        in_specs=in_specs, out_specs=[io_spec, st, st],
        out_shape=[jax.ShapeDtypeStruct(u.shape, BF16),
                   jax.ShapeDtypeStruct(sshape, F32), jax.ShapeDtypeStruct(sshape, F32)],
        scratch_shapes=[pltpu.VMEM((rows, D_S5), F32), pltpu.VMEM((rows, D_S5), F32),
                        pltpu.VMEM((nb, D_S5), F32), pltpu.VMEM((nb, D_S5), F32)],
        compiler_params=_params(1), name="s5")(
            u, h0_re, h0_im, lw["s5_ab_re"], lw["s5_ab_im"], lw["s5_bblk"],
            lw["s5_cre"], lw["s5_cim"], lw["s5_d"], lw["w_glu"], lw["b_glu"])
    return yc.reshape(nb * seq, D_C), h_re, h_im


def _merge_body(x_ref, nw_ref, ya_ref, yb_ref, yc_ref, wg_ref, wa_ref, wb_ref, wc_ref, wo_ref,
                o_ref):
    x = x_ref[...]
    h = (_rms_unit(x) * nw_ref[...]).astype(BF16)
    merged = None
    for k, (y_ref, w_ref) in enumerate(((ya_ref, wa_ref), (yb_ref, wb_ref), (yc_ref, wc_ref))):
        gate = jax.nn.sigmoid(jnp.dot(h, wg_ref[:, k * D_MODEL:(k + 1) * D_MODEL],
                                      preferred_element_type=F32))
        term = gate * _bdot(y_ref[...], w_ref[...])
        merged = term if merged is None else merged + term
    o_ref[...] = x + _bdot(merged, wo_ref[...])


def _merge(x2d, ya, yb, yc, lw, nb, seq, tm):
    nt = seq // tm
    row = lambda w: pl.BlockSpec((tm, w), lambda b, i: (b * nt + i, 0))
    wspec = _layer_spec(lw["l"], (D_MODEL, D_MODEL))
    return pl.pallas_call(
        _merge_body, grid=(nb, nt),
        in_specs=[row(D_MODEL), _const_spec((1, D_MODEL)), row(D_A), row(D_RV), row(D_C),
                  _layer_spec(lw["l"], (D_MODEL, 3 * D_MODEL)), wspec, wspec, wspec, wspec],
        out_specs=row(D_MODEL), out_shape=jax.ShapeDtypeStruct((nb * seq, D_MODEL), F32),
        compiler_params=_params(2), name="merge")(
            x2d, lw["norm_mix_w"], ya, yb, yc, lw["w_gates"],
            lw["w_br_a"], lw["w_br_b"], lw["w_br_c"], lw["w_out"])


_FFN_CW = 256


_FFN_SUB = 512


def _ffn_prompt_body(x_ref, nw_ref, wup_ref, cw_ref, cb_ref, wdn_ref, nf_ref,
                     o_ref, cs_ref, up_s, act_s, *, tm, final):
    @pl.when(pl.program_id(1) == 0)
    def _():
        up_s[0:8, :] = jnp.zeros((8, 2 * D_FF), F32)

    sub = min(tm, _FFN_SUB)
    for r in range(0, tm, sub):
        rs = slice(r, r + sub)
        x = x_ref[rs, :]
        hf = (_rms_unit(x) * nw_ref[...]).astype(BF16)
        up_s[8 + r:8 + r + sub, :] = jnp.dot(hf, wup_ref[...], preferred_element_type=F32)
        for c in range(0, D_FF, _FFN_CW):
            def conv(c0):
                cl = slice(c0, c0 + _FFN_CW)
                y = cb_ref[:, cl]
                for tap in range(CONV_F):
                    y = y + up_s[6 + tap + r:6 + tap + r + sub, cl] * cw_ref[tap:tap + 1, cl]
                return y
            act_s[rs, c:c + _FFN_CW] = (_silu(conv(c)) * conv(D_FF + c)).astype(BF16)
        out = x + jnp.dot(act_s[rs, :], wdn_ref[...], preferred_element_type=F32)
        o_ref[rs, :] = _rms_unit(out) * nf_ref[...] if final else out
    tail = up_s[tm + 6:tm + 8, :]
    cs_ref[...] = tail
    up_s[6:8, :] = tail


def _ffn_sample_body(x_ref, st_ref, nw_ref, wup_ref, cw_ref, cb_ref, wdn_ref, nf_ref,
                     o_ref, cs_ref, act_s, *, final):
    x = x_ref[...]
    hf = (_rms_unit(x) * nw_ref[...]).astype(BF16)
    up = jnp.dot(hf, wup_ref[...], preferred_element_type=F32)
    prev2, prev1 = st_ref[:, :2 * D_FF], st_ref[:, 2 * D_FF:]
    cs_ref[:, :2 * D_FF] = prev1
    cs_ref[:, 2 * D_FF:] = up
    for c in range(0, D_FF, _FFN_CW):
        def conv(c0):
            cl = slice(c0, c0 + _FFN_CW)
            return (cb_ref[:, cl] + prev2[:, cl] * cw_ref[0:1, cl] + prev1[:, cl] * cw_ref[1:2, cl]
                    + up[:, cl] * cw_ref[2:3, cl])
        act_s[:, c:c + _FFN_CW] = (_silu(conv(c)) * conv(D_FF + c)).astype(BF16)
    out = x + jnp.dot(act_s[...], wdn_ref[...], preferred_element_type=F32)
    o_ref[...] = _rms_unit(out) * nf_ref[...] if final else out


def _ffn(x2d, state, lw, nf, nb, seq, tm, final):
    wspecs = [_const_spec((1, D_MODEL)), _layer_spec(lw["l"], (D_MODEL, 2 * D_FF)),
              _const_spec((CONV_F, 2 * D_FF)), _const_spec((1, 2 * D_FF)),
              _layer_spec(lw["l"], (D_FF, D_MODEL)), _const_spec((1, D_MODEL))]
    wargs = (lw["norm_ffn_w"], lw["w_up"], lw["conv_f_w"], lw["conv_f_b"], lw["w_down"], nf)
    act = pltpu.VMEM((tm, D_FF), BF16)
    if state is None:
        nt = seq // tm
        row = pl.BlockSpec((tm, D_MODEL), lambda b, i: (b * nt + i, 0))
        return pl.pallas_call(
            functools.partial(_ffn_prompt_body, tm=tm, final=final), grid=(nb, nt),
            in_specs=[row] + wspecs,
            out_specs=[row, pl.BlockSpec((None, CONV_F - 1, 2 * D_FF), lambda b, i: (b, 0, 0))],
            out_shape=[jax.ShapeDtypeStruct((nb * seq, D_MODEL), F32),
                       jax.ShapeDtypeStruct((nb, CONV_F - 1, 2 * D_FF), F32)],
            scratch_shapes=[pltpu.VMEM((8 + tm, 2 * D_FF), F32), act],
            compiler_params=_params(2), name="ffn_prompt")(x2d, *wargs)
    sw = (CONV_F - 1) * 2 * D_FF
    return pl.pallas_call(
        functools.partial(_ffn_sample_body, final=final), grid=(nb // tm,),
        in_specs=[pl.BlockSpec((tm, D_MODEL), lambda i: (i, 0)),
                  pl.BlockSpec((tm, sw), lambda i: (i, 0))] + wspecs,
        out_specs=[pl.BlockSpec((tm, D_MODEL), lambda i: (i, 0)),
                   pl.BlockSpec((tm, sw), lambda i: (i, 0))],
        out_shape=[jax.ShapeDtypeStruct((nb, D_MODEL), F32), jax.ShapeDtypeStruct((nb, sw), F32)],
        scratch_shapes=[act], compiler_params=_params(1), name="ffn_sample")(x2d, state, *wargs)


def _ssd_sample_body(*refs, n_prev):
    (zs_ref, xbc_ref, dt_ref, cst_ref, cw_ref, cb_ref, dtb_ref, alog_ref, dx_ref, naw_ref,
     hin_ref) = refs[:11]
    (ya_ref, cso_ref, hout_ref, xa_s, xdt_t_s, dec_t_s, bm_t_s, cm_t_s, yt_s) = refs[11 + n_prev:]
    h = pl.program_id(0)

    @pl.when(h == 0)
    def _():
        acc = cb_ref[...]
        for tap in range(CONV_A - 1):
            acc = acc + cst_ref[:, tap * D_XBC:(tap + 1) * D_XBC] * cw_ref[tap:tap + 1, :]
        acc = acc + xbc_ref[...] * cw_ref[CONV_A - 1:CONV_A, :]
        cso_ref[:, :(CONV_A - 2) * D_XBC] = cst_ref[:, D_XBC:]
        cso_ref[:, (CONV_A - 2) * D_XBC:] = xbc_ref[...]
        xc = _silu(acc)
        xa = xc[:, :D_A]
        xa_s[...] = xa
        bm_t_s[...] = xc[:, D_A:D_A + G_A * N_A].T
        cm_t_s[...] = xc[:, D_A + G_A * N_A:].T
        dt = _softplus(dt_ref[...] + dtb_ref[...])
        dec_t_s[...] = jnp.exp(dt * -jnp.exp(alog_ref[...])).T
        dt_x = _dot01(dt, _tile3(_head_expand(), 0), "lhs")
        xdt_t_s[...] = (xa * dt_x).T

    g0 = pl.multiple_of((h // HPG) * N_A, N_A)
    p0 = pl.multiple_of(h * P_A, P_A)
    b_t = bm_t_s[pl.ds(g0, N_A), :]
    c_t = cm_t_s[pl.ds(g0, N_A), :]
    dec = dec_t_s[pl.ds(h, 1), :]

    def per_8p(k, carry):
        r = pl.ds(pl.multiple_of(p0 + k * 8, 8), 8)
        x_rows = xdt_t_s[r, :]
        ys = []
        for u in range(8):
            h_new = hin_ref[k * 8 + u] * dec + x_rows[u:u + 1, :] * b_t
            hout_ref[k * 8 + u] = h_new
            ys.append(jnp.sum(h_new * c_t, axis=0, keepdims=True))
        yt_s[r, :] = jnp.concatenate(ys, axis=0)
        return carry

    lax.fori_loop(0, P_A // 8, per_8p, 0)

    @pl.when(h == H_A - 1)
    def _():
        xa = xa_s[...]
        y = yt_s[...].T + dx_ref[...] * xa
        y = y * zs_ref[...]
        ya_ref[...] = (_rms_unit(y) * naw_ref[...]).astype(BF16)


def _ssd_sample(p, conv_st, h_ssd_t, lw, nb, l, prev):
    full = lambda w: _const_spec((nb, w))
    cw3 = (CONV_A - 1) * D_XBC
    st = pl.BlockSpec((None, None, P_A, N_A, nb), lambda h: (l, h, 0, 0, 0))
    in_specs = [full(D_A), full(D_XBC), full(DT_PAD), full(cw3),
                _const_spec((CONV_A, D_XBC)), _const_spec((1, D_XBC)), _const_spec((1, DT_PAD)),
                _const_spec((1, DT_PAD)), _const_spec((1, D_A)), _const_spec((1, D_A)), st]
    prev = () if prev is None else (prev,)
    in_specs += [pl.BlockSpec(memory_space=pl.ANY)] * len(prev)
    aliases = {len(in_specs) - 1: 2} if prev else {}
    keep = lambda w: pl.BlockSpec((nb, w), lambda h: (0, 0))
    scratch = [pltpu.VMEM((nb, D_A), F32), pltpu.VMEM((D_A, nb), F32), pltpu.VMEM((LANES, nb), F32),
               pltpu.VMEM((G_A * N_A, nb), F32), pltpu.VMEM((G_A * N_A, nb), F32),
               pltpu.VMEM((D_A, nb), F32)]
    ya, cs, hs = pl.pallas_call(
        functools.partial(_ssd_sample_body, n_prev=len(prev)), grid=(H_A,),
        in_specs=in_specs, out_specs=[keep(D_A), keep(cw3), st],
        out_shape=[jax.ShapeDtypeStruct((nb, D_A), BF16), jax.ShapeDtypeStruct((nb, cw3), F32),
                   jax.ShapeDtypeStruct((DEPTH, H_A, P_A, N_A, nb), F32)],
        scratch_shapes=scratch, input_output_aliases=aliases,
        compiler_params=_params(1), name="ssd_sample")(
            p["z"], p["xbc"], p["dt"], conv_st.reshape(nb, cw3),
            lw["conv_a_w"], lw["conv_a_b"], lw["dt_bias"], lw["a_log"], lw["d_x"], lw["norm_a_w"],
            h_ssd_t, *prev)
    return ya, cs.reshape(nb, CONV_A - 1, D_XBC), hs


_RET_BB = 8


def _ret_sample_body(*refs, nb, n_prev):
    q_ref, k_ref, v_ref, gs_ref, cos_ref, sina_ref, sinb_ref, sin_ref = refs[:8]
    yb_ref, sout_ref, q_t_s, k_t_s, yr_s = refs[8 + n_prev:]
    i = pl.program_id(0)

    @pl.when(i == 0)
    def _():
        cos_f, sin_a, sin_b = cos_ref[...], sina_ref[...], sinb_ref[...]
        q_t_s[...] = _rope(q_ref[...], cos_f, sin_a, sin_b).T.astype(BF16)
        k_t_s[...] = (_rope(k_ref[...], cos_f, sin_a, sin_b) * (DK_R ** -0.5)).T.astype(BF16)

    rows_i = lax.broadcasted_iota(jnp.int32, (nb, DV_R), 0)
    blk = pl.ds(pl.multiple_of(i * _RET_BB, _RET_BB), _RET_BB)
    v_blk = v_ref[blk, :]
    ys = [[] for _ in range(H_R)]
    for j in range(_RET_BB):
        onehot = jnp.where(rows_i == i * _RET_BB + j, 1.0, 0.0).astype(BF16)
        k_bc = jnp.dot(k_t_s[...], onehot, preferred_element_type=F32)
        q_bc = jnp.dot(q_t_s[...], onehot, preferred_element_type=F32)
        for h in range(H_R):
            kl = slice(h * DK_R, (h + 1) * DK_R)
            vl = slice(h * DV_R, (h + 1) * DV_R)
            s_new = math.exp(_LOG_GAMMA[h]) * sin_ref[j, h] + k_bc[kl, :] * v_blk[j:j + 1, vl]
            sout_ref[j, h] = s_new
            ys[h].append(jnp.sum(q_bc[kl, :] * s_new, axis=0, keepdims=True))
    for h in range(H_R):
        yr_s[blk, h * DV_R:(h + 1) * DV_R] = jnp.concatenate(ys[h], axis=0)

    @pl.when(i == nb // _RET_BB - 1)
    def _():
        for h in range(H_R):
            vl = slice(h * DV_R, (h + 1) * DV_R)
            yb_ref[:, vl] = (gs_ref[:, vl] * _rms_unit(yr_s[:, vl])).astype(BF16)


def _ret_sample(p, s_ret, rot, nb, l, prev):
    full = lambda w: _const_spec((nb, w))
    st = pl.BlockSpec((None, _RET_BB, H_R, DK_R, DV_R), lambda i: (l, i, 0, 0, 0))
    in_specs = [full(D_QK), full(D_QK), full(D_RV), full(D_RV),
                _const_spec((1, D_QK)), _const_spec((1, D_QK)), _const_spec((1, D_QK)), st]
    prev = () if prev is None else (prev,)
    in_specs += [pl.BlockSpec(memory_space=pl.ANY)] * len(prev)
    aliases = {len(in_specs) - 1: 1} if prev else {}
    return pl.pallas_call(
        functools.partial(_ret_sample_body, nb=nb, n_prev=len(prev)), grid=(nb // _RET_BB,),
        in_specs=in_specs, out_specs=[pl.BlockSpec((nb, D_RV), lambda i: (0, 0)), st],
        out_shape=[jax.ShapeDtypeStruct((nb, D_RV), BF16),
                   jax.ShapeDtypeStruct((DEPTH, nb, H_R, DK_R, DV_R), F32)],
        scratch_shapes=[pltpu.VMEM((D_QK, nb), BF16), pltpu.VMEM((D_QK, nb), BF16),
                        pltpu.VMEM((nb, D_RV), F32)],
        input_output_aliases=aliases, compiler_params=_params(1), name="ret_sample")(
            p["q"], p["k"], p["v"], p["g"], rot["cos"], rot["sin_a"], rot["sin_b"], s_ret, *prev)


def _rotary_tables(start, length):
    half = DK_R // 2
    pos = start + jnp.arange(length, dtype=F32)
    freqs = ROPE_BASE ** (-jnp.arange(half, dtype=F32) / half)
    ang = pos[:, None] * freqs[None, :]
    cos, sin, zero = jnp.cos(ang), jnp.sin(ang), jnp.zeros_like(ang)
    tile = lambda a, b: jnp.tile(jnp.concatenate([a, b], axis=-1), (1, H_R))
    lg = jnp.log1p(-jnp.exp2(-5.0 - jnp.arange(H_R, dtype=F32)))
    return {"cos": tile(cos, cos), "sin_a": tile(-sin, zero), "sin_b": tile(zero, sin),
            "lgx": jnp.repeat(lg, DK_R)[None, :]}


def _matmul_weights(w):
    w_in = w["w_in"]
    o_dt = D_A + D_XBC
    o_gates = w_in.shape[-1] - 3 * D_MODEL
    dt_cols = jnp.pad(w_in[..., o_dt:o_dt + H_A], ((0, 0), (0, 0), (0, DT_PAD - H_A)))
    w_proj = jnp.concatenate([w_in[..., :o_dt], w_in[..., o_dt + H_A:o_gates], dt_cols], axis=-1)
    out = {"w_proj": w_proj.astype(BF16), "w_gates": w_in[..., o_gates:].astype(BF16)}
    for name in ("w_glu", "w_br_a", "w_br_b", "w_br_c", "w_out", "w_up", "w_down"):
        out[name] = w[name].astype(BF16)
    return out


def _layer_weights(l, w, mats):
    pad_row = lambda a: jnp.pad(a, (0, DT_PAD - H_A))[None, :]
    ab_re, ab_im, bb_re, bb_im = _s5_discretise(
        w["s5_a_re"][l], w["s5_a_im"][l], w["s5_log_dt"][l], w["s5_b_re"][l], w["s5_b_im"][l])
    bblk = jnp.concatenate([_block_diag(bb_re), _block_diag(bb_im)], axis=-1)
    c_t = lambda c: _block_diag(jnp.transpose(c, (0, 2, 1)))
    return {
        "l": l, **mats,
        "norm_mix_w": w["norm_mix_w"][l][None, :],
        "conv_a_w": w["conv_a_w"][l], "conv_a_b": w["conv_a_b"][l][None, :],
        "dt_bias": pad_row(w["dt_bias"][l]), "a_log": pad_row(w["a_log"][l]),
        "d_x": jnp.repeat(w["d_a"][l], P_A)[None, :], "norm_a_w": w["norm_a_w"][l][None, :],
        "s5_ab_re": ab_re.reshape(1, D_S5), "s5_ab_im": ab_im.reshape(1, D_S5),
        "s5_bblk": bblk.astype(BF16),
        "s5_cre": c_t(w["s5_c_re"][l]).astype(BF16), "s5_cim": c_t(w["s5_c_im"][l]).astype(BF16),
        "s5_d": w["s5_d"][l][None, :], "b_glu": w["b_glu"][l][None, :],
        "norm_ffn_w": w["norm_ffn_w"][l][None, :],
        "conv_f_w": w["conv_f_w"][l], "conv_f_b": w["conv_f_b"][l][None, :],
    }


def _prompt_trunk(x, lws, nf, tm_proj, tsteps, tt, tm_merge, tm_ffn):
    nb, seq, _ = x.shape
    rot = _rotary_tables(0.0, seq)
    x2d = x.reshape(nb * seq, D_MODEL)
    zeros = jnp.zeros((nb, D_S5), F32)
    states, big = [], None
    for l, lw in enumerate(lws):
        p = _inproj(x2d, lw["norm_mix_w"], lw, rot, nb, seq, tm_proj, conv=True)
        ya, yb, *big = _mix_prompt(p, rot, lw, nb, seq, tsteps, l, big)
        yc, h_re, h_im = _s5(p["u"], zeros, zeros, lw, nb, seq, tt)
        x2d = _merge(x2d, ya, yb, yc, lw, nb, seq, tm_merge)
        x2d, fs = _ffn(x2d, None, lw, nf, nb, seq, tm_ffn, final=(l == len(lws) - 1))
        states.append((p["conv_state"], h_re.reshape(nb, G_C, P_C), h_im.reshape(nb, G_C, P_C), fs))
    cs, h_re, h_im, fs = [jnp.stack(s) for s in zip(*states)]
    return x2d.reshape(nb, seq, D_MODEL), [cs, big[0], big[1], h_re, h_im, fs]


def _sample_trunk(x, st, lws, nf):
    nb = x.shape[0]
    rot = _rotary_tables(float(PAST_LEN), 1)
    x2d = x.reshape(nb, D_MODEL)
    st_conv, st_ssd, st_ret, st_re, st_im, st_ffn = st
    st_ssd_t = jnp.transpose(st_ssd, (0, 2, 3, 4, 1))
    s5_t = lambda s: jnp.transpose(s, (0, 2, 3, 1)).reshape(DEPTH, D_S5, nb)
    st_re_t, st_im_t = s5_t(st_re), s5_t(st_im)
    states, hs, ss = [], None, None
    for l, lw in enumerate(lws):
        p = _inproj(x2d, lw["norm_mix_w"], lw, rot, 1, nb, nb, conv=False)
        ya, cs, hs = _ssd_sample(p, st_conv[l], st_ssd_t, lw, nb, l, hs)
        yb, ss = _ret_sample(p, st_ret, rot, nb, l, ss)
        yc, h_re, h_im = _s5(p["u"], st_re_t[l], st_im_t[l], lw, nb, 1, 1, state_t=True)
        x2d = _merge(x2d, ya, yb, yc, lw, 1, nb, nb)
        x2d, fs = _ffn(x2d, st_ffn[l].reshape(nb, -1), lw, nf, nb, 1, nb,
                       final=(l == len(lws) - 1))
        states.append((cs, h_re, h_im, fs.reshape(nb, CONV_F - 1, 2 * D_FF)))
    cs, h_re, h_im, fs = [jnp.stack(s) for s in zip(*states)]
    hs = jnp.transpose(hs, (0, 4, 1, 2, 3))
    s5_back = lambda s: jnp.transpose(s.reshape(DEPTH, G_C, P_C, nb), (0, 3, 1, 2))
    return x2d.reshape(nb, 1, D_MODEL), [cs, hs, ss, s5_back(h_re), s5_back(h_im), fs]


def kernel(x_prompt, x_sample, state_ssd_conv, state_ssd, state_ret, state_s5_re, state_s5_im,
           state_ffn_conv, norm_mix_w, w_in, conv_a_w, conv_a_b, dt_bias, a_log, d_a, norm_a_w,
           s5_a_re, s5_a_im, s5_log_dt, s5_b_re, s5_b_im, s5_c_re, s5_c_im, s5_d, w_glu, b_glu,
           w_br_a, w_br_b, w_br_c, w_out, norm_ffn_w, w_up, conv_f_w, conv_f_b, w_down, norm_f_w):
    w = dict(norm_mix_w=norm_mix_w, w_in=w_in, conv_a_w=conv_a_w, conv_a_b=conv_a_b,
             dt_bias=dt_bias, a_log=a_log, d_a=d_a, norm_a_w=norm_a_w, s5_a_re=s5_a_re,
             s5_a_im=s5_a_im, s5_log_dt=s5_log_dt, s5_b_re=s5_b_re, s5_b_im=s5_b_im,
             s5_c_re=s5_c_re, s5_c_im=s5_c_im, s5_d=s5_d, w_glu=w_glu, b_glu=b_glu,
             w_br_a=w_br_a, w_br_b=w_br_b, w_br_c=w_br_c, w_out=w_out, norm_ffn_w=norm_ffn_w,
             w_up=w_up, conv_f_w=conv_f_w, conv_f_b=conv_f_b, w_down=w_down)
    mats = _matmul_weights(w)
    lws = [_layer_weights(l, w, mats) for l in range(DEPTH)]
    nf = norm_f_w[None, :]
    seq = x_prompt.shape[1]
    y_p, p_st = _prompt_trunk(x_prompt, lws, nf, tm_proj=min(512, seq), tsteps=min(512, seq),
                              tt=min(64, seq), tm_merge=min(512, seq), tm_ffn=min(512, seq))
    y_s, s_st = _sample_trunk(
        x_sample, (state_ssd_conv, state_ssd, state_ret, state_s5_re, state_s5_im, state_ffn_conv),
        lws, nf)
    return (y_p, y_s, *p_st, *s_st)
```

```python
import functools
import math

import jax
import jax.numpy as jnp
import numpy as np
from jax import lax
from jax.experimental import pallas as pl
from jax.experimental.pallas import tpu as pltpu

F32 = jnp.float32
BF16 = jnp.bfloat16

D_MODEL = 1024
DEPTH = 2
PAST_LEN = 16384
H_A, P_A, N_A, G_A, CONV_A = 16, 64, 64, 2, 4
D_A = H_A * P_A
D_XBC = D_A + 2 * G_A * N_A
H_R, DK_R, DV_R = 8, 64, 128
D_QK = H_R * DK_R
D_RV = H_R * DV_R
ROPE_BASE = 10000.0
GS_C, G_C, P_C = 16, 64, 64
D_C = G_C * GS_C
D_S5 = G_C * P_C
D_FF = 2816
CONV_F = 3
CHUNK = 128
EPS = 1e-6
LANES = 128
DT_PAD = LANES
HPG = H_A // G_A
VMEM_LIMIT = 56 * 1024 * 1024

_LOG_GAMMA = [math.log1p(-(2.0 ** (-5.0 - h))) for h in range(H_R)]

_SEGS = (("z", D_A), ("xbc", D_XBC), ("q", D_QK), ("k", D_QK),
         ("v", D_RV), ("g", D_RV), ("u", D_C), ("dt", DT_PAD))
_SEG_OFF = np.cumsum([0] + [w for _, w in _SEGS]).tolist()
D_PROJ = _SEG_OFF[-1]


def _rms_unit(x):
    return x * lax.rsqrt(jnp.mean(x * x, axis=-1, keepdims=True) + EPS)


def _silu(x):
    return x * jax.nn.sigmoid(x)


def _softplus(x):
    return jnp.maximum(x, 0.0) + jnp.log1p(jnp.exp(-jnp.abs(x)))


def _bdot(a, b):
    return jnp.dot(a.astype(BF16), b.astype(BF16), preferred_element_type=F32)


def _split3(x):
    hi = x.astype(BF16)
    r1 = x - hi.astype(F32)
    mid = r1.astype(BF16)
    return hi, mid, (r1 - mid.astype(F32)).astype(BF16)


def _tile3(m01, axis):
    return jnp.concatenate([m01] * 3, axis=axis)


def _dot01(a, b, f32_side):
    if f32_side == "lhs":
        return jnp.dot(jnp.concatenate(_split3(a), axis=1), b, preferred_element_type=F32)
    return jnp.dot(a, jnp.concatenate(_split3(b), axis=0), preferred_element_type=F32)


def _const_spec(shape):
    nd = len(shape)
    return pl.BlockSpec(shape, lambda *_: (0,) * nd, pipeline_mode=pl.Buffered(1))


def _layer_spec(l, shape):
    nd = len(shape)
    return pl.BlockSpec((None,) + tuple(shape), lambda *_: (l,) + (0,) * nd,
                        pipeline_mode=pl.Buffered(1))


def _params(n_grid):
    return pltpu.CompilerParams(dimension_semantics=("arbitrary",) * n_grid,
                                vmem_limit_bytes=VMEM_LIMIT)


_PROJ_SUB = 128


def _inproj_body(x_ref, nw_ref, cw_ref, cb_ref, cos_ref, sina_ref, sinb_ref, w_ref, *refs,
                 tm, conv):
    n = len(_SEGS)
    if conv:
        cs_ref, xp_s = refs[n:]

        @pl.when(pl.program_id(1) == 0)
        def _():
            xp_s[0:8, :] = jnp.zeros((8, D_XBC), F32)

    for r in range(0, tm, _PROJ_SUB):
        rs = slice(r, r + _PROJ_SUB)
        h = (_rms_unit(x_ref[rs, :]) * nw_ref[...]).astype(BF16)
        for (name, width), off, o_ref in zip(_SEGS, _SEG_OFF, refs[:n]):
            y = jnp.dot(h, w_ref[:, off:off + width], preferred_element_type=F32)
            if name in ("z", "g"):
                y = _silu(y)
            elif name in ("q", "k") and conv:
                y = _rope(y, cos_ref[rs, :], sina_ref[rs, :], sinb_ref[rs, :])
                if name == "k":
                    y = y * (DK_R ** -0.5)
            elif name == "xbc" and conv:
                xp_s[8 + r:8 + r + _PROJ_SUB, :] = y
                y = cb_ref[...]
                for tap in range(CONV_A):
                    y = y + xp_s[5 + tap + r:5 + tap + r + _PROJ_SUB, :] * cw_ref[tap:tap + 1, :]
                y = _silu(y)
            o_ref[rs, :] = y
    if conv:
        tail = xp_s[tm + 5:tm + 8, :]
        cs_ref[...] = tail
        xp_s[5:8, :] = tail


def _inproj(x2d, nw, lw, rot, nb, seq, tm, conv):
    nt = seq // tm
    tab = (pl.BlockSpec((tm, D_QK), lambda b, i: (i, 0)) if conv
           else _const_spec(rot["cos"].shape))
    in_specs = [pl.BlockSpec((tm, D_MODEL), lambda b, i: (b * nt + i, 0)),
                _const_spec((1, D_MODEL)), _const_spec((CONV_A, D_XBC)), _const_spec((1, D_XBC)),
                tab, tab, tab, _layer_spec(lw["l"], (D_MODEL, D_PROJ))]
    out_specs, out_shape, scratch = [], [], []
    for _, width in _SEGS:
        out_specs.append(pl.BlockSpec((tm, width), lambda b, i: (b * nt + i, 0)))
        out_shape.append(jax.ShapeDtypeStruct((nb * seq, width), F32))
    if conv:
        out_specs.append(pl.BlockSpec((None, CONV_A - 1, D_XBC), lambda b, i: (b, 0, 0)))
        out_shape.append(jax.ShapeDtypeStruct((nb, CONV_A - 1, D_XBC), F32))
        scratch.append(pltpu.VMEM((8 + tm, D_XBC), F32))
    outs = pl.pallas_call(
        functools.partial(_inproj_body, tm=tm, conv=conv), grid=(nb, nt), in_specs=in_specs,
        out_specs=out_specs, out_shape=out_shape, scratch_shapes=scratch,
        compiler_params=_params(2), name="inproj")(
            x2d, nw, lw["conv_a_w"], lw["conv_a_b"], rot["cos"], rot["sin_a"], rot["sin_b"],
            lw["w_proj"])
    res = {name: o for (name, _), o in zip(_SEGS, outs)}
    if conv:
        res["conv_state"] = outs[-1]
    return res


def _rope(x, cos_f, sin_a, sin_b):
    half = DK_R // 2
    return (x * cos_f + pltpu.roll(x, D_QK - half, 1) * sin_a
            + pltpu.roll(x, half, 1) * sin_b)


def _head_expand():
    lo = lax.broadcasted_iota(jnp.int32, (LANES, D_A), 0) * P_A
    c = lax.broadcasted_iota(jnp.int32, (LANES, D_A), 1)
    return jnp.where(c >= lo, jnp.where(c < lo + P_A, 1.0, 0.0), 0.0).astype(BF16)


def _mix_body(*refs, tsteps, nt, n_prev):
    (zs_ref, xc_ref, dt_ref, q_ref, k_ref, v_ref, gs_ref, lgx_ref,
     dtb_ref, alog_ref, dx_ref, naw_ref) = refs[:12]
    (ya_ref, yb_ref, hs_ref, ss_ref,
     ht_s, s_s, intra_s, qdec_s, kdect_s) = refs[12 + n_prev:]
    i = pl.program_id(1)
    first = jnp.logical_and(pl.program_id(0) == 0, i == 0)
    rows_i = lax.broadcasted_iota(jnp.int32, (CHUNK, CHUNK), 0)
    cols_i = lax.broadcasted_iota(jnp.int32, (CHUNK, CHUNK), 1)
    causal = rows_i >= cols_i
    lane_lo = cols_i < LANES // 2

    @pl.when(first)
    def _():
        rel = (rows_i - cols_i).astype(F32)
        for h in range(H_R):
            intra_s[h] = jnp.where(causal, jnp.exp(jnp.maximum(rel, 0.0) * _LOG_GAMMA[h]), 0.0)
        ri = lax.broadcasted_iota(jnp.int32, (CHUNK, D_QK), 0).astype(F32)
        lgx = lgx_ref[...]
        qdec_s[...] = jnp.exp((ri + 1.0) * lgx)
        kdect_s[...] = jnp.exp((CHUNK - 1.0 - ri) * lgx).T

    @pl.when(i == 0)
    def _():
        ht_s[...] = jnp.zeros_like(ht_s)
        s_s[...] = jnp.zeros_like(s_s)

    tril = _tile3(jnp.where(causal, 1.0, 0.0).astype(BF16), 1)
    expand = _tile3(_head_expand(), 0)
    a_neg = -jnp.exp(alog_ref[...])

    for c in range(tsteps // CHUNK):
        r0 = c * CHUNK
        rows = slice(r0, r0 + CHUNK)
        xa = xc_ref[rows, :D_A]
        bm = xc_ref[rows, D_A:D_A + G_A * N_A]
        cm = xc_ref[rows, D_A + G_A * N_A:]
        dt = _softplus(dt_ref[rows, :] + dtb_ref[...])
        cum = _dot01(tril, dt * a_neg, "rhs")
        cum_t = cum.T
        dt_t = dt.T
        cum_last = cum[CHUNK - 1:CHUNK, :]
        scale = jnp.concatenate(
            [dt * jnp.exp(cum_last - cum), jnp.exp(cum),
             jnp.broadcast_to(jnp.exp(cum_last), (8, LANES))], axis=0)
        scale_x = _dot01(scale, expand, "lhs")
        w_x = scale_x[:CHUNK]
        ecum_x = scale_x[CHUNK:2 * CHUNK]
        dec_x = scale_x[2 * CHUNK:2 * CHUNK + 1]
        bm_t = bm.T
        ys = []
        for g in range(G_A):
            gl = slice(g * N_A, (g + 1) * N_A)
            hl = slice(g * HPG * P_A, (g + 1) * HPG * P_A)
            cg = cm[:, gl]
            scores = lax.dot_general(cg.astype(BF16), bm[:, gl].astype(BF16),
                                     (((1,), (1,)), ((), ())), preferred_element_type=F32)
            scores = jnp.where(causal, scores, 0.0)
            ht_g = ht_s[:, hl]
            y_inter = _bdot(cg, ht_g) * ecum_x[:, hl]
            for pair in range(HPG // 2):
                pl_ = slice(g * HPG * P_A + pair * LANES, g * HPG * P_A + (pair + 1) * LANES)
                x_pair = xa[:, pl_]
                sps = []
                for h in (g * HPG + 2 * pair, g * HPG + 2 * pair + 1):
                    seg = cum[:, h:h + 1] - cum_t[h:h + 1, :]
                    sps.append(scores * jnp.exp(jnp.minimum(seg, 0.0)) * dt_t[h:h + 1, :])
                rhs = jnp.concatenate([jnp.where(lane_lo, x_pair, 0.0),
                                       jnp.where(lane_lo, 0.0, x_pair)], axis=0)
                ys.append(_bdot(jnp.concatenate(sps, axis=1), rhs)
                          + y_inter[:, pair * LANES:(pair + 1) * LANES])
            ht_s[:, hl] = dec_x[:, hl] * ht_g + _bdot(bm_t[gl, :], xa[:, hl] * w_x[:, hl])
        y = jnp.concatenate(ys, axis=1) + dx_ref[...] * xa
        y = y * zs_ref[rows, :]
        ya_ref[rows, :] = (_rms_unit(y) * naw_ref[...]).astype(BF16)

        qr = q_ref[rows, :]
        k_t = k_ref[rows, :].T
        kd_t = k_t * kdect_s[...]
        qd = qr * qdec_s[...]
        for pair in range(H_R // 2):
            pl_ = slice(pair * LANES, (pair + 1) * LANES)
            s_pair = s_s[pl_, :]
            for h, keep in ((2 * pair, lane_lo), (2 * pair + 1, jnp.logical_not(lane_lo))):
                kl = slice(h * DK_R, (h + 1) * DK_R)
                vl = slice(h * DV_R, (h + 1) * DV_R)
                v_h = v_ref[rows, vl]
                sc = _bdot(jnp.where(keep, qr[:, pl_], 0.0), k_t[pl_, :]) * intra_s[h]
                y_h = _bdot(jnp.concatenate([sc, jnp.where(keep, qd[:, pl_], 0.0)], axis=1),
                            jnp.concatenate([v_h, s_pair], axis=0))
                s_s[kl, :] = (math.exp(CHUNK * _LOG_GAMMA[h]) * s_s[kl, :]
                              + _bdot(kd_t[kl, :], v_h))
                yb_ref[rows, vl] = (gs_ref[rows, vl] * _rms_unit(y_h)).astype(BF16)

    @pl.when(i == nt - 1)
    def _():
        for h in range(H_A):
            hs_ref[h] = ht_s[:, h * P_A:(h + 1) * P_A].T
        for h in range(H_R):
            ss_ref[h] = s_s[h * DK_R:(h + 1) * DK_R, :]


def _layer_block(l, tail):
    zeros = (0,) * len(tail)
    return pl.BlockSpec((None, None) + tail, lambda b, *_: (l, b) + zeros)


def _mix_prompt(p, rot, lw, nb, seq, tsteps, l, prev):
    nt = seq // tsteps
    row = lambda w: pl.BlockSpec((tsteps, w), lambda b, i: (b * nt + i, 0))
    in_specs = [row(D_A), row(D_XBC), row(DT_PAD), row(D_QK), row(D_QK), row(D_RV), row(D_RV),
                _const_spec((1, D_QK)), _const_spec((1, DT_PAD)),
                _const_spec((1, DT_PAD)), _const_spec((1, D_A)), _const_spec((1, D_A))]
    prev = () if prev is None else tuple(prev)
    in_specs += [pl.BlockSpec(memory_space=pl.ANY)] * len(prev)
    aliases = {len(in_specs) - len(prev) + k: 2 + k for k in range(len(prev))}
    out_specs = [row(D_A), row(D_RV),
                 _layer_block(l, (H_A, P_A, N_A)), _layer_block(l, (H_R, DK_R, DV_R))]
    out_shape = [jax.ShapeDtypeStruct((nb * seq, D_A), BF16),
                 jax.ShapeDtypeStruct((nb * seq, D_RV), BF16),
                 jax.ShapeDtypeStruct((DEPTH, nb, H_A, P_A, N_A), F32),
                 jax.ShapeDtypeStruct((DEPTH, nb, H_R, DK_R, DV_R), F32)]
    scratch = [pltpu.VMEM((N_A, D_A), F32),
               pltpu.VMEM((D_QK, DV_R), F32), pltpu.VMEM((H_R, CHUNK, CHUNK), F32),
               pltpu.VMEM((CHUNK, D_QK), F32), pltpu.VMEM((D_QK, CHUNK), F32)]
    return pl.pallas_call(
        functools.partial(_mix_body, tsteps=tsteps, nt=nt, n_prev=len(prev)), grid=(nb, nt),
        in_specs=in_specs, out_specs=out_specs, out_shape=out_shape, scratch_shapes=scratch,
        input_output_aliases=aliases, compiler_params=_params(2), name="mix_prompt")(
            p["z"], p["xbc"], p["dt"], p["q"], p["k"], p["v"], p["g"],
            rot["lgx"], lw["dt_bias"], lw["a_log"], lw["d_x"], lw["norm_a_w"], *prev)


def _s5_disc_body(are_ref, aim_ref, ldt_ref, bre_ref, bim_ref,
                  abre_ref, abim_ref, bbre_ref, bbim_ref):
    ar, ai = are_ref[...], aim_ref[...]
    dt = jnp.exp(ldt_ref[...])
    mag = jnp.exp(ar * dt)
    ab_re = mag * jnp.cos(ai * dt)
    ab_im = mag * jnp.sin(ai * dt)
    den = ar * ar + ai * ai
    num_re = ab_re - 1.0
    coef_re = (num_re * ar + ab_im * ai) / den
    coef_im = (ab_im * ar - num_re * ai) / den
    abre_ref[...] = ab_re
    abim_ref[...] = ab_im
    for c in range(GS_C):
        cl = slice(c * P_C, (c + 1) * P_C)
        br, bi = bre_ref[:, cl], bim_ref[:, cl]
        bbre_ref[:, cl] = coef_re * br - coef_im * bi
        bbim_ref[:, cl] = coef_re * bi + coef_im * br


def _s5_discretise(a_re, a_im, log_dt, b_re, b_im):
    gp = jax.ShapeDtypeStruct((G_C, P_C), F32)
    gcp = jax.ShapeDtypeStruct((G_C, GS_C * P_C), F32)
    b_t = lambda b: jnp.transpose(b, (0, 2, 1)).reshape(G_C, GS_C * P_C)
    ab_re, ab_im, bb_re, bb_im = pl.pallas_call(
        _s5_disc_body, out_shape=[gp, gp, gcp, gcp], name="s5_disc")(
            a_re, a_im, log_dt.reshape(G_C, 1), b_t(b_re), b_t(b_im))
    return ab_re, ab_im, bb_re.reshape(G_C, GS_C, P_C), bb_im.reshape(G_C, GS_C, P_C)


_S5_GB = LANES // GS_C
_S5_NBLK = G_C // _S5_GB
_S5_SB = _S5_GB * P_C


def _block_diag(m):
    g, r, c = m.shape
    m = m.reshape(_S5_NBLK, _S5_GB, r, c)
    eye = jnp.eye(_S5_GB, dtype=m.dtype)
    return jnp.einsum("jgrc,gk->jgrkc", m, eye).reshape(_S5_NBLK, _S5_GB * r, _S5_GB * c)


def _s5_body(u_ref, h0re_ref, h0im_ref, are_ref, aim_ref, bblk_ref, cre_ref, cim_ref,
             d_ref, wglu_ref, bglu_ref, yc_ref, hre_ref, him_ref,
             xre_s, xim_s, sre_s, sim_s, *, nb, tt, state_t):
    i = pl.program_id(0)
    rows = nb * tt

    @pl.when(i == 0)
    def _():
        sre_s[...] = h0re_ref[...].T if state_t else h0re_ref[...]
        sim_s[...] = h0im_ref[...].T if state_t else h0im_ref[...]

    u = u_ref[...]
    if tt > 1:
        u = jnp.swapaxes(u, 0, 1).reshape(rows, D_C)
    ub = u.astype(BF16)

    ys = []
    for j in range(_S5_NBLK):
        sl = slice(j * _S5_SB, (j + 1) * _S5_SB)
        bu = jnp.dot(ub[:, j * LANES:(j + 1) * LANES], bblk_ref[j], preferred_element_type=F32)
        xre_s[:, sl] = bu[:, :_S5_SB]
        xim_s[:, sl] = bu[:, _S5_SB:]
        ar = jnp.broadcast_to(are_ref[:, sl], (nb, _S5_SB))
        ai = jnp.broadcast_to(aim_ref[:, sl], (nb, _S5_SB))
        xr, xi = sre_s[:, sl], sim_s[:, sl]
        for t in range(tt):
            r = slice(t * nb, (t + 1) * nb)
            xr, xi = (ar * xr - ai * xi + xre_s[r, sl], ar * xi + ai * xr + xim_s[r, sl])
            xre_s[r, sl] = xr
            xim_s[r, sl] = xi
        sre_s[:, sl] = xr
        sim_s[:, sl] = xi
        ys.append(_bdot(xre_s[:, sl], cre_ref[j]) - _bdot(xim_s[:, sl], cim_ref[j]))
    yc = jnp.concatenate(ys, axis=1)
    parts = []
    half = rows // 2 if rows >= 256 else rows
    for r in range(0, rows, half):
        yh = jax.nn.gelu(yc[r:r + half] + d_ref[...] * u[r:r + half])
        parts.append(yh * jax.nn.sigmoid(_bdot(yh, wglu_ref[...]) + bglu_ref[...]))
    y = jnp.concatenate(parts, axis=0)
    if tt > 1:
        y = jnp.swapaxes(y.reshape(tt, nb, D_C), 0, 1)
    yc_ref[...] = y.astype(BF16)
    hre_ref[...] = sre_s[...].T if state_t else sre_s[...]
    him_ref[...] = sim_s[...].T if state_t else sim_s[...]


def _s5(u, h0_re, h0_im, lw, nb, seq, tt, state_t=False):
    rows = tt * nb
    sshape = (D_S5, nb) if state_t else (nb, D_S5)
    if tt > 1:
        assert nb % 8 == 0 and tt % 8 == 0
        u = u.reshape(nb, seq, D_C)
        io_spec = pl.BlockSpec((nb, tt, D_C), lambda i: (0, i, 0))
    else:
        io_spec = pl.BlockSpec((rows, D_C), lambda i: (i, 0))
    in_specs = [io_spec,
                _const_spec(sshape), _const_spec(sshape),
                _const_spec((1, D_S5)), _const_spec((1, D_S5)),
                _const_spec((_S5_NBLK, LANES, 2 * _S5_SB)),
                _const_spec((_S5_NBLK, _S5_SB, LANES)), _const_spec((_S5_NBLK, _S5_SB, LANES)),
                _const_spec((1, D_C)), _layer_spec(lw["l"], (D_C, D_C)), _const_spec((1, D_C))]
    st = pl.BlockSpec(sshape, lambda i: (0, 0))
    yc, h_re, h_im = pl.pallas_call(
        functools.partial(_s5_body, nb=nb, tt=tt, state_t=state_t), grid=(seq // tt,),
        in_specs=in_specs, out_specs=[io_spec, st, st],
        out_shape=[jax.ShapeDtypeStruct(u.shape, BF16),
                   jax.ShapeDtypeStruct(sshape, F32), jax.ShapeDtypeStruct(sshape, F32)],
        scratch_shapes=[pltpu.VMEM((rows, D_S5), F32), pltpu.VMEM((rows, D_S5), F32),
                        pltpu.VMEM((nb, D_S5), F32), pltpu.VMEM((nb, D_S5), F32)],
        compiler_params=_params(1), name="s5")(
            u, h0_re, h0_im, lw["s5_ab_re"], lw["s5_ab_im"], lw["s5_bblk"],
            lw["s5_cre"], lw["s5_cim"], lw["s5_d"], lw["w_glu"], lw["b_glu"])
    return yc.reshape(nb * seq, D_C), h_re, h_im


def _merge_body(x_ref, nw_ref, ya_ref, yb_ref, yc_ref, wg_ref, wa_ref, wb_ref, wc_ref, wo_ref,
                o_ref):
    x = x_ref[...]
    h = (_rms_unit(x) * nw_ref[...]).astype(BF16)
    merged = None
    for k, (y_ref, w_ref) in enumerate(((ya_ref, wa_ref), (yb_ref, wb_ref), (yc_ref, wc_ref))):
        gate = jax.nn.sigmoid(jnp.dot(h, wg_ref[:, k * D_MODEL:(k + 1) * D_MODEL],
                                      preferred_element_type=F32))
        term = gate * _bdot(y_ref[...], w_ref[...])
        merged = term if merged is None else merged + term
    o_ref[...] = x + _bdot(merged, wo_ref[...])


def _merge(x2d, ya, yb, yc, lw, nb, seq, tm):
    nt = seq // tm
    row = lambda w: pl.BlockSpec((tm, w), lambda b, i: (b * nt + i, 0))
    wspec = _layer_spec(lw["l"], (D_MODEL, D_MODEL))
    return pl.pallas_call(
        _merge_body, grid=(nb, nt),
        in_specs=[row(D_MODEL), _const_spec((1, D_MODEL)), row(D_A), row(D_RV), row(D_C),
                  _layer_spec(lw["l"], (D_MODEL, 3 * D_MODEL)), wspec, wspec, wspec, wspec],
        out_specs=row(D_MODEL), out_shape=jax.ShapeDtypeStruct((nb * seq, D_MODEL), F32),
        compiler_params=_params(2), name="merge")(
            x2d, lw["norm_mix_w"], ya, yb, yc, lw["w_gates"],
            lw["w_br_a"], lw["w_br_b"], lw["w_br_c"], lw["w_out"])


_FFN_CW = 256


_FFN_SUB = 512


def _ffn_prompt_body(x_ref, nw_ref, wup_ref, cw_ref, cb_ref, wdn_ref, nf_ref,
                     o_ref, cs_ref, up_s, act_s, *, tm, final):
    @pl.when(pl.program_id(1) == 0)
    def _():
        up_s[0:8, :] = jnp.zeros((8, 2 * D_FF), F32)

    sub = min(tm, _FFN_SUB)
    for r in range(0, tm, sub):
        rs = slice(r, r + sub)
        x = x_ref[rs, :]
        hf = (_rms_unit(x) * nw_ref[...]).astype(BF16)
        up_s[8 + r:8 + r + sub, :] = jnp.dot(hf, wup_ref[...], preferred_element_type=F32)
        for c in range(0, D_FF, _FFN_CW):
            def conv(c0):
                cl = slice(c0, c0 + _FFN_CW)
                y = cb_ref[:, cl]
                for tap in range(CONV_F):
                    y = y + up_s[6 + tap + r:6 + tap + r + sub, cl] * cw_ref[tap:tap + 1, cl]
                return y
            act_s[rs, c:c + _FFN_CW] = (_silu(conv(c)) * conv(D_FF + c)).astype(BF16)
        out = x + jnp.dot(act_s[rs, :], wdn_ref[...], preferred_element_type=F32)
        o_ref[rs, :] = _rms_unit(out) * nf_ref[...] if final else out
    tail = up_s[tm + 6:tm + 8, :]
    cs_ref[...] = tail
    up_s[6:8, :] = tail


def _ffn_sample_body(x_ref, st_ref, nw_ref, wup_ref, cw_ref, cb_ref, wdn_ref, nf_ref,
                     o_ref, cs_ref, act_s, *, final):
    x = x_ref[...]
    hf = (_rms_unit(x) * nw_ref[...]).astype(BF16)
    up = jnp.dot(hf, wup_ref[...], preferred_element_type=F32)
    prev2, prev1 = st_ref[:, :2 * D_FF], st_ref[:, 2 * D_FF:]
    cs_ref[:, :2 * D_FF] = prev1
    cs_ref[:, 2 * D_FF:] = up
    for c in range(0, D_FF, _FFN_CW):
        def conv(c0):
            cl = slice(c0, c0 + _FFN_CW)
            return (cb_ref[:, cl] + prev2[:, cl] * cw_ref[0:1, cl] + prev1[:, cl] * cw_ref[1:2, cl]
                    + up[:, cl] * cw_ref[2:3, cl])
        act_s[:, c:c + _FFN_CW] = (_silu(conv(c)) * conv(D_FF + c)).astype(BF16)
    out = x + jnp.dot(act_s[...], wdn_ref[...], preferred_element_type=F32)
    o_ref[...] = _rms_unit(out) * nf_ref[...] if final else out


def _ffn(x2d, state, lw, nf, nb, seq, tm, final):
    wspecs = [_const_spec((1, D_MODEL)), _layer_spec(lw["l"], (D_MODEL, 2 * D_FF)),
              _const_spec((CONV_F, 2 * D_FF)), _const_spec((1, 2 * D_FF)),
              _layer_spec(lw["l"], (D_FF, D_MODEL)), _const_spec((1, D_MODEL))]
    wargs = (lw["norm_ffn_w"], lw["w_up"], lw["conv_f_w"], lw["conv_f_b"], lw["w_down"], nf)
    act = pltpu.VMEM((tm, D_FF), BF16)
    if state is None:
        nt = seq // tm
        row = pl.BlockSpec((tm, D_MODEL), lambda b, i: (b * nt + i, 0))
        return pl.pallas_call(
            functools.partial(_ffn_prompt_body, tm=tm, final=final), grid=(nb, nt),
            in_specs=[row] + wspecs,
            out_specs=[row, pl.BlockSpec((None, CONV_F - 1, 2 * D_FF), lambda b, i: (b, 0, 0))],
            out_shape=[jax.ShapeDtypeStruct((nb * seq, D_MODEL), F32),
                       jax.ShapeDtypeStruct((nb, CONV_F - 1, 2 * D_FF), F32)],
            scratch_shapes=[pltpu.VMEM((8 + tm, 2 * D_FF), F32), act],
            compiler_params=_params(2), name="ffn_prompt")(x2d, *wargs)
    sw = (CONV_F - 1) * 2 * D_FF
    return pl.pallas_call(
        functools.partial(_ffn_sample_body, final=final), grid=(nb // tm,),
        in_specs=[pl.BlockSpec((tm, D_MODEL), lambda i: (i, 0)),
                  pl.BlockSpec((tm, sw), lambda i: (i, 0))] + wspecs,
        out_specs=[pl.BlockSpec((tm, D_MODEL), lambda i: (i, 0)),
                   pl.BlockSpec((tm, sw), lambda i: (i, 0))],
        out_shape=[jax.ShapeDtypeStruct((nb, D_MODEL), F32), jax.ShapeDtypeStruct((nb, sw), F32)],
        scratch_shapes=[act], compiler_params=_params(1), name="ffn_sample")(x2d, state, *wargs)


def _ssd_sample_body(*refs, n_prev):
    (zs_ref, xbc_ref, dt_ref, cst_ref, cw_ref, cb_ref, dtb_ref, alog_ref, dx_ref, naw_ref,
     hin_ref) = refs[:11]
    (ya_ref, cso_ref, hout_ref, xa_s, xdt_t_s, dec_t_s, bm_t_s, cm_t_s, yt_s) = refs[11 + n_prev:]
    h = pl.program_id(0)

    @pl.when(h == 0)
    def _():
        acc = cb_ref[...]
        for tap in range(CONV_A - 1):
            acc = acc + cst_ref[:, tap * D_XBC:(tap + 1) * D_XBC] * cw_ref[tap:tap + 1, :]
        acc = acc + xbc_ref[...] * cw_ref[CONV_A - 1:CONV_A, :]
        cso_ref[:, :(CONV_A - 2) * D_XBC] = cst_ref[:, D_XBC:]
        cso_ref[:, (CONV_A - 2) * D_XBC:] = xbc_ref[...]
        xc = _silu(acc)
        xa = xc[:, :D_A]
        xa_s[...] = xa
        bm_t_s[...] = xc[:, D_A:D_A + G_A * N_A].T
        cm_t_s[...] = xc[:, D_A + G_A * N_A:].T
        dt = _softplus(dt_ref[...] + dtb_ref[...])
        dec_t_s[...] = jnp.exp(dt * -jnp.exp(alog_ref[...])).T
        dt_x = _dot01(dt, _tile3(_head_expand(), 0), "lhs")
        xdt_t_s[...] = (xa * dt_x).T

    g0 = pl.multiple_of((h // HPG) * N_A, N_A)
    p0 = pl.multiple_of(h * P_A, P_A)
    b_t = bm_t_s[pl.ds(g0, N_A), :]
    c_t = cm_t_s[pl.ds(g0, N_A), :]
    dec = dec_t_s[pl.ds(h, 1), :]

    def per_8p(k, carry):
        r = pl.ds(pl.multiple_of(p0 + k * 8, 8), 8)
        x_rows = xdt_t_s[r, :]
        ys = []
        for u in range(8):
            h_new = hin_ref[k * 8 + u] * dec + x_rows[u:u + 1, :] * b_t
            hout_ref[k * 8 + u] = h_new
            ys.append(jnp.sum(h_new * c_t, axis=0, keepdims=True))
        yt_s[r, :] = jnp.concatenate(ys, axis=0)
        return carry

    lax.fori_loop(0, P_A // 8, per_8p, 0)

    @pl.when(h == H_A - 1)
    def _():
        xa = xa_s[...]
        y = yt_s[...].T + dx_ref[...] * xa
        y = y * zs_ref[...]
        ya_ref[...] = (_rms_unit(y) * naw_ref[...]).astype(BF16)


def _ssd_sample(p, conv_st, h_ssd_t, lw, nb, l, prev):
    full = lambda w: _const_spec((nb, w))
    cw3 = (CONV_A - 1) * D_XBC
    st = pl.BlockSpec((None, None, P_A, N_A, nb), lambda h: (l, h, 0, 0, 0))
    in_specs = [full(D_A), full(D_XBC), full(DT_PAD), full(cw3),
                _const_spec((CONV_A, D_XBC)), _const_spec((1, D_XBC)), _const_spec((1, DT_PAD)),
                _const_spec((1, DT_PAD)), _const_spec((1, D_A)), _const_spec((1, D_A)), st]
    prev = () if prev is None else (prev,)
    in_specs += [pl.BlockSpec(memory_space=pl.ANY)] * len(prev)
    aliases = {len(in_specs) - 1: 2} if prev else {}
    keep = lambda w: pl.BlockSpec((nb, w), lambda h: (0, 0))
    scratch = [pltpu.VMEM((nb, D_A), F32), pltpu.VMEM((D_A, nb), F32), pltpu.VMEM((LANES, nb), F32),
               pltpu.VMEM((G_A * N_A, nb), F32), pltpu.VMEM((G_A * N_A, nb), F32),
               pltpu.VMEM((D_A, nb), F32)]
    ya, cs, hs = pl.pallas_call(
        functools.partial(_ssd_sample_body, n_prev=len(prev)), grid=(H_A,),
        in_specs=in_specs, out_specs=[keep(D_A), keep(cw3), st],
        out_shape=[jax.ShapeDtypeStruct((nb, D_A), BF16), jax.ShapeDtypeStruct((nb, cw3), F32),
                   jax.ShapeDtypeStruct((DEPTH, H_A, P_A, N_A, nb), F32)],
        scratch_shapes=scratch, input_output_aliases=aliases,
        compiler_params=_params(1), name="ssd_sample")(
            p["z"], p["xbc"], p["dt"], conv_st.reshape(nb, cw3),
            lw["conv_a_w"], lw["conv_a_b"], lw["dt_bias"], lw["a_log"], lw["d_x"], lw["norm_a_w"],
            h_ssd_t, *prev)
    return ya, cs.reshape(nb, CONV_A - 1, D_XBC), hs


_RET_BB = 8


def _ret_sample_body(*refs, nb, n_prev):
    q_ref, k_ref, v_ref, gs_ref, cos_ref, sina_ref, sinb_ref, sin_ref = refs[:8]
    yb_ref, sout_ref, q_t_s, k_t_s, yr_s = refs[8 + n_prev:]
    i = pl.program_id(0)

    @pl.when(i == 0)
    def _():
        cos_f, sin_a, sin_b = cos_ref[...], sina_ref[...], sinb_ref[...]
        q_t_s[...] = _rope(q_ref[...], cos_f, sin_a, sin_b).T.astype(BF16)
        k_t_s[...] = (_rope(k_ref[...], cos_f, sin_a, sin_b) * (DK_R ** -0.5)).T.astype(BF16)

    rows_i = lax.broadcasted_iota(jnp.int32, (nb, DV_R), 0)
    blk = pl.ds(pl.multiple_of(i * _RET_BB, _RET_BB), _RET_BB)
    v_blk = v_ref[blk, :]
    ys = [[] for _ in range(H_R)]
    for j in range(_RET_BB):
        onehot = jnp.where(rows_i == i * _RET_BB + j, 1.0, 0.0).astype(BF16)
        k_bc = jnp.dot(k_t_s[...], onehot, preferred_element_type=F32)
        q_bc = jnp.dot(q_t_s[...], onehot, preferred_element_type=F32)
        for h in range(H_R):
            kl = slice(h * DK_R, (h + 1) * DK_R)
            vl = slice(h * DV_R, (h + 1) * DV_R)
            s_new = math.exp(_LOG_GAMMA[h]) * sin_ref[j, h] + k_bc[kl, :] * v_blk[j:j + 1, vl]
            sout_ref[j, h] = s_new
            ys[h].append(jnp.sum(q_bc[kl, :] * s_new, axis=0, keepdims=True))
    for h in range(H_R):
        yr_s[blk, h * DV_R:(h + 1) * DV_R] = jnp.concatenate(ys[h], axis=0)

    @pl.when(i == nb // _RET_BB - 1)
    def _():
        for h in range(H_R):
            vl = slice(h * DV_R, (h + 1) * DV_R)
            yb_ref[:, vl] = (gs_ref[:, vl] * _rms_unit(yr_s[:, vl])).astype(BF16)


def _ret_sample(p, s_ret, rot, nb, l, prev):
    full = lambda w: _const_spec((nb, w))
    st = pl.BlockSpec((None, _RET_BB, H_R, DK_R, DV_R), lambda i: (l, i, 0, 0, 0))
    in_specs = [full(D_QK), full(D_QK), full(D_RV), full(D_RV),
                _const_spec((1, D_QK)), _const_spec((1, D_QK)), _const_spec((1, D_QK)), st]
    prev = () if prev is None else (prev,)
    in_specs += [pl.BlockSpec(memory_space=pl.ANY)] * len(prev)
    aliases = {len(in_specs) - 1: 1} if prev else {}
    return pl.pallas_call(
        functools.partial(_ret_sample_body, nb=nb, n_prev=len(prev)), grid=(nb // _RET_BB,),
        in_specs=in_specs, out_specs=[pl.BlockSpec((nb, D_RV), lambda i: (0, 0)), st],
        out_shape=[jax.ShapeDtypeStruct((nb, D_RV), BF16),
                   jax.ShapeDtypeStruct((DEPTH, nb, H_R, DK_R, DV_R), F32)],
        scratch_shapes=[pltpu.VMEM((D_QK, nb), BF16), pltpu.VMEM((D_QK, nb), BF16),
                        pltpu.VMEM((nb, D_RV), F32)],
        input_output_aliases=aliases, compiler_params=_params(1), name="ret_sample")(
            p["q"], p["k"], p["v"], p["g"], rot["cos"], rot["sin_a"], rot["sin_b"], s_ret, *prev)


def _rotary_tables(start, length):
    half = DK_R // 2
    pos = start + jnp.arange(length, dtype=F32)
    freqs = ROPE_BASE ** (-jnp.arange(half, dtype=F32) / half)
    ang = pos[:, None] * freqs[None, :]
    cos, sin, zero = jnp.cos(ang), jnp.sin(ang), jnp.zeros_like(ang)
    tile = lambda a, b: jnp.tile(jnp.concatenate([a, b], axis=-1), (1, H_R))
    lg = jnp.log1p(-jnp.exp2(-5.0 - jnp.arange(H_R, dtype=F32)))
    return {"cos": tile(cos, cos), "sin_a": tile(-sin, zero), "sin_b": tile(zero, sin),
            "lgx": jnp.repeat(lg, DK_R)[None, :]}


def _w_in_prep_body(w_ref, p_ref, g_ref):
    o_dt = D_A + D_XBC
    n_mid = D_PROJ - DT_PAD - o_dt
    p_ref[:, :o_dt] = w_ref[:, :o_dt].astype(BF16)
    p_ref[:, o_dt:o_dt + n_mid] = w_ref[:, o_dt + H_A:o_dt + H_A + n_mid].astype(BF16)
    lane = lax.broadcasted_iota(jnp.int32, (w_ref.shape[0], DT_PAD), 1)
    p_ref[:, o_dt + n_mid:] = jnp.where(lane < H_A, w_ref[:, o_dt:o_dt + DT_PAD], 0.0).astype(BF16)
    g_ref[...] = w_ref[:, o_dt + H_A + n_mid:].astype(BF16)


def _matmul_weights(w):
    w_in = w["w_in"]
    tr = 256
    blk = lambda width: pl.BlockSpec((None, tr, width), lambda l, i: (l, i, 0))
    w_proj, w_gates = pl.pallas_call(
        _w_in_prep_body, grid=(DEPTH, D_MODEL // tr), in_specs=[blk(w_in.shape[-1])],
        out_specs=[blk(D_PROJ), blk(3 * D_MODEL)],
        out_shape=[jax.ShapeDtypeStruct((DEPTH, D_MODEL, D_PROJ), BF16),
                   jax.ShapeDtypeStruct((DEPTH, D_MODEL, 3 * D_MODEL), BF16)],
        compiler_params=_params(2), name="w_in_prep")(w_in)
    out = {"w_proj": w_proj, "w_gates": w_gates}
    for name in ("w_glu", "w_br_a", "w_br_b", "w_br_c", "w_out", "w_up", "w_down"):
        out[name] = w[name].astype(BF16)
    return out


def _layer_weights(l, w, mats):
    pad_row = lambda a: jnp.pad(a, (0, DT_PAD - H_A))[None, :]
    ab_re, ab_im, bb_re, bb_im = _s5_discretise(
        w["s5_a_re"][l], w["s5_a_im"][l], w["s5_log_dt"][l], w["s5_b_re"][l], w["s5_b_im"][l])
    bblk = jnp.concatenate([_block_diag(bb_re), _block_diag(bb_im)], axis=-1)
    c_t = lambda c: _block_diag(jnp.transpose(c, (0, 2, 1)))
    return {
        "l": l, **mats,
        "norm_mix_w": w["norm_mix_w"][l][None, :],
        "conv_a_w": w["conv_a_w"][l], "conv_a_b": w["conv_a_b"][l][None, :],
        "dt_bias": pad_row(w["dt_bias"][l]), "a_log": pad_row(w["a_log"][l]),
        "d_x": jnp.repeat(w["d_a"][l], P_A)[None, :], "norm_a_w": w["norm_a_w"][l][None, :],
        "s5_ab_re": ab_re.reshape(1, D_S5), "s5_ab_im": ab_im.reshape(1, D_S5),
        "s5_bblk": bblk.astype(BF16),
        "s5_cre": c_t(w["s5_c_re"][l]).astype(BF16), "s5_cim": c_t(w["s5_c_im"][l]).astype(BF16),
        "s5_d": w["s5_d"][l][None, :], "b_glu": w["b_glu"][l][None, :],
        "norm_ffn_w": w["norm_ffn_w"][l][None, :],
        "conv_f_w": w["conv_f_w"][l], "conv_f_b": w["conv_f_b"][l][None, :],
    }


def _prompt_trunk(x, lws, nf, tm_proj, tsteps, tt, tm_merge, tm_ffn):
    nb, seq, _ = x.shape
    rot = _rotary_tables(0.0, seq)
    x2d = x.reshape(nb * seq, D_MODEL)
    zeros = jnp.zeros((nb, D_S5), F32)
    states, big = [], None
    for l, lw in enumerate(lws):
        p = _inproj(x2d, lw["norm_mix_w"], lw, rot, nb, seq, tm_proj, conv=True)
        ya, yb, *big = _mix_prompt(p, rot, lw, nb, seq, tsteps, l, big)
        yc, h_re, h_im = _s5(p["u"], zeros, zeros, lw, nb, seq, tt)
        x2d = _merge(x2d, ya, yb, yc, lw, nb, seq, tm_merge)
        x2d, fs = _ffn(x2d, None, lw, nf, nb, seq, tm_ffn, final=(l == len(lws) - 1))
        states.append((p["conv_state"], h_re.reshape(nb, G_C, P_C), h_im.reshape(nb, G_C, P_C), fs))
    cs, h_re, h_im, fs = [jnp.stack(s) for s in zip(*states)]
    return x2d.reshape(nb, seq, D_MODEL), [cs, big[0], big[1], h_re, h_im, fs]


def _sample_trunk(x, st, lws, nf):
    nb = x.shape[0]
    rot = _rotary_tables(float(PAST_LEN), 1)
    x2d = x.reshape(nb, D_MODEL)
    st_conv, st_ssd, st_ret, st_re, st_im, st_ffn = st
    st_ssd_t = jnp.transpose(st_ssd, (0, 2, 3, 4, 1))
    s5_t = lambda s: jnp.transpose(s, (0, 2, 3, 1)).reshape(DEPTH, D_S5, nb)
    st_re_t, st_im_t = s5_t(st_re), s5_t(st_im)
    states, hs, ss = [], None, None
    for l, lw in enumerate(lws):
        p = _inproj(x2d, lw["norm_mix_w"], lw, rot, 1, nb, nb, conv=False)
        ya, cs, hs = _ssd_sample(p, st_conv[l], st_ssd_t, lw, nb, l, hs)
        yb, ss = _ret_sample(p, st_ret, rot, nb, l, ss)
        yc, h_re, h_im = _s5(p["u"], st_re_t[l], st_im_t[l], lw, nb, 1, 1, state_t=True)
        x2d = _merge(x2d, ya, yb, yc, lw, 1, nb, nb)
        x2d, fs = _ffn(x2d, st_ffn[l].reshape(nb, -1), lw, nf, nb, 1, nb,
                       final=(l == len(lws) - 1))
        states.append((cs, h_re, h_im, fs.reshape(nb, CONV_F - 1, 2 * D_FF)))
    cs, h_re, h_im, fs = [jnp.stack(s) for s in zip(*states)]
    hs = jnp.transpose(hs, (0, 4, 1, 2, 3))
    s5_back = lambda s: jnp.transpose(s.reshape(DEPTH, G_C, P_C, nb), (0, 3, 1, 2))
    return x2d.reshape(nb, 1, D_MODEL), [cs, hs, ss, s5_back(h_re), s5_back(h_im), fs]


def kernel(x_prompt, x_sample, state_ssd_conv, state_ssd, state_ret, state_s5_re, state_s5_im,
           state_ffn_conv, norm_mix_w, w_in, conv_a_w, conv_a_b, dt_bias, a_log, d_a, norm_a_w,
           s5_a_re, s5_a_im, s5_log_dt, s5_b_re, s5_b_im, s5_c_re, s5_c_im, s5_d, w_glu, b_glu,
           w_br_a, w_br_b, w_br_c, w_out, norm_ffn_w, w_up, conv_f_w, conv_f_b, w_down, norm_f_w):
    w = dict(norm_mix_w=norm_mix_w, w_in=w_in, conv_a_w=conv_a_w, conv_a_b=conv_a_b,
             dt_bias=dt_bias, a_log=a_log, d_a=d_a, norm_a_w=norm_a_w, s5_a_re=s5_a_re,
             s5_a_im=s5_a_im, s5_log_dt=s5_log_dt, s5_b_re=s5_b_re, s5_b_im=s5_b_im,
             s5_c_re=s5_c_re, s5_c_im=s5_c_im, s5_d=s5_d, w_glu=w_glu, b_glu=b_glu,
             w_br_a=w_br_a, w_br_b=w_br_b, w_br_c=w_br_c, w_out=w_out, norm_ffn_w=norm_ffn_w,
             w_up=w_up, conv_f_w=conv_f_w, conv_f_b=conv_f_b, w_down=w_down)
    mats = _matmul_weights(w)
    lws = [_layer_weights(l, w, mats) for l in range(DEPTH)]
    nf = norm_f_w[None, :]
    seq = x_prompt.shape[1]
    y_p, p_st = _prompt_trunk(x_prompt, lws, nf, tm_proj=min(512, seq), tsteps=min(512, seq),
                              tt=min(64, seq), tm_merge=min(512, seq), tm_ffn=min(512, seq))
    y_s, s_st = _sample_trunk(
        x_sample, (state_ssd_conv, state_ssd, state_ret, state_s5_re, state_s5_im, state_ffn_conv),
        lws, nf)
    return (y_p, y_s, *p_st, *s_st)
```

```python
import functools
import math

import jax
import jax.numpy as jnp
import numpy as np
from jax import lax
from jax.experimental import pallas as pl
from jax.experimental.pallas import tpu as pltpu

F32 = jnp.float32
BF16 = jnp.bfloat16

D_MODEL = 1024
DEPTH = 2
PAST_LEN = 16384
H_A, P_A, N_A, G_A, CONV_A = 16, 64, 64, 2, 4
D_A = H_A * P_A
D_XBC = D_A + 2 * G_A * N_A
H_R, DK_R, DV_R = 8, 64, 128
D_QK = H_R * DK_R
D_RV = H_R * DV_R
ROPE_BASE = 10000.0
GS_C, G_C, P_C = 16, 64, 64
D_C = G_C * GS_C
D_S5 = G_C * P_C
D_FF = 2816
CONV_F = 3
CHUNK = 128
EPS = 1e-6
LANES = 128
DT_PAD = LANES
HPG = H_A // G_A
VMEM_LIMIT = 56 * 1024 * 1024

_LOG_GAMMA = [math.log1p(-(2.0 ** (-5.0 - h))) for h in range(H_R)]

_SEGS = (("z", D_A), ("xbc", D_XBC), ("q", D_QK), ("k", D_QK),
         ("v", D_RV), ("g", D_RV), ("u", D_C), ("dt", DT_PAD))
_SEG_OFF = np.cumsum([0] + [w for _, w in _SEGS]).tolist()
D_PROJ = _SEG_OFF[-1]


def _rms_unit(x):
    return x * lax.rsqrt(jnp.mean(x * x, axis=-1, keepdims=True) + EPS)


def _silu(x):
    return x * jax.nn.sigmoid(x)


def _softplus(x):
    return jnp.maximum(x, 0.0) + jnp.log1p(jnp.exp(-jnp.abs(x)))


def _bdot(a, b):
    return jnp.dot(a.astype(BF16), b.astype(BF16), preferred_element_type=F32)


def _split3(x):
    hi = x.astype(BF16)
    r1 = x - hi.astype(F32)
    mid = r1.astype(BF16)
    return hi, mid, (r1 - mid.astype(F32)).astype(BF16)


def _tile3(m01, axis):
    return jnp.concatenate([m01] * 3, axis=axis)


def _dot01(a, b, f32_side):
    if f32_side == "lhs":
        return jnp.dot(jnp.concatenate(_split3(a), axis=1), b, preferred_element_type=F32)
    return jnp.dot(a, jnp.concatenate(_split3(b), axis=0), preferred_element_type=F32)


def _const_spec(shape):
    nd = len(shape)
    return pl.BlockSpec(shape, lambda *_: (0,) * nd, pipeline_mode=pl.Buffered(1))


def _layer_spec(l, shape):
    nd = len(shape)
    return pl.BlockSpec((None,) + tuple(shape), lambda *_: (l,) + (0,) * nd,
                        pipeline_mode=pl.Buffered(1))


def _params(n_grid):
    return pltpu.CompilerParams(dimension_semantics=("arbitrary",) * n_grid,
                                vmem_limit_bytes=VMEM_LIMIT)


_PROJ_SUB = 128


def _inproj_body(x_ref, nw_ref, cw_ref, cb_ref, cos_ref, sina_ref, sinb_ref, w_ref, *refs,
                 tm, conv):
    n = len(_SEGS)
    if conv:
        cs_ref, xp_s = refs[n:]

        @pl.when(pl.program_id(1) == 0)
        def _():
            xp_s[0:8, :] = jnp.zeros((8, D_XBC), F32)

    for r in range(0, tm, _PROJ_SUB):
        rs = slice(r, r + _PROJ_SUB)
        h = (_rms_unit(x_ref[rs, :]) * nw_ref[...]).astype(BF16)
        for (name, width), off, o_ref in zip(_SEGS, _SEG_OFF, refs[:n]):
            y = jnp.dot(h, w_ref[:, off:off + width], preferred_element_type=F32)
            if name in ("z", "g"):
                y = _silu(y)
            elif name in ("q", "k") and conv:
                y = _rope(y, cos_ref[rs, :], sina_ref[rs, :], sinb_ref[rs, :])
                if name == "k":
                    y = y * (DK_R ** -0.5)
            elif name == "xbc" and conv:
                xp_s[8 + r:8 + r + _PROJ_SUB, :] = y
                y = cb_ref[...]
                for tap in range(CONV_A):
                    y = y + xp_s[5 + tap + r:5 + tap + r + _PROJ_SUB, :] * cw_ref[tap:tap + 1, :]
                y = _silu(y)
            o_ref[rs, :] = y
    if conv:
        tail = xp_s[tm + 5:tm + 8, :]
        cs_ref[...] = tail
        xp_s[5:8, :] = tail


def _inproj(x2d, nw, lw, rot, nb, seq, tm, conv):
    nt = seq // tm
    tab = (pl.BlockSpec((tm, D_QK), lambda b, i: (i, 0)) if conv
           else _const_spec(rot["cos"].shape))
    in_specs = [pl.BlockSpec((tm, D_MODEL), lambda b, i: (b * nt + i, 0)),
                _const_spec((1, D_MODEL)), _const_spec((CONV_A, D_XBC)), _const_spec((1, D_XBC)),
                tab, tab, tab, _layer_spec(lw["l"], (D_MODEL, D_PROJ))]
    out_specs, out_shape, scratch = [], [], []
    for _, width in _SEGS:
        out_specs.append(pl.BlockSpec((tm, width), lambda b, i: (b * nt + i, 0)))
        out_shape.append(jax.ShapeDtypeStruct((nb * seq, width), F32))
    if conv:
        out_specs.append(pl.BlockSpec((None, CONV_A - 1, D_XBC), lambda b, i: (b, 0, 0)))
        out_shape.append(jax.ShapeDtypeStruct((nb, CONV_A - 1, D_XBC), F32))
        scratch.append(pltpu.VMEM((8 + tm, D_XBC), F32))
    outs = pl.pallas_call(
        functools.partial(_inproj_body, tm=tm, conv=conv), grid=(nb, nt), in_specs=in_specs,
        out_specs=out_specs, out_shape=out_shape, scratch_shapes=scratch,
        compiler_params=_params(2), name="inproj")(
            x2d, nw, lw["conv_a_w"], lw["conv_a_b"], rot["cos"], rot["sin_a"], rot["sin_b"],
            lw["w_proj"])
    res = {name: o for (name, _), o in zip(_SEGS, outs)}
    if conv:
        res["conv_state"] = outs[-1]
    return res


def _rope(x, cos_f, sin_a, sin_b):
    half = DK_R // 2
    return (x * cos_f + pltpu.roll(x, D_QK - half, 1) * sin_a
            + pltpu.roll(x, half, 1) * sin_b)


def _head_expand():
    lo = lax.broadcasted_iota(jnp.int32, (LANES, D_A), 0) * P_A
    c = lax.broadcasted_iota(jnp.int32, (LANES, D_A), 1)
    return jnp.where(c >= lo, jnp.where(c < lo + P_A, 1.0, 0.0), 0.0).astype(BF16)


def _mix_body(*refs, tsteps, nt, n_prev):
    (zs_ref, xc_ref, dt_ref, q_ref, k_ref, v_ref, gs_ref, lgx_ref,
     dtb_ref, alog_ref, dx_ref, naw_ref) = refs[:12]
    (ya_ref, yb_ref, hs_ref, ss_ref,
     ht_s, s_s, intra_s, qdec_s, kdect_s) = refs[12 + n_prev:]
    i = pl.program_id(1)
    first = jnp.logical_and(pl.program_id(0) == 0, i == 0)
    rows_i = lax.broadcasted_iota(jnp.int32, (CHUNK, CHUNK), 0)
    cols_i = lax.broadcasted_iota(jnp.int32, (CHUNK, CHUNK), 1)
    causal = rows_i >= cols_i
    lane_lo = cols_i < LANES // 2

    @pl.when(first)
    def _():
        rel = (rows_i - cols_i).astype(F32)
        for h in range(H_R):
            intra_s[h] = jnp.where(causal, jnp.exp(jnp.maximum(rel, 0.0) * _LOG_GAMMA[h]), 0.0)
        ri = lax.broadcasted_iota(jnp.int32, (CHUNK, D_QK), 0).astype(F32)
        lgx = lgx_ref[...]
        qdec_s[...] = jnp.exp((ri + 1.0) * lgx)
        kdect_s[...] = jnp.exp((CHUNK - 1.0 - ri) * lgx).T

    @pl.when(i == 0)
    def _():
        ht_s[...] = jnp.zeros_like(ht_s)
        s_s[...] = jnp.zeros_like(s_s)

    tril = _tile3(jnp.where(causal, 1.0, 0.0).astype(BF16), 1)
    expand = _tile3(_head_expand(), 0)
    a_neg = -jnp.exp(alog_ref[...])

    for c in range(tsteps // CHUNK):
        r0 = c * CHUNK
        rows = slice(r0, r0 + CHUNK)
        xa = xc_ref[rows, :D_A]
        bm = xc_ref[rows, D_A:D_A + G_A * N_A]
        cm = xc_ref[rows, D_A + G_A * N_A:]
        dt = _softplus(dt_ref[rows, :] + dtb_ref[...])
        cum = _dot01(tril, dt * a_neg, "rhs")
        cum_t = cum.T
        dt_t = dt.T
        cum_last = cum[CHUNK - 1:CHUNK, :]
        scale = jnp.concatenate(
            [dt * jnp.exp(cum_last - cum), jnp.exp(cum),
             jnp.broadcast_to(jnp.exp(cum_last), (8, LANES))], axis=0)
        scale_x = _dot01(scale, expand, "lhs")
        w_x = scale_x[:CHUNK]
        ecum_x = scale_x[CHUNK:2 * CHUNK]
        dec_x = scale_x[2 * CHUNK:2 * CHUNK + 1]
        bm_t = bm.T
        ys = []
        for g in range(G_A):
            gl = slice(g * N_A, (g + 1) * N_A)
            hl = slice(g * HPG * P_A, (g + 1) * HPG * P_A)
            cg = cm[:, gl]
            scores = lax.dot_general(cg.astype(BF16), bm[:, gl].astype(BF16),
                                     (((1,), (1,)), ((), ())), preferred_element_type=F32)
            scores = jnp.where(causal, scores, 0.0)
            ht_g = ht_s[:, hl]
            y_inter = _bdot(cg, ht_g) * ecum_x[:, hl]
            for pair in range(HPG // 2):
                pl_ = slice(g * HPG * P_A + pair * LANES, g * HPG * P_A + (pair + 1) * LANES)
                x_pair = xa[:, pl_]
                sps = []
                for h in (g * HPG + 2 * pair, g * HPG + 2 * pair + 1):
                    seg = cum[:, h:h + 1] - cum_t[h:h + 1, :]
                    sps.append(scores * jnp.exp(jnp.minimum(seg, 0.0)) * dt_t[h:h + 1, :])
                rhs = jnp.concatenate([jnp.where(lane_lo, x_pair, 0.0),
                                       jnp.where(lane_lo, 0.0, x_pair)], axis=0)
                ys.append(_bdot(jnp.concatenate(sps, axis=1), rhs)
                          + y_inter[:, pair * LANES:(pair + 1) * LANES])
            ht_s[:, hl] = dec_x[:, hl] * ht_g + _bdot(bm_t[gl, :], xa[:, hl] * w_x[:, hl])
        y = jnp.concatenate(ys, axis=1) + dx_ref[...] * xa
        y = y * zs_ref[rows, :]
        ya_ref[rows, :] = (_rms_unit(y) * naw_ref[...]).astype(BF16)

        qr = q_ref[rows, :]
        k_t = k_ref[rows, :].T
        kd_t = k_t * kdect_s[...]
        qd = qr * qdec_s[...]
        for pair in range(H_R // 2):
            pl_ = slice(pair * LANES, (pair + 1) * LANES)
            s_pair = s_s[pl_, :]
            for h, keep in ((2 * pair, lane_lo), (2 * pair + 1, jnp.logical_not(lane_lo))):
                kl = slice(h * DK_R, (h + 1) * DK_R)
                vl = slice(h * DV_R, (h + 1) * DV_R)
                v_h = v_ref[rows, vl]
                sc = _bdot(jnp.where(keep, qr[:, pl_], 0.0), k_t[pl_, :]) * intra_s[h]
                y_h = _bdot(jnp.concatenate([sc, jnp.where(keep, qd[:, pl_], 0.0)], axis=1),
                            jnp.concatenate([v_h, s_pair], axis=0))
                s_s[kl, :] = (math.exp(CHUNK * _LOG_GAMMA[h]) * s_s[kl, :]
                              + _bdot(kd_t[kl, :], v_h))
                yb_ref[rows, vl] = (gs_ref[rows, vl] * _rms_unit(y_h)).astype(BF16)

    @pl.when(i == nt - 1)
    def _():
        for h in range(H_A):
            hs_ref[h] = ht_s[:, h * P_A:(h + 1) * P_A].T
        for h in range(H_R):
            ss_ref[h] = s_s[h * DK_R:(h + 1) * DK_R, :]


def _layer_block(l, tail):
    zeros = (0,) * len(tail)
    return pl.BlockSpec((None, None) + tail, lambda b, *_: (l, b) + zeros)


def _mix_prompt(p, rot, lw, nb, seq, tsteps, l, prev):
    nt = seq // tsteps
    row = lambda w: pl.BlockSpec((tsteps, w), lambda b, i: (b * nt + i, 0))
    in_specs = [row(D_A), row(D_XBC), row(DT_PAD), row(D_QK), row(D_QK), row(D_RV), row(D_RV),
                _const_spec((1, D_QK)), _const_spec((1, DT_PAD)),
                _const_spec((1, DT_PAD)), _const_spec((1, D_A)), _const_spec((1, D_A))]
    prev = () if prev is None else tuple(prev)
    in_specs += [pl.BlockSpec(memory_space=pl.ANY)] * len(prev)
    aliases = {len(in_specs) - len(prev) + k: 2 + k for k in range(len(prev))}
    out_specs = [row(D_A), row(D_RV),
                 _layer_block(l, (H_A, P_A, N_A)), _layer_block(l, (H_R, DK_R, DV_R))]
    out_shape = [jax.ShapeDtypeStruct((nb * seq, D_A), BF16),
                 jax.ShapeDtypeStruct((nb * seq, D_RV), BF16),
                 jax.ShapeDtypeStruct((DEPTH, nb, H_A, P_A, N_A), F32),
                 jax.ShapeDtypeStruct((DEPTH, nb, H_R, DK_R, DV_R), F32)]
    scratch = [pltpu.VMEM((N_A, D_A), F32),
               pltpu.VMEM((D_QK, DV_R), F32), pltpu.VMEM((H_R, CHUNK, CHUNK), F32),
               pltpu.VMEM((CHUNK, D_QK), F32), pltpu.VMEM((D_QK, CHUNK), F32)]
    return pl.pallas_call(
        functools.partial(_mix_body, tsteps=tsteps, nt=nt, n_prev=len(prev)), grid=(nb, nt),
        in_specs=in_specs, out_specs=out_specs, out_shape=out_shape, scratch_shapes=scratch,
        input_output_aliases=aliases, compiler_params=_params(2), name="mix_prompt")(
            p["z"], p["xbc"], p["dt"], p["q"], p["k"], p["v"], p["g"],
            rot["lgx"], lw["dt_bias"], lw["a_log"], lw["d_x"], lw["norm_a_w"], *prev)


def _s5_disc_body(are_ref, aim_ref, ldt_ref, bre_ref, bim_ref,
                  abre_ref, abim_ref, bbre_ref, bbim_ref):
    ar, ai = are_ref[...], aim_ref[...]
    dt = jnp.exp(ldt_ref[...])
    mag = jnp.exp(ar * dt)
    ab_re = mag * jnp.cos(ai * dt)
    ab_im = mag * jnp.sin(ai * dt)
    den = ar * ar + ai * ai
    num_re = ab_re - 1.0
    coef_re = (num_re * ar + ab_im * ai) / den
    coef_im = (ab_im * ar - num_re * ai) / den
    abre_ref[...] = ab_re
    abim_ref[...] = ab_im
    for c in range(GS_C):
        cl = slice(c * P_C, (c + 1) * P_C)
        br, bi = bre_ref[:, cl], bim_ref[:, cl]
        bbre_ref[:, cl] = coef_re * br - coef_im * bi
        bbim_ref[:, cl] = coef_re * bi + coef_im * br


def _s5_discretise(a_re, a_im, log_dt, b_re, b_im):
    gp = jax.ShapeDtypeStruct((G_C, P_C), F32)
    gcp = jax.ShapeDtypeStruct((G_C, GS_C * P_C), F32)
    b_t = lambda b: jnp.transpose(b, (0, 2, 1)).reshape(G_C, GS_C * P_C)
    ab_re, ab_im, bb_re, bb_im = pl.pallas_call(
        _s5_disc_body, out_shape=[gp, gp, gcp, gcp], name="s5_disc")(
            a_re, a_im, log_dt.reshape(G_C, 1), b_t(b_re), b_t(b_im))
    return ab_re, ab_im, bb_re.reshape(G_C, GS_C, P_C), bb_im.reshape(G_C, GS_C, P_C)


_S5_GB = LANES // GS_C
_S5_NBLK = G_C // _S5_GB
_S5_SB = _S5_GB * P_C


def _block_diag(m):
    g, r, c = m.shape
    m = m.reshape(_S5_NBLK, _S5_GB, r, c)
    eye = jnp.eye(_S5_GB, dtype=m.dtype)
    return jnp.einsum("jgrc,gk->jgrkc", m, eye).reshape(_S5_NBLK, _S5_GB * r, _S5_GB * c)


def _s5_body(u_ref, h0re_ref, h0im_ref, are_ref, aim_ref, bblk_ref, cre_ref, cim_ref,
             d_ref, wglu_ref, bglu_ref, yc_ref, hre_ref, him_ref,
             xre_s, xim_s, sre_s, sim_s, *, nb, tt, state_t):
    i = pl.program_id(0)
    rows = nb * tt

    @pl.when(i == 0)
    def _():
        sre_s[...] = h0re_ref[...].T if state_t else h0re_ref[...]
        sim_s[...] = h0im_ref[...].T if state_t else h0im_ref[...]

    u = u_ref[...]
    if tt > 1:
        u = jnp.swapaxes(u, 0, 1).reshape(rows, D_C)
    ub = u.astype(BF16)

    ys = []
    for j in range(_S5_NBLK):
        sl = slice(j * _S5_SB, (j + 1) * _S5_SB)
        bu = jnp.dot(ub[:, j * LANES:(j + 1) * LANES], bblk_ref[j], preferred_element_type=F32)
        xre_s[:, sl] = bu[:, :_S5_SB]
        xim_s[:, sl] = bu[:, _S5_SB:]
        ar = jnp.broadcast_to(are_ref[:, sl], (nb, _S5_SB))
        ai = jnp.broadcast_to(aim_ref[:, sl], (nb, _S5_SB))
        xr, xi = sre_s[:, sl], sim_s[:, sl]
        for t in range(tt):
            r = slice(t * nb, (t + 1) * nb)
            xr, xi = (ar * xr - ai * xi + xre_s[r, sl], ar * xi + ai * xr + xim_s[r, sl])
            xre_s[r, sl] = xr
            xim_s[r, sl] = xi
        sre_s[:, sl] = xr
        sim_s[:, sl] = xi
        ys.append(_bdot(xre_s[:, sl], cre_ref[j]) - _bdot(xim_s[:, sl], cim_ref[j]))
    yc = jnp.concatenate(ys, axis=1)
    parts = []
    half = rows // 2 if rows >= 256 else rows
    for r in range(0, rows, half):
        yh = jax.nn.gelu(yc[r:r + half] + d_ref[...] * u[r:r + half])
        parts.append(yh * jax.nn.sigmoid(_bdot(yh, wglu_ref[...]) + bglu_ref[...]))
    y = jnp.concatenate(parts, axis=0)
    if tt > 1:
        y = jnp.swapaxes(y.reshape(tt, nb, D_C), 0, 1)
    yc_ref[...] = y.astype(BF16)
    hre_ref[...] = sre_s[...].T if state_t else sre_s[...]
    him_ref[...] = sim_s[...].T if state_t else sim_s[...]


def _s5(u, h0_re, h0_im, lw, nb, seq, tt, state_t=False):
    rows = tt * nb
    sshape = (D_S5, nb) if state_t else (nb, D_S5)
    if tt > 1:
        assert nb % 8 == 0 and tt % 8 == 0
        u = u.reshape(nb, seq, D_C)
        io_spec = pl.BlockSpec((nb, tt, D_C), lambda i: (0, i, 0))
    else:
        io_spec = pl.BlockSpec((rows, D_C), lambda i: (i, 0))
    in_specs = [io_spec,
                _const_spec(sshape), _const_spec(sshape),
                _const_spec((1, D_S5)), _const_spec((1, D_S5)),
                _const_spec((_S5_NBLK, LANES, 2 * _S5_SB)),
                _const_spec((_S5_NBLK, _S5_SB, LANES)), _const_spec((_S5_NBLK, _S5_SB, LANES)),
                _const_spec((1, D_C)), _layer_spec(lw["l"], (D_C, D_C)), _const_spec((1, D_C))]
    st = pl.BlockSpec(sshape, lambda i: (0, 0))
    yc, h_re, h_im = pl.pallas_call(
        functools.partial(_s5_body, nb=nb, tt=tt, state_t=state_t), grid=(seq // tt,),
        in_specs=in_specs, out_specs=[io_spec, st, st],
        out_shape=[jax.ShapeDtypeStruct(u.shape, BF16),
                   jax.ShapeDtypeStruct(sshape, F32), jax.ShapeDtypeStruct(sshape, F32)],
        scratch_shapes=[pltpu.VMEM((rows, D_S5), F32), pltpu.VMEM((rows, D_S5), F32),
                        pltpu.VMEM((nb, D_S5), F32), pltpu.VMEM((nb, D_S5), F32)],
        compiler_params=_params(1), name="s5")(
            u, h0_re, h0_im, lw["s5_ab_re"], lw["s5_ab_im"], lw["s5_bblk"],
            lw["s5_cre"], lw["s5_cim"], lw["s5_d"], lw["w_glu"], lw["b_glu"])
    return yc.reshape(nb * seq, D_C), h_re, h_im


def _merge_body(x_ref, nw_ref, ya_ref, yb_ref, yc_ref, wg_ref, wa_ref, wb_ref, wc_ref, wo_ref,
                o_ref):
    x = x_ref[...]
    h = (_rms_unit(x) * nw_ref[...]).astype(BF16)
    merged = None
    for k, (y_ref, w_ref) in enumerate(((ya_ref, wa_ref), (yb_ref, wb_ref), (yc_ref, wc_ref))):
        gate = jax.nn.sigmoid(jnp.dot(h, wg_ref[:, k * D_MODEL:(k + 1) * D_MODEL],
                                      preferred_element_type=F32))
        term = gate * _bdot(y_ref[...], w_ref[...])
        merged = term if merged is None else merged + term
    o_ref[...] = x + _bdot(merged, wo_ref[...])


def _merge(x2d, ya, yb, yc, lw, nb, seq, tm):
    nt = seq // tm
    row = lambda w: pl.BlockSpec((tm, w), lambda b, i: (b * nt + i, 0))
    wspec = _layer_spec(lw["l"], (D_MODEL, D_MODEL))
    return pl.pallas_call(
        _merge_body, grid=(nb, nt),
        in_specs=[row(D_MODEL), _const_spec((1, D_MODEL)), row(D_A), row(D_RV), row(D_C),
                  _layer_spec(lw["l"], (D_MODEL, 3 * D_MODEL)), wspec, wspec, wspec, wspec],
        out_specs=row(D_MODEL), out_shape=jax.ShapeDtypeStruct((nb * seq, D_MODEL), F32),
        compiler_params=_params(2), name="merge")(
            x2d, lw["norm_mix_w"], ya, yb, yc, lw["w_gates"],
            lw["w_br_a"], lw["w_br_b"], lw["w_br_c"], lw["w_out"])


_FFN_CW = 256


_FFN_SUB = 512


def _ffn_prompt_body(x_ref, nw_ref, wup_ref, cw_ref, cb_ref, wdn_ref, nf_ref,
                     o_ref, cs_ref, up_s, act_s, *, tm, final):
    @pl.when(pl.program_id(1) == 0)
    def _():
        up_s[0:8, :] = jnp.zeros((8, 2 * D_FF), F32)

    sub = min(tm, _FFN_SUB)
    for r in range(0, tm, sub):
        rs = slice(r, r + sub)
        x = x_ref[rs, :]
        hf = (_rms_unit(x) * nw_ref[...]).astype(BF16)
        up_s[8 + r:8 + r + sub, :] = jnp.dot(hf, wup_ref[...], preferred_element_type=F32)
        for c in range(0, D_FF, _FFN_CW):
            def conv(c0):
                cl = slice(c0, c0 + _FFN_CW)
                y = cb_ref[:, cl]
                for tap in range(CONV_F):
                    y = y + up_s[6 + tap + r:6 + tap + r + sub, cl] * cw_ref[tap:tap + 1, cl]
                return y
            act_s[rs, c:c + _FFN_CW] = (_silu(conv(c)) * conv(D_FF + c)).astype(BF16)
        out = x + jnp.dot(act_s[rs, :], wdn_ref[...], preferred_element_type=F32)
        o_ref[rs, :] = _rms_unit(out) * nf_ref[...] if final else out
    tail = up_s[tm + 6:tm + 8, :]
    cs_ref[...] = tail
    up_s[6:8, :] = tail


def _ffn_sample_body(x_ref, st_ref, nw_ref, wup_ref, cw_ref, cb_ref, wdn_ref, nf_ref,
                     o_ref, cs_ref, act_s, *, final):
    x = x_ref[...]
    hf = (_rms_unit(x) * nw_ref[...]).astype(BF16)
    up = jnp.dot(hf, wup_ref[...], preferred_element_type=F32)
    prev2, prev1 = st_ref[:, :2 * D_FF], st_ref[:, 2 * D_FF:]
    cs_ref[:, :2 * D_FF] = prev1
    cs_ref[:, 2 * D_FF:] = up
    for c in range(0, D_FF, _FFN_CW):
        def conv(c0):
            cl = slice(c0, c0 + _FFN_CW)
            return (cb_ref[:, cl] + prev2[:, cl] * cw_ref[0:1, cl] + prev1[:, cl] * cw_ref[1:2, cl]
                    + up[:, cl] * cw_ref[2:3, cl])
        act_s[:, c:c + _FFN_CW] = (_silu(conv(c)) * conv(D_FF + c)).astype(BF16)
    out = x + jnp.dot(act_s[...], wdn_ref[...], preferred_element_type=F32)
    o_ref[...] = _rms_unit(out) * nf_ref[...] if final else out


def _ffn(x2d, state, lw, nf, nb, seq, tm, final):
    wspecs = [_const_spec((1, D_MODEL)), _layer_spec(lw["l"], (D_MODEL, 2 * D_FF)),
              _const_spec((CONV_F, 2 * D_FF)), _const_spec((1, 2 * D_FF)),
              _layer_spec(lw["l"], (D_FF, D_MODEL)), _const_spec((1, D_MODEL))]
    wargs = (lw["norm_ffn_w"], lw["w_up"], lw["conv_f_w"], lw["conv_f_b"], lw["w_down"], nf)
    act = pltpu.VMEM((tm, D_FF), BF16)
    if state is None:
        nt = seq // tm
        row = pl.BlockSpec((tm, D_MODEL), lambda b, i: (b * nt + i, 0))
        return pl.pallas_call(
            functools.partial(_ffn_prompt_body, tm=tm, final=final), grid=(nb, nt),
            in_specs=[row] + wspecs,
            out_specs=[row, pl.BlockSpec((None, CONV_F - 1, 2 * D_FF), lambda b, i: (b, 0, 0))],
            out_shape=[jax.ShapeDtypeStruct((nb * seq, D_MODEL), F32),
                       jax.ShapeDtypeStruct((nb, CONV_F - 1, 2 * D_FF), F32)],
            scratch_shapes=[pltpu.VMEM((8 + tm, 2 * D_FF), F32), act],
            compiler_params=_params(2), name="ffn_prompt")(x2d, *wargs)
    sw = (CONV_F - 1) * 2 * D_FF
    return pl.pallas_call(
        functools.partial(_ffn_sample_body, final=final), grid=(nb // tm,),
        in_specs=[pl.BlockSpec((tm, D_MODEL), lambda i: (i, 0)),
                  pl.BlockSpec((tm, sw), lambda i: (i, 0))] + wspecs,
        out_specs=[pl.BlockSpec((tm, D_MODEL), lambda i: (i, 0)),
                   pl.BlockSpec((tm, sw), lambda i: (i, 0))],
        out_shape=[jax.ShapeDtypeStruct((nb, D_MODEL), F32), jax.ShapeDtypeStruct((nb, sw), F32)],
        scratch_shapes=[act], compiler_params=_params(1), name="ffn_sample")(x2d, state, *wargs)


def _ssd_sample_body(*refs, n_prev):
    (zs_ref, xbc_ref, dt_ref, cst_ref, cw_ref, cb_ref, dtb_ref, alog_ref, dx_ref, naw_ref,
     hin_ref) = refs[:11]
    (ya_ref, cso_ref, hout_ref, xa_s, xdt_t_s, dec_t_s, bm_t_s, cm_t_s, yt_s) = refs[11 + n_prev:]
    i = pl.program_id(0)

    @pl.when(i == 0)
    def _():
        acc = cb_ref[...]
        for tap in range(CONV_A - 1):
            acc = acc + cst_ref[tap] * cw_ref[tap:tap + 1, :]
        acc = acc + xbc_ref[...] * cw_ref[CONV_A - 1:CONV_A, :]
        for tap in range(CONV_A - 2):
            cso_ref[tap] = cst_ref[tap + 1]
        cso_ref[CONV_A - 2] = xbc_ref[...]
        xc = _silu(acc)
        xa = xc[:, :D_A]
        xa_s[...] = xa
        bm_t_s[...] = xc[:, D_A:D_A + G_A * N_A].T
        cm_t_s[...] = xc[:, D_A + G_A * N_A:].T
        dt = _softplus(dt_ref[...] + dtb_ref[...])
        dec_t_s[...] = jnp.exp(dt * -jnp.exp(alog_ref[...])).T
        dt_x = _dot01(dt, _tile3(_head_expand(), 0), "lhs")
        xdt_t_s[...] = (xa * dt_x).T

    for hh in range(_SSD_HB):
        h = i * _SSD_HB + hh
        g0 = pl.multiple_of((h // HPG) * N_A, N_A)
        p0 = pl.multiple_of(h * P_A, P_A)
        b_t = bm_t_s[pl.ds(g0, N_A), :]
        c_t = cm_t_s[pl.ds(g0, N_A), :]
        dec = dec_t_s[pl.ds(h, 1), :]

        def per_8p(k, carry, hh=hh, p0=p0, b_t=b_t, c_t=c_t, dec=dec):
            r = pl.ds(pl.multiple_of(p0 + k * 8, 8), 8)
            x_rows = xdt_t_s[r, :]
            ys = []
            for u in range(8):
                h_new = hin_ref[hh, k * 8 + u] * dec + x_rows[u:u + 1, :] * b_t
                hout_ref[hh, k * 8 + u] = h_new
                ys.append(jnp.sum(h_new * c_t, axis=0, keepdims=True))
            yt_s[r, :] = jnp.concatenate(ys, axis=0)
            return carry

        lax.fori_loop(0, P_A // 8, per_8p, 0)

    @pl.when(i == H_A // _SSD_HB - 1)
    def _():
        xa = xa_s[...]
        y = yt_s[...].T + dx_ref[...] * xa
        y = y * zs_ref[...]
        ya_ref[...] = (_rms_unit(y) * naw_ref[...]).astype(BF16)


_SSD_HB = 2


def _ssd_sample(p, conv_st_t, h_ssd_t, lw, nb, l, prev):
    full = lambda w: _const_spec((nb, w))
    cshape = (CONV_A - 1, nb, D_XBC)
    st = pl.BlockSpec((None, _SSD_HB, P_A, N_A, nb), lambda i: (l, i, 0, 0, 0))
    in_specs = [full(D_A), full(D_XBC), full(DT_PAD), _layer_spec(l, cshape),
                _const_spec((CONV_A, D_XBC)), _const_spec((1, D_XBC)), _const_spec((1, DT_PAD)),
                _const_spec((1, DT_PAD)), _const_spec((1, D_A)), _const_spec((1, D_A)), st]
    prev = () if prev is None else (prev,)
    in_specs += [pl.BlockSpec(memory_space=pl.ANY)] * len(prev)
    aliases = {len(in_specs) - 1: 2} if prev else {}
    scratch = [pltpu.VMEM((nb, D_A), F32), pltpu.VMEM((D_A, nb), F32), pltpu.VMEM((LANES, nb), F32),
               pltpu.VMEM((G_A * N_A, nb), F32), pltpu.VMEM((G_A * N_A, nb), F32),
               pltpu.VMEM((D_A, nb), F32)]
    return pl.pallas_call(
        functools.partial(_ssd_sample_body, n_prev=len(prev)), grid=(H_A // _SSD_HB,),
        in_specs=in_specs,
        out_specs=[pl.BlockSpec((nb, D_A), lambda i: (0, 0)),
                   pl.BlockSpec(cshape, lambda i: (0, 0, 0)), st],
        out_shape=[jax.ShapeDtypeStruct((nb, D_A), BF16), jax.ShapeDtypeStruct(cshape, F32),
                   jax.ShapeDtypeStruct((DEPTH, H_A, P_A, N_A, nb), F32)],
        scratch_shapes=scratch, input_output_aliases=aliases,
        compiler_params=_params(1), name="ssd_sample")(
            p["z"], p["xbc"], p["dt"], conv_st_t,
            lw["conv_a_w"], lw["conv_a_b"], lw["dt_bias"], lw["a_log"], lw["d_x"], lw["norm_a_w"],
            h_ssd_t, *prev)


_RET_BB = 16


def _ret_sample_body(*refs, nb, n_prev):
    q_ref, k_ref, v_ref, gs_ref, cos_ref, sina_ref, sinb_ref, sin_ref = refs[:8]
    yb_ref, sout_ref, q_t_s, k_t_s, yr_s = refs[8 + n_prev:]
    i = pl.program_id(0)

    @pl.when(i == 0)
    def _():
        cos_f, sin_a, sin_b = cos_ref[...], sina_ref[...], sinb_ref[...]
        q_t_s[...] = _rope(q_ref[...], cos_f, sin_a, sin_b).T.astype(BF16)
        k_t_s[...] = (_rope(k_ref[...], cos_f, sin_a, sin_b) * (DK_R ** -0.5)).T.astype(BF16)

    rows_i = lax.broadcasted_iota(jnp.int32, (nb, DV_R), 0)
    blk = pl.ds(pl.multiple_of(i * _RET_BB, _RET_BB), _RET_BB)
    v_blk = v_ref[blk, :]
    ys = [[] for _ in range(H_R)]
    for j in range(_RET_BB):
        onehot = jnp.where(rows_i == i * _RET_BB + j, 1.0, 0.0).astype(BF16)
        k_bc = jnp.dot(k_t_s[...], onehot, preferred_element_type=F32)
        q_bc = jnp.dot(q_t_s[...], onehot, preferred_element_type=F32)
        for h in range(H_R):
            kl = slice(h * DK_R, (h + 1) * DK_R)
            vl = slice(h * DV_R, (h + 1) * DV_R)
            s_new = math.exp(_LOG_GAMMA[h]) * sin_ref[j, h] + k_bc[kl, :] * v_blk[j:j + 1, vl]
            sout_ref[j, h] = s_new
            ys[h].append(jnp.sum(q_bc[kl, :] * s_new, axis=0, keepdims=True))
    for h in range(H_R):
        yr_s[blk, h * DV_R:(h + 1) * DV_R] = jnp.concatenate(ys[h], axis=0)

    @pl.when(i == nb // _RET_BB - 1)
    def _():
        for h in range(H_R):
            vl = slice(h * DV_R, (h + 1) * DV_R)
            yb_ref[:, vl] = (gs_ref[:, vl] * _rms_unit(yr_s[:, vl])).astype(BF16)


def _ret_sample(p, s_ret, rot, nb, l, prev):
    full = lambda w: _const_spec((nb, w))
    st = pl.BlockSpec((None, _RET_BB, H_R, DK_R, DV_R), lambda i: (l, i, 0, 0, 0))
    in_specs = [full(D_QK), full(D_QK), full(D_RV), full(D_RV),
                _const_spec((1, D_QK)), _const_spec((1, D_QK)), _const_spec((1, D_QK)), st]
    prev = () if prev is None else (prev,)
    in_specs += [pl.BlockSpec(memory_space=pl.ANY)] * len(prev)
    aliases = {len(in_specs) - 1: 1} if prev else {}
    return pl.pallas_call(
        functools.partial(_ret_sample_body, nb=nb, n_prev=len(prev)), grid=(nb // _RET_BB,),
        in_specs=in_specs, out_specs=[pl.BlockSpec((nb, D_RV), lambda i: (0, 0)), st],
        out_shape=[jax.ShapeDtypeStruct((nb, D_RV), BF16),
                   jax.ShapeDtypeStruct((DEPTH, nb, H_R, DK_R, DV_R), F32)],
        scratch_shapes=[pltpu.VMEM((D_QK, nb), BF16), pltpu.VMEM((D_QK, nb), BF16),
                        pltpu.VMEM((nb, D_RV), F32)],
        input_output_aliases=aliases, compiler_params=_params(1), name="ret_sample")(
            p["q"], p["k"], p["v"], p["g"], rot["cos"], rot["sin_a"], rot["sin_b"], s_ret, *prev)


def _rotary_tables(start, length):
    half = DK_R // 2
    pos = start + jnp.arange(length, dtype=F32)
    freqs = ROPE_BASE ** (-jnp.arange(half, dtype=F32) / half)
    ang = pos[:, None] * freqs[None, :]
    cos, sin, zero = jnp.cos(ang), jnp.sin(ang), jnp.zeros_like(ang)
    tile = lambda a, b: jnp.tile(jnp.concatenate([a, b], axis=-1), (1, H_R))
    lg = jnp.log1p(-jnp.exp2(-5.0 - jnp.arange(H_R, dtype=F32)))
    return {"cos": tile(cos, cos), "sin_a": tile(-sin, zero), "sin_b": tile(zero, sin),
            "lgx": jnp.repeat(lg, DK_R)[None, :]}


def _matmul_weights(w):
    w_in = w["w_in"]
    o_dt = D_A + D_XBC
    o_gates = w_in.shape[-1] - 3 * D_MODEL
    dt_cols = jnp.pad(w_in[..., o_dt:o_dt + H_A], ((0, 0), (0, 0), (0, DT_PAD - H_A)))
    w_proj = jnp.concatenate([w_in[..., :o_dt], w_in[..., o_dt + H_A:o_gates], dt_cols], axis=-1)
    out = {"w_proj": w_proj.astype(BF16), "w_gates": w_in[..., o_gates:].astype(BF16)}
    for name in ("w_glu", "w_br_a", "w_br_b", "w_br_c", "w_out", "w_up", "w_down"):
        out[name] = w[name].astype(BF16)
    return out


def _layer_weights(l, w, mats):
    pad_row = lambda a: jnp.pad(a, (0, DT_PAD - H_A))[None, :]
    ab_re, ab_im, bb_re, bb_im = _s5_discretise(
        w["s5_a_re"][l], w["s5_a_im"][l], w["s5_log_dt"][l], w["s5_b_re"][l], w["s5_b_im"][l])
    bblk = jnp.concatenate([_block_diag(bb_re), _block_diag(bb_im)], axis=-1)
    c_t = lambda c: _block_diag(jnp.transpose(c, (0, 2, 1)))
    return {
        "l": l, **mats,
        "norm_mix_w": w["norm_mix_w"][l][None, :],
        "conv_a_w": w["conv_a_w"][l], "conv_a_b": w["conv_a_b"][l][None, :],
        "dt_bias": pad_row(w["dt_bias"][l]), "a_log": pad_row(w["a_log"][l]),
        "d_x": jnp.repeat(w["d_a"][l], P_A)[None, :], "norm_a_w": w["norm_a_w"][l][None, :],
        "s5_ab_re": ab_re.reshape(1, D_S5), "s5_ab_im": ab_im.reshape(1, D_S5),
        "s5_bblk": bblk.astype(BF16),
        "s5_cre": c_t(w["s5_c_re"][l]).astype(BF16), "s5_cim": c_t(w["s5_c_im"][l]).astype(BF16),
        "s5_d": w["s5_d"][l][None, :], "b_glu": w["b_glu"][l][None, :],
        "norm_ffn_w": w["norm_ffn_w"][l][None, :],
        "conv_f_w": w["conv_f_w"][l], "conv_f_b": w["conv_f_b"][l][None, :],
    }


def _prompt_trunk(x, lws, nf, tm_proj, tsteps, tt, tm_merge, tm_ffn):
    nb, seq, _ = x.shape
    rot = _rotary_tables(0.0, seq)
    x2d = x.reshape(nb * seq, D_MODEL)
    zeros = jnp.zeros((nb, D_S5), F32)
    states, big = [], None
    for l, lw in enumerate(lws):
        p = _inproj(x2d, lw["norm_mix_w"], lw, rot, nb, seq, tm_proj, conv=True)
        ya, yb, *big = _mix_prompt(p, rot, lw, nb, seq, tsteps, l, big)
        yc, h_re, h_im = _s5(p["u"], zeros, zeros, lw, nb, seq, tt)
        x2d = _merge(x2d, ya, yb, yc, lw, nb, seq, tm_merge)
        x2d, fs = _ffn(x2d, None, lw, nf, nb, seq, tm_ffn, final=(l == len(lws) - 1))
        states.append((p["conv_state"], h_re.reshape(nb, G_C, P_C), h_im.reshape(nb, G_C, P_C), fs))
    cs, h_re, h_im, fs = [jnp.stack(s) for s in zip(*states)]
    return x2d.reshape(nb, seq, D_MODEL), [cs, big[0], big[1], h_re, h_im, fs]


def _sample_trunk(x, st, lws, nf):
    nb = x.shape[0]
    rot = _rotary_tables(float(PAST_LEN), 1)
    x2d = x.reshape(nb, D_MODEL)
    st_conv, st_ssd, st_ret, st_re, st_im, st_ffn = st
    st_ssd_t = jnp.transpose(st_ssd, (0, 2, 3, 4, 1))
    st_conv_t = jnp.transpose(st_conv, (0, 2, 1, 3))
    s5_t = lambda s: jnp.transpose(s, (0, 2, 3, 1)).reshape(DEPTH, D_S5, nb)
    st_re_t, st_im_t = s5_t(st_re), s5_t(st_im)
    states, hs, ss = [], None, None
    for l, lw in enumerate(lws):
        p = _inproj(x2d, lw["norm_mix_w"], lw, rot, 1, nb, nb, conv=False)
        ya, cs, hs = _ssd_sample(p, st_conv_t, st_ssd_t, lw, nb, l, hs)
        yb, ss = _ret_sample(p, st_ret, rot, nb, l, ss)
        yc, h_re, h_im = _s5(p["u"], st_re_t[l], st_im_t[l], lw, nb, 1, 1, state_t=True)
        x2d = _merge(x2d, ya, yb, yc, lw, 1, nb, nb)
        x2d, fs = _ffn(x2d, st_ffn[l].reshape(nb, -1), lw, nf, nb, 1, nb,
                       final=(l == len(lws) - 1))
        states.append((cs, h_re, h_im, fs.reshape(nb, CONV_F - 1, 2 * D_FF)))
    cs, h_re, h_im, fs = [jnp.stack(s) for s in zip(*states)]
    hs = jnp.transpose(hs, (0, 4, 1, 2, 3))
    cs = jnp.transpose(cs, (0, 2, 1, 3))
    s5_back = lambda s: jnp.transpose(s.reshape(DEPTH, G_C, P_C, nb), (0, 3, 1, 2))
    return x2d.reshape(nb, 1, D_MODEL), [cs, hs, ss, s5_back(h_re), s5_back(h_im), fs]


def kernel(x_prompt, x_sample, state_ssd_conv, state_ssd, state_ret, state_s5_re, state_s5_im,
           state_ffn_conv, norm_mix_w, w_in, conv_a_w, conv_a_b, dt_bias, a_log, d_a, norm_a_w,
           s5_a_re, s5_a_im, s5_log_dt, s5_b_re, s5_b_im, s5_c_re, s5_c_im, s5_d, w_glu, b_glu,
           w_br_a, w_br_b, w_br_c, w_out, norm_ffn_w, w_up, conv_f_w, conv_f_b, w_down, norm_f_w):
    w = dict(norm_mix_w=norm_mix_w, w_in=w_in, conv_a_w=conv_a_w, conv_a_b=conv_a_b,
             dt_bias=dt_bias, a_log=a_log, d_a=d_a, norm_a_w=norm_a_w, s5_a_re=s5_a_re,
             s5_a_im=s5_a_im, s5_log_dt=s5_log_dt, s5_b_re=s5_b_re, s5_b_im=s5_b_im,
             s5_c_re=s5_c_re, s5_c_im=s5_c_im, s5_d=s5_d, w_glu=w_glu, b_glu=b_glu,
             w_br_a=w_br_a, w_br_b=w_br_b, w_br_c=w_br_c, w_out=w_out, norm_ffn_w=norm_ffn_w,
             w_up=w_up, conv_f_w=conv_f_w, conv_f_b=conv_f_b, w_down=w_down)
    mats = _matmul_weights(w)
    lws = [_layer_weights(l, w, mats) for l in range(DEPTH)]
    nf = norm_f_w[None, :]
    seq = x_prompt.shape[1]
    y_p, p_st = _prompt_trunk(x_prompt, lws, nf, tm_proj=min(512, seq), tsteps=min(512, seq),
                              tt=min(64, seq), tm_merge=min(512, seq), tm_ffn=min(512, seq))
    y_s, s_st = _sample_trunk(
        x_sample, (state_ssd_conv, state_ssd, state_ret, state_s5_re, state_s5_im, state_ffn_conv),
        lws, nf)
    return (y_p, y_s, *p_st, *s_st)
```

```python
import functools
import math

import jax
import jax.numpy as jnp
import numpy as np
from jax import lax
from jax.experimental import pallas as pl
from jax.experimental.pallas import tpu as pltpu

F32 = jnp.float32
BF16 = jnp.bfloat16

D_MODEL = 1024
DEPTH = 2
PAST_LEN = 16384
H_A, P_A, N_A, G_A, CONV_A = 16, 64, 64, 2, 4
D_A = H_A * P_A
D_XBC = D_A + 2 * G_A * N_A
H_R, DK_R, DV_R = 8, 64, 128
D_QK = H_R * DK_R
D_RV = H_R * DV_R
ROPE_BASE = 10000.0
GS_C, G_C, P_C = 16, 64, 64
D_C = G_C * GS_C
D_S5 = G_C * P_C
D_FF = 2816
CONV_F = 3
CHUNK = 128
EPS = 1e-6
LANES = 128
DT_PAD = LANES
HPG = H_A // G_A
VMEM_LIMIT = 56 * 1024 * 1024

_LOG_GAMMA = [math.log1p(-(2.0 ** (-5.0 - h))) for h in range(H_R)]

_SEGS = (("z", D_A), ("xbc", D_XBC), ("q", D_QK), ("k", D_QK),
         ("v", D_RV), ("g", D_RV), ("u", D_C), ("dt", DT_PAD))
_SEG_OFF = np.cumsum([0] + [w for _, w in _SEGS]).tolist()
D_PROJ = _SEG_OFF[-1]


def _rms_unit(x):
    return x * lax.rsqrt(jnp.mean(x * x, axis=-1, keepdims=True) + EPS)


def _silu(x):
    return x * jax.nn.sigmoid(x)


def _softplus(x):
    return jnp.maximum(x, 0.0) + jnp.log1p(jnp.exp(-jnp.abs(x)))


def _bdot(a, b):
    return jnp.dot(a.astype(BF16), b.astype(BF16), preferred_element_type=F32)


def _split3(x):
    hi = x.astype(BF16)
    r1 = x - hi.astype(F32)
    mid = r1.astype(BF16)
    return hi, mid, (r1 - mid.astype(F32)).astype(BF16)


def _tile3(m01, axis):
    return jnp.concatenate([m01] * 3, axis=axis)


def _dot01(a, b, f32_side):
    if f32_side == "lhs":
        return jnp.dot(jnp.concatenate(_split3(a), axis=1), b, preferred_element_type=F32)
    return jnp.dot(a, jnp.concatenate(_split3(b), axis=0), preferred_element_type=F32)


def _const_spec(shape):
    nd = len(shape)
    return pl.BlockSpec(shape, lambda *_: (0,) * nd, pipeline_mode=pl.Buffered(1))


def _layer_spec(l, shape):
    nd = len(shape)
    return pl.BlockSpec((None,) + tuple(shape), lambda *_: (l,) + (0,) * nd,
                        pipeline_mode=pl.Buffered(1))


def _params(n_grid):
    return pltpu.CompilerParams(dimension_semantics=("arbitrary",) * n_grid,
                                vmem_limit_bytes=VMEM_LIMIT)


_PROJ_SUB = 128


def _inproj_body(x_ref, nw_ref, cw_ref, cb_ref, cos_ref, sina_ref, sinb_ref, w_ref, *refs,
                 tm, conv):
    n = len(_SEGS)
    if conv:
        cs_ref, xp_s = refs[n:]

        @pl.when(pl.program_id(1) == 0)
        def _():
            xp_s[0:8, :] = jnp.zeros((8, D_XBC), F32)

    for r in range(0, tm, _PROJ_SUB):
        rs = slice(r, r + _PROJ_SUB)
        h = (_rms_unit(x_ref[rs, :]) * nw_ref[...]).astype(BF16)
        for (name, width), off, o_ref in zip(_SEGS, _SEG_OFF, refs[:n]):
            y = jnp.dot(h, w_ref[:, off:off + width], preferred_element_type=F32)
            if name in ("z", "g"):
                y = _silu(y)
            elif name in ("q", "k") and conv:
                y = _rope(y, cos_ref[rs, :], sina_ref[rs, :], sinb_ref[rs, :])
                if name == "k":
                    y = y * (DK_R ** -0.5)
            elif name == "xbc" and conv:
                xp_s[8 + r:8 + r + _PROJ_SUB, :] = y
                y = cb_ref[...]
                for tap in range(CONV_A):
                    y = y + xp_s[5 + tap + r:5 + tap + r + _PROJ_SUB, :] * cw_ref[tap:tap + 1, :]
                y = _silu(y)
            o_ref[rs, :] = y
    if conv:
        tail = xp_s[tm + 5:tm + 8, :]
        cs_ref[...] = tail
        xp_s[5:8, :] = tail


def _inproj(x2d, nw, lw, rot, nb, seq, tm, conv):
    nt = seq // tm
    tab = (pl.BlockSpec((tm, D_QK), lambda b, i: (i, 0)) if conv
           else _const_spec(rot["cos"].shape))
    in_specs = [pl.BlockSpec((tm, D_MODEL), lambda b, i: (b * nt + i, 0)),
                _const_spec((1, D_MODEL)), _const_spec((CONV_A, D_XBC)), _const_spec((1, D_XBC)),
                tab, tab, tab, _layer_spec(lw["l"], (D_MODEL, D_PROJ))]
    out_specs, out_shape, scratch = [], [], []
    for _, width in _SEGS:
        out_specs.append(pl.BlockSpec((tm, width), lambda b, i: (b * nt + i, 0)))
        out_shape.append(jax.ShapeDtypeStruct((nb * seq, width), F32))
    if conv:
        out_specs.append(pl.BlockSpec((None, CONV_A - 1, D_XBC), lambda b, i: (b, 0, 0)))
        out_shape.append(jax.ShapeDtypeStruct((nb, CONV_A - 1, D_XBC), F32))
        scratch.append(pltpu.VMEM((8 + tm, D_XBC), F32))
    outs = pl.pallas_call(
        functools.partial(_inproj_body, tm=tm, conv=conv), grid=(nb, nt), in_specs=in_specs,
        out_specs=out_specs, out_shape=out_shape, scratch_shapes=scratch,
        compiler_params=_params(2), name="inproj")(
            x2d, nw, lw["conv_a_w"], lw["conv_a_b"], rot["cos"], rot["sin_a"], rot["sin_b"],
            lw["w_proj"])
    res = {name: o for (name, _), o in zip(_SEGS, outs)}
    if conv:
        res["conv_state"] = outs[-1]
    return res


def _rope(x, cos_f, sin_a, sin_b):
    half = DK_R // 2
    return (x * cos_f + pltpu.roll(x, D_QK - half, 1) * sin_a
            + pltpu.roll(x, half, 1) * sin_b)


def _head_expand():
    lo = lax.broadcasted_iota(jnp.int32, (LANES, D_A), 0) * P_A
    c = lax.broadcasted_iota(jnp.int32, (LANES, D_A), 1)
    return jnp.where(c >= lo, jnp.where(c < lo + P_A, 1.0, 0.0), 0.0).astype(BF16)


def _mix_body(*refs, tsteps, nt, n_prev):
    (zs_ref, xc_ref, dt_ref, q_ref, k_ref, v_ref, gs_ref, lgx_ref,
     dtb_ref, alog_ref, dx_ref, naw_ref) = refs[:12]
    (ya_ref, yb_ref, hs_ref, ss_ref,
     ht_s, s_s, intra_s, qdec_s, kdect_s) = refs[12 + n_prev:]
    i = pl.program_id(1)
    first = jnp.logical_and(pl.program_id(0) == 0, i == 0)
    rows_i = lax.broadcasted_iota(jnp.int32, (CHUNK, CHUNK), 0)
    cols_i = lax.broadcasted_iota(jnp.int32, (CHUNK, CHUNK), 1)
    causal = rows_i >= cols_i
    lane_lo = cols_i < LANES // 2

    @pl.when(first)
    def _():
        rel = (rows_i - cols_i).astype(F32)
        for h in range(H_R):
            intra_s[h] = jnp.where(causal, jnp.exp(jnp.maximum(rel, 0.0) * _LOG_GAMMA[h]), 0.0)
        ri = lax.broadcasted_iota(jnp.int32, (CHUNK, D_QK), 0).astype(F32)
        lgx = lgx_ref[...]
        qdec_s[...] = jnp.exp((ri + 1.0) * lgx)
        kdect_s[...] = jnp.exp((CHUNK - 1.0 - ri) * lgx).T

    @pl.when(i == 0)
    def _():
        ht_s[...] = jnp.zeros_like(ht_s)
        s_s[...] = jnp.zeros_like(s_s)

    tril = _tile3(jnp.where(causal, 1.0, 0.0).astype(BF16), 1)
    expand = _tile3(_head_expand(), 0)
    a_neg = -jnp.exp(alog_ref[...])

    for c in range(tsteps // CHUNK):
        r0 = c * CHUNK
        rows = slice(r0, r0 + CHUNK)
        xa = xc_ref[rows, :D_A]
        bm = xc_ref[rows, D_A:D_A + G_A * N_A]
        cm = xc_ref[rows, D_A + G_A * N_A:]
        dt = _softplus(dt_ref[rows, :] + dtb_ref[...])
        cum = _dot01(tril, dt * a_neg, "rhs")
        cum_t = cum.T
        dt_t = dt.T
        cum_last = cum[CHUNK - 1:CHUNK, :]
        scale = jnp.concatenate(
            [dt * jnp.exp(cum_last - cum), jnp.exp(cum),
             jnp.broadcast_to(jnp.exp(cum_last), (8, LANES))], axis=0)
        scale_x = _dot01(scale, expand, "lhs")
        w_x = scale_x[:CHUNK]
        ecum_x = scale_x[CHUNK:2 * CHUNK]
        dec_x = scale_x[2 * CHUNK:2 * CHUNK + 1]
        bm_t = bm.T
        ys = []
        for g in range(G_A):
            gl = slice(g * N_A, (g + 1) * N_A)
            hl = slice(g * HPG * P_A, (g + 1) * HPG * P_A)
            cg = cm[:, gl]
            scores = lax.dot_general(cg.astype(BF16), bm[:, gl].astype(BF16),
                                     (((1,), (1,)), ((), ())), preferred_element_type=F32)
            scores = jnp.where(causal, scores, 0.0)
            ht_g = ht_s[:, hl]
            y_inter = _bdot(cg, ht_g) * ecum_x[:, hl]
            for pair in range(HPG // 2):
                pl_ = slice(g * HPG * P_A + pair * LANES, g * HPG * P_A + (pair + 1) * LANES)
                x_pair = xa[:, pl_]
                sps = []
                for h in (g * HPG + 2 * pair, g * HPG + 2 * pair + 1):
                    seg = cum[:, h:h + 1] - cum_t[h:h + 1, :]
                    sps.append(scores * jnp.exp(jnp.minimum(seg, 0.0)) * dt_t[h:h + 1, :])
                rhs = jnp.concatenate([jnp.where(lane_lo, x_pair, 0.0),
                                       jnp.where(lane_lo, 0.0, x_pair)], axis=0)
                ys.append(_bdot(jnp.concatenate(sps, axis=1), rhs)
                          + y_inter[:, pair * LANES:(pair + 1) * LANES])
            ht_s[:, hl] = dec_x[:, hl] * ht_g + _bdot(bm_t[gl, :], xa[:, hl] * w_x[:, hl])
        y = jnp.concatenate(ys, axis=1) + dx_ref[...] * xa
        y = y * zs_ref[rows, :]
        ya_ref[rows, :] = (_rms_unit(y) * naw_ref[...]).astype(BF16)

        qr = q_ref[rows, :]
        k_t = k_ref[rows, :].T
        kd_t = k_t * kdect_s[...]
        qd = qr * qdec_s[...]
        for pair in range(H_R // 2):
            pl_ = slice(pair * LANES, (pair + 1) * LANES)
            s_pair = s_s[pl_, :]
            for h, keep in ((2 * pair, lane_lo), (2 * pair + 1, jnp.logical_not(lane_lo))):
                kl = slice(h * DK_R, (h + 1) * DK_R)
                vl = slice(h * DV_R, (h + 1) * DV_R)
                v_h = v_ref[rows, vl]
                sc = _bdot(jnp.where(keep, qr[:, pl_], 0.0), k_t[pl_, :]) * intra_s[h]
                y_h = _bdot(jnp.concatenate([sc, jnp.where(keep, qd[:, pl_], 0.0)], axis=1),
                            jnp.concatenate([v_h, s_pair], axis=0))
                s_s[kl, :] = (math.exp(CHUNK * _LOG_GAMMA[h]) * s_s[kl, :]
                              + _bdot(kd_t[kl, :], v_h))
                yb_ref[rows, vl] = (gs_ref[rows, vl] * _rms_unit(y_h)).astype(BF16)

    @pl.when(i == nt - 1)
    def _():
        for h in range(H_A):
            hs_ref[h] = ht_s[:, h * P_A:(h + 1) * P_A].T
        for h in range(H_R):
            ss_ref[h] = s_s[h * DK_R:(h + 1) * DK_R, :]


def _layer_block(l, tail):
    zeros = (0,) * len(tail)
    return pl.BlockSpec((None, None) + tail, lambda b, *_: (l, b) + zeros)


def _mix_prompt(p, rot, lw, nb, seq, tsteps, l, prev):
    nt = seq // tsteps
    row = lambda w: pl.BlockSpec((tsteps, w), lambda b, i: (b * nt + i, 0))
    in_specs = [row(D_A), row(D_XBC), row(DT_PAD), row(D_QK), row(D_QK), row(D_RV), row(D_RV),
                _const_spec((1, D_QK)), _const_spec((1, DT_PAD)),
                _const_spec((1, DT_PAD)), _const_spec((1, D_A)), _const_spec((1, D_A))]
    prev = () if prev is None else tuple(prev)
    in_specs += [pl.BlockSpec(memory_space=pl.ANY)] * len(prev)
    aliases = {len(in_specs) - len(prev) + k: 2 + k for k in range(len(prev))}
    out_specs = [row(D_A), row(D_RV),
                 _layer_block(l, (H_A, P_A, N_A)), _layer_block(l, (H_R, DK_R, DV_R))]
    out_shape = [jax.ShapeDtypeStruct((nb * seq, D_A), BF16),
                 jax.ShapeDtypeStruct((nb * seq, D_RV), BF16),
                 jax.ShapeDtypeStruct((DEPTH, nb, H_A, P_A, N_A), F32),
                 jax.ShapeDtypeStruct((DEPTH, nb, H_R, DK_R, DV_R), F32)]
    scratch = [pltpu.VMEM((N_A, D_A), F32),
               pltpu.VMEM((D_QK, DV_R), F32), pltpu.VMEM((H_R, CHUNK, CHUNK), F32),
               pltpu.VMEM((CHUNK, D_QK), F32), pltpu.VMEM((D_QK, CHUNK), F32)]
    return pl.pallas_call(
        functools.partial(_mix_body, tsteps=tsteps, nt=nt, n_prev=len(prev)), grid=(nb, nt),
        in_specs=in_specs, out_specs=out_specs, out_shape=out_shape, scratch_shapes=scratch,
        input_output_aliases=aliases, compiler_params=_params(2), name="mix_prompt")(
            p["z"], p["xbc"], p["dt"], p["q"], p["k"], p["v"], p["g"],
            rot["lgx"], lw["dt_bias"], lw["a_log"], lw["d_x"], lw["norm_a_w"], *prev)


def _s5_disc_body(are_ref, aim_ref, ldt_ref, bre_ref, bim_ref,
                  abre_ref, abim_ref, bbre_ref, bbim_ref):
    ar, ai = are_ref[...], aim_ref[...]
    dt = jnp.exp(ldt_ref[...])
    mag = jnp.exp(ar * dt)
    ab_re = mag * jnp.cos(ai * dt)
    ab_im = mag * jnp.sin(ai * dt)
    den = ar * ar + ai * ai
    num_re = ab_re - 1.0
    coef_re = (num_re * ar + ab_im * ai) / den
    coef_im = (ab_im * ar - num_re * ai) / den
    abre_ref[...] = ab_re
    abim_ref[...] = ab_im
    for c in range(GS_C):
        cl = slice(c * P_C, (c + 1) * P_C)
        br, bi = bre_ref[:, cl], bim_ref[:, cl]
        bbre_ref[:, cl] = coef_re * br - coef_im * bi
        bbim_ref[:, cl] = coef_re * bi + coef_im * br


def _s5_discretise(a_re, a_im, log_dt, b_re, b_im):
    gp = jax.ShapeDtypeStruct((G_C, P_C), F32)
    gcp = jax.ShapeDtypeStruct((G_C, GS_C * P_C), F32)
    b_t = lambda b: jnp.transpose(b, (0, 2, 1)).reshape(G_C, GS_C * P_C)
    ab_re, ab_im, bb_re, bb_im = pl.pallas_call(
        _s5_disc_body, out_shape=[gp, gp, gcp, gcp], name="s5_disc")(
            a_re, a_im, log_dt.reshape(G_C, 1), b_t(b_re), b_t(b_im))
    return ab_re, ab_im, bb_re.reshape(G_C, GS_C, P_C), bb_im.reshape(G_C, GS_C, P_C)


_S5_GB = LANES // GS_C
_S5_NBLK = G_C // _S5_GB
_S5_SB = _S5_GB * P_C


def _block_diag(m):
    g, r, c = m.shape
    m = m.reshape(_S5_NBLK, _S5_GB, r, c)
    eye = jnp.eye(_S5_GB, dtype=m.dtype)
    return jnp.einsum("jgrc,gk->jgrkc", m, eye).reshape(_S5_NBLK, _S5_GB * r, _S5_GB * c)


def _s5_body(u_ref, h0re_ref, h0im_ref, are_ref, aim_ref, bblk_ref, cre_ref, cim_ref,
             d_ref, wglu_ref, bglu_ref, yc_ref, hre_ref, him_ref,
             xre_s, xim_s, sre_s, sim_s, *, nb, tt, state_t):
    i = pl.program_id(0)
    rows = nb * tt

    @pl.when(i == 0)
    def _():
        sre_s[...] = h0re_ref[...].T if state_t else h0re_ref[...]
        sim_s[...] = h0im_ref[...].T if state_t else h0im_ref[...]

    u = u_ref[...]
    if tt > 1:
        u = jnp.swapaxes(u, 0, 1).reshape(rows, D_C)
    ub = u.astype(BF16)

    ys = []
    for j in range(_S5_NBLK):
        sl = slice(j * _S5_SB, (j + 1) * _S5_SB)
        bu = jnp.dot(ub[:, j * LANES:(j + 1) * LANES], bblk_ref[j], preferred_element_type=F32)
        xre_s[:, sl] = bu[:, :_S5_SB]
        xim_s[:, sl] = bu[:, _S5_SB:]
        ar = jnp.broadcast_to(are_ref[:, sl], (nb, _S5_SB))
        ai = jnp.broadcast_to(aim_ref[:, sl], (nb, _S5_SB))
        xr, xi = sre_s[:, sl], sim_s[:, sl]
        for t in range(tt):
            r = slice(t * nb, (t + 1) * nb)
            xr, xi = (ar * xr - ai * xi + xre_s[r, sl], ar * xi + ai * xr + xim_s[r, sl])
            xre_s[r, sl] = xr
            xim_s[r, sl] = xi
        sre_s[:, sl] = xr
        sim_s[:, sl] = xi
        ys.append(_bdot(xre_s[:, sl], cre_ref[j]) - _bdot(xim_s[:, sl], cim_ref[j]))
    yc = jnp.concatenate(ys, axis=1)
    parts = []
    half = rows // 2 if rows >= 256 else rows
    for r in range(0, rows, half):
        yh = jax.nn.gelu(yc[r:r + half] + d_ref[...] * u[r:r + half])
        parts.append(yh * jax.nn.sigmoid(_bdot(yh, wglu_ref[...]) + bglu_ref[...]))
    y = jnp.concatenate(parts, axis=0)
    if tt > 1:
        y = jnp.swapaxes(y.reshape(tt, nb, D_C), 0, 1)
    yc_ref[...] = y.astype(BF16)
    hre_ref[...] = sre_s[...].T if state_t else sre_s[...]
    him_ref[...] = sim_s[...].T if state_t else sim_s[...]


def _s5(u, h0_re, h0_im, lw, nb, seq, tt, state_t=False):
    rows = tt * nb
    sshape = (D_S5, nb) if state_t else (nb, D_S5)
    if tt > 1:
        assert nb % 8 == 0 and tt % 8 == 0
        u = u.reshape(nb, seq, D_C)
        io_spec = pl.BlockSpec((nb, tt, D_C), lambda i: (0, i, 0))
    else:
        io_spec = pl.BlockSpec((rows, D_C), lambda i: (i, 0))
    in_specs = [io_spec,
                _const_spec(sshape), _const_spec(sshape),
                _const_spec((1, D_S5)), _const_spec((1, D_S5)),
                _const_spec((_S5_NBLK, LANES, 2 * _S5_SB)),
                _const_spec((_S5_NBLK, _S5_SB, LANES)), _const_spec((_S5_NBLK, _S5_SB, LANES)),
                _const_spec((1, D_C)), _layer_spec(lw["l"], (D_C, D_C)), _const_spec((1, D_C))]
    st = pl.BlockSpec(sshape, lambda i: (0, 0))
    yc, h_re, h_im = pl.pallas_call(
        functools.partial(_s5_body, nb=nb, tt=tt, state_t=state_t), grid=(seq // tt,),
        in_specs=in_specs, out_specs=[io_spec, st, st],
        out_shape=[jax.ShapeDtypeStruct(u.shape, BF16),
                   jax.ShapeDtypeStruct(sshape, F32), jax.ShapeDtypeStruct(sshape, F32)],
        scratch_shapes=[pltpu.VMEM((rows, D_S5), F32), pltpu.VMEM((rows, D_S5), F32),
                        pltpu.VMEM((nb, D_S5), F32), pltpu.VMEM((nb, D_S5), F32)],
        compiler_params=_params(1), name="s5")(
            u, h0_re, h0_im, lw["s5_ab_re"], lw["s5_ab_im"], lw["s5_bblk"],
            lw["s5_cre"], lw["s5_cim"], lw["s5_d"], lw["w_glu"], lw["b_glu"])
    return yc.reshape(nb * seq, D_C), h_re, h_im


def _merge_body(x_ref, nw_ref, ya_ref, yb_ref, yc_ref, wg_ref, wa_ref, wb_ref, wc_ref, wo_ref,
                o_ref):
    x = x_ref[...]
    h = (_rms_unit(x) * nw_ref[...]).astype(BF16)
    merged = None
    for k, (y_ref, w_ref) in enumerate(((ya_ref, wa_ref), (yb_ref, wb_ref), (yc_ref, wc_ref))):
        gate = jax.nn.sigmoid(jnp.dot(h, wg_ref[:, k * D_MODEL:(k + 1) * D_MODEL],
                                      preferred_element_type=F32))
        term = gate * _bdot(y_ref[...], w_ref[...])
        merged = term if merged is None else merged + term
    o_ref[...] = x + _bdot(merged, wo_ref[...])


def _merge(x2d, ya, yb, yc, lw, nb, seq, tm):
    nt = seq // tm
    row = lambda w: pl.BlockSpec((tm, w), lambda b, i: (b * nt + i, 0))
    wspec = _layer_spec(lw["l"], (D_MODEL, D_MODEL))
    return pl.pallas_call(
        _merge_body, grid=(nb, nt),
        in_specs=[row(D_MODEL), _const_spec((1, D_MODEL)), row(D_A), row(D_RV), row(D_C),
                  _layer_spec(lw["l"], (D_MODEL, 3 * D_MODEL)), wspec, wspec, wspec, wspec],
        out_specs=row(D_MODEL), out_shape=jax.ShapeDtypeStruct((nb * seq, D_MODEL), F32),
        compiler_params=_params(2), name="merge")(
            x2d, lw["norm_mix_w"], ya, yb, yc, lw["w_gates"],
            lw["w_br_a"], lw["w_br_b"], lw["w_br_c"], lw["w_out"])


_FFN_CW = 256


_FFN_SUB = 512


def _ffn_prompt_body(x_ref, nw_ref, wup_ref, cw_ref, cb_ref, wdn_ref, nf_ref,
                     o_ref, cs_ref, up_s, act_s, *, tm, final):
    @pl.when(pl.program_id(1) == 0)
    def _():
        up_s[0:8, :] = jnp.zeros((8, 2 * D_FF), F32)

    sub = min(tm, _FFN_SUB)
    for r in range(0, tm, sub):
        rs = slice(r, r + sub)
        x = x_ref[rs, :]
        hf = (_rms_unit(x) * nw_ref[...]).astype(BF16)
        up_s[8 + r:8 + r + sub, :] = jnp.dot(hf, wup_ref[...], preferred_element_type=F32)
        for c in range(0, D_FF, _FFN_CW):
            def conv(c0):
                cl = slice(c0, c0 + _FFN_CW)
                y = cb_ref[:, cl]
                for tap in range(CONV_F):
                    y = y + up_s[6 + tap + r:6 + tap + r + sub, cl] * cw_ref[tap:tap + 1, cl]
                return y
            act_s[rs, c:c + _FFN_CW] = (_silu(conv(c)) * conv(D_FF + c)).astype(BF16)
        out = x + jnp.dot(act_s[rs, :], wdn_ref[...], preferred_element_type=F32)
        o_ref[rs, :] = _rms_unit(out) * nf_ref[...] if final else out
    tail = up_s[tm + 6:tm + 8, :]
    cs_ref[...] = tail
    up_s[6:8, :] = tail


def _ffn_sample_body(x_ref, st_ref, nw_ref, wup_ref, cw_ref, cb_ref, wdn_ref, nf_ref, *refs,
                     final):
    o_ref, cs_ref, act_s = refs[-3:]
    x = x_ref[...]
    hf = (_rms_unit(x) * nw_ref[...]).astype(BF16)
    up = jnp.dot(hf, wup_ref[...], preferred_element_type=F32)
    prev2, prev1 = st_ref[:, 0, :], st_ref[:, 1, :]
    cs_ref[:, 0, :] = prev1
    cs_ref[:, 1, :] = up
    for c in range(0, D_FF, _FFN_CW):
        def conv(c0):
            cl = slice(c0, c0 + _FFN_CW)
            return (cb_ref[:, cl] + prev2[:, cl] * cw_ref[0:1, cl] + prev1[:, cl] * cw_ref[1:2, cl]
                    + up[:, cl] * cw_ref[2:3, cl])
        act_s[:, c:c + _FFN_CW] = (_silu(conv(c)) * conv(D_FF + c)).astype(BF16)
    out = x + jnp.dot(act_s[...], wdn_ref[...], preferred_element_type=F32)
    o_ref[...] = _rms_unit(out) * nf_ref[...] if final else out


def _ffn(x2d, state, lw, nf, nb, seq, tm, final, prev=None):
    wspecs = [_const_spec((1, D_MODEL)), _layer_spec(lw["l"], (D_MODEL, 2 * D_FF)),
              _const_spec((CONV_F, 2 * D_FF)), _const_spec((1, 2 * D_FF)),
              _layer_spec(lw["l"], (D_FF, D_MODEL)), _const_spec((1, D_MODEL))]
    wargs = (lw["norm_ffn_w"], lw["w_up"], lw["conv_f_w"], lw["conv_f_b"], lw["w_down"], nf)
    act = pltpu.VMEM((tm, D_FF), BF16)
    if state is None:
        nt = seq // tm
        row = pl.BlockSpec((tm, D_MODEL), lambda b, i: (b * nt + i, 0))
        return pl.pallas_call(
            functools.partial(_ffn_prompt_body, tm=tm, final=final), grid=(nb, nt),
            in_specs=[row] + wspecs,
            out_specs=[row, pl.BlockSpec((None, CONV_F - 1, 2 * D_FF), lambda b, i: (b, 0, 0))],
            out_shape=[jax.ShapeDtypeStruct((nb * seq, D_MODEL), F32),
                       jax.ShapeDtypeStruct((nb, CONV_F - 1, 2 * D_FF), F32)],
            scratch_shapes=[pltpu.VMEM((8 + tm, 2 * D_FF), F32), act],
            compiler_params=_params(2), name="ffn_prompt")(x2d, *wargs)
    l = lw["l"]
    st = pl.BlockSpec((None, tm, CONV_F - 1, 2 * D_FF), lambda i: (l, i, 0, 0))
    prev = () if prev is None else (prev,)
    in_specs = ([pl.BlockSpec((tm, D_MODEL), lambda i: (i, 0)), st] + wspecs
                + [pl.BlockSpec(memory_space=pl.ANY)] * len(prev))
    return pl.pallas_call(
        functools.partial(_ffn_sample_body, final=final), grid=(nb // tm,), in_specs=in_specs,
        out_specs=[pl.BlockSpec((tm, D_MODEL), lambda i: (i, 0)), st],
        out_shape=[jax.ShapeDtypeStruct((nb, D_MODEL), F32),
                   jax.ShapeDtypeStruct(state.shape, F32)],
        scratch_shapes=[act], input_output_aliases={len(in_specs) - 1: 1} if prev else {},
        compiler_params=_params(1), name="ffn_sample")(x2d, state, *wargs, *prev)


def _ssd_sample_body(*refs, n_prev):
    (zs_ref, xbc_ref, dt_ref, cst_ref, cw_ref, cb_ref, dtb_ref, alog_ref, dx_ref, naw_ref,
     hin_ref) = refs[:11]
    (ya_ref, cso_ref, hout_ref, xa_s, xdt_t_s, dec_t_s, bm_t_s, cm_t_s, yt_s) = refs[11 + n_prev:]
    i = pl.program_id(0)

    @pl.when(i == 0)
    def _():
        acc = cb_ref[...]
        for tap in range(CONV_A - 1):
            acc = acc + cst_ref[tap] * cw_ref[tap:tap + 1, :]
        acc = acc + xbc_ref[...] * cw_ref[CONV_A - 1:CONV_A, :]
        for tap in range(CONV_A - 2):
            cso_ref[tap] = cst_ref[tap + 1]
        cso_ref[CONV_A - 2] = xbc_ref[...]
        xc = _silu(acc)
        xa = xc[:, :D_A]
        xa_s[...] = xa
        bm_t_s[...] = xc[:, D_A:D_A + G_A * N_A].T
        cm_t_s[...] = xc[:, D_A + G_A * N_A:].T
        dt = _softplus(dt_ref[...] + dtb_ref[...])
        dec_t_s[...] = jnp.exp(dt * -jnp.exp(alog_ref[...])).T
        dt_x = _dot01(dt, _tile3(_head_expand(), 0), "lhs")
        xdt_t_s[...] = (xa * dt_x).T

    for hh in range(_SSD_HB):
        h = i * _SSD_HB + hh
        g0 = pl.multiple_of((h // HPG) * N_A, N_A)
        p0 = pl.multiple_of(h * P_A, P_A)
        b_t = bm_t_s[pl.ds(g0, N_A), :]
        c_t = cm_t_s[pl.ds(g0, N_A), :]
        dec = dec_t_s[pl.ds(h, 1), :]

        def per_8p(k, carry, hh=hh, p0=p0, b_t=b_t, c_t=c_t, dec=dec):
            r = pl.ds(pl.multiple_of(p0 + k * 8, 8), 8)
            x_rows = xdt_t_s[r, :]
            ys = []
            for u in range(8):
                h_new = hin_ref[hh, k * 8 + u] * dec + x_rows[u:u + 1, :] * b_t
                hout_ref[hh, k * 8 + u] = h_new
                ys.append(jnp.sum(h_new * c_t, axis=0, keepdims=True))
            yt_s[r, :] = jnp.concatenate(ys, axis=0)
            return carry

        lax.fori_loop(0, P_A // 8, per_8p, 0)

    @pl.when(i == H_A // _SSD_HB - 1)
    def _():
        xa = xa_s[...]
        y = yt_s[...].T + dx_ref[...] * xa
        y = y * zs_ref[...]
        ya_ref[...] = (_rms_unit(y) * naw_ref[...]).astype(BF16)


_SSD_HB = 2


def _ssd_sample(p, conv_st_t, h_ssd_t, lw, nb, l, prev):
    full = lambda w: _const_spec((nb, w))
    cshape = (CONV_A - 1, nb, D_XBC)
    st = pl.BlockSpec((None, _SSD_HB, P_A, N_A, nb), lambda i: (l, i, 0, 0, 0))
    in_specs = [full(D_A), full(D_XBC), full(DT_PAD), _layer_spec(l, cshape),
                _const_spec((CONV_A, D_XBC)), _const_spec((1, D_XBC)), _const_spec((1, DT_PAD)),
                _const_spec((1, DT_PAD)), _const_spec((1, D_A)), _const_spec((1, D_A)), st]
    prev = () if prev is None else (prev,)
    in_specs += [pl.BlockSpec(memory_space=pl.ANY)] * len(prev)
    aliases = {len(in_specs) - 1: 2} if prev else {}
    scratch = [pltpu.VMEM((nb, D_A), F32), pltpu.VMEM((D_A, nb), F32), pltpu.VMEM((LANES, nb), F32),
               pltpu.VMEM((G_A * N_A, nb), F32), pltpu.VMEM((G_A * N_A, nb), F32),
               pltpu.VMEM((D_A, nb), F32)]
    return pl.pallas_call(
        functools.partial(_ssd_sample_body, n_prev=len(prev)), grid=(H_A // _SSD_HB,),
        in_specs=in_specs,
        out_specs=[pl.BlockSpec((nb, D_A), lambda i: (0, 0)),
                   pl.BlockSpec(cshape, lambda i: (0, 0, 0)), st],
        out_shape=[jax.ShapeDtypeStruct((nb, D_A), BF16), jax.ShapeDtypeStruct(cshape, F32),
                   jax.ShapeDtypeStruct((DEPTH, H_A, P_A, N_A, nb), F32)],
        scratch_shapes=scratch, input_output_aliases=aliases,
        compiler_params=_params(1), name="ssd_sample")(
            p["z"], p["xbc"], p["dt"], conv_st_t,
            lw["conv_a_w"], lw["conv_a_b"], lw["dt_bias"], lw["a_log"], lw["d_x"], lw["norm_a_w"],
            h_ssd_t, *prev)


_RET_BB = 16


def _ret_sample_body(*refs, nb, n_prev):
    q_ref, k_ref, v_ref, gs_ref, cos_ref, sina_ref, sinb_ref, sin_ref = refs[:8]
    yb_ref, sout_ref, q_t_s, k_t_s, yr_s = refs[8 + n_prev:]
    i = pl.program_id(0)

    @pl.when(i == 0)
    def _():
        cos_f, sin_a, sin_b = cos_ref[...], sina_ref[...], sinb_ref[...]
        q_t_s[...] = _rope(q_ref[...], cos_f, sin_a, sin_b).T.astype(BF16)
        k_t_s[...] = (_rope(k_ref[...], cos_f, sin_a, sin_b) * (DK_R ** -0.5)).T.astype(BF16)

    rows_i = lax.broadcasted_iota(jnp.int32, (nb, DV_R), 0)
    blk = pl.ds(pl.multiple_of(i * _RET_BB, _RET_BB), _RET_BB)
    v_blk = v_ref[blk, :]
    ys = [[] for _ in range(H_R)]
    for j in range(_RET_BB):
        onehot = jnp.where(rows_i == i * _RET_BB + j, 1.0, 0.0).astype(BF16)
        k_bc = jnp.dot(k_t_s[...], onehot, preferred_element_type=F32)
        q_bc = jnp.dot(q_t_s[...], onehot, preferred_element_type=F32)
        for h in range(H_R):
            kl = slice(h * DK_R, (h + 1) * DK_R)
            vl = slice(h * DV_R, (h + 1) * DV_R)
            s_new = math.exp(_LOG_GAMMA[h]) * sin_ref[j, h] + k_bc[kl, :] * v_blk[j:j + 1, vl]
            sout_ref[j, h] = s_new
            ys[h].append(jnp.sum(q_bc[kl, :] * s_new, axis=0, keepdims=True))
    for h in range(H_R):
        yr_s[blk, h * DV_R:(h + 1) * DV_R] = jnp.concatenate(ys[h], axis=0)

    @pl.when(i == nb // _RET_BB - 1)
    def _():
        for h in range(H_R):
            vl = slice(h * DV_R, (h + 1) * DV_R)
            yb_ref[:, vl] = (gs_ref[:, vl] * _rms_unit(yr_s[:, vl])).astype(BF16)


def _ret_sample(p, s_ret, rot, nb, l, prev):
    full = lambda w: _const_spec((nb, w))
    st = pl.BlockSpec((None, _RET_BB, H_R, DK_R, DV_R), lambda i: (l, i, 0, 0, 0))
    in_specs = [full(D_QK), full(D_QK), full(D_RV), full(D_RV),
                _const_spec((1, D_QK)), _const_spec((1, D_QK)), _const_spec((1, D_QK)), st]
    prev = () if prev is None else (prev,)
    in_specs += [pl.BlockSpec(memory_space=pl.ANY)] * len(prev)
    aliases = {len(in_specs) - 1: 1} if prev else {}
    return pl.pallas_call(
        functools.partial(_ret_sample_body, nb=nb, n_prev=len(prev)), grid=(nb // _RET_BB,),
        in_specs=in_specs, out_specs=[pl.BlockSpec((nb, D_RV), lambda i: (0, 0)), st],
        out_shape=[jax.ShapeDtypeStruct((nb, D_RV), BF16),
                   jax.ShapeDtypeStruct((DEPTH, nb, H_R, DK_R, DV_R), F32)],
        scratch_shapes=[pltpu.VMEM((D_QK, nb), BF16), pltpu.VMEM((D_QK, nb), BF16),
                        pltpu.VMEM((nb, D_RV), F32)],
        input_output_aliases=aliases, compiler_params=_params(1), name="ret_sample")(
            p["q"], p["k"], p["v"], p["g"], rot["cos"], rot["sin_a"], rot["sin_b"], s_ret, *prev)


def _rotary_tables(start, length):
    half = DK_R // 2
    pos = start + jnp.arange(length, dtype=F32)
    freqs = ROPE_BASE ** (-jnp.arange(half, dtype=F32) / half)
    ang = pos[:, None] * freqs[None, :]
    cos, sin, zero = jnp.cos(ang), jnp.sin(ang), jnp.zeros_like(ang)
    tile = lambda a, b: jnp.tile(jnp.concatenate([a, b], axis=-1), (1, H_R))
    lg = jnp.log1p(-jnp.exp2(-5.0 - jnp.arange(H_R, dtype=F32)))
    return {"cos": tile(cos, cos), "sin_a": tile(-sin, zero), "sin_b": tile(zero, sin),
            "lgx": jnp.repeat(lg, DK_R)[None, :]}


def _matmul_weights(w):
    w_in = w["w_in"]
    o_dt = D_A + D_XBC
    o_gates = w_in.shape[-1] - 3 * D_MODEL
    dt_cols = jnp.pad(w_in[..., o_dt:o_dt + H_A], ((0, 0), (0, 0), (0, DT_PAD - H_A)))
    w_proj = jnp.concatenate([w_in[..., :o_dt], w_in[..., o_dt + H_A:o_gates], dt_cols], axis=-1)
    out = {"w_proj": w_proj.astype(BF16), "w_gates": w_in[..., o_gates:].astype(BF16)}
    for name in ("w_glu", "w_br_a", "w_br_b", "w_br_c", "w_out", "w_up", "w_down"):
        out[name] = w[name].astype(BF16)
    return out


def _layer_weights(l, w, mats):
    pad_row = lambda a: jnp.pad(a, (0, DT_PAD - H_A))[None, :]
    ab_re, ab_im, bb_re, bb_im = _s5_discretise(
        w["s5_a_re"][l], w["s5_a_im"][l], w["s5_log_dt"][l], w["s5_b_re"][l], w["s5_b_im"][l])
    bblk = jnp.concatenate([_block_diag(bb_re), _block_diag(bb_im)], axis=-1)
    c_t = lambda c: _block_diag(jnp.transpose(c, (0, 2, 1)))
    return {
        "l": l, **mats,
        "norm_mix_w": w["norm_mix_w"][l][None, :],
        "conv_a_w": w["conv_a_w"][l], "conv_a_b": w["conv_a_b"][l][None, :],
        "dt_bias": pad_row(w["dt_bias"][l]), "a_log": pad_row(w["a_log"][l]),
        "d_x": jnp.repeat(w["d_a"][l], P_A)[None, :], "norm_a_w": w["norm_a_w"][l][None, :],
        "s5_ab_re": ab_re.reshape(1, D_S5), "s5_ab_im": ab_im.reshape(1, D_S5),
        "s5_bblk": bblk.astype(BF16),
        "s5_cre": c_t(w["s5_c_re"][l]).astype(BF16), "s5_cim": c_t(w["s5_c_im"][l]).astype(BF16),
        "s5_d": w["s5_d"][l][None, :], "b_glu": w["b_glu"][l][None, :],
        "norm_ffn_w": w["norm_ffn_w"][l][None, :],
        "conv_f_w": w["conv_f_w"][l], "conv_f_b": w["conv_f_b"][l][None, :],
    }


def _prompt_trunk(x, lws, nf, tm_proj, tsteps, tt, tm_merge, tm_ffn):
    nb, seq, _ = x.shape
    rot = _rotary_tables(0.0, seq)
    x2d = x.reshape(nb * seq, D_MODEL)
    zeros = jnp.zeros((nb, D_S5), F32)
    states, big = [], None
    for l, lw in enumerate(lws):
        p = _inproj(x2d, lw["norm_mix_w"], lw, rot, nb, seq, tm_proj, conv=True)
        ya, yb, *big = _mix_prompt(p, rot, lw, nb, seq, tsteps, l, big)
        yc, h_re, h_im = _s5(p["u"], zeros, zeros, lw, nb, seq, tt)
        x2d = _merge(x2d, ya, yb, yc, lw, nb, seq, tm_merge)
        x2d, fs = _ffn(x2d, None, lw, nf, nb, seq, tm_ffn, final=(l == len(lws) - 1))
        states.append((p["conv_state"], h_re.reshape(nb, G_C, P_C), h_im.reshape(nb, G_C, P_C), fs))
    cs, h_re, h_im, fs = [jnp.stack(s) for s in zip(*states)]
    return x2d.reshape(nb, seq, D_MODEL), [cs, big[0], big[1], h_re, h_im, fs]


def _sample_trunk(x, st, lws, nf):
    nb = x.shape[0]
    rot = _rotary_tables(float(PAST_LEN), 1)
    x2d = x.reshape(nb, D_MODEL)
    st_conv, st_ssd, st_ret, st_re, st_im, st_ffn = st
    st_ssd_t = jnp.transpose(st_ssd, (0, 2, 3, 4, 1))
    st_conv_t = jnp.transpose(st_conv, (0, 2, 1, 3))
    s5_t = lambda s: jnp.transpose(s, (0, 2, 3, 1)).reshape(DEPTH, D_S5, nb)
    st_re_t, st_im_t = s5_t(st_re), s5_t(st_im)
    states, hs, ss, fs = [], None, None, None
    for l, lw in enumerate(lws):
        p = _inproj(x2d, lw["norm_mix_w"], lw, rot, 1, nb, nb, conv=False)
        ya, cs, hs = _ssd_sample(p, st_conv_t, st_ssd_t, lw, nb, l, hs)
        yb, ss = _ret_sample(p, st_ret, rot, nb, l, ss)
        yc, h_re, h_im = _s5(p["u"], st_re_t[l], st_im_t[l], lw, nb, 1, 1, state_t=True)
        x2d = _merge(x2d, ya, yb, yc, lw, 1, nb, nb)
        x2d, fs = _ffn(x2d, st_ffn, lw, nf, nb, 1, nb, final=(l == len(lws) - 1), prev=fs)
        states.append((cs, h_re, h_im))
    cs, h_re, h_im = [jnp.stack(s) for s in zip(*states)]
    hs = jnp.transpose(hs, (0, 4, 1, 2, 3))
    cs = jnp.transpose(cs, (0, 2, 1, 3))
    s5_back = lambda s: jnp.transpose(s.reshape(DEPTH, G_C, P_C, nb), (0, 3, 1, 2))
    return x2d.reshape(nb, 1, D_MODEL), [cs, hs, ss, s5_back(h_re), s5_back(h_im), fs]


def kernel(x_prompt, x_sample, state_ssd_conv, state_ssd, state_ret, state_s5_re, state_s5_im,
           state_ffn_conv, norm_mix_w, w_in, conv_a_w, conv_a_b, dt_bias, a_log, d_a, norm_a_w,
           s5_a_re, s5_a_im, s5_log_dt, s5_b_re, s5_b_im, s5_c_re, s5_c_im, s5_d, w_glu, b_glu,
           w_br_a, w_br_b, w_br_c, w_out, norm_ffn_w, w_up, conv_f_w, conv_f_b, w_down, norm_f_w):
    w = dict(norm_mix_w=norm_mix_w, w_in=w_in, conv_a_w=conv_a_w, conv_a_b=conv_a_b,
             dt_bias=dt_bias, a_log=a_log, d_a=d_a, norm_a_w=norm_a_w, s5_a_re=s5_a_re,
             s5_a_im=s5_a_im, s5_log_dt=s5_log_dt, s5_b_re=s5_b_re, s5_b_im=s5_b_im,
             s5_c_re=s5_c_re, s5_c_im=s5_c_im, s5_d=s5_d, w_glu=w_glu, b_glu=b_glu,
             w_br_a=w_br_a, w_br_b=w_br_b, w_br_c=w_br_c, w_out=w_out, norm_ffn_w=norm_ffn_w,
             w_up=w_up, conv_f_w=conv_f_w, conv_f_b=conv_f_b, w_down=w_down)
    mats = _matmul_weights(w)
    lws = [_layer_weights(l, w, mats) for l in range(DEPTH)]
    nf = norm_f_w[None, :]
    seq = x_prompt.shape[1]
    y_p, p_st = _prompt_trunk(x_prompt, lws, nf, tm_proj=min(512, seq), tsteps=min(512, seq),
                              tt=min(64, seq), tm_merge=min(512, seq), tm_ffn=min(512, seq))
    y_s, s_st = _sample_trunk(
        x_sample, (state_ssd_conv, state_ssd, state_ret, state_s5_re, state_s5_im, state_ffn_conv),
        lws, nf)
    return (y_p, y_s, *p_st, *s_st)
```

```python
import functools
import math

import jax
import jax.numpy as jnp
import numpy as np
from jax import lax
from jax.experimental import pallas as pl
from jax.experimental.pallas import tpu as pltpu

F32 = jnp.float32
BF16 = jnp.bfloat16

D_MODEL = 1024
DEPTH = 2
PAST_LEN = 16384
H_A, P_A, N_A, G_A, CONV_A = 16, 64, 64, 2, 4
D_A = H_A * P_A
D_XBC = D_A + 2 * G_A * N_A
H_R, DK_R, DV_R = 8, 64, 128
D_QK = H_R * DK_R
D_RV = H_R * DV_R
ROPE_BASE = 10000.0
GS_C, G_C, P_C = 16, 64, 64
D_C = G_C * GS_C
D_S5 = G_C * P_C
D_FF = 2816
CONV_F = 3
CHUNK = 128
EPS = 1e-6
LANES = 128
DT_PAD = LANES
HPG = H_A // G_A
VMEM_LIMIT = 56 * 1024 * 1024

_LOG_GAMMA = [math.log1p(-(2.0 ** (-5.0 - h))) for h in range(H_R)]

_SEGS = (("z", D_A), ("xbc", D_XBC), ("q", D_QK), ("k", D_QK),
         ("v", D_RV), ("g", D_RV), ("u", D_C), ("dt", DT_PAD))
_SEG_OFF = np.cumsum([0] + [w for _, w in _SEGS]).tolist()
D_PROJ = _SEG_OFF[-1]


def _rms_unit(x):
    return x * lax.rsqrt(jnp.mean(x * x, axis=-1, keepdims=True) + EPS)


def _silu(x):
    return x * jax.nn.sigmoid(x)


def _softplus(x):
    return jnp.maximum(x, 0.0) + jnp.log1p(jnp.exp(-jnp.abs(x)))


def _bdot(a, b):
    return jnp.dot(a.astype(BF16), b.astype(BF16), preferred_element_type=F32)


def _split3(x):
    hi = x.astype(BF16)
    r1 = x - hi.astype(F32)
    mid = r1.astype(BF16)
    return hi, mid, (r1 - mid.astype(F32)).astype(BF16)


def _tile3(m01, axis):
    return jnp.concatenate([m01] * 3, axis=axis)


def _dot01(a, b, f32_side):
    if f32_side == "lhs":
        return jnp.dot(jnp.concatenate(_split3(a), axis=1), b, preferred_element_type=F32)
    return jnp.dot(a, jnp.concatenate(_split3(b), axis=0), preferred_element_type=F32)


def _const_spec(shape):
    nd = len(shape)
    return pl.BlockSpec(shape, lambda *_: (0,) * nd, pipeline_mode=pl.Buffered(1))


def _layer_spec(l, shape):
    nd = len(shape)
    return pl.BlockSpec((None,) + tuple(shape), lambda *_: (l,) + (0,) * nd,
                        pipeline_mode=pl.Buffered(1))


def _params(n_grid):
    return pltpu.CompilerParams(dimension_semantics=("arbitrary",) * n_grid,
                                vmem_limit_bytes=VMEM_LIMIT)


_PROJ_SUB = 128


def _inproj_body(x_ref, nw_ref, cw_ref, cb_ref, cos_ref, sina_ref, sinb_ref, w_ref, *refs,
                 tm, conv):
    n = len(_SEGS)
    if conv:
        cs_ref, xp_s = refs[n:]

        @pl.when(pl.program_id(1) == 0)
        def _():
            xp_s[0:8, :] = jnp.zeros((8, D_XBC), F32)

    for r in range(0, tm, _PROJ_SUB):
        rs = slice(r, r + _PROJ_SUB)
        h = (_rms_unit(x_ref[rs, :]) * nw_ref[...]).astype(BF16)
        for (name, width), off, o_ref in zip(_SEGS, _SEG_OFF, refs[:n]):
            y = jnp.dot(h, w_ref[:, off:off + width], preferred_element_type=F32)
            if name in ("z", "g"):
                y = _silu(y)
            elif name in ("q", "k") and conv:
                y = _rope(y, cos_ref[rs, :], sina_ref[rs, :], sinb_ref[rs, :])
                if name == "k":
                    y = y * (DK_R ** -0.5)
            elif name == "xbc" and conv:
                xp_s[8 + r:8 + r + _PROJ_SUB, :] = y
                y = cb_ref[...]
                for tap in range(CONV_A):
                    y = y + xp_s[5 + tap + r:5 + tap + r + _PROJ_SUB, :] * cw_ref[tap:tap + 1, :]
                y = _silu(y)
            o_ref[rs, :] = y.astype(o_ref.dtype)
    if conv:
        tail = xp_s[tm + 5:tm + 8, :]
        cs_ref[...] = tail
        xp_s[5:8, :] = tail


def _inproj(x2d, nw, lw, rot, nb, seq, tm, conv):
    nt = seq // tm
    tab = (pl.BlockSpec((tm, D_QK), lambda b, i: (i, 0)) if conv
           else _const_spec(rot["cos"].shape))
    in_specs = [pl.BlockSpec((tm, D_MODEL), lambda b, i: (b * nt + i, 0)),
                _const_spec((1, D_MODEL)), _const_spec((CONV_A, D_XBC)), _const_spec((1, D_XBC)),
                tab, tab, tab, _layer_spec(lw["l"], (D_MODEL, D_PROJ))]
    out_specs, out_shape, scratch = [], [], []
    for name, width in _SEGS:
        dtype = BF16 if (name == "v" and conv) else F32
        out_specs.append(pl.BlockSpec((tm, width), lambda b, i: (b * nt + i, 0)))
        out_shape.append(jax.ShapeDtypeStruct((nb * seq, width), dtype))
    if conv:
        out_specs.append(pl.BlockSpec((None, CONV_A - 1, D_XBC), lambda b, i: (b, 0, 0)))
        out_shape.append(jax.ShapeDtypeStruct((nb, CONV_A - 1, D_XBC), F32))
        scratch.append(pltpu.VMEM((8 + tm, D_XBC), F32))
    outs = pl.pallas_call(
        functools.partial(_inproj_body, tm=tm, conv=conv), grid=(nb, nt), in_specs=in_specs,
        out_specs=out_specs, out_shape=out_shape, scratch_shapes=scratch,
        compiler_params=_params(2), name="inproj")(
            x2d, nw, lw["conv_a_w"], lw["conv_a_b"], rot["cos"], rot["sin_a"], rot["sin_b"],
            lw["w_proj"])
    res = {name: o for (name, _), o in zip(_SEGS, outs)}
    if conv:
        res["conv_state"] = outs[-1]
    return res


def _rope(x, cos_f, sin_a, sin_b):
    half = DK_R // 2
    return (x * cos_f + pltpu.roll(x, D_QK - half, 1) * sin_a
            + pltpu.roll(x, half, 1) * sin_b)


def _head_expand():
    lo = lax.broadcasted_iota(jnp.int32, (LANES, D_A), 0) * P_A
    c = lax.broadcasted_iota(jnp.int32, (LANES, D_A), 1)
    return jnp.where(c >= lo, jnp.where(c < lo + P_A, 1.0, 0.0), 0.0).astype(BF16)


def _mix_body(*refs, tsteps, nt, n_prev):
    (zs_ref, xc_ref, dt_ref, q_ref, k_ref, v_ref, gs_ref, lgx_ref,
     dtb_ref, alog_ref, dx_ref, naw_ref) = refs[:12]
    (ya_ref, yb_ref, hs_ref, ss_ref,
     ht_s, s_s, intra_s, qdec_s, kdect_s) = refs[12 + n_prev:]
    i = pl.program_id(1)
    first = jnp.logical_and(pl.program_id(0) == 0, i == 0)
    rows_i = lax.broadcasted_iota(jnp.int32, (CHUNK, CHUNK), 0)
    cols_i = lax.broadcasted_iota(jnp.int32, (CHUNK, CHUNK), 1)
    causal = rows_i >= cols_i
    lane_lo = cols_i < LANES // 2

    @pl.when(first)
    def _():
        rel = (rows_i - cols_i).astype(F32)
        for h in range(H_R):
            intra_s[h] = jnp.where(causal, jnp.exp(jnp.maximum(rel, 0.0) * _LOG_GAMMA[h]), 0.0)
        ri = lax.broadcasted_iota(jnp.int32, (CHUNK, D_QK), 0).astype(F32)
        lgx = lgx_ref[...]
        qdec_s[...] = jnp.exp((ri + 1.0) * lgx)
        kdect_s[...] = jnp.exp((CHUNK - 1.0 - ri) * lgx).T

    @pl.when(i == 0)
    def _():
        ht_s[...] = jnp.zeros_like(ht_s)
        s_s[...] = jnp.zeros_like(s_s)

    tril = _tile3(jnp.where(causal, 1.0, 0.0).astype(BF16), 1)
    expand = _tile3(_head_expand(), 0)
    a_neg = -jnp.exp(alog_ref[...])

    for c in range(tsteps // CHUNK):
        r0 = c * CHUNK
        rows = slice(r0, r0 + CHUNK)
        xa = xc_ref[rows, :D_A]
        bm = xc_ref[rows, D_A:D_A + G_A * N_A]
        cm = xc_ref[rows, D_A + G_A * N_A:]
        dt = _softplus(dt_ref[rows, :] + dtb_ref[...])
        cum = _dot01(tril, dt * a_neg, "rhs")
        cum_t = cum.T
        dt_t = dt.T
        cum_last = cum[CHUNK - 1:CHUNK, :]
        scale = jnp.concatenate(
            [dt * jnp.exp(cum_last - cum), jnp.exp(cum),
             jnp.broadcast_to(jnp.exp(cum_last), (8, LANES))], axis=0)
        scale_x = _dot01(scale, expand, "lhs")
        w_x = scale_x[:CHUNK]
        ecum_x = scale_x[CHUNK:2 * CHUNK]
        dec_x = scale_x[2 * CHUNK:2 * CHUNK + 1]
        bm_t = bm.T
        ys = []
        for g in range(G_A):
            gl = slice(g * N_A, (g + 1) * N_A)
            hl = slice(g * HPG * P_A, (g + 1) * HPG * P_A)
            cg = cm[:, gl]
            scores = lax.dot_general(cg.astype(BF16), bm[:, gl].astype(BF16),
                                     (((1,), (1,)), ((), ())), preferred_element_type=F32)
            scores = jnp.where(causal, scores, 0.0)
            ht_g = ht_s[:, hl]
            y_inter = _bdot(cg, ht_g) * ecum_x[:, hl]
            for pair in range(HPG // 2):
                pl_ = slice(g * HPG * P_A + pair * LANES, g * HPG * P_A + (pair + 1) * LANES)
                x_pair = xa[:, pl_]
                sps = []
                for h in (g * HPG + 2 * pair, g * HPG + 2 * pair + 1):
                    seg = cum[:, h:h + 1] - cum_t[h:h + 1, :]
                    sps.append(scores * jnp.exp(jnp.minimum(seg, 0.0)) * dt_t[h:h + 1, :])
                rhs = jnp.concatenate([jnp.where(lane_lo, x_pair, 0.0),
                                       jnp.where(lane_lo, 0.0, x_pair)], axis=0)
                ys.append(_bdot(jnp.concatenate(sps, axis=1), rhs)
                          + y_inter[:, pair * LANES:(pair + 1) * LANES])
            ht_s[:, hl] = dec_x[:, hl] * ht_g + _bdot(bm_t[gl, :], xa[:, hl] * w_x[:, hl])
        y = jnp.concatenate(ys, axis=1) + dx_ref[...] * xa
        y = y * zs_ref[rows, :]
        ya_ref[rows, :] = (_rms_unit(y) * naw_ref[...]).astype(BF16)

        qr = q_ref[rows, :]
        k_t = k_ref[rows, :].T
        kd_t = k_t * kdect_s[...]
        qd = qr * qdec_s[...]
        for pair in range(H_R // 2):
            pl_ = slice(pair * LANES, (pair + 1) * LANES)
            s_pair = s_s[pl_, :]
            for h, keep in ((2 * pair, lane_lo), (2 * pair + 1, jnp.logical_not(lane_lo))):
                kl = slice(h * DK_R, (h + 1) * DK_R)
                vl = slice(h * DV_R, (h + 1) * DV_R)
                v_h = v_ref[rows, vl]
                sc = _bdot(jnp.where(keep, qr[:, pl_], 0.0), k_t[pl_, :]) * intra_s[h]
                y_h = _bdot(jnp.concatenate([sc, jnp.where(keep, qd[:, pl_], 0.0)], axis=1),
                            jnp.concatenate([v_h, s_pair.astype(BF16)], axis=0))
                s_s[kl, :] = (math.exp(CHUNK * _LOG_GAMMA[h]) * s_s[kl, :]
                              + _bdot(kd_t[kl, :], v_h))
                yb_ref[rows, vl] = (gs_ref[rows, vl] * _rms_unit(y_h)).astype(BF16)

    @pl.when(i == nt - 1)
    def _():
        for h in range(H_A):
            hs_ref[h] = ht_s[:, h * P_A:(h + 1) * P_A].T
        for h in range(H_R):
            ss_ref[h] = s_s[h * DK_R:(h + 1) * DK_R, :]


def _layer_block(l, tail):
    zeros = (0,) * len(tail)
    return pl.BlockSpec((None, None) + tail, lambda b, *_: (l, b) + zeros)


def _mix_prompt(p, rot, lw, nb, seq, tsteps, l, prev):
    nt = seq // tsteps
    row = lambda w: pl.BlockSpec((tsteps, w), lambda b, i: (b * nt + i, 0))
    in_specs = [row(D_A), row(D_XBC), row(DT_PAD), row(D_QK), row(D_QK), row(D_RV), row(D_RV),
                _const_spec((1, D_QK)), _const_spec((1, DT_PAD)),
                _const_spec((1, DT_PAD)), _const_spec((1, D_A)), _const_spec((1, D_A))]
    prev = () if prev is None else tuple(prev)
    in_specs += [pl.BlockSpec(memory_space=pl.ANY)] * len(prev)
    aliases = {len(in_specs) - len(prev) + k: 2 + k for k in range(len(prev))}
    out_specs = [row(D_A), row(D_RV),
                 _layer_block(l, (H_A, P_A, N_A)), _layer_block(l, (H_R, DK_R, DV_R))]
    out_shape = [jax.ShapeDtypeStruct((nb * seq, D_A), BF16),
                 jax.ShapeDtypeStruct((nb * seq, D_RV), BF16),
                 jax.ShapeDtypeStruct((DEPTH, nb, H_A, P_A, N_A), F32),
                 jax.ShapeDtypeStruct((DEPTH, nb, H_R, DK_R, DV_R), F32)]
    scratch = [pltpu.VMEM((N_A, D_A), F32),
               pltpu.VMEM((D_QK, DV_R), F32), pltpu.VMEM((H_R, CHUNK, CHUNK), F32),
               pltpu.VMEM((CHUNK, D_QK), F32), pltpu.VMEM((D_QK, CHUNK), F32)]
    return pl.pallas_call(
        functools.partial(_mix_body, tsteps=tsteps, nt=nt, n_prev=len(prev)), grid=(nb, nt),
        in_specs=in_specs, out_specs=out_specs, out_shape=out_shape, scratch_shapes=scratch,
        input_output_aliases=aliases, compiler_params=_params(2), name="mix_prompt")(
            p["z"], p["xbc"], p["dt"], p["q"], p["k"], p["v"], p["g"],
            rot["lgx"], lw["dt_bias"], lw["a_log"], lw["d_x"], lw["norm_a_w"], *prev)


def _s5_disc_body(are_ref, aim_ref, ldt_ref, bre_ref, bim_ref,
                  abre_ref, abim_ref, bbre_ref, bbim_ref):
    ar, ai = are_ref[...], aim_ref[...]
    dt = jnp.exp(ldt_ref[...])
    mag = jnp.exp(ar * dt)
    ab_re = mag * jnp.cos(ai * dt)
    ab_im = mag * jnp.sin(ai * dt)
    den = ar * ar + ai * ai
    num_re = ab_re - 1.0
    coef_re = (num_re * ar + ab_im * ai) / den
    coef_im = (ab_im * ar - num_re * ai) / den
    abre_ref[...] = ab_re
    abim_ref[...] = ab_im
    for c in range(GS_C):
        cl = slice(c * P_C, (c + 1) * P_C)
        br, bi = bre_ref[:, cl], bim_ref[:, cl]
        bbre_ref[:, cl] = coef_re * br - coef_im * bi
        bbim_ref[:, cl] = coef_re * bi + coef_im * br


def _s5_discretise(a_re, a_im, log_dt, b_re, b_im):
    gp = jax.ShapeDtypeStruct((G_C, P_C), F32)
    gcp = jax.ShapeDtypeStruct((G_C, GS_C * P_C), F32)
    b_t = lambda b: jnp.transpose(b, (0, 2, 1)).reshape(G_C, GS_C * P_C)
    ab_re, ab_im, bb_re, bb_im = pl.pallas_call(
        _s5_disc_body, out_shape=[gp, gp, gcp, gcp], name="s5_disc")(
            a_re, a_im, log_dt.reshape(G_C, 1), b_t(b_re), b_t(b_im))
    return ab_re, ab_im, bb_re.reshape(G_C, GS_C, P_C), bb_im.reshape(G_C, GS_C, P_C)


_S5_GB = LANES // GS_C
_S5_NBLK = G_C // _S5_GB
_S5_SB = _S5_GB * P_C


def _block_diag(m):
    g, r, c = m.shape
    m = m.reshape(_S5_NBLK, _S5_GB, r, c)
    eye = jnp.eye(_S5_GB, dtype=m.dtype)
    return jnp.einsum("jgrc,gk->jgrkc", m, eye).reshape(_S5_NBLK, _S5_GB * r, _S5_GB * c)


def _s5_body(u_ref, h0re_ref, h0im_ref, are_ref, aim_ref, bblk_ref, cre_ref, cim_ref,
             d_ref, wglu_ref, bglu_ref, yc_ref, hre_ref, him_ref,
             xre_s, xim_s, sre_s, sim_s, *, nb, tt, state_t):
    i = pl.program_id(0)
    rows = nb * tt

    @pl.when(i == 0)
    def _():
        sre_s[...] = h0re_ref[...].T if state_t else h0re_ref[...]
        sim_s[...] = h0im_ref[...].T if state_t else h0im_ref[...]

    u = u_ref[...]
    if tt > 1:
        u = jnp.swapaxes(u, 0, 1).reshape(rows, D_C)
    ub = u.astype(BF16)

    ys = []
    for j in range(_S5_NBLK):
        sl = slice(j * _S5_SB, (j + 1) * _S5_SB)
        bu = jnp.dot(ub[:, j * LANES:(j + 1) * LANES], bblk_ref[j], preferred_element_type=F32)
        xre_s[:, sl] = bu[:, :_S5_SB]
        xim_s[:, sl] = bu[:, _S5_SB:]
        ar = jnp.broadcast_to(are_ref[:, sl], (nb, _S5_SB))
        ai = jnp.broadcast_to(aim_ref[:, sl], (nb, _S5_SB))
        xr, xi = sre_s[:, sl], sim_s[:, sl]
        for t in range(tt):
            r = slice(t * nb, (t + 1) * nb)
            xr, xi = (ar * xr - ai * xi + xre_s[r, sl], ar * xi + ai * xr + xim_s[r, sl])
            xre_s[r, sl] = xr
            xim_s[r, sl] = xi
        sre_s[:, sl] = xr
        sim_s[:, sl] = xi
        ys.append(_bdot(xre_s[:, sl], cre_ref[j]) - _bdot(xim_s[:, sl], cim_ref[j]))
    yc = jnp.concatenate(ys, axis=1)
    parts = []
    half = rows // 2 if rows >= 256 else rows
    for r in range(0, rows, half):
        yh = jax.nn.gelu(yc[r:r + half] + d_ref[...] * u[r:r + half])
        parts.append(yh * jax.nn.sigmoid(_bdot(yh, wglu_ref[...]) + bglu_ref[...]))
    y = jnp.concatenate(parts, axis=0)
    if tt > 1:
        y = jnp.swapaxes(y.reshape(tt, nb, D_C), 0, 1)
    yc_ref[...] = y.astype(BF16)
    hre_ref[...] = sre_s[...].T if state_t else sre_s[...]
    him_ref[...] = sim_s[...].T if state_t else sim_s[...]


def _s5(u, h0_re, h0_im, lw, nb, seq, tt, state_t=False):
    rows = tt * nb
    sshape = (D_S5, nb) if state_t else (nb, D_S5)
    if tt > 1:
        assert nb % 8 == 0 and tt % 8 == 0
        u = u.reshape(nb, seq, D_C)
        io_spec = pl.BlockSpec((nb, tt, D_C), lambda i: (0, i, 0))
    else:
        io_spec = pl.BlockSpec((rows, D_C), lambda i: (i, 0))
    in_specs = [io_spec,
                _const_spec(sshape), _const_spec(sshape),
                _const_spec((1, D_S5)), _const_spec((1, D_S5)),
                _const_spec((_S5_NBLK, LANES, 2 * _S5_SB)),
                _const_spec((_S5_NBLK, _S5_SB, LANES)), _const_spec((_S5_NBLK, _S5_SB, LANES)),
                _const_spec((1, D_C)), _layer_spec(lw["l"], (D_C, D_C)), _const_spec((1, D_C))]
    st = pl.BlockSpec(sshape, lambda i: (0, 0))
    yc, h_re, h_im = pl.pallas_call(
        functools.partial(_s5_body, nb=nb, tt=tt, state_t=state_t), grid=(seq // tt,),
        in_specs=in_specs, out_specs=[io_spec, st, st],
        out_shape=[jax.ShapeDtypeStruct(u.shape, BF16),
                   jax.ShapeDtypeStruct(sshape, F32), jax.ShapeDtypeStruct(sshape, F32)],
        scratch_shapes=[pltpu.VMEM((rows, D_S5), F32), pltpu.VMEM((rows, D_S5), F32),
                        pltpu.VMEM((nb, D_S5), F32), pltpu.VMEM((nb, D_S5), F32)],
        compiler_params=_params(1), name="s5")(
            u, h0_re, h0_im, lw["s5_ab_re"], lw["s5_ab_im"], lw["s5_bblk"],
            lw["s5_cre"], lw["s5_cim"], lw["s5_d"], lw["w_glu"], lw["b_glu"])
    return yc.reshape(nb * seq, D_C), h_re, h_im


def _merge_body(x_ref, nw_ref, ya_ref, yb_ref, yc_ref, wg_ref, wa_ref, wb_ref, wc_ref, wo_ref,
                o_ref):
    x = x_ref[...]
    h = (_rms_unit(x) * nw_ref[...]).astype(BF16)
    merged = None
    for k, (y_ref, w_ref) in enumerate(((ya_ref, wa_ref), (yb_ref, wb_ref), (yc_ref, wc_ref))):
        gate = jax.nn.sigmoid(jnp.dot(h, wg_ref[:, k * D_MODEL:(k + 1) * D_MODEL],
                                      preferred_element_type=F32))
        term = gate * _bdot(y_ref[...], w_ref[...])
        merged = term if merged is None else merged + term
    o_ref[...] = x + _bdot(merged, wo_ref[...])


def _merge(x2d, ya, yb, yc, lw, nb, seq, tm):
    nt = seq // tm
    row = lambda w: pl.BlockSpec((tm, w), lambda b, i: (b * nt + i, 0))
    wspec = _layer_spec(lw["l"], (D_MODEL, D_MODEL))
    return pl.pallas_call(
        _merge_body, grid=(nb, nt),
        in_specs=[row(D_MODEL), _const_spec((1, D_MODEL)), row(D_A), row(D_RV), row(D_C),
                  _layer_spec(lw["l"], (D_MODEL, 3 * D_MODEL)), wspec, wspec, wspec, wspec],
        out_specs=row(D_MODEL), out_shape=jax.ShapeDtypeStruct((nb * seq, D_MODEL), F32),
        compiler_params=_params(2), name="merge")(
            x2d, lw["norm_mix_w"], ya, yb, yc, lw["w_gates"],
            lw["w_br_a"], lw["w_br_b"], lw["w_br_c"], lw["w_out"])


_FFN_CW = 256


_FFN_SUB = 512


def _ffn_prompt_body(x_ref, nw_ref, wup_ref, cw_ref, cb_ref, wdn_ref, nf_ref,
                     o_ref, cs_ref, up_s, act_s, *, tm, final):
    @pl.when(pl.program_id(1) == 0)
    def _():
        up_s[0:8, :] = jnp.zeros((8, 2 * D_FF), F32)

    sub = min(tm, _FFN_SUB)
    for r in range(0, tm, sub):
        rs = slice(r, r + sub)
        x = x_ref[rs, :]
        hf = (_rms_unit(x) * nw_ref[...]).astype(BF16)
        up_s[8 + r:8 + r + sub, :] = jnp.dot(hf, wup_ref[...], preferred_element_type=F32)
        for c in range(0, D_FF, _FFN_CW):
            def conv(c0):
                cl = slice(c0, c0 + _FFN_CW)
                y = cb_ref[:, cl]
                for tap in range(CONV_F):
                    y = y + up_s[6 + tap + r:6 + tap + r + sub, cl] * cw_ref[tap:tap + 1, cl]
                return y
            act_s[rs, c:c + _FFN_CW] = (_silu(conv(c)) * conv(D_FF + c)).astype(BF16)
        out = x + jnp.dot(act_s[rs, :], wdn_ref[...], preferred_element_type=F32)
        o_ref[rs, :] = _rms_unit(out) * nf_ref[...] if final else out
    tail = up_s[tm + 6:tm + 8, :]
    cs_ref[...] = tail
    up_s[6:8, :] = tail


def _ffn_sample_body(x_ref, st_ref, nw_ref, wup_ref, cw_ref, cb_ref, wdn_ref, nf_ref, *refs,
                     final):
    o_ref, cs_ref, act_s = refs[-3:]
    x = x_ref[...]
    hf = (_rms_unit(x) * nw_ref[...]).astype(BF16)
    up = jnp.dot(hf, wup_ref[...], preferred_element_type=F32)
    prev2, prev1 = st_ref[:, 0, :], st_ref[:, 1, :]
    cs_ref[:, 0, :] = prev1
    cs_ref[:, 1, :] = up
    for c in range(0, D_FF, _FFN_CW):
        def conv(c0):
            cl = slice(c0, c0 + _FFN_CW)
            return (cb_ref[:, cl] + prev2[:, cl] * cw_ref[0:1, cl] + prev1[:, cl] * cw_ref[1:2, cl]
                    + up[:, cl] * cw_ref[2:3, cl])
        act_s[:, c:c + _FFN_CW] = (_silu(conv(c)) * conv(D_FF + c)).astype(BF16)
    out = x + jnp.dot(act_s[...], wdn_ref[...], preferred_element_type=F32)
    o_ref[...] = _rms_unit(out) * nf_ref[...] if final else out


def _ffn(x2d, state, lw, nf, nb, seq, tm, final, prev=None):
    wspecs = [_const_spec((1, D_MODEL)), _layer_spec(lw["l"], (D_MODEL, 2 * D_FF)),
              _const_spec((CONV_F, 2 * D_FF)), _const_spec((1, 2 * D_FF)),
              _layer_spec(lw["l"], (D_FF, D_MODEL)), _const_spec((1, D_MODEL))]
    wargs = (lw["norm_ffn_w"], lw["w_up"], lw["conv_f_w"], lw["conv_f_b"], lw["w_down"], nf)
    act = pltpu.VMEM((tm, D_FF), BF16)
    if state is None:
        nt = seq // tm
        row = pl.BlockSpec((tm, D_MODEL), lambda b, i: (b * nt + i, 0))
        return pl.pallas_call(
            functools.partial(_ffn_prompt_body, tm=tm, final=final), grid=(nb, nt),
            in_specs=[row] + wspecs,
            out_specs=[row, pl.BlockSpec((None, CONV_F - 1, 2 * D_FF), lambda b, i: (b, 0, 0))],
            out_shape=[jax.ShapeDtypeStruct((nb * seq, D_MODEL), F32),
                       jax.ShapeDtypeStruct((nb, CONV_F - 1, 2 * D_FF), F32)],
            scratch_shapes=[pltpu.VMEM((8 + tm, 2 * D_FF), F32), act],
            compiler_params=_params(2), name="ffn_prompt")(x2d, *wargs)
    l = lw["l"]
    st = pl.BlockSpec((None, tm, CONV_F - 1, 2 * D_FF), lambda i: (l, i, 0, 0))
    prev = () if prev is None else (prev,)
    in_specs = ([pl.BlockSpec((tm, D_MODEL), lambda i: (i, 0)), st] + wspecs
                + [pl.BlockSpec(memory_space=pl.ANY)] * len(prev))
    return pl.pallas_call(
        functools.partial(_ffn_sample_body, final=final), grid=(nb // tm,), in_specs=in_specs,
        out_specs=[pl.BlockSpec((tm, D_MODEL), lambda i: (i, 0)), st],
        out_shape=[jax.ShapeDtypeStruct((nb, D_MODEL), F32),
                   jax.ShapeDtypeStruct(state.shape, F32)],
        scratch_shapes=[act], input_output_aliases={len(in_specs) - 1: 1} if prev else {},
        compiler_params=_params(1), name="ffn_sample")(x2d, state, *wargs, *prev)


def _ssd_sample_body(*refs, n_prev):
    (zs_ref, xbc_ref, dt_ref, cst_ref, cw_ref, cb_ref, dtb_ref, alog_ref, dx_ref, naw_ref,
     hin_ref) = refs[:11]
    (ya_ref, cso_ref, hout_ref, xa_s, xdt_t_s, dec_t_s, bm_t_s, cm_t_s, yt_s) = refs[11 + n_prev:]
    i = pl.program_id(0)

    @pl.when(i == 0)
    def _():
        acc = cb_ref[...]
        for tap in range(CONV_A - 1):
            acc = acc + cst_ref[tap] * cw_ref[tap:tap + 1, :]
        acc = acc + xbc_ref[...] * cw_ref[CONV_A - 1:CONV_A, :]
        for tap in range(CONV_A - 2):
            cso_ref[tap] = cst_ref[tap + 1]
        cso_ref[CONV_A - 2] = xbc_ref[...]
        xc = _silu(acc)
        xa = xc[:, :D_A]
        xa_s[...] = xa
        bm_t_s[...] = xc[:, D_A:D_A + G_A * N_A].T
        cm_t_s[...] = xc[:, D_A + G_A * N_A:].T
        dt = _softplus(dt_ref[...] + dtb_ref[...])
        dec_t_s[...] = jnp.exp(dt * -jnp.exp(alog_ref[...])).T
        dt_x = _dot01(dt, _tile3(_head_expand(), 0), "lhs")
        xdt_t_s[...] = (xa * dt_x).T

    for hh in range(_SSD_HB):
        h = i * _SSD_HB + hh
        g0 = pl.multiple_of((h // HPG) * N_A, N_A)
        p0 = pl.multiple_of(h * P_A, P_A)
        b_t = bm_t_s[pl.ds(g0, N_A), :]
        c_t = cm_t_s[pl.ds(g0, N_A), :]
        dec = dec_t_s[pl.ds(h, 1), :]

        def per_8p(k, carry, hh=hh, p0=p0, b_t=b_t, c_t=c_t, dec=dec):
            r = pl.ds(pl.multiple_of(p0 + k * 8, 8), 8)
            x_rows = xdt_t_s[r, :]
            ys = []
            for u in range(8):
                h_new = hin_ref[hh, k * 8 + u] * dec + x_rows[u:u + 1, :] * b_t
                hout_ref[hh, k * 8 + u] = h_new
                ys.append(jnp.sum(h_new * c_t, axis=0, keepdims=True))
            yt_s[r, :] = jnp.concatenate(ys, axis=0)
            return carry

        lax.fori_loop(0, P_A // 8, per_8p, 0)

    @pl.when(i == H_A // _SSD_HB - 1)
    def _():
        xa = xa_s[...]
        y = yt_s[...].T + dx_ref[...] * xa
        y = y * zs_ref[...]
        ya_ref[...] = (_rms_unit(y) * naw_ref[...]).astype(BF16)


_SSD_HB = 4


def _ssd_sample(p, conv_st_t, h_ssd_t, lw, nb, l, prev):
    full = lambda w: _const_spec((nb, w))
    cshape = (CONV_A - 1, nb, D_XBC)
    st = pl.BlockSpec((None, _SSD_HB, P_A, N_A, nb), lambda i: (l, i, 0, 0, 0))
    in_specs = [full(D_A), full(D_XBC), full(DT_PAD), _layer_spec(l, cshape),
                _const_spec((CONV_A, D_XBC)), _const_spec((1, D_XBC)), _const_spec((1, DT_PAD)),
                _const_spec((1, DT_PAD)), _const_spec((1, D_A)), _const_spec((1, D_A)), st]
    prev = () if prev is None else (prev,)
    in_specs += [pl.BlockSpec(memory_space=pl.ANY)] * len(prev)
    aliases = {len(in_specs) - 1: 2} if prev else {}
    scratch = [pltpu.VMEM((nb, D_A), F32), pltpu.VMEM((D_A, nb), F32), pltpu.VMEM((LANES, nb), F32),
               pltpu.VMEM((G_A * N_A, nb), F32), pltpu.VMEM((G_A * N_A, nb), F32),
               pltpu.VMEM((D_A, nb), F32)]
    return pl.pallas_call(
        functools.partial(_ssd_sample_body, n_prev=len(prev)), grid=(H_A // _SSD_HB,),
        in_specs=in_specs,
        out_specs=[pl.BlockSpec((nb, D_A), lambda i: (0, 0)),
                   pl.BlockSpec(cshape, lambda i: (0, 0, 0)), st],
        out_shape=[jax.ShapeDtypeStruct((nb, D_A), BF16), jax.ShapeDtypeStruct(cshape, F32),
                   jax.ShapeDtypeStruct((DEPTH, H_A, P_A, N_A, nb), F32)],
        scratch_shapes=scratch, input_output_aliases=aliases,
        compiler_params=_params(1), name="ssd_sample")(
            p["z"], p["xbc"], p["dt"], conv_st_t,
            lw["conv_a_w"], lw["conv_a_b"], lw["dt_bias"], lw["a_log"], lw["d_x"], lw["norm_a_w"],
            h_ssd_t, *prev)


_RET_BB = 32


def _ret_sample_body(*refs, nb, n_prev):
    q_ref, k_ref, v_ref, gs_ref, cos_ref, sina_ref, sinb_ref, sin_ref = refs[:8]
    yb_ref, sout_ref, q_t_s, k_t_s, yr_s = refs[8 + n_prev:]
    i = pl.program_id(0)

    @pl.when(i == 0)
    def _():
        cos_f, sin_a, sin_b = cos_ref[...], sina_ref[...], sinb_ref[...]
        q_t_s[...] = _rope(q_ref[...], cos_f, sin_a, sin_b).T.astype(BF16)
        k_t_s[...] = (_rope(k_ref[...], cos_f, sin_a, sin_b) * (DK_R ** -0.5)).T.astype(BF16)

    rows_i = lax.broadcasted_iota(jnp.int32, (nb, DV_R), 0)
    blk = pl.ds(pl.multiple_of(i * _RET_BB, _RET_BB), _RET_BB)
    v_blk = v_ref[blk, :]
    ys = [[] for _ in range(H_R)]
    for j in range(_RET_BB):
        onehot = jnp.where(rows_i == i * _RET_BB + j, 1.0, 0.0).astype(BF16)
        k_bc = jnp.dot(k_t_s[...], onehot, preferred_element_type=F32)
        q_bc = jnp.dot(q_t_s[...], onehot, preferred_element_type=F32)
        for h in range(H_R):
            kl = slice(h * DK_R, (h + 1) * DK_R)
            vl = slice(h * DV_R, (h + 1) * DV_R)
            s_new = math.exp(_LOG_GAMMA[h]) * sin_ref[j, h] + k_bc[kl, :] * v_blk[j:j + 1, vl]
            sout_ref[j, h] = s_new
            ys[h].append(jnp.sum(q_bc[kl, :] * s_new, axis=0, keepdims=True))
    for h in range(H_R):
        yr_s[blk, h * DV_R:(h + 1) * DV_R] = jnp.concatenate(ys[h], axis=0)

    @pl.when(i == nb // _RET_BB - 1)
    def _():
        for h in range(H_R):
            vl = slice(h * DV_R, (h + 1) * DV_R)
            yb_ref[:, vl] = (gs_ref[:, vl] * _rms_unit(yr_s[:, vl])).astype(BF16)


def _ret_sample(p, s_ret, rot, nb, l, prev):
    full = lambda w: _const_spec((nb, w))
    st = pl.BlockSpec((None, _RET_BB, H_R, DK_R, DV_R), lambda i: (l, i, 0, 0, 0))
    in_specs = [full(D_QK), full(D_QK), full(D_RV), full(D_RV),
                _const_spec((1, D_QK)), _const_spec((1, D_QK)), _const_spec((1, D_QK)), st]
    prev = () if prev is None else (prev,)
    in_specs += [pl.BlockSpec(memory_space=pl.ANY)] * len(prev)
    aliases = {len(in_specs) - 1: 1} if prev else {}
    return pl.pallas_call(
        functools.partial(_ret_sample_body, nb=nb, n_prev=len(prev)), grid=(nb // _RET_BB,),
        in_specs=in_specs, out_specs=[pl.BlockSpec((nb, D_RV), lambda i: (0, 0)), st],
        out_shape=[jax.ShapeDtypeStruct((nb, D_RV), BF16),
                   jax.ShapeDtypeStruct((DEPTH, nb, H_R, DK_R, DV_R), F32)],
        scratch_shapes=[pltpu.VMEM((D_QK, nb), BF16), pltpu.VMEM((D_QK, nb), BF16),
                        pltpu.VMEM((nb, D_RV), F32)],
        input_output_aliases=aliases, compiler_params=_params(1), name="ret_sample")(
            p["q"], p["k"], p["v"], p["g"], rot["cos"], rot["sin_a"], rot["sin_b"], s_ret, *prev)


def _rotary_tables(start, length):
    half = DK_R // 2
    pos = start + jnp.arange(length, dtype=F32)
    freqs = ROPE_BASE ** (-jnp.arange(half, dtype=F32) / half)
    ang = pos[:, None] * freqs[None, :]
    cos, sin, zero = jnp.cos(ang), jnp.sin(ang), jnp.zeros_like(ang)
    tile = lambda a, b: jnp.tile(jnp.concatenate([a, b], axis=-1), (1, H_R))
    lg = jnp.log1p(-jnp.exp2(-5.0 - jnp.arange(H_R, dtype=F32)))
    return {"cos": tile(cos, cos), "sin_a": tile(-sin, zero), "sin_b": tile(zero, sin),
            "lgx": jnp.repeat(lg, DK_R)[None, :]}


def _matmul_weights(w):
    w_in = w["w_in"]
    o_dt = D_A + D_XBC
    o_gates = w_in.shape[-1] - 3 * D_MODEL
    dt_cols = jnp.pad(w_in[..., o_dt:o_dt + H_A], ((0, 0), (0, 0), (0, DT_PAD - H_A)))
    w_proj = jnp.concatenate([w_in[..., :o_dt], w_in[..., o_dt + H_A:o_gates], dt_cols], axis=-1)
    out = {"w_proj": w_proj.astype(BF16), "w_gates": w_in[..., o_gates:].astype(BF16)}
    for name in ("w_glu", "w_br_a", "w_br_b", "w_br_c", "w_out", "w_up", "w_down"):
        out[name] = w[name].astype(BF16)
    return out


def _layer_weights(l, w, mats):
    pad_row = lambda a: jnp.pad(a, (0, DT_PAD - H_A))[None, :]
    ab_re, ab_im, bb_re, bb_im = _s5_discretise(
        w["s5_a_re"][l], w["s5_a_im"][l], w["s5_log_dt"][l], w["s5_b_re"][l], w["s5_b_im"][l])
    bblk = jnp.concatenate([_block_diag(bb_re), _block_diag(bb_im)], axis=-1)
    c_t = lambda c: _block_diag(jnp.transpose(c, (0, 2, 1)))
    return {
        "l": l, **mats,
        "norm_mix_w": w["norm_mix_w"][l][None, :],
        "conv_a_w": w["conv_a_w"][l], "conv_a_b": w["conv_a_b"][l][None, :],
        "dt_bias": pad_row(w["dt_bias"][l]), "a_log": pad_row(w["a_log"][l]),
        "d_x": jnp.repeat(w["d_a"][l], P_A)[None, :], "norm_a_w": w["norm_a_w"][l][None, :],
        "s5_ab_re": ab_re.reshape(1, D_S5), "s5_ab_im": ab_im.reshape(1, D_S5),
        "s5_bblk": bblk.astype(BF16),
        "s5_cre": c_t(w["s5_c_re"][l]).astype(BF16), "s5_cim": c_t(w["s5_c_im"][l]).astype(BF16),
        "s5_d": w["s5_d"][l][None, :], "b_glu": w["b_glu"][l][None, :],
        "norm_ffn_w": w["norm_ffn_w"][l][None, :],
        "conv_f_w": w["conv_f_w"][l], "conv_f_b": w["conv_f_b"][l][None, :],
    }


def _prompt_trunk(x, lws, nf, tm_proj, tsteps, tt, tm_merge, tm_ffn):
    nb, seq, _ = x.shape
    rot = _rotary_tables(0.0, seq)
    x2d = x.reshape(nb * seq, D_MODEL)
    zeros = jnp.zeros((nb, D_S5), F32)
    states, big = [], None
    for l, lw in enumerate(lws):
        p = _inproj(x2d, lw["norm_mix_w"], lw, rot, nb, seq, tm_proj, conv=True)
        ya, yb, *big = _mix_prompt(p, rot, lw, nb, seq, tsteps, l, big)
        yc, h_re, h_im = _s5(p["u"], zeros, zeros, lw, nb, seq, tt)
        x2d = _merge(x2d, ya, yb, yc, lw, nb, seq, tm_merge)
        x2d, fs = _ffn(x2d, None, lw, nf, nb, seq, tm_ffn, final=(l == len(lws) - 1))
        states.append((p["conv_state"], h_re.reshape(nb, G_C, P_C), h_im.reshape(nb, G_C, P_C), fs))
    cs, h_re, h_im, fs = [jnp.stack(s) for s in zip(*states)]
    return x2d.reshape(nb, seq, D_MODEL), [cs, big[0], big[1], h_re, h_im, fs]


def _sample_trunk(x, st, lws, nf):
    nb = x.shape[0]
    rot = _rotary_tables(float(PAST_LEN), 1)
    x2d = x.reshape(nb, D_MODEL)
    st_conv, st_ssd, st_ret, st_re, st_im, st_ffn = st
    st_ssd_t = jnp.transpose(st_ssd, (0, 2, 3, 4, 1))
    st_conv_t = jnp.transpose(st_conv, (0, 2, 1, 3))
    s5_t = lambda s: jnp.transpose(s, (0, 2, 3, 1)).reshape(DEPTH, D_S5, nb)
    st_re_t, st_im_t = s5_t(st_re), s5_t(st_im)
    states, hs, ss, fs = [], None, None, None
    for l, lw in enumerate(lws):
        p = _inproj(x2d, lw["norm_mix_w"], lw, rot, 1, nb, nb, conv=False)
        ya, cs, hs = _ssd_sample(p, st_conv_t, st_ssd_t, lw, nb, l, hs)
        yb, ss = _ret_sample(p, st_ret, rot, nb, l, ss)
        yc, h_re, h_im = _s5(p["u"], st_re_t[l], st_im_t[l], lw, nb, 1, 1, state_t=True)
        x2d = _merge(x2d, ya, yb, yc, lw, 1, nb, nb)
        x2d, fs = _ffn(x2d, st_ffn, lw, nf, nb, 1, nb, final=(l == len(lws) - 1), prev=fs)
        states.append((cs, h_re, h_im))
    cs, h_re, h_im = [jnp.stack(s) for s in zip(*states)]
    hs = jnp.transpose(hs, (0, 4, 1, 2, 3))
    cs = jnp.transpose(cs, (0, 2, 1, 3))
    s5_back = lambda s: jnp.transpose(s.reshape(DEPTH, G_C, P_C, nb), (0, 3, 1, 2))
    return x2d.reshape(nb, 1, D_MODEL), [cs, hs, ss, s5_back(h_re), s5_back(h_im), fs]


def kernel(x_prompt, x_sample, state_ssd_conv, state_ssd, state_ret, state_s5_re, state_s5_im,
           state_ffn_conv, norm_mix_w, w_in, conv_a_w, conv_a_b, dt_bias, a_log, d_a, norm_a_w,
           s5_a_re, s5_a_im, s5_log_dt, s5_b_re, s5_b_im, s5_c_re, s5_c_im, s5_d, w_glu, b_glu,
           w_br_a, w_br_b, w_br_c, w_out, norm_ffn_w, w_up, conv_f_w, conv_f_b, w_down, norm_f_w):
    w = dict(norm_mix_w=norm_mix_w, w_in=w_in, conv_a_w=conv_a_w, conv_a_b=conv_a_b,
             dt_bias=dt_bias, a_log=a_log, d_a=d_a, norm_a_w=norm_a_w, s5_a_re=s5_a_re,
             s5_a_im=s5_a_im, s5_log_dt=s5_log_dt, s5_b_re=s5_b_re, s5_b_im=s5_b_im,
             s5_c_re=s5_c_re, s5_c_im=s5_c_im, s5_d=s5_d, w_glu=w_glu, b_glu=b_glu,
             w_br_a=w_br_a, w_br_b=w_br_b, w_br_c=w_br_c, w_out=w_out, norm_ffn_w=norm_ffn_w,
             w_up=w_up, conv_f_w=conv_f_w, conv_f_b=conv_f_b, w_down=w_down)
    mats = _matmul_weights(w)
    lws = [_layer_weights(l, w, mats) for l in range(DEPTH)]
    nf = norm_f_w[None, :]
    seq = x_prompt.shape[1]
    y_p, p_st = _prompt_trunk(x_prompt, lws, nf, tm_proj=min(512, seq), tsteps=min(512, seq),
                              tt=min(64, seq), tm_merge=min(512, seq), tm_ffn=min(512, seq))
    y_s, s_st = _sample_trunk(
        x_sample, (state_ssd_conv, state_ssd, state_ret, state_s5_re, state_s5_im, state_ffn_conv),
        lws, nf)
    return (y_p, y_s, *p_st, *s_st)
```

```python
import functools
import math

import jax
import jax.numpy as jnp
import numpy as np
from jax import lax
from jax.experimental import pallas as pl
from jax.experimental.pallas import tpu as pltpu

F32 = jnp.float32
BF16 = jnp.bfloat16

D_MODEL = 1024
DEPTH = 2
PAST_LEN = 16384
H_A, P_A, N_A, G_A, CONV_A = 16, 64, 64, 2, 4
D_A = H_A * P_A
D_XBC = D_A + 2 * G_A * N_A
H_R, DK_R, DV_R = 8, 64, 128
D_QK = H_R * DK_R
D_RV = H_R * DV_R
ROPE_BASE = 10000.0
GS_C, G_C, P_C = 16, 64, 64
D_C = G_C * GS_C
D_S5 = G_C * P_C
D_FF = 2816
CONV_F = 3
CHUNK = 128
EPS = 1e-6
LANES = 128
DT_PAD = LANES
HPG = H_A // G_A
VMEM_LIMIT = 56 * 1024 * 1024

_LOG_GAMMA = [math.log1p(-(2.0 ** (-5.0 - h))) for h in range(H_R)]

_SEGS = (("z", D_A), ("xbc", D_XBC), ("q", D_QK), ("k", D_QK),
         ("v", D_RV), ("g", D_RV), ("u", D_C), ("dt", DT_PAD))
_SEG_OFF = np.cumsum([0] + [w for _, w in _SEGS]).tolist()
D_PROJ = _SEG_OFF[-1]


def _rms_unit(x):
    return x * lax.rsqrt(jnp.mean(x * x, axis=-1, keepdims=True) + EPS)


def _silu(x):
    return x * jax.nn.sigmoid(x)


def _softplus(x):
    return jnp.maximum(x, 0.0) + jnp.log1p(jnp.exp(-jnp.abs(x)))


def _bdot(a, b):
    return jnp.dot(a.astype(BF16), b.astype(BF16), preferred_element_type=F32)


def _split3(x):
    hi = x.astype(BF16)
    r1 = x - hi.astype(F32)
    mid = r1.astype(BF16)
    return hi, mid, (r1 - mid.astype(F32)).astype(BF16)


def _tile3(m01, axis):
    return jnp.concatenate([m01] * 3, axis=axis)


def _dot01(a, b, f32_side):
    if f32_side == "lhs":
        return jnp.dot(jnp.concatenate(_split3(a), axis=1), b, preferred_element_type=F32)
    return jnp.dot(a, jnp.concatenate(_split3(b), axis=0), preferred_element_type=F32)


def _const_spec(shape):
    nd = len(shape)
    return pl.BlockSpec(shape, lambda *_: (0,) * nd, pipeline_mode=pl.Buffered(1))


def _layer_spec(l, shape):
    nd = len(shape)
    return pl.BlockSpec((None,) + tuple(shape), lambda *_: (l,) + (0,) * nd,
                        pipeline_mode=pl.Buffered(1))


def _params(n_grid):
    return pltpu.CompilerParams(dimension_semantics=("arbitrary",) * n_grid,
                                vmem_limit_bytes=VMEM_LIMIT)


_PROJ_SUB = 128


def _inproj_body(x_ref, nw_ref, cw_ref, cb_ref, cos_ref, sina_ref, sinb_ref, w_ref, *refs,
                 tm, conv):
    n = len(_SEGS)
    if conv:
        cs_ref, xp_s = refs[n:]

        @pl.when(pl.program_id(1) == 0)
        def _():
            xp_s[0:8, :] = jnp.zeros((8, D_XBC), F32)

    for r in range(0, tm, _PROJ_SUB):
        rs = slice(r, r + _PROJ_SUB)
        h = (_rms_unit(x_ref[rs, :]) * nw_ref[...]).astype(BF16)
        for (name, width), off, o_ref in zip(_SEGS, _SEG_OFF, refs[:n]):
            y = jnp.dot(h, w_ref[:, off:off + width], preferred_element_type=F32)
            if name in ("z", "g"):
                y = _silu(y)
            elif name in ("q", "k") and conv:
                y = _rope(y, cos_ref[rs, :], sina_ref[rs, :], sinb_ref[rs, :])
                if name == "k":
                    y = y * (DK_R ** -0.5)
            elif name == "xbc" and conv:
                xp_s[8 + r:8 + r + _PROJ_SUB, :] = y
                y = cb_ref[...]
                for tap in range(CONV_A):
                    y = y + xp_s[5 + tap + r:5 + tap + r + _PROJ_SUB, :] * cw_ref[tap:tap + 1, :]
                y = _silu(y)
            o_ref[rs, :] = y
    if conv:
        tail = xp_s[tm + 5:tm + 8, :]
        cs_ref[...] = tail
        xp_s[5:8, :] = tail


def _inproj(x2d, nw, lw, rot, nb, seq, tm, conv):
    nt = seq // tm
    tab = (pl.BlockSpec((tm, D_QK), lambda b, i: (i, 0)) if conv
           else _const_spec(rot["cos"].shape))
    in_specs = [pl.BlockSpec((tm, D_MODEL), lambda b, i: (b * nt + i, 0)),
                _const_spec((1, D_MODEL)), _const_spec((CONV_A, D_XBC)), _const_spec((1, D_XBC)),
                tab, tab, tab, _layer_spec(lw["l"], (D_MODEL, D_PROJ))]
    out_specs, out_shape, scratch = [], [], []
    for _, width in _SEGS:
        out_specs.append(pl.BlockSpec((tm, width), lambda b, i: (b * nt + i, 0)))
        out_shape.append(jax.ShapeDtypeStruct((nb * seq, width), F32))
    if conv:
        out_specs.append(pl.BlockSpec((None, CONV_A - 1, D_XBC), lambda b, i: (b, 0, 0)))
        out_shape.append(jax.ShapeDtypeStruct((nb, CONV_A - 1, D_XBC), F32))
        scratch.append(pltpu.VMEM((8 + tm, D_XBC), F32))
    outs = pl.pallas_call(
        functools.partial(_inproj_body, tm=tm, conv=conv), grid=(nb, nt), in_specs=in_specs,
        out_specs=out_specs, out_shape=out_shape, scratch_shapes=scratch,
        compiler_params=_params(2), name="inproj")(
            x2d, nw, lw["conv_a_w"], lw["conv_a_b"], rot["cos"], rot["sin_a"], rot["sin_b"],
            lw["w_proj"])
    res = {name: o for (name, _), o in zip(_SEGS, outs)}
    if conv:
        res["conv_state"] = outs[-1]
    return res


def _rope(x, cos_f, sin_a, sin_b):
    half = DK_R // 2
    return (x * cos_f + pltpu.roll(x, D_QK - half, 1) * sin_a
            + pltpu.roll(x, half, 1) * sin_b)


def _head_expand():
    lo = lax.broadcasted_iota(jnp.int32, (LANES, D_A), 0) * P_A
    c = lax.broadcasted_iota(jnp.int32, (LANES, D_A), 1)
    return jnp.where(c >= lo, jnp.where(c < lo + P_A, 1.0, 0.0), 0.0).astype(BF16)


def _mix_body(*refs, tsteps, nt, n_prev):
    (zs_ref, xc_ref, dt_ref, q_ref, k_ref, v_ref, gs_ref, lgx_ref,
     dtb_ref, alog_ref, dx_ref, naw_ref) = refs[:12]
    (ya_ref, yb_ref, hs_ref, ss_ref,
     ht_s, s_s, intra_s, qdec_s, kdect_s) = refs[12 + n_prev:]
    i = pl.program_id(1)
    first = jnp.logical_and(pl.program_id(0) == 0, i == 0)
    rows_i = lax.broadcasted_iota(jnp.int32, (CHUNK, CHUNK), 0)
    cols_i = lax.broadcasted_iota(jnp.int32, (CHUNK, CHUNK), 1)
    causal = rows_i >= cols_i
    lane_lo = cols_i < LANES // 2

    @pl.when(first)
    def _():
        rel = (rows_i - cols_i).astype(F32)
        for h in range(H_R):
            intra_s[h] = jnp.where(causal, jnp.exp(jnp.maximum(rel, 0.0) * _LOG_GAMMA[h]), 0.0)
        ri = lax.broadcasted_iota(jnp.int32, (CHUNK, D_QK), 0).astype(F32)
        lgx = lgx_ref[...]
        qdec_s[...] = jnp.exp((ri + 1.0) * lgx)
        kdect_s[...] = jnp.exp((CHUNK - 1.0 - ri) * lgx).T

    @pl.when(i == 0)
    def _():
        ht_s[...] = jnp.zeros_like(ht_s)
        s_s[...] = jnp.zeros_like(s_s)

    tril = _tile3(jnp.where(causal, 1.0, 0.0).astype(BF16), 1)
    expand = _tile3(_head_expand(), 0)
    a_neg = -jnp.exp(alog_ref[...])

    for c in range(tsteps // CHUNK):
        r0 = c * CHUNK
        rows = slice(r0, r0 + CHUNK)
        xa = xc_ref[rows, :D_A]
        bm = xc_ref[rows, D_A:D_A + G_A * N_A]
        cm = xc_ref[rows, D_A + G_A * N_A:]
        dt = _softplus(dt_ref[rows, :] + dtb_ref[...])
        cum = _dot01(tril, dt * a_neg, "rhs")
        cum_t = cum.T
        dt_t = dt.T
        cum_last = cum[CHUNK - 1:CHUNK, :]
        scale = jnp.concatenate(
            [dt * jnp.exp(cum_last - cum), jnp.exp(cum),
             jnp.broadcast_to(jnp.exp(cum_last), (8, LANES))], axis=0)
        scale_x = _dot01(scale, expand, "lhs")
        w_x = scale_x[:CHUNK]
        ecum_x = scale_x[CHUNK:2 * CHUNK]
        dec_x = scale_x[2 * CHUNK:2 * CHUNK + 1]
        bm_t = bm.T
        ys = []
        for g in range(G_A):
            gl = slice(g * N_A, (g + 1) * N_A)
            hl = slice(g * HPG * P_A, (g + 1) * HPG * P_A)
            cg = cm[:, gl]
            scores = lax.dot_general(cg.astype(BF16), bm[:, gl].astype(BF16),
                                     (((1,), (1,)), ((), ())), preferred_element_type=F32)
            scores = jnp.where(causal, scores, 0.0)
            ht_g = ht_s[:, hl]
            y_inter = _bdot(cg, ht_g) * ecum_x[:, hl]
            for pair in range(HPG // 2):
                pl_ = slice(g * HPG * P_A + pair * LANES, g * HPG * P_A + (pair + 1) * LANES)
                x_pair = xa[:, pl_]
                sps = []
                for h in (g * HPG + 2 * pair, g * HPG + 2 * pair + 1):
                    seg = cum[:, h:h + 1] - cum_t[h:h + 1, :]
                    sps.append(scores * jnp.exp(jnp.minimum(seg, 0.0)) * dt_t[h:h + 1, :])
                rhs = jnp.concatenate([jnp.where(lane_lo, x_pair, 0.0),
                                       jnp.where(lane_lo, 0.0, x_pair)], axis=0)
                ys.append(_bdot(jnp.concatenate(sps, axis=1), rhs)
                          + y_inter[:, pair * LANES:(pair + 1) * LANES])
            ht_s[:, hl] = dec_x[:, hl] * ht_g + _bdot(bm_t[gl, :], xa[:, hl] * w_x[:, hl])
        y = jnp.concatenate(ys, axis=1) + dx_ref[...] * xa
        y = y * zs_ref[rows, :]
        ya_ref[rows, :] = (_rms_unit(y) * naw_ref[...]).astype(BF16)

        qr = q_ref[rows, :]
        k_t = k_ref[rows, :].T
        kd_t = k_t * kdect_s[...]
        qd = qr * qdec_s[...]
        for pair in range(H_R // 2):
            pl_ = slice(pair * LANES, (pair + 1) * LANES)
            s_pair = s_s[pl_, :]
            for h, keep in ((2 * pair, lane_lo), (2 * pair + 1, jnp.logical_not(lane_lo))):
                kl = slice(h * DK_R, (h + 1) * DK_R)
                vl = slice(h * DV_R, (h + 1) * DV_R)
                v_h = v_ref[rows, vl]
                sc = _bdot(jnp.where(keep, qr[:, pl_], 0.0), k_t[pl_, :]) * intra_s[h]
                y_h = _bdot(jnp.concatenate([sc, jnp.where(keep, qd[:, pl_], 0.0)], axis=1),
                            jnp.concatenate([v_h, s_pair], axis=0))
                s_s[kl, :] = (math.exp(CHUNK * _LOG_GAMMA[h]) * s_s[kl, :]
                              + _bdot(kd_t[kl, :], v_h))
                yb_ref[rows, vl] = (gs_ref[rows, vl] * _rms_unit(y_h)).astype(BF16)

    @pl.when(i == nt - 1)
    def _():
        for h in range(H_A):
            hs_ref[h] = ht_s[:, h * P_A:(h + 1) * P_A].T
        for h in range(H_R):
            ss_ref[h] = s_s[h * DK_R:(h + 1) * DK_R, :]


def _layer_block(l, tail):
    zeros = (0,) * len(tail)
    return pl.BlockSpec((None, None) + tail, lambda b, *_: (l, b) + zeros)


def _mix_prompt(p, rot, lw, nb, seq, tsteps, l, prev):
    nt = seq // tsteps
    row = lambda w: pl.BlockSpec((tsteps, w), lambda b, i: (b * nt + i, 0))
    in_specs = [row(D_A), row(D_XBC), row(DT_PAD), row(D_QK), row(D_QK), row(D_RV), row(D_RV),
                _const_spec((1, D_QK)), _const_spec((1, DT_PAD)),
                _const_spec((1, DT_PAD)), _const_spec((1, D_A)), _const_spec((1, D_A))]
    prev = () if prev is None else tuple(prev)
    in_specs += [pl.BlockSpec(memory_space=pl.ANY)] * len(prev)
    aliases = {len(in_specs) - len(prev) + k: 2 + k for k in range(len(prev))}
    out_specs = [row(D_A), row(D_RV),
                 _layer_block(l, (H_A, P_A, N_A)), _layer_block(l, (H_R, DK_R, DV_R))]
    out_shape = [jax.ShapeDtypeStruct((nb * seq, D_A), BF16),
                 jax.ShapeDtypeStruct((nb * seq, D_RV), BF16),
                 jax.ShapeDtypeStruct((DEPTH, nb, H_A, P_A, N_A), F32),
                 jax.ShapeDtypeStruct((DEPTH, nb, H_R, DK_R, DV_R), F32)]
    scratch = [pltpu.VMEM((N_A, D_A), F32),
               pltpu.VMEM((D_QK, DV_R), F32), pltpu.VMEM((H_R, CHUNK, CHUNK), F32),
               pltpu.VMEM((CHUNK, D_QK), F32), pltpu.VMEM((D_QK, CHUNK), F32)]
    return pl.pallas_call(
        functools.partial(_mix_body, tsteps=tsteps, nt=nt, n_prev=len(prev)), grid=(nb, nt),
        in_specs=in_specs, out_specs=out_specs, out_shape=out_shape, scratch_shapes=scratch,
        input_output_aliases=aliases, compiler_params=_params(2), name="mix_prompt")(
            p["z"], p["xbc"], p["dt"], p["q"], p["k"], p["v"], p["g"],
            rot["lgx"], lw["dt_bias"], lw["a_log"], lw["d_x"], lw["norm_a_w"], *prev)


def _s5_disc_body(are_ref, aim_ref, ldt_ref, bre_ref, bim_ref,
                  abre_ref, abim_ref, bbre_ref, bbim_ref):
    ar, ai = are_ref[...], aim_ref[...]
    dt = jnp.exp(ldt_ref[...])
    mag = jnp.exp(ar * dt)
    ab_re = mag * jnp.cos(ai * dt)
    ab_im = mag * jnp.sin(ai * dt)
    den = ar * ar + ai * ai
    num_re = ab_re - 1.0
    coef_re = (num_re * ar + ab_im * ai) / den
    coef_im = (ab_im * ar - num_re * ai) / den
    abre_ref[...] = ab_re
    abim_ref[...] = ab_im
    for c in range(GS_C):
        cl = slice(c * P_C, (c + 1) * P_C)
        br, bi = bre_ref[:, cl], bim_ref[:, cl]
        bbre_ref[:, cl] = coef_re * br - coef_im * bi
        bbim_ref[:, cl] = coef_re * bi + coef_im * br


def _s5_discretise(a_re, a_im, log_dt, b_re, b_im):
    gp = jax.ShapeDtypeStruct((G_C, P_C), F32)
    gcp = jax.ShapeDtypeStruct((G_C, GS_C * P_C), F32)
    b_t = lambda b: jnp.transpose(b, (0, 2, 1)).reshape(G_C, GS_C * P_C)
    ab_re, ab_im, bb_re, bb_im = pl.pallas_call(
        _s5_disc_body, out_shape=[gp, gp, gcp, gcp], name="s5_disc")(
            a_re, a_im, log_dt.reshape(G_C, 1), b_t(b_re), b_t(b_im))
    return ab_re, ab_im, bb_re.reshape(G_C, GS_C, P_C), bb_im.reshape(G_C, GS_C, P_C)


_S5_GB = LANES // GS_C
_S5_NBLK = G_C // _S5_GB
_S5_SB = _S5_GB * P_C


def _block_diag(m):
    g, r, c = m.shape
    m = m.reshape(_S5_NBLK, _S5_GB, r, c)
    eye = jnp.eye(_S5_GB, dtype=m.dtype)
    return jnp.einsum("jgrc,gk->jgrkc", m, eye).reshape(_S5_NBLK, _S5_GB * r, _S5_GB * c)


def _s5_body(u_ref, h0re_ref, h0im_ref, are_ref, aim_ref, bblk_ref, cre_ref, cim_ref,
             d_ref, wglu_ref, bglu_ref, yc_ref, hre_ref, him_ref,
             xre_s, xim_s, sre_s, sim_s, *, nb, tt, state_t):
    i = pl.program_id(0)
    rows = nb * tt

    @pl.when(i == 0)
    def _():
        sre_s[...] = h0re_ref[...].T if state_t else h0re_ref[...]
        sim_s[...] = h0im_ref[...].T if state_t else h0im_ref[...]

    u = u_ref[...]
    if tt > 1:
        u = jnp.swapaxes(u, 0, 1).reshape(rows, D_C)
    ub = u.astype(BF16)

    ys = []
    for j in range(_S5_NBLK):
        sl = slice(j * _S5_SB, (j + 1) * _S5_SB)
        bu = jnp.dot(ub[:, j * LANES:(j + 1) * LANES], bblk_ref[j], preferred_element_type=F32)
        xre_s[:, sl] = bu[:, :_S5_SB]
        xim_s[:, sl] = bu[:, _S5_SB:]
        ar = jnp.broadcast_to(are_ref[:, sl], (nb, _S5_SB))
        ai = jnp.broadcast_to(aim_ref[:, sl], (nb, _S5_SB))
        xr, xi = sre_s[:, sl], sim_s[:, sl]
        for t in range(tt):
            r = slice(t * nb, (t + 1) * nb)
            xr, xi = (ar * xr - ai * xi + xre_s[r, sl], ar * xi + ai * xr + xim_s[r, sl])
            xre_s[r, sl] = xr
            xim_s[r, sl] = xi
        sre_s[:, sl] = xr
        sim_s[:, sl] = xi
        ys.append(_bdot(xre_s[:, sl], cre_ref[j]) - _bdot(xim_s[:, sl], cim_ref[j]))
    yc = jnp.concatenate(ys, axis=1)
    parts = []
    half = rows // 2 if rows >= 256 else rows
    for r in range(0, rows, half):
        yh = jax.nn.gelu(yc[r:r + half] + d_ref[...] * u[r:r + half])
        parts.append(yh * jax.nn.sigmoid(_bdot(yh, wglu_ref[...]) + bglu_ref[...]))
    y = jnp.concatenate(parts, axis=0)
    if tt > 1:
        y = jnp.swapaxes(y.reshape(tt, nb, D_C), 0, 1)
    yc_ref[...] = y.astype(BF16)
    hre_ref[...] = sre_s[...].T if state_t else sre_s[...]
    him_ref[...] = sim_s[...].T if state_t else sim_s[...]


def _s5(u, h0_re, h0_im, lw, nb, seq, tt, state_t=False):
    rows = tt * nb
    sshape = (D_S5, nb) if state_t else (nb, D_S5)
    if tt > 1:
        assert nb % 8 == 0 and tt % 8 == 0
        u = u.reshape(nb, seq, D_C)
        io_spec = pl.BlockSpec((nb, tt, D_C), lambda i: (0, i, 0))
    else:
        io_spec = pl.BlockSpec((rows, D_C), lambda i: (i, 0))
    in_specs = [io_spec,
                _const_spec(sshape), _const_spec(sshape),
                _const_spec((1, D_S5)), _const_spec((1, D_S5)),
                _const_spec((_S5_NBLK, LANES, 2 * _S5_SB)),
                _const_spec((_S5_NBLK, _S5_SB, LANES)), _const_spec((_S5_NBLK, _S5_SB, LANES)),
                _const_spec((1, D_C)), _layer_spec(lw["l"], (D_C, D_C)), _const_spec((1, D_C))]
    st = pl.BlockSpec(sshape, lambda i: (0, 0))
    yc, h_re, h_im = pl.pallas_call(
        functools.partial(_s5_body, nb=nb, tt=tt, state_t=state_t), grid=(seq // tt,),
        in_specs=in_specs, out_specs=[io_spec, st, st],
        out_shape=[jax.ShapeDtypeStruct(u.shape, BF16),
                   jax.ShapeDtypeStruct(sshape, F32), jax.ShapeDtypeStruct(sshape, F32)],
        scratch_shapes=[pltpu.VMEM((rows, D_S5), F32), pltpu.VMEM((rows, D_S5), F32),
                        pltpu.VMEM((nb, D_S5), F32), pltpu.VMEM((nb, D_S5), F32)],
        compiler_params=_params(1), name="s5")(
            u, h0_re, h0_im, lw["s5_ab_re"], lw["s5_ab_im"], lw["s5_bblk"],
            lw["s5_cre"], lw["s5_cim"], lw["s5_d"], lw["w_glu"], lw["b_glu"])
    return yc.reshape(nb * seq, D_C), h_re, h_im


def _merge_body(x_ref, nw_ref, ya_ref, yb_ref, yc_ref, wg_ref, wa_ref, wb_ref, wc_ref, wo_ref,
                o_ref):
    x = x_ref[...]
    h = (_rms_unit(x) * nw_ref[...]).astype(BF16)
    merged = None
    for k, (y_ref, w_ref) in enumerate(((ya_ref, wa_ref), (yb_ref, wb_ref), (yc_ref, wc_ref))):
        gate = jax.nn.sigmoid(jnp.dot(h, wg_ref[:, k * D_MODEL:(k + 1) * D_MODEL],
                                      preferred_element_type=F32))
        term = gate * _bdot(y_ref[...], w_ref[...])
        merged = term if merged is None else merged + term
    o_ref[...] = x + _bdot(merged, wo_ref[...])


def _merge(x2d, ya, yb, yc, lw, nb, seq, tm):
    nt = seq // tm
    row = lambda w: pl.BlockSpec((tm, w), lambda b, i: (b * nt + i, 0))
    wspec = _layer_spec(lw["l"], (D_MODEL, D_MODEL))
    return pl.pallas_call(
        _merge_body, grid=(nb, nt),
        in_specs=[row(D_MODEL), _const_spec((1, D_MODEL)), row(D_A), row(D_RV), row(D_C),
                  _layer_spec(lw["l"], (D_MODEL, 3 * D_MODEL)), wspec, wspec, wspec, wspec],
        out_specs=row(D_MODEL), out_shape=jax.ShapeDtypeStruct((nb * seq, D_MODEL), F32),
        compiler_params=_params(2), name="merge")(
            x2d, lw["norm_mix_w"], ya, yb, yc, lw["w_gates"],
            lw["w_br_a"], lw["w_br_b"], lw["w_br_c"], lw["w_out"])


_FFN_CW = 256


def _ffn_prompt_body(x_ref, nw_ref, wup_ref, cw_ref, cb_ref, wdn_ref, nf_ref,
                     o_ref, cs_ref, up_s, act_s, *, tm, final):
    @pl.when(pl.program_id(1) == 0)
    def _():
        up_s[0:8, :] = jnp.zeros((8, 2 * D_FF), F32)

    x = x_ref[...]
    hf = (_rms_unit(x) * nw_ref[...]).astype(BF16)
    up_s[8:8 + tm, :] = jnp.dot(hf, wup_ref[...], preferred_element_type=F32)
    for c in range(0, D_FF, _FFN_CW):
        def conv(c0):
            cl = slice(c0, c0 + _FFN_CW)
            y = cb_ref[:, cl]
            for tap in range(CONV_F):
                y = y + up_s[6 + tap:6 + tap + tm, cl] * cw_ref[tap:tap + 1, cl]
            return y
        act_s[:, c:c + _FFN_CW] = (_silu(conv(c)) * conv(D_FF + c)).astype(BF16)
    out = x + jnp.dot(act_s[...], wdn_ref[...], preferred_element_type=F32)
    o_ref[...] = _rms_unit(out) * nf_ref[...] if final else out
    tail = up_s[tm + 6:tm + 8, :]
    cs_ref[...] = tail
    up_s[6:8, :] = tail


def _ffn_sample_body(x_ref, st_ref, nw_ref, wup_ref, cw_ref, cb_ref, wdn_ref, nf_ref, *refs,
                     final):
    o_ref, cs_ref, act_s = refs[-3:]
    x = x_ref[...]
    hf = (_rms_unit(x) * nw_ref[...]).astype(BF16)
    up = jnp.dot(hf, wup_ref[...], preferred_element_type=F32)
    prev2, prev1 = st_ref[:, 0, :], st_ref[:, 1, :]
    cs_ref[:, 0, :] = prev1
    cs_ref[:, 1, :] = up
    for c in range(0, D_FF, _FFN_CW):
        def conv(c0):
            cl = slice(c0, c0 + _FFN_CW)
            return (cb_ref[:, cl] + prev2[:, cl] * cw_ref[0:1, cl] + prev1[:, cl] * cw_ref[1:2, cl]
                    + up[:, cl] * cw_ref[2:3, cl])
        act_s[:, c:c + _FFN_CW] = (_silu(conv(c)) * conv(D_FF + c)).astype(BF16)
    out = x + jnp.dot(act_s[...], wdn_ref[...], preferred_element_type=F32)
    o_ref[...] = _rms_unit(out) * nf_ref[...] if final else out


def _ffn(x2d, state, lw, nf, nb, seq, tm, final, prev=None):
    wspecs = [_const_spec((1, D_MODEL)), _layer_spec(lw["l"], (D_MODEL, 2 * D_FF)),
              _const_spec((CONV_F, 2 * D_FF)), _const_spec((1, 2 * D_FF)),
              _layer_spec(lw["l"], (D_FF, D_MODEL)), _const_spec((1, D_MODEL))]
    wargs = (lw["norm_ffn_w"], lw["w_up"], lw["conv_f_w"], lw["conv_f_b"], lw["w_down"], nf)
    act = pltpu.VMEM((tm, D_FF), BF16)
    if state is None:
        nt = seq // tm
        row = pl.BlockSpec((tm, D_MODEL), lambda b, i: (b * nt + i, 0))
        return pl.pallas_call(
            functools.partial(_ffn_prompt_body, tm=tm, final=final), grid=(nb, nt),
            in_specs=[row] + wspecs,
            out_specs=[row, pl.BlockSpec((None, CONV_F - 1, 2 * D_FF), lambda b, i: (b, 0, 0))],
            out_shape=[jax.ShapeDtypeStruct((nb * seq, D_MODEL), F32),
                       jax.ShapeDtypeStruct((nb, CONV_F - 1, 2 * D_FF), F32)],
            scratch_shapes=[pltpu.VMEM((8 + tm, 2 * D_FF), F32), act],
            compiler_params=_params(2), name="ffn_prompt")(x2d, *wargs)
    l = lw["l"]
    st = pl.BlockSpec((None, tm, CONV_F - 1, 2 * D_FF), lambda i: (l, i, 0, 0))
    prev = () if prev is None else (prev,)
    in_specs = ([pl.BlockSpec((tm, D_MODEL), lambda i: (i, 0)), st] + wspecs
                + [pl.BlockSpec(memory_space=pl.ANY)] * len(prev))
    return pl.pallas_call(
        functools.partial(_ffn_sample_body, final=final), grid=(nb // tm,), in_specs=in_specs,
        out_specs=[pl.BlockSpec((tm, D_MODEL), lambda i: (i, 0)), st],
        out_shape=[jax.ShapeDtypeStruct((nb, D_MODEL), F32),
                   jax.ShapeDtypeStruct(state.shape, F32)],
        scratch_shapes=[act], input_output_aliases={len(in_specs) - 1: 1} if prev else {},
        compiler_params=_params(1), name="ffn_sample")(x2d, state, *wargs, *prev)


def _ssd_sample_body(*refs, n_prev):
    (zs_ref, xbc_ref, dt_ref, cst_ref, cw_ref, cb_ref, dtb_ref, alog_ref, dx_ref, naw_ref,
     hin_ref) = refs[:11]
    (ya_ref, cso_ref, hout_ref, xa_s, xdt_t_s, dec_t_s, bm_t_s, cm_t_s, yt_s) = refs[11 + n_prev:]
    i = pl.program_id(0)

    @pl.when(i == 0)
    def _():
        acc = cb_ref[...]
        for tap in range(CONV_A - 1):
            acc = acc + cst_ref[tap] * cw_ref[tap:tap + 1, :]
        acc = acc + xbc_ref[...] * cw_ref[CONV_A - 1:CONV_A, :]
        for tap in range(CONV_A - 2):
            cso_ref[tap] = cst_ref[tap + 1]
        cso_ref[CONV_A - 2] = xbc_ref[...]
        xc = _silu(acc)
        xa = xc[:, :D_A]
        xa_s[...] = xa
        bm_t_s[...] = xc[:, D_A:D_A + G_A * N_A].T
        cm_t_s[...] = xc[:, D_A + G_A * N_A:].T
        dt = _softplus(dt_ref[...] + dtb_ref[...])
        dec_t_s[...] = jnp.exp(dt * -jnp.exp(alog_ref[...])).T
        dt_x = _dot01(dt, _tile3(_head_expand(), 0), "lhs")
        xdt_t_s[...] = (xa * dt_x).T

    for hh in range(_SSD_HB):
        h = i * _SSD_HB + hh
        g0 = pl.multiple_of((h // HPG) * N_A, N_A)
        p0 = pl.multiple_of(h * P_A, P_A)
        b_t = bm_t_s[pl.ds(g0, N_A), :]
        c_t = cm_t_s[pl.ds(g0, N_A), :]
        dec = dec_t_s[pl.ds(h, 1), :]

        def per_8p(k, carry, hh=hh, p0=p0, b_t=b_t, c_t=c_t, dec=dec):
            r = pl.ds(pl.multiple_of(p0 + k * 8, 8), 8)
            x_rows = xdt_t_s[r, :]
            ys = []
            for u in range(8):
                h_new = hin_ref[hh, k * 8 + u] * dec + x_rows[u:u + 1, :] * b_t
                hout_ref[hh, k * 8 + u] = h_new
                ys.append(jnp.sum(h_new * c_t, axis=0, keepdims=True))
            yt_s[r, :] = jnp.concatenate(ys, axis=0)
            return carry

        lax.fori_loop(0, P_A // 8, per_8p, 0)

    @pl.when(i == H_A // _SSD_HB - 1)
    def _():
        xa = xa_s[...]
        y = yt_s[...].T + dx_ref[...] * xa
        y = y * zs_ref[...]
        ya_ref[...] = (_rms_unit(y) * naw_ref[...]).astype(BF16)


_SSD_HB = 2


def _ssd_sample(p, conv_st_t, h_ssd_t, lw, nb, l, prev):
    full = lambda w: _const_spec((nb, w))
    cshape = (CONV_A - 1, nb, D_XBC)
    st = pl.BlockSpec((None, _SSD_HB, P_A, N_A, nb), lambda i: (l, i, 0, 0, 0))
    in_specs = [full(D_A), full(D_XBC), full(DT_PAD), _layer_spec(l, cshape),
                _const_spec((CONV_A, D_XBC)), _const_spec((1, D_XBC)), _const_spec((1, DT_PAD)),
                _const_spec((1, DT_PAD)), _const_spec((1, D_A)), _const_spec((1, D_A)), st]
    prev = () if prev is None else (prev,)
    in_specs += [pl.BlockSpec(memory_space=pl.ANY)] * len(prev)
    aliases = {len(in_specs) - 1: 2} if prev else {}
    scratch = [pltpu.VMEM((nb, D_A), F32), pltpu.VMEM((D_A, nb), F32), pltpu.VMEM((LANES, nb), F32),
               pltpu.VMEM((G_A * N_A, nb), F32), pltpu.VMEM((G_A * N_A, nb), F32),
               pltpu.VMEM((D_A, nb), F32)]
    return pl.pallas_call(
        functools.partial(_ssd_sample_body, n_prev=len(prev)), grid=(H_A // _SSD_HB,),
        in_specs=in_specs,
        out_specs=[pl.BlockSpec((nb, D_A), lambda i: (0, 0)),
                   pl.BlockSpec(cshape, lambda i: (0, 0, 0)), st],
        out_shape=[jax.ShapeDtypeStruct((nb, D_A), BF16), jax.ShapeDtypeStruct(cshape, F32),
                   jax.ShapeDtypeStruct((DEPTH, H_A, P_A, N_A, nb), F32)],
        scratch_shapes=scratch, input_output_aliases=aliases,
        compiler_params=_params(1), name="ssd_sample")(
            p["z"], p["xbc"], p["dt"], conv_st_t,
            lw["conv_a_w"], lw["conv_a_b"], lw["dt_bias"], lw["a_log"], lw["d_x"], lw["norm_a_w"],
            h_ssd_t, *prev)


_RET_BB = 16


def _ret_sample_body(*refs, nb, n_prev):
    q_ref, k_ref, v_ref, gs_ref, cos_ref, sina_ref, sinb_ref, sin_ref = refs[:8]
    yb_ref, sout_ref, q_t_s, k_t_s, yr_s = refs[8 + n_prev:]
    i = pl.program_id(0)

    @pl.when(i == 0)
    def _():
        cos_f, sin_a, sin_b = cos_ref[...], sina_ref[...], sinb_ref[...]
        q_t_s[...] = _rope(q_ref[...], cos_f, sin_a, sin_b).T.astype(BF16)
        k_t_s[...] = (_rope(k_ref[...], cos_f, sin_a, sin_b) * (DK_R ** -0.5)).T.astype(BF16)

    rows_i = lax.broadcasted_iota(jnp.int32, (nb, DV_R), 0)
    blk = pl.ds(pl.multiple_of(i * _RET_BB, _RET_BB), _RET_BB)
    v_blk = v_ref[blk, :]
    ys = [[] for _ in range(H_R)]
    for j in range(_RET_BB):
        onehot = jnp.where(rows_i == i * _RET_BB + j, 1.0, 0.0).astype(BF16)
        k_bc = jnp.dot(k_t_s[...], onehot, preferred_element_type=F32)
        q_bc = jnp.dot(q_t_s[...], onehot, preferred_element_type=F32)
        for h in range(H_R):
            kl = slice(h * DK_R, (h + 1) * DK_R)
            vl = slice(h * DV_R, (h + 1) * DV_R)
            s_new = math.exp(_LOG_GAMMA[h]) * sin_ref[j, h] + k_bc[kl, :] * v_blk[j:j + 1, vl]
            sout_ref[j, h] = s_new
            ys[h].append(jnp.sum(q_bc[kl, :] * s_new, axis=0, keepdims=True))
    for h in range(H_R):
        yr_s[blk, h * DV_R:(h + 1) * DV_R] = jnp.concatenate(ys[h], axis=0)

    @pl.when(i == nb // _RET_BB - 1)
    def _():
        for h in range(H_R):
            vl = slice(h * DV_R, (h + 1) * DV_R)
            yb_ref[:, vl] = (gs_ref[:, vl] * _rms_unit(yr_s[:, vl])).astype(BF16)


def _ret_sample(p, s_ret, rot, nb, l, prev):
    full = lambda w: _const_spec((nb, w))
    st = pl.BlockSpec((None, _RET_BB, H_R, DK_R, DV_R), lambda i: (l, i, 0, 0, 0))
    in_specs = [full(D_QK), full(D_QK), full(D_RV), full(D_RV),
                _const_spec((1, D_QK)), _const_spec((1, D_QK)), _const_spec((1, D_QK)), st]
    prev = () if prev is None else (prev,)
    in_specs += [pl.BlockSpec(memory_space=pl.ANY)] * len(prev)
    aliases = {len(in_specs) - 1: 1} if prev else {}
    return pl.pallas_call(
        functools.partial(_ret_sample_body, nb=nb, n_prev=len(prev)), grid=(nb // _RET_BB,),
        in_specs=in_specs, out_specs=[pl.BlockSpec((nb, D_RV), lambda i: (0, 0)), st],
        out_shape=[jax.ShapeDtypeStruct((nb, D_RV), BF16),
                   jax.ShapeDtypeStruct((DEPTH, nb, H_R, DK_R, DV_R), F32)],
        scratch_shapes=[pltpu.VMEM((D_QK, nb), BF16), pltpu.VMEM((D_QK, nb), BF16),
                        pltpu.VMEM((nb, D_RV), F32)],
        input_output_aliases=aliases, compiler_params=_params(1), name="ret_sample")(
            p["q"], p["k"], p["v"], p["g"], rot["cos"], rot["sin_a"], rot["sin_b"], s_ret, *prev)


def _rotary_tables(start, length):
    half = DK_R // 2
    pos = start + jnp.arange(length, dtype=F32)
    freqs = ROPE_BASE ** (-jnp.arange(half, dtype=F32) / half)
    ang = pos[:, None] * freqs[None, :]
    cos, sin, zero = jnp.cos(ang), jnp.sin(ang), jnp.zeros_like(ang)
    tile = lambda a, b: jnp.tile(jnp.concatenate([a, b], axis=-1), (1, H_R))
    lg = jnp.log1p(-jnp.exp2(-5.0 - jnp.arange(H_R, dtype=F32)))
    return {"cos": tile(cos, cos), "sin_a": tile(-sin, zero), "sin_b": tile(zero, sin),
            "lgx": jnp.repeat(lg, DK_R)[None, :]}


def _matmul_weights(w):
    w_in = w["w_in"]
    o_dt = D_A + D_XBC
    o_gates = w_in.shape[-1] - 3 * D_MODEL
    dt_cols = jnp.pad(w_in[..., o_dt:o_dt + H_A], ((0, 0), (0, 0), (0, DT_PAD - H_A)))
    w_proj = jnp.concatenate([w_in[..., :o_dt], w_in[..., o_dt + H_A:o_gates], dt_cols], axis=-1)
    out = {"w_proj": w_proj.astype(BF16), "w_gates": w_in[..., o_gates:].astype(BF16)}
    for name in ("w_glu", "w_br_a", "w_br_b", "w_br_c", "w_out", "w_up", "w_down"):
        out[name] = w[name].astype(BF16)
    return out


def _layer_weights(l, w, mats):
    pad_row = lambda a: jnp.pad(a, (0, DT_PAD - H_A))[None, :]
    ab_re, ab_im, bb_re, bb_im = _s5_discretise(
        w["s5_a_re"][l], w["s5_a_im"][l], w["s5_log_dt"][l], w["s5_b_re"][l], w["s5_b_im"][l])
    bblk = jnp.concatenate([_block_diag(bb_re), _block_diag(bb_im)], axis=-1)
    c_t = lambda c: _block_diag(jnp.transpose(c, (0, 2, 1)))
    return {
        "l": l, **mats,
        "norm_mix_w": w["norm_mix_w"][l][None, :],
        "conv_a_w": w["conv_a_w"][l], "conv_a_b": w["conv_a_b"][l][None, :],
        "dt_bias": pad_row(w["dt_bias"][l]), "a_log": pad_row(w["a_log"][l]),
        "d_x": jnp.repeat(w["d_a"][l], P_A)[None, :], "norm_a_w": w["norm_a_w"][l][None, :],
        "s5_ab_re": ab_re.reshape(1, D_S5), "s5_ab_im": ab_im.reshape(1, D_S5),
        "s5_bblk": bblk.astype(BF16),
        "s5_cre": c_t(w["s5_c_re"][l]).astype(BF16), "s5_cim": c_t(w["s5_c_im"][l]).astype(BF16),
        "s5_d": w["s5_d"][l][None, :], "b_glu": w["b_glu"][l][None, :],
        "norm_ffn_w": w["norm_ffn_w"][l][None, :],
        "conv_f_w": w["conv_f_w"][l], "conv_f_b": w["conv_f_b"][l][None, :],
    }


def _prompt_trunk(x, lws, nf, tm_proj, tsteps, tt, tm_merge, tm_ffn):
    nb, seq, _ = x.shape
    rot = _rotary_tables(0.0, seq)
    x2d = x.reshape(nb * seq, D_MODEL)
    zeros = jnp.zeros((nb, D_S5), F32)
    states, big = [], None
    for l, lw in enumerate(lws):
        p = _inproj(x2d, lw["norm_mix_w"], lw, rot, nb, seq, tm_proj, conv=True)
        ya, yb, *big = _mix_prompt(p, rot, lw, nb, seq, tsteps, l, big)
        yc, h_re, h_im = _s5(p["u"], zeros, zeros, lw, nb, seq, tt)
        x2d = _merge(x2d, ya, yb, yc, lw, nb, seq, tm_merge)
        x2d, fs = _ffn(x2d, None, lw, nf, nb, seq, tm_ffn, final=(l == len(lws) - 1))
        states.append((p["conv_state"], h_re.reshape(nb, G_C, P_C), h_im.reshape(nb, G_C, P_C), fs))
    cs, h_re, h_im, fs = [jnp.stack(s) for s in zip(*states)]
    return x2d.reshape(nb, seq, D_MODEL), [cs, big[0], big[1], h_re, h_im, fs]


def _sample_trunk(x, st, lws, nf):
    nb = x.shape[0]
    rot = _rotary_tables(float(PAST_LEN), 1)
    x2d = x.reshape(nb, D_MODEL)
    st_conv, st_ssd, st_ret, st_re, st_im, st_ffn = st
    st_ssd_t = jnp.transpose(st_ssd, (0, 2, 3, 4, 1))
    st_conv_t = jnp.transpose(st_conv, (0, 2, 1, 3))
    s5_t = lambda s: jnp.transpose(s, (0, 2, 3, 1)).reshape(DEPTH, D_S5, nb)
    st_re_t, st_im_t = s5_t(st_re), s5_t(st_im)
    states, hs, ss, fs = [], None, None, None
    for l, lw in enumerate(lws):
        p = _inproj(x2d, lw["norm_mix_w"], lw, rot, 1, nb, nb, conv=False)
        ya, cs, hs = _ssd_sample(p, st_conv_t, st_ssd_t, lw, nb, l, hs)
        yb, ss = _ret_sample(p, st_ret, rot, nb, l, ss)
        yc, h_re, h_im = _s5(p["u"], st_re_t[l], st_im_t[l], lw, nb, 1, 1, state_t=True)
        x2d = _merge(x2d, ya, yb, yc, lw, 1, nb, nb)
        x2d, fs = _ffn(x2d, st_ffn, lw, nf, nb, 1, nb, final=(l == len(lws) - 1), prev=fs)
        states.append((cs, h_re, h_im))
    cs, h_re, h_im = [jnp.stack(s) for s in zip(*states)]
    hs = jnp.transpose(hs, (0, 4, 1, 2, 3))
    cs = jnp.transpose(cs, (0, 2, 1, 3))
    s5_back = lambda s: jnp.transpose(s.reshape(DEPTH, G_C, P_C, nb), (0, 3, 1, 2))
    return x2d.reshape(nb, 1, D_MODEL), [cs, hs, ss, s5_back(h_re), s5_back(h_im), fs]


def kernel(x_prompt, x_sample, state_ssd_conv, state_ssd, state_ret, state_s5_re, state_s5_im,
           state_ffn_conv, norm_mix_w, w_in, conv_a_w, conv_a_b, dt_bias, a_log, d_a, norm_a_w,
           s5_a_re, s5_a_im, s5_log_dt, s5_b_re, s5_b_im, s5_c_re, s5_c_im, s5_d, w_glu, b_glu,
           w_br_a, w_br_b, w_br_c, w_out, norm_ffn_w, w_up, conv_f_w, conv_f_b, w_down, norm_f_w):
    w = dict(norm_mix_w=norm_mix_w, w_in=w_in, conv_a_w=conv_a_w, conv_a_b=conv_a_b,
             dt_bias=dt_bias, a_log=a_log, d_a=d_a, norm_a_w=norm_a_w, s5_a_re=s5_a_re,
             s5_a_im=s5_a_im, s5_log_dt=s5_log_dt, s5_b_re=s5_b_re, s5_b_im=s5_b_im,
             s5_c_re=s5_c_re, s5_c_im=s5_c_im, s5_d=s5_d, w_glu=w_glu, b_glu=b_glu,
             w_br_a=w_br_a, w_br_b=w_br_b, w_br_c=w_br_c, w_out=w_out, norm_ffn_w=norm_ffn_w,
             w_up=w_up, conv_f_w=conv_f_w, conv_f_b=conv_f_b, w_down=w_down)
    mats = _matmul_weights(w)
    lws = [_layer_weights(l, w, mats) for l in range(DEPTH)]
    nf = norm_f_w[None, :]
    seq = x_prompt.shape[1]
    y_p, p_st = _prompt_trunk(x_prompt, lws, nf, tm_proj=min(512, seq), tsteps=min(512, seq),
                              tt=min(64, seq), tm_merge=min(512, seq), tm_ffn=min(512, seq))
    y_s, s_st = _sample_trunk(
        x_sample, (state_ssd_conv, state_ssd, state_ret, state_s5_re, state_s5_im, state_ffn_conv),
        lws, nf)
    return (y_p, y_s, *p_st, *s_st)
```
